```python
import math
import jax, jax.numpy as jnp
from jax import lax
import numpy as np

D_MODEL = 1024
BATCH = 4
SEQ = 4096
DEPTH = 1

GRID_W = 64
NA_WIDTH = D_MODEL // 2
NA_HEAD_DIM = 64
NA_HEADS = NA_WIDTH // NA_HEAD_DIM
NA_KH_MAX = 8
NA_KW = 16
S5_WIDTH = D_MODEL // 2
S5_GROUP = 16
S5_GROUPS = S5_WIDTH // S5_GROUP
S5_STATE = 64
DT_MIN = 1e-3
DT_MAX = 1e-1
MEM_TOKENS = 256
MEM_HEADS = 4
MEM_WIDTH = D_MODEL // 2
MEM_HEAD_DIM = MEM_WIDTH // MEM_HEADS
N_BRANCH = 3
BRANCH_WIDTH = D_MODEL // 2
IN_WIDTH = 3 * NA_WIDTH + S5_WIDTH + MEM_WIDTH
D_FF = -(-8 * D_MODEL // (3 * 256)) * 256
EPS = 1e-6
NEG_INF = -1e30

kernel_name = "hybrid_natten_s5_memxattn_gated_block"


def rms_norm(x, g):
    x32 = x.astype(jnp.float32)
    y = x32 * lax.rsqrt(jnp.mean(x32 * x32, axis=-1, keepdims=True) + EPS)
    return (y * g.astype(jnp.float32)).astype(x.dtype)


def neighbourhood_attention(q, k, v, rpb):
    b, s, h, dh = q.shape
    rows = s // GRID_W
    kh = min(NA_KH_MAX, rows)
    qg = q.reshape(b, rows, GRID_W, h, dh)
    kg = k.reshape(b, rows, GRID_W, h, dh)
    vg = v.reshape(b, rows, GRID_W, h, dh)
    r = jnp.arange(rows)
    row_start = jnp.clip(r - kh // 2, 0, rows - kh)
    row_idx = row_start[:, None] + jnp.arange(kh)[None, :]
    k_band = kg[:, row_idx]
    v_band = vg[:, row_idx]
    scores = jnp.einsum('brqhd,brikhd->bhrqik', qg, k_band).astype(jnp.float32) * (dh ** -0.5)
    c = jnp.arange(GRID_W)
    col_start = jnp.clip(c - NA_KW // 2, 0, GRID_W - NA_KW)
    in_win = (c[None, :] >= col_start[:, None]) & (c[None, :] < col_start[:, None] + NA_KW)
    rel_row = row_idx - r[:, None] + (NA_KH_MAX - 1)
    rel_col = jnp.clip(c[None, :] - c[:, None] + (NA_KW - 1), 0, 2 * NA_KW - 2)
    bias = rpb[:, rel_row[:, None, :, None], rel_col[None, :, None, :]]
    scores = jnp.where(in_win[:, None, :], scores + bias[None].astype(jnp.float32), NEG_INF)
    probs = jax.nn.softmax(scores.reshape(b, h, rows, GRID_W, kh * GRID_W), axis=-1)
    probs = probs.reshape(b, h, rows, GRID_W, kh, GRID_W).astype(v.dtype)
    out = jnp.einsum('bhrqik,brikhd->brqhd', probs, v_band)
    return out.reshape(b, s, h * dh)


def _ssm_combine(e1, e2):
    a1, b1 = e1
    a2, b2 = e2
    return a2 * a1, a2 * b1 + b2


def s5_bidirectional(u, a_re, a_im, log_dt, b_re, b_im, c_re, c_im, d, w_glu):
    bsz, s, _ = u.shape
    f32 = jnp.float32
    u32 = u.astype(f32)
    ug = u32.reshape(bsz, s, S5_GROUPS, S5_GROUP).astype(jnp.complex64)
    y = d.astype(f32) * u32
    for direction, reverse in ((0, False), (1, True)):
        lam = lax.complex(a_re[direction].astype(f32), a_im[direction].astype(f32))
        dt = jnp.exp(log_dt[direction].astype(f32))[:, None]
        lam_bar = jnp.exp(lam * dt)
        b = lax.complex(b_re[direction].astype(f32), b_im[direction].astype(f32))
        b_bar = ((lam_bar - 1.0) / lam)[..., None] * b
        bu = jnp.einsum('bsgc,gpc->bsgp', ug, b_bar)
        a = jnp.broadcast_to(lam_bar, bu.shape)
        _, states = lax.associative_scan(_ssm_combine, (a, bu), reverse=reverse, axis=1)
        cm = lax.complex(c_re[direction].astype(f32), c_im[direction].astype(f32))
        y = y + jnp.einsum('bsgp,gcp->bsgc', states, cm).real.reshape(bsz, s, S5_WIDTH)
    z = jax.nn.gelu(y)
    out = z * jax.nn.sigmoid(z @ w_glu.astype(f32))
    return out.astype(u.dtype)


def memory_cross_attention(q, mem_n, w_mem_kv):
    b, s, _ = q.shape
    m = mem_n.shape[1]
    kv = mem_n @ w_mem_kv
    k, v = jnp.split(kv, 2, axis=-1)
    qh = q.reshape(b, s, MEM_HEADS, MEM_HEAD_DIM)
    kh = k.reshape(b, m, MEM_HEADS, MEM_HEAD_DIM)
    vh = v.reshape(b, m, MEM_HEADS, MEM_HEAD_DIM)
    scores = jnp.einsum('bshd,bmhd->bhsm', qh, kh).astype(jnp.float32) * (MEM_HEAD_DIM ** -0.5)
    probs = jax.nn.softmax(scores, axis=-1).astype(vh.dtype)
    return jnp.einsum('bhsm,bmhd->bshd', probs, vh).reshape(b, s, MEM_WIDTH)


def setup_inputs(seed: int = 0) -> dict:
    key = jax.random.key(seed)
    ks = jax.random.split(key, 32)
    f32 = jnp.float32
    L = DEPTH

    def nrm(i, shape, scale):
        return jax.random.normal(ks[i], shape, f32) * scale

    n = jnp.arange(S5_STATE, dtype=f32)
    log_dt = math.log(DT_MIN) + jax.random.uniform(ks[13], (L, 2, S5_GROUPS), f32) * (math.log(DT_MAX) - math.log(DT_MIN))
    return {
        "x": nrm(0, (BATCH, SEQ, D_MODEL), 1.0),
        "mem": nrm(1, (BATCH, MEM_TOKENS, D_MODEL), 1.0),
        "g_mix": 1.0 + nrm(2, (L, D_MODEL), 0.02),
        "g_mem": 1.0 + nrm(3, (L, D_MODEL), 0.02),
        "g_ffn": 1.0 + nrm(4, (L, D_MODEL), 0.02),
        "g_final": 1.0 + nrm(5, (D_MODEL,), 0.02),
        "w_in": nrm(6, (L, D_MODEL, IN_WIDTH), D_MODEL ** -0.5),
        "w_gate": nrm(7, (L, D_MODEL, N_BRANCH * D_MODEL), D_MODEL ** -0.5),
        "b_gate": nrm(8, (L, N_BRANCH * D_MODEL), 0.01),
        "rpb": nrm(9, (L, NA_HEADS, 2 * NA_KH_MAX - 1, 2 * NA_KW - 1), 0.02),
        "w_mem_kv": nrm(10, (L, D_MODEL, 2 * MEM_WIDTH), D_MODEL ** -0.5),
        "a_re": -0.5 + nrm(11, (L, 2, S5_GROUPS, S5_STATE), 0.01),
        "a_im": math.pi * n + nrm(12, (L, 2, S5_GROUPS, S5_STATE), 0.01),
        "log_dt": log_dt,
        "b_re": nrm(14, (L, 2, S5_GROUPS, S5_STATE, S5_GROUP), (2 * S5_GROUP) ** -0.5),
        "b_im": nrm(15, (L, 2, S5_GROUPS, S5_STATE, S5_GROUP), (2 * S5_GROUP) ** -0.5),
        "c_re": nrm(16, (L, 2, S5_GROUPS, S5_GROUP, S5_STATE), S5_STATE ** -0.5),
        "c_im": nrm(17, (L, 2, S5_GROUPS, S5_GROUP, S5_STATE), S5_STATE ** -0.5),
        "s5_d": nrm(18, (L, S5_WIDTH), 1.0),
        "w_glu": nrm(19, (L, S5_WIDTH, S5_WIDTH), S5_WIDTH ** -0.5),
        "w_branch": nrm(20, (L, N_BRANCH, BRANCH_WIDTH, D_MODEL), BRANCH_WIDTH ** -0.5),
        "w_o": nrm(21, (L, D_MODEL, D_MODEL), D_MODEL ** -0.5),
        "w_ffn1": nrm(22, (L, D_MODEL, D_FF), D_MODEL ** -0.5),
        "w_ffn3": nrm(23, (L, D_MODEL, D_FF), D_MODEL ** -0.5),
        "w_ffn2": nrm(24, (L, D_FF, D_MODEL), D_FF ** -0.5),
    }


def reference(x, mem, g_mix, g_mem, g_ffn, g_final, w_in, w_gate, b_gate, rpb, w_mem_kv,
              a_re, a_im, log_dt, b_re, b_im, c_re, c_im, s5_d, w_glu, w_branch, w_o,
              w_ffn1, w_ffn3, w_ffn2):
    bsz, s, d = x.shape
    split_at = [NA_WIDTH, 2 * NA_WIDTH, 3 * NA_WIDTH, 3 * NA_WIDTH + S5_WIDTH]
    for l in range(DEPTH):
        h = rms_norm(x, g_mix[l])
        proj = h @ w_in[l]
        q_na, k_na, v_na, u_s5, q_mem = jnp.split(proj, split_at, axis=-1)
        hd = (bsz, s, NA_HEADS, NA_HEAD_DIM)
        y_na = neighbourhood_attention(q_na.reshape(hd), k_na.reshape(hd), v_na.reshape(hd), rpb[l])
        y_s5 = s5_bidirectional(u_s5, a_re[l], a_im[l], log_dt[l], b_re[l], b_im[l],
                                c_re[l], c_im[l], s5_d[l], w_glu[l])
        y_mem = memory_cross_attention(q_mem, rms_norm(mem, g_mem[l]), w_mem_kv[l])
        ys = jnp.stack([y_na, y_s5, y_mem], axis=2)
        up = jnp.einsum('bsnc,ncd->bsnd', ys, w_branch[l])
        gates = jax.nn.sigmoid(h @ w_gate[l] + b_gate[l]).reshape(bsz, s, N_BRANCH, d)
        merged = jnp.sum(gates * up, axis=2)
        x = x + merged @ w_o[l]
        h2 = rms_norm(x, g_ffn[l])
        x = x + (jax.nn.silu(h2 @ w_ffn1[l]) * (h2 @ w_ffn3[l])) @ w_ffn2[l]
    return rms_norm(x, g_final)
```

```python
import functools

import jax
import jax.numpy as jnp
from jax import lax
from jax.experimental import pallas as pl
from jax.experimental.pallas import tpu as pltpu

F32 = jnp.float32
BF16 = jnp.bfloat16

D = 1024
GRID_W = 64
NA_HEADS = 8
NA_KH = 8
NA_KW = 16
S5_GROUPS = 32
S5_GROUP = 16
S5_STATE = 64
CHUNK = 64
MEM_HEADS = 4
MEM_HEAD_DIM = 128
D_FF = 2816
EPS = 1e-6
NEG_INF = -1e30

VMEM_LIMIT = 56 * 1024 * 1024

NT_DIMS = (((1,), (1,)), ((), ()))


def _cparams(n_axes):
    return pltpu.CompilerParams(
        dimension_semantics=("arbitrary",) * n_axes,
        vmem_limit_bytes=VMEM_LIMIT)


def _rms(x, g):
    return x * lax.rsqrt(jnp.mean(x * x, axis=-1, keepdims=True) + EPS) * g


def _const_spec(shape):
    nd = len(shape)
    return pl.BlockSpec(shape, lambda *_: (0,) * nd, pipeline_mode=pl.Buffered(1))


def _proj_kernel(x_ref, g_ref, w_ref, q_ref, k_ref, v_ref, qm_ref):
    h = _rms(x_ref[...], g_ref[...]).astype(BF16)
    p = jnp.dot(h, w_ref[...], preferred_element_type=F32)
    q_ref[...] = p[:, 0:512].astype(BF16)
    k_ref[...] = p[:, 512:1024].astype(BF16)
    v_ref[...] = p[:, 1024:1536].astype(BF16)
    qm_ref[...] = p[:, 1536:2048].astype(BF16)


def _proj(x2d, g, w):
    n = x2d.shape[0]
    tm = 512
    out = jax.ShapeDtypeStruct((n, 512), BF16)
    ospec = pl.BlockSpec((tm, 512), lambda i: (i, 0))
    return pl.pallas_call(
        _proj_kernel,
        grid=(n // tm,),
        in_specs=[pl.BlockSpec((tm, D), lambda i: (i, 0)),
                  _const_spec((1, D)),
                  _const_spec((D, 2048))],
        out_specs=[ospec] * 4,
        out_shape=[out] * 4,
        compiler_params=_cparams(1),
        name="proj",
    )(x2d, g, w)


S5T_DT = 8


def _proj_s5t_kernel(x_ref, g_ref, wt_ref, o_ref):
    g = g_ref[...]
    wt = wt_ref[...]
    for j in range(S5T_DT):
        h = _rms(x_ref[:, j * D:(j + 1) * D], g).astype(BF16)
        ut = lax.dot_general(wt, h, NT_DIMS, preferred_element_type=F32)
        o_ref[:, j * S5_GROUP:(j + 1) * S5_GROUP, :] = (
            ut.astype(BF16).reshape(S5_GROUPS, S5_GROUP, 256))


def _proj_s5t(xc, g, wt):
    nrows = xc.shape[0]
    return pl.pallas_call(
        _proj_s5t_kernel,
        grid=(CHUNK // S5T_DT,),
        in_specs=[pl.BlockSpec((nrows, S5T_DT * D), lambda i: (0, i)),
                  _const_spec((1, D)),
                  _const_spec((512, D))],
        out_specs=pl.BlockSpec((S5_GROUPS, S5T_DT * S5_GROUP, nrows), lambda i: (0, i, 0)),
        out_shape=jax.ShapeDtypeStruct((S5_GROUPS, CHUNK * S5_GROUP, nrows), BF16),
        compiler_params=_cparams(1),
        name="proj_s5t",
    )(xc, g, wt)


N_SCAN_LEVELS = 6


def _gelu_tanh(x):
    c = 0.7978845608028654
    return 0.5 * x * (1.0 + jnp.tanh(c * (x + 0.044715 * (x * x * x))))


def _s5_kernel(ut_ref, m_ref, p_ref, n_ref, d_ref, z_ref):
    ut = ut_ref[0]
    y = jnp.dot(m_ref[0], ut, preferred_element_type=F32)
    v = jnp.dot(p_ref[0], ut, preferred_element_type=F32).T
    nrows = v.shape[0]
    pos = lax.broadcasted_iota(jnp.int32, (nrows, 128), 0) & (CHUNK - 1)

    def shift(x, s, up):
        if up:
            return jnp.where(pos < CHUNK - s, pltpu.roll(x, nrows - s, axis=0), 0.0)
        return jnp.where(pos >= s, pltpu.roll(x, s, axis=0), 0.0)

    def scan(vh, lanes, up):
        x = shift(vh, 1, up)
        for lvl in range(N_SCAN_LEVELS):
            xs = shift(x, 1 << lvl, up)
            a = d_ref[0, lvl:lvl + 1, lanes]
            b = d_ref[0, 8 + lvl:9 + lvl, lanes]
            x = x + a * xs + b * pltpu.roll(xs, 64, axis=1)
        return x

    xf = scan(v[:, 0:128], slice(0, 128), False)
    xb = scan(v[:, 128:256], slice(128, 256), True)
    xin = jnp.concatenate([xf, xb], axis=1).astype(BF16)
    y = y + lax.dot_general(n_ref[0], xin, NT_DIMS, preferred_element_type=F32)
    z_ref[0] = _gelu_tanh(y).astype(BF16)


def _s5_core(ut, m, p, n, dtab):
    nrows = ut.shape[2]
    tc = CHUNK * S5_GROUP
    return pl.pallas_call(
        _s5_kernel,
        grid=(S5_GROUPS,),
        in_specs=[pl.BlockSpec((1, tc, nrows), lambda g: (g, 0, 0)),
                  pl.BlockSpec((1, tc, tc), lambda g: (g, 0, 0)),
                  pl.BlockSpec((1, 256, tc), lambda g: (g, 0, 0)),
                  pl.BlockSpec((1, tc, 256), lambda g: (g, 0, 0)),
                  pl.BlockSpec((1, 16, 256), lambda g: (g, 0, 0))],
        out_specs=pl.BlockSpec((1, tc, nrows), lambda g: (g, 0, 0)),
        out_shape=jax.ShapeDtypeStruct((S5_GROUPS, tc, nrows), BF16),
        compiler_params=_cparams(1),
        name="s5_core",
    )(ut, m, p, n, dtab)


def _s5_glu_kernel(z_ref, wt_ref, o_ref):
    wt = wt_ref[...]
    nrows = z_ref.shape[2]
    eye = jnp.where(lax.broadcasted_iota(jnp.int32, (nrows, nrows), 0)
                    == lax.broadcasted_iota(jnp.int32, (nrows, nrows), 1), 1.0, 0.0).astype(BF16)
    for j in range(S5T_DT):
        zt = z_ref[:, j * S5_GROUP:(j + 1) * S5_GROUP, :].reshape(512, nrows)
        gl = jnp.dot(wt, zt, preferred_element_type=F32)
        o = (zt.astype(F32) * jax.nn.sigmoid(gl)).astype(BF16)
        nat = lax.dot_general(eye, o, NT_DIMS, preferred_element_type=F32)
        o_ref[:, j * 512:(j + 1) * 512] = nat.astype(BF16)


def _s5_glu(zt, wglu_t):
    nrows = zt.shape[2]
    return pl.pallas_call(
        _s5_glu_kernel,
        grid=(CHUNK // S5T_DT,),
        in_specs=[pl.BlockSpec((S5_GROUPS, S5T_DT * S5_GROUP, nrows), lambda i: (0, i, 0)),
                  _const_spec((512, 512))],
        out_specs=pl.BlockSpec((nrows, S5T_DT * 512), lambda i: (0, i)),
        out_shape=jax.ShapeDtypeStruct((nrows, CHUNK * 512), BF16),
        compiler_params=_cparams(1),
        name="s5_glu",
    )(zt, wglu_t)


NA_ROWS_PER_STEP = 8


def _na_kernel(q_ref, k_ref, v_ref, bias_ref, mask_ref, o_ref):
    rb = pl.program_id(1)
    lo = lax.broadcasted_iota(jnp.int32, (GRID_W, 128), 1) < 64
    valid = mask_ref[...] > 0.0
    nkeys = NA_KH * GRID_W

    def row_body(i, carry):
        r = rb * NA_ROWS_PER_STEP + i
        start = jnp.clip(r - NA_KH // 2, 0, GRID_W - NA_KH)
        case = r - start
        koff = pl.multiple_of(start * GRID_W, GRID_W)
        qoff = pl.multiple_of(i * GRID_W, GRID_W)
        for s in range(4):
            cols = slice(128 * s, 128 * (s + 1))
            q_s = q_ref[0, pl.ds(qoff, GRID_W), cols]
            k_s = k_ref[0, pl.ds(koff, nkeys), cols]
            v_s = v_ref[0, pl.ds(koff, nkeys), cols]
            outs = []
            for hh in range(2):
                qh = jnp.where(lo if hh == 0 else jnp.logical_not(lo), q_s, jnp.zeros_like(q_s))
                sc = lax.dot_general(qh, k_s, NT_DIMS, preferred_element_type=F32)
                sc = jnp.where(valid, sc + bias_ref[case, 2 * s + hh], NEG_INF)
                m = jnp.max(sc, axis=-1, keepdims=True)
                p = jnp.exp(sc - m)
                l = jnp.sum(p, axis=-1, keepdims=True)
                o = jnp.dot(p.astype(BF16), v_s, preferred_element_type=F32)
                outs.append(o / l)
            o_ref[0, pl.ds(qoff, GRID_W), cols] = jnp.where(lo, outs[0], outs[1]).astype(BF16)
        return carry

    lax.fori_loop(0, NA_ROWS_PER_STEP, row_body, 0)


def _na(q, k, v, bias, mask):
    b, s, _ = q.shape
    tq = NA_ROWS_PER_STEP * GRID_W
    return pl.pallas_call(
        _na_kernel,
        grid=(b, s // tq),
        in_specs=[pl.BlockSpec((1, tq, 512), lambda bi, ri: (bi, ri, 0)),
                  pl.BlockSpec((1, s, 512), lambda bi, ri: (bi, 0, 0)),
                  pl.BlockSpec((1, s, 512), lambda bi, ri: (bi, 0, 0)),
                  _const_spec(bias.shape),
                  _const_spec(mask.shape)],
        out_specs=pl.BlockSpec((1, tq, 512), lambda bi, ri: (bi, ri, 0)),
        out_shape=jax.ShapeDtypeStruct((b, s, 512), BF16),
        compiler_params=_cparams(2),
        name="na",
    )(q, k, v, bias, mask)


def _memkv_kernel(mem_ref, g_ref, w_ref, k_ref, v_ref):
    mn = _rms(mem_ref[0], g_ref[...]).astype(BF16)
    kv = jnp.dot(mn, w_ref[...], preferred_element_type=F32)
    k_ref[0] = kv[:, 0:512].astype(BF16)
    v_ref[0] = kv[:, 512:1024].astype(BF16)


def _memkv(mem, g, w):
    b, m, _ = mem.shape
    out = jax.ShapeDtypeStruct((b, m, 512), BF16)
    ospec = pl.BlockSpec((1, m, 512), lambda i: (i, 0, 0))
    return pl.pallas_call(
        _memkv_kernel,
        grid=(b,),
        in_specs=[pl.BlockSpec((1, m, D), lambda i: (i, 0, 0)),
                  _const_spec((1, D)),
                  _const_spec((D, 1024))],
        out_specs=[ospec, ospec],
        out_shape=[out, out],
        compiler_params=_cparams(1),
        name="memkv",
    )(mem, g, w)


def _mem_kernel(q_ref, k_ref, v_ref, o_ref):
    scale = MEM_HEAD_DIM ** -0.5
    for h in range(MEM_HEADS):
        cols = slice(MEM_HEAD_DIM * h, MEM_HEAD_DIM * (h + 1))
        sc = lax.dot_general(q_ref[0, :, cols], k_ref[0, :, cols], NT_DIMS,
                             preferred_element_type=F32) * scale
        m = jnp.max(sc, axis=-1, keepdims=True)
        p = jnp.exp(sc - m)
        l = jnp.sum(p, axis=-1, keepdims=True)
        o = jnp.dot(p.astype(BF16), v_ref[0, :, cols], preferred_element_type=F32)
        o_ref[0, :, cols] = (o / l).astype(BF16)


def _mem_attn(q, k, v):
    b, s, _ = q.shape
    m = k.shape[1]
    tq = 1024
    return pl.pallas_call(
        _mem_kernel,
        grid=(b, s // tq),
        in_specs=[pl.BlockSpec((1, tq, 512), lambda bi, i: (bi, i, 0)),
                  pl.BlockSpec((1, m, 512), lambda bi, i: (bi, 0, 0)),
                  pl.BlockSpec((1, m, 512), lambda bi, i: (bi, 0, 0))],
        out_specs=pl.BlockSpec((1, tq, 512), lambda bi, i: (bi, i, 0)),
        out_shape=jax.ShapeDtypeStruct((b, s, 512), BF16),
        compiler_params=_cparams(2),
        name="mem_attn",
    )(q, k, v)


def _merge_kernel(x_ref, g_ref, yna_ref, ys5_ref, ymem_ref, wg_ref, bg_ref, wb_ref, wo_ref, o_ref):
    x = x_ref[...]
    h = _rms(x, g_ref[...]).astype(BF16)
    merged = None
    for b, y_ref in enumerate((yna_ref, ys5_ref, ymem_ref)):
        cols = slice(D * b, D * (b + 1))
        gate = jax.nn.sigmoid(jnp.dot(h, wg_ref[:, cols], preferred_element_type=F32) + bg_ref[:, cols])
        up = jnp.dot(y_ref[...], wb_ref[b], preferred_element_type=F32)
        merged = gate * up if merged is None else merged + gate * up
    o_ref[...] = x + jnp.dot(merged.astype(BF16), wo_ref[...], preferred_element_type=F32)


def _merge(x2d, g, yna, ys5, ymem, wg, bg, wb, wo):
    n = x2d.shape[0]
    tm = 512
    yspec = pl.BlockSpec((tm, 512), lambda i: (i, 0))
    return pl.pallas_call(
        _merge_kernel,
        grid=(n // tm,),
        in_specs=[pl.BlockSpec((tm, D), lambda i: (i, 0)),
                  _const_spec((1, D)),
                  yspec, yspec, yspec,
                  _const_spec((D, 3 * D)),
                  _const_spec((1, 3 * D)),
                  _const_spec((3, 512, D)),
                  _const_spec((D, D))],
        out_specs=pl.BlockSpec((tm, D), lambda i: (i, 0)),
        out_shape=jax.ShapeDtypeStruct((n, D), F32),
        compiler_params=_cparams(1),
        name="merge",
    )(x2d, g, yna, ys5, ymem, wg, bg, wb, wo)


def _ffn_kernel(x_ref, g_ref, gf_ref, w1_ref, w3_ref, w2_ref, o_ref):
    x = x_ref[...]
    h = _rms(x, g_ref[...]).astype(BF16)
    a = jnp.dot(h, w1_ref[...], preferred_element_type=F32)
    c = jnp.dot(h, w3_ref[...], preferred_element_type=F32)
    mid = (a * jax.nn.sigmoid(a) * c).astype(BF16)
    x2 = x + jnp.dot(mid, w2_ref[...], preferred_element_type=F32)
    o_ref[...] = _rms(x2, gf_ref[...])


def _ffn(x2d, g, gf, w1, w3, w2):
    n = x2d.shape[0]
    tm = 512
    return pl.pallas_call(
        _ffn_kernel,
        grid=(n // tm,),
        in_specs=[pl.BlockSpec((tm, D), lambda i: (i, 0)),
                  _const_spec((1, D)),
                  _const_spec((1, D)),
                  _const_spec((D, D_FF)),
                  _const_spec((D, D_FF)),
                  _const_spec((D_FF, D))],
        out_specs=pl.BlockSpec((tm, D), lambda i: (i, 0)),
        out_shape=jax.ShapeDtypeStruct((n, D), F32),
        compiler_params=_cparams(1),
        name="ffn",
    )(x2d, g, gf, w1, w3, w2)


def _s5_tables(a_re, a_im, log_dt, b_re, b_im, c_re, c_im, s5_d):
    hi = lax.Precision.HIGHEST
    t = CHUNK
    lam = lax.complex(a_re.astype(F32), a_im.astype(F32))
    ldt = lam * jnp.exp(log_dt.astype(F32))[..., None]
    lam_bar = jnp.exp(ldt)
    bb = ((lam_bar - 1.0) / lam)[..., None] * lax.complex(b_re.astype(F32), b_im.astype(F32))
    cm = lax.complex(c_re.astype(F32), c_im.astype(F32))
    tau = jnp.arange(t + 1, dtype=F32)
    pw = jnp.exp(tau[:, None, None, None] * ldt[None])

    w1 = cm[:, None] * jnp.moveaxis(pw[:t], 1, 0)[:, :, :, None, :]
    kk = (jnp.einsum('dtgcp,dgpi->dtgci', jnp.real(w1), jnp.real(bb), precision=hi)
          - jnp.einsum('dtgcp,dgpi->dtgci', jnp.imag(w1), jnp.imag(bb), precision=hi))
    dmat = jnp.eye(S5_GROUP, dtype=F32)[None] * s5_d.astype(F32).reshape(S5_GROUPS, S5_GROUP)[:, :, None]
    k0 = kk[0, 0] + kk[1, 0] + dmat
    kfull = jnp.concatenate([kk[1, 1:][::-1], k0[None], kk[0, 1:]], axis=0)
    idx = jnp.arange(t)[:, None] - jnp.arange(t)[None, :] + (t - 1)
    m = kfull[idx]
    m = jnp.transpose(m, (2, 0, 3, 1, 4)).reshape(S5_GROUPS, t * S5_GROUP, t * S5_GROUP)

    wf = jnp.moveaxis(pw[:t][::-1, 0], 0, -1)
    wb = jnp.moveaxis(pw[:t, 1], 0, -1)
    pf = (wf[..., None] * bb[0][:, :, None, :]).reshape(S5_GROUPS, S5_STATE, t * S5_GROUP)
    pb = (wb[..., None] * bb[1][:, :, None, :]).reshape(S5_GROUPS, S5_STATE, t * S5_GROUP)
    p = jnp.concatenate([jnp.real(pf), jnp.imag(pf), jnp.real(pb), jnp.imag(pb)], axis=1)

    nf = cm[0][:, None] * jnp.moveaxis(pw[1:t + 1, 0], 0, 1)[:, :, None, :]
    nb = cm[1][:, None] * jnp.moveaxis(pw[1:t + 1][::-1, 1], 0, 1)[:, :, None, :]
    n = jnp.concatenate([jnp.real(nf), -jnp.imag(nf), jnp.real(nb), -jnp.imag(nb)], axis=-1)
    n = n.reshape(S5_GROUPS, t * S5_GROUP, 4 * S5_STATE)

    lv = (t * (2.0 ** jnp.arange(N_SCAN_LEVELS, dtype=F32)))[:, None, None, None]
    dl = jnp.exp(lv * ldt[None])
    dr, di = jnp.real(dl), jnp.imag(dl)
    a_rows = jnp.concatenate([dr[:, 0], dr[:, 0], dr[:, 1], dr[:, 1]], axis=-1)
    b_rows = jnp.concatenate([-di[:, 0], di[:, 0], -di[:, 1], di[:, 1]], axis=-1)
    pad = jnp.zeros((8 - N_SCAN_LEVELS, S5_GROUPS, 4 * S5_STATE), F32)
    dtab = jnp.transpose(jnp.concatenate([a_rows, pad, b_rows, pad], axis=0), (1, 0, 2))
    return m.astype(BF16), p.astype(BF16), n.astype(BF16), dtab


def _na_bias_tables(rpb):
    c = jnp.arange(GRID_W)
    col_start = jnp.clip(c - NA_KW // 2, 0, GRID_W - NA_KW)
    in_win = (c[None, :] >= col_start[:, None]) & (c[None, :] < col_start[:, None] + NA_KW)
    rel_col = jnp.clip(c[None, :] - c[:, None] + (NA_KW - 1), 0, 2 * NA_KW - 2)
    case = jnp.arange(NA_KH)
    rel_row = jnp.arange(NA_KH)[None, :] - case[:, None] + (NA_KH - 1)
    bias = rpb.astype(F32)[:, rel_row[:, None, :, None], rel_col[None, :, None, :]]
    bias = jnp.transpose(bias, (1, 0, 2, 3, 4)).reshape(NA_KH, NA_HEADS, GRID_W, NA_KH * GRID_W)
    mask = jnp.broadcast_to(in_win[:, None, :], (GRID_W, NA_KH, GRID_W)).reshape(GRID_W, NA_KH * GRID_W)
    return bias, mask.astype(F32)


def kernel(x, mem, g_mix, g_mem, g_ffn, g_final, w_in, w_gate, b_gate, rpb, w_mem_kv,
           a_re, a_im, log_dt, b_re, b_im, c_re, c_im, s5_d, w_glu, w_branch, w_o,
           w_ffn1, w_ffn3, w_ffn2):
    bsz, s, d = x.shape
    n = bsz * s
    x2d = x.reshape(n, d)
    gm = g_mix[0].reshape(1, d).astype(F32)

    wi = w_in[0]
    w_main = jnp.concatenate([wi[:, 0:512] * (64 ** -0.5), wi[:, 512:1536], wi[:, 2048:2560]],
                             axis=1).astype(BF16)
    wu_t = wi[:, 1536:2048].T.astype(BF16)
    m_tab, p_tab, n_tab, dtab = _s5_tables(a_re[0], a_im[0], log_dt[0], b_re[0], b_im[0],
                                           c_re[0], c_im[0], s5_d[0])
    bias, mask = _na_bias_tables(rpb[0])

    q, k, v, qm = _proj(x2d, gm, w_main)

    ut = _proj_s5t(x.reshape(n // CHUNK, CHUNK * d), gm, wu_t)
    zt = _s5_core(ut, m_tab, p_tab, n_tab, dtab)
    y_s5 = _s5_glu(zt, w_glu[0].T.astype(BF16)).reshape(n, 512)

    y_na = _na(q.reshape(bsz, s, 512), k.reshape(bsz, s, 512), v.reshape(bsz, s, 512),
               bias, mask).reshape(n, 512)

    k_mem, v_mem = _memkv(mem, g_mem[0].reshape(1, d).astype(F32), w_mem_kv[0].astype(BF16))
    y_mem = _mem_attn(qm.reshape(bsz, s, 512), k_mem, v_mem).reshape(n, 512)

    x1 = _merge(x2d, gm, y_na, y_s5, y_mem, w_gate[0].astype(BF16),
                b_gate[0].reshape(1, 3 * d).astype(F32), w_branch[0].astype(BF16), w_o[0].astype(BF16))
    out = _ffn(x1, g_ffn[0].reshape(1, d).astype(F32), g_final.reshape(1, d).astype(F32),
               w_ffn1[0].astype(BF16), w_ffn3[0].astype(BF16), w_ffn2[0].astype(BF16))
    return out.reshape(bsz, s, d)
```

```python
import functools

import jax
import jax.numpy as jnp
from jax import lax
from jax.experimental import pallas as pl
from jax.experimental.pallas import tpu as pltpu

F32 = jnp.float32
BF16 = jnp.bfloat16

D = 1024
GRID_W = 64
NA_HEADS = 8
NA_KH = 8
NA_KW = 16
S5_GROUPS = 32
S5_GROUP = 16
S5_STATE = 64
CHUNK = 64
MEM_HEADS = 4
MEM_HEAD_DIM = 128
D_FF = 2816
EPS = 1e-6
NEG_INF = -1e30

VMEM_LIMIT = 56 * 1024 * 1024

NT_DIMS = (((1,), (1,)), ((), ()))


def _cparams(n_axes):
    return pltpu.CompilerParams(
        dimension_semantics=("arbitrary",) * n_axes,
        vmem_limit_bytes=VMEM_LIMIT)


def _rms(x, g):
    return x * lax.rsqrt(jnp.mean(x * x, axis=-1, keepdims=True) + EPS) * g


def _const_spec(shape):
    nd = len(shape)
    return pl.BlockSpec(shape, lambda *_: (0,) * nd, pipeline_mode=pl.Buffered(1))


def _proj_kernel(x_ref, g_ref, w_ref, q_ref, k_ref, v_ref, qm_ref):
    h = _rms(x_ref[...], g_ref[...]).astype(BF16)
    p = jnp.dot(h, w_ref[...], preferred_element_type=F32)
    q_ref[...] = p[:, 0:512].astype(BF16)
    k_ref[...] = p[:, 512:1024].astype(BF16)
    v_ref[...] = p[:, 1024:1536].astype(BF16)
    qm_ref[...] = p[:, 1536:2048].astype(BF16)


def _proj(x2d, g, w):
    n = x2d.shape[0]
    tm = 512
    out = jax.ShapeDtypeStruct((n, 512), BF16)
    ospec = pl.BlockSpec((tm, 512), lambda i: (i, 0))
    return pl.pallas_call(
        _proj_kernel,
        grid=(n // tm,),
        in_specs=[pl.BlockSpec((tm, D), lambda i: (i, 0)),
                  _const_spec((1, D)),
                  _const_spec((D, 2048))],
        out_specs=[ospec] * 4,
        out_shape=[out] * 4,
        compiler_params=_cparams(1),
        name="proj",
    )(x2d, g, w)


S5T_DT = 8


def _proj_s5t_kernel(x_ref, g_ref, wt_ref, o_ref):
    g = g_ref[...]
    wt = wt_ref[...]
    for j in range(S5T_DT):
        h = _rms(x_ref[:, j * D:(j + 1) * D], g).astype(BF16)
        ut = lax.dot_general(wt, h, NT_DIMS, preferred_element_type=F32)
        o_ref[:, j * S5_GROUP:(j + 1) * S5_GROUP, :] = (
            ut.astype(BF16).reshape(S5_GROUPS, S5_GROUP, 256))


def _proj_s5t(xc, g, wt):
    nrows = xc.shape[0]
    return pl.pallas_call(
        _proj_s5t_kernel,
        grid=(CHUNK // S5T_DT,),
        in_specs=[pl.BlockSpec((nrows, S5T_DT * D), lambda i: (0, i)),
                  _const_spec((1, D)),
                  _const_spec((512, D))],
        out_specs=pl.BlockSpec((S5_GROUPS, S5T_DT * S5_GROUP, nrows), lambda i: (0, i, 0)),
        out_shape=jax.ShapeDtypeStruct((S5_GROUPS, CHUNK * S5_GROUP, nrows), BF16),
        compiler_params=_cparams(1),
        name="proj_s5t",
    )(xc, g, wt)


N_SCAN_LEVELS = 6


def _gelu_tanh(x):
    c = 0.7978845608028654
    return 0.5 * x * (1.0 + jnp.tanh(c * (x + 0.044715 * (x * x * x))))


def _s5_kernel(ut_ref, rs_ref, cm_ref, bb_ref, pw_ref, d_ref, z_ref, m_s, n_s, pt_s):
    cma, cmb = cm_ref[0, 0], cm_ref[0, 1]
    bba, bbb = bb_ref[0, 0], bb_ref[0, 1]
    for t in range(CHUNK):
        a, k = divmod(CHUNK - 1 - t, 8)
        rows = slice(S5_GROUP * t, S5_GROUP * (t + 1))
        m_s[rows, :] = rs_ref[0, k, :, 128 * a:128 * a + CHUNK * S5_GROUP]
        n_s[rows, :] = (cma * pw_ref[0, 0, t:t + 1, :] + cmb * pw_ref[0, 1, t:t + 1, :]).astype(BF16)
        pt_s[rows, :] = (bba * pw_ref[0, 2, t:t + 1, :] + bbb * pw_ref[0, 3, t:t + 1, :]).astype(BF16)

    ut = ut_ref[0]
    nrows = ut.shape[1]
    y = jnp.dot(m_s[...], ut, preferred_element_type=F32)
    eye = jnp.where(lax.broadcasted_iota(jnp.int32, (nrows, nrows), 0)
                    == lax.broadcasted_iota(jnp.int32, (nrows, nrows), 1), 1.0, 0.0).astype(BF16)
    u_rows = lax.dot_general(eye, ut, NT_DIMS, preferred_element_type=F32).astype(BF16)
    v = jnp.dot(u_rows, pt_s[...], preferred_element_type=F32)
    pos = lax.broadcasted_iota(jnp.int32, (nrows, 128), 0) & (CHUNK - 1)

    def shift(x, s, up):
        if up:
            return jnp.where(pos < CHUNK - s, pltpu.roll(x, nrows - s, axis=0), 0.0)
        return jnp.where(pos >= s, pltpu.roll(x, s, axis=0), 0.0)

    def scan(vh, lanes, up):
        x = shift(vh, 1, up)
        for lvl in range(N_SCAN_LEVELS):
            xs = shift(x, 1 << lvl, up)
            a = d_ref[0, lvl:lvl + 1, lanes]
            b = d_ref[0, 8 + lvl:9 + lvl, lanes]
            x = x + a * xs + b * pltpu.roll(xs, 64, axis=1)
        return x

    xf = scan(v[:, 0:128], slice(0, 128), False)
    xb = scan(v[:, 128:256], slice(128, 256), True)
    xin = jnp.concatenate([xf, xb], axis=1).astype(BF16)
    y = y + lax.dot_general(n_s[...], xin, NT_DIMS, preferred_element_type=F32)
    z_ref[0] = _gelu_tanh(y).astype(BF16)


RS_LANES = 128 * 7 + CHUNK * S5_GROUP


def _s5_core(ut, rs, cm, bb, pw, dtab):
    nrows = ut.shape[2]
    tc = CHUNK * S5_GROUP
    return pl.pallas_call(
        _s5_kernel,
        grid=(S5_GROUPS,),
        in_specs=[pl.BlockSpec((1, tc, nrows), lambda g: (g, 0, 0)),
                  pl.BlockSpec((1, 8, S5_GROUP, RS_LANES), lambda g: (g, 0, 0, 0)),
                  pl.BlockSpec((1, 2, S5_GROUP, 256), lambda g: (g, 0, 0, 0)),
                  pl.BlockSpec((1, 2, S5_GROUP, 256), lambda g: (g, 0, 0, 0)),
                  pl.BlockSpec((1, 4, CHUNK, 256), lambda g: (g, 0, 0, 0)),
                  pl.BlockSpec((1, 16, 256), lambda g: (g, 0, 0))],
        out_specs=pl.BlockSpec((1, tc, nrows), lambda g: (g, 0, 0)),
        out_shape=jax.ShapeDtypeStruct((S5_GROUPS, tc, nrows), BF16),
        scratch_shapes=[pltpu.VMEM((tc, tc), BF16),
                        pltpu.VMEM((tc, 256), BF16),
                        pltpu.VMEM((tc, 256), BF16)],
        compiler_params=_cparams(1),
        name="s5_core",
    )(ut, rs, cm, bb, pw, dtab)


def _s5_glu_kernel(z_ref, wt_ref, o_ref):
    wt = wt_ref[...]
    nrows = z_ref.shape[2]
    eye = jnp.where(lax.broadcasted_iota(jnp.int32, (nrows, nrows), 0)
                    == lax.broadcasted_iota(jnp.int32, (nrows, nrows), 1), 1.0, 0.0).astype(BF16)
    for j in range(S5T_DT):
        zt = z_ref[:, j * S5_GROUP:(j + 1) * S5_GROUP, :].reshape(512, nrows)
        gl = jnp.dot(wt, zt, preferred_element_type=F32)
        o = (zt.astype(F32) * jax.nn.sigmoid(gl)).astype(BF16)
        nat = lax.dot_general(eye, o, NT_DIMS, preferred_element_type=F32)
        o_ref[:, j * 512:(j + 1) * 512] = nat.astype(BF16)


def _s5_glu(zt, wglu_t):
    nrows = zt.shape[2]
    return pl.pallas_call(
        _s5_glu_kernel,
        grid=(CHUNK // S5T_DT,),
        in_specs=[pl.BlockSpec((S5_GROUPS, S5T_DT * S5_GROUP, nrows), lambda i: (0, i, 0)),
                  _const_spec((512, 512))],
        out_specs=pl.BlockSpec((nrows, S5T_DT * 512), lambda i: (0, i)),
        out_shape=jax.ShapeDtypeStruct((nrows, CHUNK * 512), BF16),
        compiler_params=_cparams(1),
        name="s5_glu",
    )(zt, wglu_t)


NA_ROWS_PER_STEP = 8


def _na_kernel(q_ref, k_ref, v_ref, bias_ref, mask_ref, o_ref):
    rb = pl.program_id(1)
    lo = lax.broadcasted_iota(jnp.int32, (GRID_W, 128), 1) < 64
    valid = mask_ref[...] > 0.0
    nkeys = NA_KH * GRID_W

    def row_body(i, carry):
        r = rb * NA_ROWS_PER_STEP + i
        start = jnp.clip(r - NA_KH // 2, 0, GRID_W - NA_KH)
        case = r - start
        koff = pl.multiple_of(start * GRID_W, GRID_W)
        qoff = pl.multiple_of(i * GRID_W, GRID_W)
        for s in range(4):
            cols = slice(128 * s, 128 * (s + 1))
            q_s = q_ref[0, pl.ds(qoff, GRID_W), cols]
            k_s = k_ref[0, pl.ds(koff, nkeys), cols]
            v_s = v_ref[0, pl.ds(koff, nkeys), cols]
            outs = []
            for hh in range(2):
                qh = jnp.where(lo if hh == 0 else jnp.logical_not(lo), q_s, jnp.zeros_like(q_s))
                sc = lax.dot_general(qh, k_s, NT_DIMS, preferred_element_type=F32)
                bias = jnp.concatenate(
                    [bias_ref[2 * s + hh, 2 * j - case + (NA_KH - 1)] for j in range(NA_KH // 2)], axis=1)
                sc = jnp.where(valid, sc + bias, NEG_INF)
                m = jnp.max(sc, axis=-1, keepdims=True)
                p = jnp.exp(sc - m)
                l = jnp.sum(p, axis=-1, keepdims=True)
                o = jnp.dot(p.astype(BF16), v_s, preferred_element_type=F32)
                outs.append(o / l)
            o_ref[0, pl.ds(qoff, GRID_W), cols] = jnp.where(lo, outs[0], outs[1]).astype(BF16)
        return carry

    lax.fori_loop(0, NA_ROWS_PER_STEP, row_body, 0)


def _na(q, k, v, bias, mask):
    b, s, _ = q.shape
    tq = NA_ROWS_PER_STEP * GRID_W
    return pl.pallas_call(
        _na_kernel,
        grid=(b, s // tq),
        in_specs=[pl.BlockSpec((1, tq, 512), lambda bi, ri: (bi, ri, 0)),
                  pl.BlockSpec((1, s, 512), lambda bi, ri: (bi, 0, 0)),
                  pl.BlockSpec((1, s, 512), lambda bi, ri: (bi, 0, 0)),
                  _const_spec(bias.shape),
                  _const_spec(mask.shape)],
        out_specs=pl.BlockSpec((1, tq, 512), lambda bi, ri: (bi, ri, 0)),
        out_shape=jax.ShapeDtypeStruct((b, s, 512), BF16),
        compiler_params=_cparams(2),
        name="na",
    )(q, k, v, bias, mask)


def _memkv_kernel(mem_ref, g_ref, w_ref, k_ref, v_ref):
    mn = _rms(mem_ref[0], g_ref[...]).astype(BF16)
    kv = jnp.dot(mn, w_ref[...], preferred_element_type=F32)
    k_ref[0] = kv[:, 0:512].astype(BF16)
    v_ref[0] = kv[:, 512:1024].astype(BF16)


def _memkv(mem, g, w):
    b, m, _ = mem.shape
    out = jax.ShapeDtypeStruct((b, m, 512), BF16)
    ospec = pl.BlockSpec((1, m, 512), lambda i: (i, 0, 0))
    return pl.pallas_call(
        _memkv_kernel,
        grid=(b,),
        in_specs=[pl.BlockSpec((1, m, D), lambda i: (i, 0, 0)),
                  _const_spec((1, D)),
                  _const_spec((D, 1024))],
        out_specs=[ospec, ospec],
        out_shape=[out, out],
        compiler_params=_cparams(1),
        name="memkv",
    )(mem, g, w)


def _mem_kernel(q_ref, k_ref, v_ref, o_ref):
    scale = MEM_HEAD_DIM ** -0.5
    for h in range(MEM_HEADS):
        cols = slice(MEM_HEAD_DIM * h, MEM_HEAD_DIM * (h + 1))
        sc = lax.dot_general(q_ref[0, :, cols], k_ref[0, :, cols], NT_DIMS,
                             preferred_element_type=F32) * scale
        m = jnp.max(sc, axis=-1, keepdims=True)
        p = jnp.exp(sc - m)
        l = jnp.sum(p, axis=-1, keepdims=True)
        o = jnp.dot(p.astype(BF16), v_ref[0, :, cols], preferred_element_type=F32)
        o_ref[0, :, cols] = (o / l).astype(BF16)


def _mem_attn(q, k, v):
    b, s, _ = q.shape
    m = k.shape[1]
    tq = 1024
    return pl.pallas_call(
        _mem_kernel,
        grid=(b, s // tq),
        in_specs=[pl.BlockSpec((1, tq, 512), lambda bi, i: (bi, i, 0)),
                  pl.BlockSpec((1, m, 512), lambda bi, i: (bi, 0, 0)),
                  pl.BlockSpec((1, m, 512), lambda bi, i: (bi, 0, 0))],
        out_specs=pl.BlockSpec((1, tq, 512), lambda bi, i: (bi, i, 0)),
        out_shape=jax.ShapeDtypeStruct((b, s, 512), BF16),
        compiler_params=_cparams(2),
        name="mem_attn",
    )(q, k, v)


def _merge_kernel(x_ref, g_ref, yna_ref, ys5_ref, ymem_ref, wg_ref, bg_ref, wb_ref, wo_ref, o_ref):
    x = x_ref[...]
    h = _rms(x, g_ref[...]).astype(BF16)
    merged = None
    for b, y_ref in enumerate((yna_ref, ys5_ref, ymem_ref)):
        cols = slice(D * b, D * (b + 1))
        gate = jax.nn.sigmoid(jnp.dot(h, wg_ref[:, cols], preferred_element_type=F32) + bg_ref[:, cols])
        up = jnp.dot(y_ref[...], wb_ref[b], preferred_element_type=F32)
        merged = gate * up if merged is None else merged + gate * up
    o_ref[...] = x + jnp.dot(merged.astype(BF16), wo_ref[...], preferred_element_type=F32)


def _merge(x2d, g, yna, ys5, ymem, wg, bg, wb, wo):
    n = x2d.shape[0]
    tm = 512
    yspec = pl.BlockSpec((tm, 512), lambda i: (i, 0))
    return pl.pallas_call(
        _merge_kernel,
        grid=(n // tm,),
        in_specs=[pl.BlockSpec((tm, D), lambda i: (i, 0)),
                  _const_spec((1, D)),
                  yspec, yspec, yspec,
                  _const_spec((D, 3 * D)),
                  _const_spec((1, 3 * D)),
                  _const_spec((3, 512, D)),
                  _const_spec((D, D))],
        out_specs=pl.BlockSpec((tm, D), lambda i: (i, 0)),
        out_shape=jax.ShapeDtypeStruct((n, D), F32),
        compiler_params=_cparams(1),
        name="merge",
    )(x2d, g, yna, ys5, ymem, wg, bg, wb, wo)


def _ffn_kernel(x_ref, g_ref, gf_ref, w1_ref, w3_ref, w2_ref, o_ref):
    x = x_ref[...]
    h = _rms(x, g_ref[...]).astype(BF16)
    a = jnp.dot(h, w1_ref[...], preferred_element_type=F32)
    c = jnp.dot(h, w3_ref[...], preferred_element_type=F32)
    mid = (a * jax.nn.sigmoid(a) * c).astype(BF16)
    x2 = x + jnp.dot(mid, w2_ref[...], preferred_element_type=F32)
    o_ref[...] = _rms(x2, gf_ref[...])


def _ffn(x2d, g, gf, w1, w3, w2):
    n = x2d.shape[0]
    tm = 512
    return pl.pallas_call(
        _ffn_kernel,
        grid=(n // tm,),
        in_specs=[pl.BlockSpec((tm, D), lambda i: (i, 0)),
                  _const_spec((1, D)),
                  _const_spec((1, D)),
                  _const_spec((D, D_FF)),
                  _const_spec((D, D_FF)),
                  _const_spec((D_FF, D))],
        out_specs=pl.BlockSpec((tm, D), lambda i: (i, 0)),
        out_shape=jax.ShapeDtypeStruct((n, D), F32),
        compiler_params=_cparams(1),
        name="ffn",
    )(x2d, g, gf, w1, w3, w2)


def _s5_tables(a_re, a_im, log_dt, b_re, b_im, c_re, c_im, s5_d):
    hi = lax.Precision.HIGHEST
    t = CHUNK
    ar, ai = a_re.astype(F32), a_im.astype(F32)
    dt = jnp.exp(log_dt.astype(F32))[..., None]
    lr, li = ar * dt, ai * dt
    mag = jnp.exp(lr)
    lbr, lbi = mag * jnp.cos(li), mag * jnp.sin(li)
    den = ar * ar + ai * ai
    rr = ((lbr - 1.0) * ar + lbi * ai) / den
    ri = (lbi * ar - (lbr - 1.0) * ai) / den
    br, bi = b_re.astype(F32), b_im.astype(F32)
    bbr = rr[..., None] * br - ri[..., None] * bi
    bbi = rr[..., None] * bi + ri[..., None] * br
    cmr, cmi = c_re.astype(F32), c_im.astype(F32)
    tau = jnp.arange(t + 1, dtype=F32)[:, None, None, None]
    pmag = jnp.exp(tau * lr[None])
    pwr, pwi = pmag * jnp.cos(tau * li[None]), pmag * jnp.sin(tau * li[None])

    pr = jnp.moveaxis(pwr[:t], 1, 0)[:, :, :, None, :]
    pi = jnp.moveaxis(pwi[:t], 1, 0)[:, :, :, None, :]
    w1 = jnp.concatenate([cmr[:, None] * pr - cmi[:, None] * pi,
                          -(cmr[:, None] * pi + cmi[:, None] * pr)], axis=-1)
    kk = jnp.einsum('dtgcq,dgqi->dtgci', w1, jnp.concatenate([bbr, bbi], axis=2), precision=hi)
    dmat = jnp.eye(S5_GROUP, dtype=F32)[None] * s5_d.astype(F32).reshape(S5_GROUPS, S5_GROUP)[:, :, None]
    k0 = kk[0, 0] + kk[1, 0] + dmat
    krev = jnp.concatenate([kk[0, 1:][::-1], k0[None], kk[1, 1:]], axis=0)
    r = jnp.transpose(krev, (1, 2, 0, 3)).reshape(S5_GROUPS, S5_GROUP, (2 * t - 1) * S5_GROUP)
    rs = jnp.stack([r[:, :, S5_GROUP * k:S5_GROUP * k + RS_LANES] for k in range(8)], axis=1)

    def lanes4(f0, f1, b0, b1):
        return jnp.concatenate([f0, f1, b0, b1], axis=-1)

    cm = jnp.stack([lanes4(cmr[0], -cmi[0], cmr[1], -cmi[1]),
                    lanes4(-cmi[0], -cmr[0], -cmi[1], -cmr[1])], axis=1)
    fr, fi = jnp.moveaxis(pwr[1:t + 1, 0], 0, 1), jnp.moveaxis(pwi[1:t + 1, 0], 0, 1)
    gr, gi = jnp.moveaxis(pwr[1:t + 1][::-1, 1], 0, 1), jnp.moveaxis(pwi[1:t + 1][::-1, 1], 0, 1)
    bt = lambda z: jnp.swapaxes(z, -1, -2)
    bb = jnp.stack([lanes4(bt(bbr[0]), bt(bbi[0]), bt(bbr[1]), bt(bbi[1])),
                    lanes4(-bt(bbi[0]), bt(bbr[0]), -bt(bbi[1]), bt(bbr[1]))], axis=1)
    wr, wi = jnp.moveaxis(pwr[:t][::-1, 0], 0, 1), jnp.moveaxis(pwi[:t][::-1, 0], 0, 1)
    vr, vi = jnp.moveaxis(pwr[:t, 1], 0, 1), jnp.moveaxis(pwi[:t, 1], 0, 1)
    pw = jnp.stack([lanes4(fr, fr, gr, gr), lanes4(fi, fi, gi, gi),
                    lanes4(wr, wr, vr, vr), lanes4(wi, wi, vi, vi)], axis=1)

    dr, di = pwr[t], pwi[t]
    a_rows, b_rows = [], []
    for _ in range(N_SCAN_LEVELS):
        a_rows.append(lanes4(dr[0], dr[0], dr[1], dr[1]))
        b_rows.append(lanes4(-di[0], di[0], -di[1], di[1]))
        dr, di = dr * dr - di * di, 2.0 * dr * di
    pad = [jnp.zeros_like(a_rows[0])] * (8 - N_SCAN_LEVELS)
    dtab = jnp.stack(a_rows + pad + b_rows + pad, axis=1)
    return rs.astype(BF16), cm, bb, pw, dtab


def _na_bias_tables(rpb):
    c = jnp.arange(GRID_W)
    col_start = jnp.clip(c - NA_KW // 2, 0, GRID_W - NA_KW)
    in_win = (c[None, :] >= col_start[:, None]) & (c[None, :] < col_start[:, None] + NA_KW)
    rel_col = jnp.clip(c[None, :] - c[:, None] + (NA_KW - 1), 0, 2 * NA_KW - 2)
    bc = rpb.astype(F32)[:, :, rel_col]
    bias = jnp.concatenate([bc[:, :-1], bc[:, 1:]], axis=-1)
    mask = jnp.broadcast_to(in_win[:, None, :], (GRID_W, NA_KH, GRID_W)).reshape(GRID_W, NA_KH * GRID_W)
    return bias, mask.astype(F32)


def kernel(x, mem, g_mix, g_mem, g_ffn, g_final, w_in, w_gate, b_gate, rpb, w_mem_kv,
           a_re, a_im, log_dt, b_re, b_im, c_re, c_im, s5_d, w_glu, w_branch, w_o,
           w_ffn1, w_ffn3, w_ffn2):
    bsz, s, d = x.shape
    n = bsz * s
    x2d = x.reshape(n, d)
    gm = g_mix[0].reshape(1, d).astype(F32)

    wi = w_in[0]
    w_main = jnp.concatenate([wi[:, 0:512] * (64 ** -0.5), wi[:, 512:1536], wi[:, 2048:2560]],
                             axis=1).astype(BF16)
    wu_t = wi[:, 1536:2048].T.astype(BF16)
    rs_tab, cm_tab, bb_tab, pw_tab, dtab = _s5_tables(a_re[0], a_im[0], log_dt[0], b_re[0], b_im[0],
                                                      c_re[0], c_im[0], s5_d[0])
    bias, mask = _na_bias_tables(rpb[0])

    q, k, v, qm = _proj(x2d, gm, w_main)

    ut = _proj_s5t(x.reshape(n // CHUNK, CHUNK * d), gm, wu_t)
    zt = _s5_core(ut, rs_tab, cm_tab, bb_tab, pw_tab, dtab)
    y_s5 = _s5_glu(zt, w_glu[0].T.astype(BF16)).reshape(n, 512)

    y_na = _na(q.reshape(bsz, s, 512), k.reshape(bsz, s, 512), v.reshape(bsz, s, 512),
               bias, mask).reshape(n, 512)

    k_mem, v_mem = _memkv(mem, g_mem[0].reshape(1, d).astype(F32), w_mem_kv[0].astype(BF16))
    y_mem = _mem_attn(qm.reshape(bsz, s, 512), k_mem, v_mem).reshape(n, 512)

    x1 = _merge(x2d, gm, y_na, y_s5, y_mem, w_gate[0].astype(BF16),
                b_gate[0].reshape(1, 3 * d).astype(F32), w_branch[0].astype(BF16), w_o[0].astype(BF16))
    out = _ffn(x1, g_ffn[0].reshape(1, d).astype(F32), g_final.reshape(1, d).astype(F32),
               w_ffn1[0].astype(BF16), w_ffn3[0].astype(BF16), w_ffn2[0].astype(BF16))
    return out.reshape(bsz, s, d)
```

```python
import functools

import jax
import jax.numpy as jnp
from jax import lax
from jax.experimental import pallas as pl
from jax.experimental.pallas import tpu as pltpu

F32 = jnp.float32
BF16 = jnp.bfloat16

D = 1024
GRID_W = 64
NA_HEADS = 8
NA_KH = 8
NA_KW = 16
S5_GROUPS = 32
S5_GROUP = 16
S5_STATE = 64
CHUNK = 64
MEM_HEADS = 4
MEM_HEAD_DIM = 128
D_FF = 2816
EPS = 1e-6
NEG_INF = -1e30

VMEM_LIMIT = 56 * 1024 * 1024

NT_DIMS = (((1,), (1,)), ((), ()))


def _cparams(n_axes):
    return pltpu.CompilerParams(
        dimension_semantics=("arbitrary",) * n_axes,
        vmem_limit_bytes=VMEM_LIMIT)


def _rms(x, g):
    return x * lax.rsqrt(jnp.mean(x * x, axis=-1, keepdims=True) + EPS) * g


def _const_spec(shape):
    nd = len(shape)
    return pl.BlockSpec(shape, lambda *_: (0,) * nd, pipeline_mode=pl.Buffered(1))


TOK_BLK = 128


def _proj_kernel(x_ref, g_ref, w_ref, wt_ref, k_ref, qm_ref, qt_ref, vt_ref):
    h = _rms(x_ref[...], g_ref[...]).astype(BF16)
    p = jnp.dot(h, w_ref[...], preferred_element_type=F32)
    k_ref[...] = p[:, 0:512].astype(BF16)
    qm_ref[...] = p[:, 512:1024].astype(BF16)
    pt = lax.dot_general(wt_ref[...], h, NT_DIMS, preferred_element_type=F32)
    for i in range(x_ref.shape[0] // TOK_BLK):
        toks = slice(TOK_BLK * i, TOK_BLK * (i + 1))
        qt_ref[i] = pt[0:512, toks].astype(BF16)
        vt_ref[i] = pt[512:1024, toks].astype(BF16)


def _proj(x2d, g, w, wt):
    n = x2d.shape[0]
    tm = 512
    nat = jax.ShapeDtypeStruct((n, 512), BF16)
    chm = jax.ShapeDtypeStruct((n // TOK_BLK, 512, TOK_BLK), BF16)
    nat_spec = pl.BlockSpec((tm, 512), lambda i: (i, 0))
    chm_spec = pl.BlockSpec((tm // TOK_BLK, 512, TOK_BLK), lambda i: (i, 0, 0))
    return pl.pallas_call(
        _proj_kernel,
        grid=(n // tm,),
        in_specs=[pl.BlockSpec((tm, D), lambda i: (i, 0)),
                  _const_spec((1, D)),
                  _const_spec((D, 1024)),
                  _const_spec((1024, D))],
        out_specs=[nat_spec, nat_spec, chm_spec, chm_spec],
        out_shape=[nat, nat, chm, chm],
        compiler_params=_cparams(1),
        name="proj",
    )(x2d, g, w, wt)


S5T_DT = 8


def _proj_s5t_kernel(x_ref, g_ref, wt_ref, o_ref):
    g = g_ref[...]
    wt = wt_ref[...]
    for j in range(S5T_DT):
        h = _rms(x_ref[:, j, :], g).astype(BF16)
        ut = lax.dot_general(wt, h, NT_DIMS, preferred_element_type=F32)
        o_ref[:, j * S5_GROUP:(j + 1) * S5_GROUP, :] = (
            ut.astype(BF16).reshape(S5_GROUPS, S5_GROUP, 256))


def _proj_s5t(xc, g, wt):
    nrows = xc.shape[0]
    return pl.pallas_call(
        _proj_s5t_kernel,
        grid=(CHUNK // S5T_DT,),
        in_specs=[pl.BlockSpec((nrows, S5T_DT, D), lambda i: (0, i, 0)),
                  _const_spec((1, D)),
                  _const_spec((512, D))],
        out_specs=pl.BlockSpec((S5_GROUPS, S5T_DT * S5_GROUP, nrows), lambda i: (0, i, 0)),
        out_shape=jax.ShapeDtypeStruct((S5_GROUPS, CHUNK * S5_GROUP, nrows), BF16),
        compiler_params=_cparams(1),
        name="proj_s5t",
    )(xc, g, wt)


N_SCAN_LEVELS = 6


def _gelu_tanh(x):
    c = 0.7978845608028654
    return 0.5 * x * (1.0 + jnp.tanh(c * (x + 0.044715 * (x * x * x))))


def _s5_kernel(ut_ref, rs_ref, cm_ref, bb_ref, pw_ref, d_ref, z_ref, m_s, n_s, pt_s):
    cma, cmb = cm_ref[0, 0], cm_ref[0, 1]
    bba, bbb = bb_ref[0, 0], bb_ref[0, 1]
    for t in range(CHUNK):
        a, k = divmod(CHUNK - 1 - t, 8)
        rows = slice(S5_GROUP * t, S5_GROUP * (t + 1))
        m_s[rows, :] = rs_ref[0, k, :, 128 * a:128 * a + CHUNK * S5_GROUP]
        n_s[rows, :] = (cma * pw_ref[0, 0, t:t + 1, :] + cmb * pw_ref[0, 1, t:t + 1, :]).astype(BF16)
        pt_s[rows, :] = (bba * pw_ref[0, 2, t:t + 1, :] + bbb * pw_ref[0, 3, t:t + 1, :]).astype(BF16)

    ut = ut_ref[0]
    nrows = ut.shape[1]
    y = jnp.dot(m_s[...], ut, preferred_element_type=F32)
    eye = jnp.where(lax.broadcasted_iota(jnp.int32, (nrows, nrows), 0)
                    == lax.broadcasted_iota(jnp.int32, (nrows, nrows), 1), 1.0, 0.0).astype(BF16)
    u_rows = lax.dot_general(eye, ut, NT_DIMS, preferred_element_type=F32).astype(BF16)
    v = jnp.dot(u_rows, pt_s[...], preferred_element_type=F32)
    pos = lax.broadcasted_iota(jnp.int32, (nrows, 128), 0) & (CHUNK - 1)

    def shift(x, s, up):
        if up:
            return jnp.where(pos < CHUNK - s, pltpu.roll(x, nrows - s, axis=0), 0.0)
        return jnp.where(pos >= s, pltpu.roll(x, s, axis=0), 0.0)

    def scan(vh, lanes, up):
        x = shift(vh, 1, up)
        for lvl in range(N_SCAN_LEVELS):
            xs = shift(x, 1 << lvl, up)
            a = d_ref[0, lvl:lvl + 1, lanes]
            b = d_ref[0, 8 + lvl:9 + lvl, lanes]
            x = x + a * xs + b * pltpu.roll(xs, 64, axis=1)
        return x

    xf = scan(v[:, 0:128], slice(0, 128), False)
    xb = scan(v[:, 128:256], slice(128, 256), True)
    xin = jnp.concatenate([xf, xb], axis=1).astype(BF16)
    y = y + lax.dot_general(n_s[...], xin, NT_DIMS, preferred_element_type=F32)
    z_ref[0] = _gelu_tanh(y).astype(BF16)


RS_LANES = 128 * 7 + CHUNK * S5_GROUP


def _s5_core(ut, rs, cm, bb, pw, dtab):
    nrows = ut.shape[2]
    tc = CHUNK * S5_GROUP
    return pl.pallas_call(
        _s5_kernel,
        grid=(S5_GROUPS,),
        in_specs=[pl.BlockSpec((1, tc, nrows), lambda g: (g, 0, 0)),
                  pl.BlockSpec((1, 8, S5_GROUP, RS_LANES), lambda g: (g, 0, 0, 0)),
                  pl.BlockSpec((1, 2, S5_GROUP, 256), lambda g: (g, 0, 0, 0)),
                  pl.BlockSpec((1, 2, S5_GROUP, 256), lambda g: (g, 0, 0, 0)),
                  pl.BlockSpec((1, 4, CHUNK, 256), lambda g: (g, 0, 0, 0)),
                  pl.BlockSpec((1, 16, 256), lambda g: (g, 0, 0))],
        out_specs=pl.BlockSpec((1, tc, nrows), lambda g: (g, 0, 0)),
        out_shape=jax.ShapeDtypeStruct((S5_GROUPS, tc, nrows), BF16),
        scratch_shapes=[pltpu.VMEM((tc, tc), BF16),
                        pltpu.VMEM((tc, 256), BF16),
                        pltpu.VMEM((tc, 256), BF16)],
        compiler_params=_cparams(1),
        name="s5_core",
    )(ut, rs, cm, bb, pw, dtab)


def _s5_glu_kernel(z_ref, wt_ref, o_ref):
    wt = wt_ref[...]
    nrows = z_ref.shape[2]
    eye = jnp.where(lax.broadcasted_iota(jnp.int32, (nrows, nrows), 0)
                    == lax.broadcasted_iota(jnp.int32, (nrows, nrows), 1), 1.0, 0.0).astype(BF16)
    for j in range(S5T_DT):
        zt = z_ref[:, j * S5_GROUP:(j + 1) * S5_GROUP, :].reshape(512, nrows)
        gl = jnp.dot(wt, zt, preferred_element_type=F32)
        o = (zt.astype(F32) * jax.nn.sigmoid(gl)).astype(BF16)
        nat = lax.dot_general(eye, o, NT_DIMS, preferred_element_type=F32)
        o_ref[:, j, :] = nat.astype(BF16)


def _s5_glu(zt, wglu_t):
    nrows = zt.shape[2]
    return pl.pallas_call(
        _s5_glu_kernel,
        grid=(CHUNK // S5T_DT,),
        in_specs=[pl.BlockSpec((S5_GROUPS, S5T_DT * S5_GROUP, nrows), lambda i: (0, i, 0)),
                  _const_spec((512, 512))],
        out_specs=pl.BlockSpec((nrows, S5T_DT, 512), lambda i: (0, i, 0)),
        out_shape=jax.ShapeDtypeStruct((nrows, CHUNK, 512), BF16),
        compiler_params=_cparams(1),
        name="s5_glu",
    )(zt, wglu_t)


NA_ROWS_PER_STEP = 8


NA_PAIRS_PER_STEP = NA_ROWS_PER_STEP // 2
NA_WIN_ROWS = 10
NA_WIN_KEYS = NA_WIN_ROWS * GRID_W
NA_WIN_BLKS = NA_WIN_KEYS // TOK_BLK


NA_RR_INVALID = 2 * NA_KH - 1


def _na_kernel(qt_ref, k_ref, vt_ref, bias_ref, cmask_ref, o_ref, *scratch):
    st_s, pt_s = scratch[0:4], scratch[4:8]
    rb = pl.program_id(1)
    lane = lax.broadcasted_iota(jnp.int32, (128, 128), 1)
    row = lax.broadcasted_iota(jnp.int32, (128, 128), 0)
    lo = lane < 64
    lo_q = lax.broadcasted_iota(jnp.int32, (GRID_W, 128), 1) < 64
    same_head = (row < 64) == lo
    eye = jnp.where(lax.broadcasted_iota(jnp.int32, (256, 256), 0)
                    == lax.broadcasted_iota(jnp.int32, (256, 256), 1), 1.0, 0.0).astype(BF16)
    half = NA_KH // 2

    def pair_body(pi, carry):
        r0 = rb * NA_ROWS_PER_STEP + 2 * pi
        win = jnp.clip(r0 - half, 0, GRID_W - NA_WIN_ROWS)
        blk0 = lax.shift_right_logical(win, 1)
        koff = pl.multiple_of(win * GRID_W, TOK_BLK)
        rr = []
        for p in range(2):
            r = r0 + p
            first = jnp.clip(r - half, 0, GRID_W - NA_KH) - win
            rel = win - r + (NA_KH - 1)
            rr.append([jnp.where((wr >= first) & (wr < first + NA_KH), wr + rel, NA_RR_INVALID)
                       for wr in range(NA_WIN_ROWS)])
        for s in range(4):
            ch = slice(128 * s, 128 * (s + 1))
            a = qt_ref[pi, ch, :].astype(F32)
            ar = pltpu.roll(a, 64, axis=1)
            w = jnp.concatenate([jnp.where(same_head, jnp.where(lo, a, ar), 0.0),
                                 jnp.where(same_head, jnp.where(lo, ar, a), 0.0)], axis=1).astype(BF16)
            kw = k_ref[0, pl.ds(koff, NA_WIN_KEYS), ch]
            sc = jnp.dot(kw, w, preferred_element_type=F32)
            sums = []
            for p in range(2):
                lanes = slice(128 * p, 128 * (p + 1))
                m = None
                for wr in range(NA_WIN_ROWS):
                    rows = slice(GRID_W * wr, GRID_W * (wr + 1))
                    t = jnp.where(cmask_ref[rr[p][wr]] > 0.0, sc[rows, lanes] + bias_ref[s, rr[p][wr]], NEG_INF)
                    st_s[s][rows, lanes] = t
                    m = t if m is None else jnp.maximum(m, t)
                m = jnp.max(m, axis=0, keepdims=True)
                l = None
                for wr in range(NA_WIN_ROWS):
                    rows = slice(GRID_W * wr, GRID_W * (wr + 1))
                    e = jnp.exp(st_s[s][rows, lanes] - m)
                    pt_s[s][rows, lanes] = e.astype(BF16)
                    l = e if l is None else l + e
                sums.append(jnp.sum(l, axis=0, keepdims=True))
            vw = jnp.concatenate([vt_ref[blk0 + i, ch, :] for i in range(NA_WIN_BLKS)], axis=1)
            ot = jnp.dot(vw, pt_s[s][...], preferred_element_type=F32)
            ot = (ot / jnp.concatenate(sums, axis=1)).astype(BF16)
            tt = lax.dot_general(eye, ot, NT_DIMS, preferred_element_type=F32)
            for p in range(2):
                nat = jnp.where(lo_q, tt[128 * p:128 * p + 64], tt[128 * p + 64:128 * p + 128])
                qoff = pl.multiple_of((2 * pi + p) * GRID_W, GRID_W)
                o_ref[0, pl.ds(qoff, GRID_W), ch] = nat.astype(BF16)
        return carry

    lax.fori_loop(0, NA_PAIRS_PER_STEP, pair_body, 0)


def _na(qt, k, vt, bias, cmask):
    b, s, _ = k.shape
    tq = NA_ROWS_PER_STEP * GRID_W
    blks = s // TOK_BLK
    return pl.pallas_call(
        _na_kernel,
        grid=(b, s // tq),
        in_specs=[pl.BlockSpec((NA_PAIRS_PER_STEP, 512, TOK_BLK), lambda bi, ri: (bi * (blks // NA_PAIRS_PER_STEP) + ri, 0, 0)),
                  pl.BlockSpec((1, s, 512), lambda bi, ri: (bi, 0, 0)),
                  pl.BlockSpec((blks, 512, TOK_BLK), lambda bi, ri: (bi, 0, 0)),
                  _const_spec(bias.shape),
                  _const_spec(cmask.shape)],
        out_specs=pl.BlockSpec((1, tq, 512), lambda bi, ri: (bi, ri, 0)),
        out_shape=jax.ShapeDtypeStruct((b, s, 512), BF16),
        scratch_shapes=([pltpu.VMEM((NA_WIN_KEYS, 256), F32)] * 4
                        + [pltpu.VMEM((NA_WIN_KEYS, 256), BF16)] * 4),
        compiler_params=_cparams(2),
        name="na",
    )(qt, k, vt, bias, cmask)


def _memkv_kernel(mem_ref, g_ref, w_ref, k_ref, v_ref):
    mn = _rms(mem_ref[0], g_ref[...]).astype(BF16)
    kv = jnp.dot(mn, w_ref[...], preferred_element_type=F32)
    k_ref[0] = kv[:, 0:512].astype(BF16)
    v_ref[0] = kv[:, 512:1024].astype(BF16)


def _memkv(mem, g, w):
    b, m, _ = mem.shape
    out = jax.ShapeDtypeStruct((b, m, 512), BF16)
    ospec = pl.BlockSpec((1, m, 512), lambda i: (i, 0, 0))
    return pl.pallas_call(
        _memkv_kernel,
        grid=(b,),
        in_specs=[pl.BlockSpec((1, m, D), lambda i: (i, 0, 0)),
                  _const_spec((1, D)),
                  _const_spec((D, 1024))],
        out_specs=[ospec, ospec],
        out_shape=[out, out],
        compiler_params=_cparams(1),
        name="memkv",
    )(mem, g, w)


def _mem_kernel(q_ref, k_ref, v_ref, o_ref):
    scale = MEM_HEAD_DIM ** -0.5
    for h in range(MEM_HEADS):
        cols = slice(MEM_HEAD_DIM * h, MEM_HEAD_DIM * (h + 1))
        sc = lax.dot_general(q_ref[0, :, cols], k_ref[0, :, cols], NT_DIMS,
                             preferred_element_type=F32) * scale
        m = jnp.max(sc, axis=-1, keepdims=True)
        p = jnp.exp(sc - m)
        l = jnp.sum(p, axis=-1, keepdims=True)
        o = jnp.dot(p.astype(BF16), v_ref[0, :, cols], preferred_element_type=F32)
        o_ref[0, :, cols] = (o / l).astype(BF16)


def _mem_attn(q, k, v):
    b, s, _ = q.shape
    m = k.shape[1]
    tq = 1024
    return pl.pallas_call(
        _mem_kernel,
        grid=(b, s // tq),
        in_specs=[pl.BlockSpec((1, tq, 512), lambda bi, i: (bi, i, 0)),
                  pl.BlockSpec((1, m, 512), lambda bi, i: (bi, 0, 0)),
                  pl.BlockSpec((1, m, 512), lambda bi, i: (bi, 0, 0))],
        out_specs=pl.BlockSpec((1, tq, 512), lambda bi, i: (bi, i, 0)),
        out_shape=jax.ShapeDtypeStruct((b, s, 512), BF16),
        compiler_params=_cparams(2),
        name="mem_attn",
    )(q, k, v)


def _merge_kernel(x_ref, g_ref, yna_ref, ys5_ref, ymem_ref, wg_ref, bg_ref, wb_ref, wo_ref, o_ref):
    x = x_ref[...]
    h = _rms(x, g_ref[...]).astype(BF16)
    merged = None
    for b, y_ref in enumerate((yna_ref, ys5_ref, ymem_ref)):
        cols = slice(D * b, D * (b + 1))
        gate = jax.nn.sigmoid(jnp.dot(h, wg_ref[:, cols], preferred_element_type=F32) + bg_ref[:, cols])
        up = jnp.dot(y_ref[...], wb_ref[b], preferred_element_type=F32)
        merged = gate * up if merged is None else merged + gate * up
    o_ref[...] = x + jnp.dot(merged.astype(BF16), wo_ref[...], preferred_element_type=F32)


def _merge(x2d, g, yna, ys5, ymem, wg, bg, wb, wo):
    n = x2d.shape[0]
    tm = 512
    yspec = pl.BlockSpec((tm, 512), lambda i: (i, 0))
    return pl.pallas_call(
        _merge_kernel,
        grid=(n // tm,),
        in_specs=[pl.BlockSpec((tm, D), lambda i: (i, 0)),
                  _const_spec((1, D)),
                  yspec, yspec, yspec,
                  _const_spec((D, 3 * D)),
                  _const_spec((1, 3 * D)),
                  _const_spec((3, 512, D)),
                  _const_spec((D, D))],
        out_specs=pl.BlockSpec((tm, D), lambda i: (i, 0)),
        out_shape=jax.ShapeDtypeStruct((n, D), F32),
        compiler_params=_cparams(1),
        name="merge",
    )(x2d, g, yna, ys5, ymem, wg, bg, wb, wo)


def _ffn_kernel(x_ref, g_ref, gf_ref, w1_ref, w3_ref, w2_ref, o_ref):
    x = x_ref[...]
    h = _rms(x, g_ref[...]).astype(BF16)
    a = jnp.dot(h, w1_ref[...], preferred_element_type=F32)
    c = jnp.dot(h, w3_ref[...], preferred_element_type=F32)
    mid = (a * jax.nn.sigmoid(a) * c).astype(BF16)
    x2 = x + jnp.dot(mid, w2_ref[...], preferred_element_type=F32)
    o_ref[...] = _rms(x2, gf_ref[...])


def _ffn(x2d, g, gf, w1, w3, w2):
    n = x2d.shape[0]
    tm = 512
    return pl.pallas_call(
        _ffn_kernel,
        grid=(n // tm,),
        in_specs=[pl.BlockSpec((tm, D), lambda i: (i, 0)),
                  _const_spec((1, D)),
                  _const_spec((1, D)),
                  _const_spec((D, D_FF)),
                  _const_spec((D, D_FF)),
                  _const_spec((D_FF, D))],
        out_specs=pl.BlockSpec((tm, D), lambda i: (i, 0)),
        out_shape=jax.ShapeDtypeStruct((n, D), F32),
        compiler_params=_cparams(1),
        name="ffn",
    )(x2d, g, gf, w1, w3, w2)


def _s5_tables(a_re, a_im, log_dt, b_re, b_im, c_re, c_im, s5_d):
    hi = lax.Precision.HIGHEST
    t = CHUNK
    ar, ai = a_re.astype(F32), a_im.astype(F32)
    dt = jnp.exp(log_dt.astype(F32))[..., None]
    lr, li = ar * dt, ai * dt
    mag = jnp.exp(lr)
    lbr, lbi = mag * jnp.cos(li), mag * jnp.sin(li)
    den = ar * ar + ai * ai
    rr = ((lbr - 1.0) * ar + lbi * ai) / den
    ri = (lbi * ar - (lbr - 1.0) * ai) / den
    br, bi = b_re.astype(F32), b_im.astype(F32)
    bbr = rr[..., None] * br - ri[..., None] * bi
    bbi = rr[..., None] * bi + ri[..., None] * br
    cmr, cmi = c_re.astype(F32), c_im.astype(F32)
    tau = jnp.arange(t + 1, dtype=F32)[:, None, None, None]
    pmag = jnp.exp(tau * lr[None])
    pwr, pwi = pmag * jnp.cos(tau * li[None]), pmag * jnp.sin(tau * li[None])

    pr = jnp.moveaxis(pwr[:t], 1, 0)[:, :, :, None, :]
    pi = jnp.moveaxis(pwi[:t], 1, 0)[:, :, :, None, :]
    w1 = jnp.concatenate([cmr[:, None] * pr - cmi[:, None] * pi,
                          -(cmr[:, None] * pi + cmi[:, None] * pr)], axis=-1)
    kk = jnp.einsum('dtgcq,dgqi->dtgci', w1, jnp.concatenate([bbr, bbi], axis=2), precision=hi)
    dmat = jnp.eye(S5_GROUP, dtype=F32)[None] * s5_d.astype(F32).reshape(S5_GROUPS, S5_GROUP)[:, :, None]
    k0 = kk[0, 0] + kk[1, 0] + dmat
    krev = jnp.concatenate([kk[0, 1:][::-1], k0[None], kk[1, 1:]], axis=0)
    r = jnp.transpose(krev, (1, 2, 0, 3)).reshape(S5_GROUPS, S5_GROUP, (2 * t - 1) * S5_GROUP)
    rs = jnp.stack([r[:, :, S5_GROUP * k:S5_GROUP * k + RS_LANES] for k in range(8)], axis=1)

    def lanes4(f0, f1, b0, b1):
        return jnp.concatenate([f0, f1, b0, b1], axis=-1)

    cm = jnp.stack([lanes4(cmr[0], -cmi[0], cmr[1], -cmi[1]),
                    lanes4(-cmi[0], -cmr[0], -cmi[1], -cmr[1])], axis=1)
    fr, fi = jnp.moveaxis(pwr[1:t + 1, 0], 0, 1), jnp.moveaxis(pwi[1:t + 1, 0], 0, 1)
    gr, gi = jnp.moveaxis(pwr[1:t + 1][::-1, 1], 0, 1), jnp.moveaxis(pwi[1:t + 1][::-1, 1], 0, 1)
    bt = lambda z: jnp.swapaxes(z, -1, -2)
    bb = jnp.stack([lanes4(bt(bbr[0]), bt(bbi[0]), bt(bbr[1]), bt(bbi[1])),
                    lanes4(-bt(bbi[0]), bt(bbr[0]), -bt(bbi[1]), bt(bbr[1]))], axis=1)
    wr, wi = jnp.moveaxis(pwr[:t][::-1, 0], 0, 1), jnp.moveaxis(pwi[:t][::-1, 0], 0, 1)
    vr, vi = jnp.moveaxis(pwr[:t, 1], 0, 1), jnp.moveaxis(pwi[:t, 1], 0, 1)
    pw = jnp.stack([lanes4(fr, fr, gr, gr), lanes4(fi, fi, gi, gi),
                    lanes4(wr, wr, vr, vr), lanes4(wi, wi, vi, vi)], axis=1)

    dr, di = pwr[t], pwi[t]
    a_rows, b_rows = [], []
    for _ in range(N_SCAN_LEVELS):
        a_rows.append(lanes4(dr[0], dr[0], dr[1], dr[1]))
        b_rows.append(lanes4(-di[0], di[0], -di[1], di[1]))
        dr, di = dr * dr - di * di, 2.0 * dr * di
    pad = [jnp.zeros_like(a_rows[0])] * (8 - N_SCAN_LEVELS)
    dtab = jnp.stack(a_rows + pad + b_rows + pad, axis=1)
    return rs.astype(BF16), cm, bb, pw, dtab


def _na_bias_tables(rpb):
    c = jnp.arange(GRID_W)
    col_start = jnp.clip(c - NA_KW // 2, 0, GRID_W - NA_KW)
    in_win = (c[:, None] >= col_start[None, :]) & (c[:, None] < col_start[None, :] + NA_KW)
    rel_col = jnp.clip(c[:, None] - c[None, :] + (NA_KW - 1), 0, 2 * NA_KW - 2)
    bc = rpb.astype(F32)[:, :, rel_col]
    bias = jnp.concatenate([bc[0::2], bc[1::2]], axis=-1)
    bias = jnp.pad(bias, ((0, 0), (0, 1), (0, 0), (0, 0)))
    win2 = jnp.concatenate([in_win, in_win], axis=-1).astype(F32)
    cmask = jnp.concatenate([jnp.broadcast_to(win2, (NA_RR_INVALID,) + win2.shape),
                             jnp.zeros((1,) + win2.shape, F32)], axis=0)
    return bias, cmask


def kernel(x, mem, g_mix, g_mem, g_ffn, g_final, w_in, w_gate, b_gate, rpb, w_mem_kv,
           a_re, a_im, log_dt, b_re, b_im, c_re, c_im, s5_d, w_glu, w_branch, w_o,
           w_ffn1, w_ffn3, w_ffn2):
    bsz, s, d = x.shape
    n = bsz * s
    x2d = x.reshape(n, d)
    gm = g_mix[0].reshape(1, d).astype(F32)

    wi = w_in[0]
    w_nat = jnp.concatenate([wi[:, 512:1024], wi[:, 2048:2560]], axis=1).astype(BF16)
    w_chm = jnp.concatenate([wi[:, 0:512] * (64 ** -0.5), wi[:, 1024:1536]], axis=1).T.astype(BF16)
    wu_t = wi[:, 1536:2048].T.astype(BF16)
    rs_tab, cm_tab, bb_tab, pw_tab, dtab = _s5_tables(a_re[0], a_im[0], log_dt[0], b_re[0], b_im[0],
                                                      c_re[0], c_im[0], s5_d[0])
    bias, cmask = _na_bias_tables(rpb[0])

    k, qm, qt, vt = _proj(x2d, gm, w_nat, w_chm)

    ut = _proj_s5t(x.reshape(n // CHUNK, CHUNK, d), gm, wu_t)
    zt = _s5_core(ut, rs_tab, cm_tab, bb_tab, pw_tab, dtab)
    y_s5 = _s5_glu(zt, w_glu[0].T.astype(BF16)).reshape(n, 512)

    y_na = _na(qt, k.reshape(bsz, s, 512), vt, bias, cmask).reshape(n, 512)

    k_mem, v_mem = _memkv(mem, g_mem[0].reshape(1, d).astype(F32), w_mem_kv[0].astype(BF16))
    y_mem = _mem_attn(qm.reshape(bsz, s, 512), k_mem, v_mem).reshape(n, 512)

    x1 = _merge(x2d, gm, y_na, y_s5, y_mem, w_gate[0].astype(BF16),
                b_gate[0].reshape(1, 3 * d).astype(F32), w_branch[0].astype(BF16), w_o[0].astype(BF16))
    out = _ffn(x1, g_ffn[0].reshape(1, d).astype(F32), g_final.reshape(1, d).astype(F32),
               w_ffn1[0].astype(BF16), w_ffn3[0].astype(BF16), w_ffn2[0].astype(BF16))
    return out.reshape(bsz, s, d)
```

```python
import functools

import jax
import jax.numpy as jnp
from jax import lax
from jax.experimental import pallas as pl
from jax.experimental.pallas import tpu as pltpu

F32 = jnp.float32
BF16 = jnp.bfloat16

D = 1024
GRID_W = 64
NA_HEADS = 8
NA_KH = 8
NA_KW = 16
S5_GROUPS = 32
S5_GROUP = 16
S5_STATE = 64
CHUNK = 64
MEM_HEADS = 4
MEM_HEAD_DIM = 128
D_FF = 2816
EPS = 1e-6
NEG_INF = -1e30

VMEM_LIMIT = 56 * 1024 * 1024

NT_DIMS = (((1,), (1,)), ((), ()))


def _cparams(n_axes):
    return pltpu.CompilerParams(
        dimension_semantics=("arbitrary",) * n_axes,
        vmem_limit_bytes=VMEM_LIMIT)


def _rms(x, g):
    return x * lax.rsqrt(jnp.mean(x * x, axis=-1, keepdims=True) + EPS) * g


def _const_spec(shape):
    nd = len(shape)
    return pl.BlockSpec(shape, lambda *_: (0,) * nd, pipeline_mode=pl.Buffered(1))


TOK_BLK = 128


def _proj_kernel(x_ref, g_ref, w_ref, wt_ref, k_ref, qm_ref, qt_ref, vt_ref):
    h = _rms(x_ref[...], g_ref[...]).astype(BF16)
    p = jnp.dot(h, w_ref[...], preferred_element_type=F32)
    k_ref[...] = p[:, 0:512].astype(BF16)
    qm_ref[...] = p[:, 512:1024].astype(BF16)
    pt = lax.dot_general(wt_ref[...], h, NT_DIMS, preferred_element_type=F32)
    for i in range(x_ref.shape[0] // TOK_BLK):
        toks = slice(TOK_BLK * i, TOK_BLK * (i + 1))
        qt_ref[i] = pt[0:512, toks].astype(BF16)
        vt_ref[i] = pt[512:1024, toks].astype(BF16)


def _proj(x2d, g, w, wt):
    n = x2d.shape[0]
    tm = 512
    nat = jax.ShapeDtypeStruct((n, 512), BF16)
    chm = jax.ShapeDtypeStruct((n // TOK_BLK, 512, TOK_BLK), BF16)
    nat_spec = pl.BlockSpec((tm, 512), lambda i: (i, 0))
    chm_spec = pl.BlockSpec((tm // TOK_BLK, 512, TOK_BLK), lambda i: (i, 0, 0))
    return pl.pallas_call(
        _proj_kernel,
        grid=(n // tm,),
        in_specs=[pl.BlockSpec((tm, D), lambda i: (i, 0)),
                  _const_spec((1, D)),
                  _const_spec((D, 1024)),
                  _const_spec((1024, D))],
        out_specs=[nat_spec, nat_spec, chm_spec, chm_spec],
        out_shape=[nat, nat, chm, chm],
        compiler_params=_cparams(1),
        name="proj",
    )(x2d, g, w, wt)


S5T_DT = 8


def _proj_s5t_kernel(x_ref, g_ref, wt_ref, o_ref, h_s):
    wt = wt_ref[...]
    nrows = x_ref.shape[0]
    hn = _rms(x_ref[...], g_ref[...]).reshape(nrows * S5T_DT, D)
    for c in range(D // 128):
        h_s[c] = hn[:, 128 * c:128 * (c + 1)]
    for j in range(S5T_DT):
        h = jnp.concatenate([h_s[c, pl.ds(j, nrows, stride=S5T_DT), :] for c in range(D // 128)],
                            axis=1).astype(BF16)
        ut = lax.dot_general(wt, h, NT_DIMS, preferred_element_type=F32)
        o_ref[:, j * S5_GROUP:(j + 1) * S5_GROUP, :] = (
            ut.astype(BF16).reshape(S5_GROUPS, S5_GROUP, 256))


def _proj_s5t(xc, g, wt):
    nrows = xc.shape[0]
    return pl.pallas_call(
        _proj_s5t_kernel,
        grid=(CHUNK // S5T_DT,),
        in_specs=[pl.BlockSpec((nrows, S5T_DT, D), lambda i: (0, i, 0)),
                  _const_spec((1, D)),
                  _const_spec((512, D))],
        out_specs=pl.BlockSpec((S5_GROUPS, S5T_DT * S5_GROUP, nrows), lambda i: (0, i, 0)),
        out_shape=jax.ShapeDtypeStruct((S5_GROUPS, CHUNK * S5_GROUP, nrows), BF16),
        scratch_shapes=[pltpu.VMEM((D // 128, nrows * S5T_DT, 128), F32)],
        compiler_params=_cparams(1),
        name="proj_s5t",
    )(xc, g, wt)


N_SCAN_LEVELS = 6


def _gelu_tanh(x):
    c = 0.7978845608028654
    return 0.5 * x * (1.0 + jnp.tanh(c * (x + 0.044715 * (x * x * x))))


def _split_bf16(x):
    hi = x.astype(BF16)
    return hi, (x - hi.astype(F32)).astype(BF16)


def _nt_f32(a, b):
    ah, al = _split_bf16(a)
    bh, bl = _split_bf16(b)
    nt = lambda u, w: lax.dot_general(u, w, NT_DIMS, preferred_element_type=F32)
    return nt(ah, bh) + nt(ah, bl) + nt(al, bh)


def _s5_kernel(ut_ref, cm_ref, bb_ref, pw_ref, d_ref, dd_ref, z_ref, m_s, n_s, pt_s, ptf_s, rs_s):
    tc = CHUNK * S5_GROUP
    cma, cmb = cm_ref[0, 0], cm_ref[0, 1]
    bba, bbb = bb_ref[0, 0], bb_ref[0, 1]
    for t in range(CHUNK):
        rows = slice(S5_GROUP * t, S5_GROUP * (t + 1))
        n_s[rows, :] = (cma * pw_ref[0, 0, t:t + 1, :] + cmb * pw_ref[0, 1, t:t + 1, :]).astype(BF16)
        ptf_s[rows, :] = bba * pw_ref[0, 2, t:t + 1, :] + bbb * pw_ref[0, 3, t:t + 1, :]
    ptf = ptf_s[...]
    pt_s[...] = ptf.astype(BF16)

    ra = _nt_f32(cma[:, 0:128], ptf[:, 0:128])
    rb = _nt_f32(cma[:, 128:256], ptf[:, 128:256])
    zeros = jnp.zeros((S5_GROUP, tc), F32)
    r = (jnp.concatenate([ra, zeros], axis=1)
         + pltpu.roll(jnp.concatenate([rb, zeros], axis=1), (CHUNK - 1) * S5_GROUP, axis=1)
         + jnp.concatenate([zeros[:, 0:tc - 128], dd_ref[0], zeros], axis=1))
    for k in range(8):
        rk = r if k == 0 else pltpu.roll(r, 2 * tc - S5_GROUP * k, axis=1)
        rs_s[k] = rk[:, 0:RS_LANES].astype(BF16)
    for t in range(CHUNK):
        a, k = divmod(CHUNK - 1 - t, 8)
        m_s[S5_GROUP * t:S5_GROUP * (t + 1), :] = rs_s[k, :, 128 * a:128 * a + tc]

    ut = ut_ref[0]
    nrows = ut.shape[1]
    y = jnp.dot(m_s[...], ut, preferred_element_type=F32)
    eye = jnp.where(lax.broadcasted_iota(jnp.int32, (nrows, nrows), 0)
                    == lax.broadcasted_iota(jnp.int32, (nrows, nrows), 1), 1.0, 0.0).astype(BF16)
    u_rows = lax.dot_general(eye, ut, NT_DIMS, preferred_element_type=F32).astype(BF16)
    v = jnp.dot(u_rows, pt_s[...], preferred_element_type=F32)
    pos = lax.broadcasted_iota(jnp.int32, (nrows, 128), 0) & (CHUNK - 1)

    def shift(x, s, up):
        if up:
            return jnp.where(pos < CHUNK - s, pltpu.roll(x, nrows - s, axis=0), 0.0)
        return jnp.where(pos >= s, pltpu.roll(x, s, axis=0), 0.0)

    def scan(vh, lanes, up):
        x = shift(vh, 1, up)
        for lvl in range(N_SCAN_LEVELS):
            xs = shift(x, 1 << lvl, up)
            a = d_ref[0, lvl:lvl + 1, lanes]
            b = d_ref[0, 8 + lvl:9 + lvl, lanes]
            x = x + a * xs + b * pltpu.roll(xs, 64, axis=1)
        return x

    xf = scan(v[:, 0:128], slice(0, 128), False)
    xb = scan(v[:, 128:256], slice(128, 256), True)
    xin = jnp.concatenate([xf, xb], axis=1).astype(BF16)
    y = y + lax.dot_general(n_s[...], xin, NT_DIMS, preferred_element_type=F32)
    z_ref[0] = _gelu_tanh(y).astype(BF16)


RS_LANES = 128 * 7 + CHUNK * S5_GROUP


def _s5_core(ut, cm, bb, pw, dtab, dd):
    nrows = ut.shape[2]
    tc = CHUNK * S5_GROUP
    return pl.pallas_call(
        _s5_kernel,
        grid=(S5_GROUPS,),
        in_specs=[pl.BlockSpec((1, tc, nrows), lambda g: (g, 0, 0)),
                  pl.BlockSpec((1, 2, S5_GROUP, 256), lambda g: (g, 0, 0, 0)),
                  pl.BlockSpec((1, 2, S5_GROUP, 256), lambda g: (g, 0, 0, 0)),
                  pl.BlockSpec((1, 4, CHUNK, 256), lambda g: (g, 0, 0, 0)),
                  pl.BlockSpec((1, 16, 256), lambda g: (g, 0, 0)),
                  pl.BlockSpec((1, S5_GROUP, 128), lambda g: (g, 0, 0))],
        out_specs=pl.BlockSpec((1, tc, nrows), lambda g: (g, 0, 0)),
        out_shape=jax.ShapeDtypeStruct((S5_GROUPS, tc, nrows), BF16),
        scratch_shapes=[pltpu.VMEM((tc, tc), BF16),
                        pltpu.VMEM((tc, 256), BF16),
                        pltpu.VMEM((tc, 256), BF16),
                        pltpu.VMEM((tc, 256), F32),
                        pltpu.VMEM((8, S5_GROUP, RS_LANES), BF16)],
        compiler_params=_cparams(1),
        name="s5_core",
    )(ut, cm, bb, pw, dtab, dd)


def _s5_glu_kernel(z_ref, wt_ref, o_ref):
    wt = wt_ref[...]
    nrows = z_ref.shape[2]
    eye = jnp.where(lax.broadcasted_iota(jnp.int32, (nrows, nrows), 0)
                    == lax.broadcasted_iota(jnp.int32, (nrows, nrows), 1), 1.0, 0.0).astype(BF16)
    for j in range(S5T_DT):
        zt = z_ref[:, j * S5_GROUP:(j + 1) * S5_GROUP, :].reshape(512, nrows)
        gl = jnp.dot(wt, zt, preferred_element_type=F32)
        o = (zt.astype(F32) * jax.nn.sigmoid(gl)).astype(BF16)
        nat = lax.dot_general(eye, o, NT_DIMS, preferred_element_type=F32)
        o_ref[:, j, :] = nat.astype(BF16)


def _s5_glu(zt, wglu_t):
    nrows = zt.shape[2]
    return pl.pallas_call(
        _s5_glu_kernel,
        grid=(CHUNK // S5T_DT,),
        in_specs=[pl.BlockSpec((S5_GROUPS, S5T_DT * S5_GROUP, nrows), lambda i: (0, i, 0)),
                  _const_spec((512, 512))],
        out_specs=pl.BlockSpec((nrows, S5T_DT, 512), lambda i: (0, i, 0)),
        out_shape=jax.ShapeDtypeStruct((nrows, CHUNK, 512), BF16),
        compiler_params=_cparams(1),
        name="s5_glu",
    )(zt, wglu_t)


NA_ROWS_PER_STEP = 8


NA_PAIRS_PER_STEP = NA_ROWS_PER_STEP // 2
NA_WIN_ROWS = 10
NA_WIN_KEYS = NA_WIN_ROWS * GRID_W
NA_WIN_BLKS = NA_WIN_KEYS // TOK_BLK


NA_RR_INVALID = 2 * NA_KH - 1


def _na_kernel(qt_ref, k_ref, vt_ref, bias_ref, cmask_ref, o_ref, *scratch):
    st_s, pt_s = scratch[0:4], scratch[4:8]
    rb = pl.program_id(1)
    lane = lax.broadcasted_iota(jnp.int32, (128, 128), 1)
    row = lax.broadcasted_iota(jnp.int32, (128, 128), 0)
    lo = lane < 64
    lo_q = lax.broadcasted_iota(jnp.int32, (GRID_W, 128), 1) < 64
    same_head = (row < 64) == lo
    eye = jnp.where(lax.broadcasted_iota(jnp.int32, (256, 256), 0)
                    == lax.broadcasted_iota(jnp.int32, (256, 256), 1), 1.0, 0.0).astype(BF16)
    half = NA_KH // 2

    def pair_body(pi, carry):
        r0 = rb * NA_ROWS_PER_STEP + 2 * pi
        win = jnp.clip(r0 - half, 0, GRID_W - NA_WIN_ROWS)
        blk0 = lax.shift_right_logical(win, 1)
        koff = pl.multiple_of(win * GRID_W, TOK_BLK)
        rr = []
        for p in range(2):
            r = r0 + p
            first = jnp.clip(r - half, 0, GRID_W - NA_KH) - win
            rel = win - r + (NA_KH - 1)
            rr.append([jnp.where((wr >= first) & (wr < first + NA_KH), wr + rel, NA_RR_INVALID)
                       for wr in range(NA_WIN_ROWS)])
        for s in range(4):
            ch = slice(128 * s, 128 * (s + 1))
            a = qt_ref[pi, ch, :].astype(F32)
            ar = pltpu.roll(a, 64, axis=1)
            w = jnp.concatenate([jnp.where(same_head, jnp.where(lo, a, ar), 0.0),
                                 jnp.where(same_head, jnp.where(lo, ar, a), 0.0)], axis=1).astype(BF16)
            kw = k_ref[0, pl.ds(koff, NA_WIN_KEYS), ch]
            sc = jnp.dot(kw, w, preferred_element_type=F32)
            sums = []
            for p in range(2):
                lanes = slice(128 * p, 128 * (p + 1))
                m = None
                for wr in range(NA_WIN_ROWS):
                    rows = slice(GRID_W * wr, GRID_W * (wr + 1))
                    t = jnp.where(cmask_ref[rr[p][wr]] > 0.0, sc[rows, lanes] + bias_ref[s, rr[p][wr]], NEG_INF)
                    st_s[s][rows, lanes] = t
                    m = t if m is None else jnp.maximum(m, t)
                m = jnp.max(m, axis=0, keepdims=True)
                l = None
                for wr in range(NA_WIN_ROWS):
                    rows = slice(GRID_W * wr, GRID_W * (wr + 1))
                    e = jnp.exp(st_s[s][rows, lanes] - m)
                    pt_s[s][rows, lanes] = e.astype(BF16)
                    l = e if l is None else l + e
                sums.append(jnp.sum(l, axis=0, keepdims=True))
            vw = jnp.concatenate([vt_ref[blk0 + i, ch, :] for i in range(NA_WIN_BLKS)], axis=1)
            ot = jnp.dot(vw, pt_s[s][...], preferred_element_type=F32)
            ot = (ot / jnp.concatenate(sums, axis=1)).astype(BF16)
            tt = lax.dot_general(eye, ot, NT_DIMS, preferred_element_type=F32)
            for p in range(2):
                nat = jnp.where(lo_q, tt[128 * p:128 * p + 64], tt[128 * p + 64:128 * p + 128])
                qoff = pl.multiple_of((2 * pi + p) * GRID_W, GRID_W)
                o_ref[0, pl.ds(qoff, GRID_W), ch] = nat.astype(BF16)
        return carry

    lax.fori_loop(0, NA_PAIRS_PER_STEP, pair_body, 0)


def _na(qt, k, vt, bias, cmask):
    b, s, _ = k.shape
    tq = NA_ROWS_PER_STEP * GRID_W
    blks = s // TOK_BLK
    return pl.pallas_call(
        _na_kernel,
        grid=(b, s // tq),
        in_specs=[pl.BlockSpec((NA_PAIRS_PER_STEP, 512, TOK_BLK), lambda bi, ri: (bi * (blks // NA_PAIRS_PER_STEP) + ri, 0, 0)),
                  pl.BlockSpec((1, s, 512), lambda bi, ri: (bi, 0, 0)),
                  pl.BlockSpec((blks, 512, TOK_BLK), lambda bi, ri: (bi, 0, 0)),
                  _const_spec(bias.shape),
                  _const_spec(cmask.shape)],
        out_specs=pl.BlockSpec((1, tq, 512), lambda bi, ri: (bi, ri, 0)),
        out_shape=jax.ShapeDtypeStruct((b, s, 512), BF16),
        scratch_shapes=([pltpu.VMEM((NA_WIN_KEYS, 256), F32)] * 4
                        + [pltpu.VMEM((NA_WIN_KEYS, 256), BF16)] * 4),
        compiler_params=_cparams(2),
        name="na",
    )(qt, k, vt, bias, cmask)


def _memkv_kernel(mem_ref, g_ref, w_ref, k_ref, v_ref):
    mn = _rms(mem_ref[0], g_ref[...]).astype(BF16)
    kv = jnp.dot(mn, w_ref[...], preferred_element_type=F32)
    k_ref[0] = kv[:, 0:512].astype(BF16)
    v_ref[0] = kv[:, 512:1024].astype(BF16)


def _memkv(mem, g, w):
    b, m, _ = mem.shape
    out = jax.ShapeDtypeStruct((b, m, 512), BF16)
    ospec = pl.BlockSpec((1, m, 512), lambda i: (i, 0, 0))
    return pl.pallas_call(
        _memkv_kernel,
        grid=(b,),
        in_specs=[pl.BlockSpec((1, m, D), lambda i: (i, 0, 0)),
                  _const_spec((1, D)),
                  _const_spec((D, 1024))],
        out_specs=[ospec, ospec],
        out_shape=[out, out],
        compiler_params=_cparams(1),
        name="memkv",
    )(mem, g, w)


def _mem_kernel(q_ref, k_ref, v_ref, o_ref):
    scale = MEM_HEAD_DIM ** -0.5
    for h in range(MEM_HEADS):
        cols = slice(MEM_HEAD_DIM * h, MEM_HEAD_DIM * (h + 1))
        sc = lax.dot_general(q_ref[0, :, cols], k_ref[0, :, cols], NT_DIMS,
                             preferred_element_type=F32) * scale
        m = jnp.max(sc, axis=-1, keepdims=True)
        p = jnp.exp(sc - m)
        l = jnp.sum(p, axis=-1, keepdims=True)
        o = jnp.dot(p.astype(BF16), v_ref[0, :, cols], preferred_element_type=F32)
        o_ref[0, :, cols] = (o / l).astype(BF16)


def _mem_attn(q, k, v):
    b, s, _ = q.shape
    m = k.shape[1]
    tq = 1024
    return pl.pallas_call(
        _mem_kernel,
        grid=(b, s // tq),
        in_specs=[pl.BlockSpec((1, tq, 512), lambda bi, i: (bi, i, 0)),
                  pl.BlockSpec((1, m, 512), lambda bi, i: (bi, 0, 0)),
                  pl.BlockSpec((1, m, 512), lambda bi, i: (bi, 0, 0))],
        out_specs=pl.BlockSpec((1, tq, 512), lambda bi, i: (bi, i, 0)),
        out_shape=jax.ShapeDtypeStruct((b, s, 512), BF16),
        compiler_params=_cparams(2),
        name="mem_attn",
    )(q, k, v)


def _merge_kernel(x_ref, g_ref, yna_ref, ys5_ref, ymem_ref, wg_ref, bg_ref, wb_ref, wo_ref, o_ref):
    x = x_ref[...]
    h = _rms(x, g_ref[...]).astype(BF16)
    merged = None
    for b, y_ref in enumerate((yna_ref, ys5_ref, ymem_ref)):
        cols = slice(D * b, D * (b + 1))
        gate = jax.nn.sigmoid(jnp.dot(h, wg_ref[:, cols], preferred_element_type=F32) + bg_ref[:, cols])
        up = jnp.dot(y_ref[...], wb_ref[b], preferred_element_type=F32)
        merged = gate * up if merged is None else merged + gate * up
    o_ref[...] = x + jnp.dot(merged.astype(BF16), wo_ref[...], preferred_element_type=F32)


def _merge(x2d, g, yna, ys5, ymem, wg, bg, wb, wo):
    n = x2d.shape[0]
    tm = 512
    yspec = pl.BlockSpec((tm, 512), lambda i: (i, 0))
    return pl.pallas_call(
        _merge_kernel,
        grid=(n // tm,),
        in_specs=[pl.BlockSpec((tm, D), lambda i: (i, 0)),
                  _const_spec((1, D)),
                  yspec, yspec, yspec,
                  _const_spec((D, 3 * D)),
                  _const_spec((1, 3 * D)),
                  _const_spec((3, 512, D)),
                  _const_spec((D, D))],
        out_specs=pl.BlockSpec((tm, D), lambda i: (i, 0)),
        out_shape=jax.ShapeDtypeStruct((n, D), F32),
        compiler_params=_cparams(1),
        name="merge",
    )(x2d, g, yna, ys5, ymem, wg, bg, wb, wo)


def _ffn_kernel(x_ref, g_ref, gf_ref, w1_ref, w3_ref, w2_ref, o_ref):
    x = x_ref[...]
    h = _rms(x, g_ref[...]).astype(BF16)
    a = jnp.dot(h, w1_ref[...], preferred_element_type=F32)
    c = jnp.dot(h, w3_ref[...], preferred_element_type=F32)
    mid = (a * jax.nn.sigmoid(a) * c).astype(BF16)
    x2 = x + jnp.dot(mid, w2_ref[...], preferred_element_type=F32)
    o_ref[...] = _rms(x2, gf_ref[...])


def _ffn(x2d, g, gf, w1, w3, w2):
    n = x2d.shape[0]
    tm = 512
    return pl.pallas_call(
        _ffn_kernel,
        grid=(n // tm,),
        in_specs=[pl.BlockSpec((tm, D), lambda i: (i, 0)),
                  _const_spec((1, D)),
                  _const_spec((1, D)),
                  _const_spec((D, D_FF)),
                  _const_spec((D, D_FF)),
                  _const_spec((D_FF, D))],
        out_specs=pl.BlockSpec((tm, D), lambda i: (i, 0)),
        out_shape=jax.ShapeDtypeStruct((n, D), F32),
        compiler_params=_cparams(1),
        name="ffn",
    )(x2d, g, gf, w1, w3, w2)


def _s5_tables(a_re, a_im, log_dt, b_re, b_im, c_re, c_im, s5_d):
    t = CHUNK
    ar, ai = a_re.astype(F32), a_im.astype(F32)
    dt = jnp.exp(log_dt.astype(F32))[..., None]
    lr, li = ar * dt, ai * dt
    mag = jnp.exp(lr)
    lbr, lbi = mag * jnp.cos(li), mag * jnp.sin(li)
    den = ar * ar + ai * ai
    rr = ((lbr - 1.0) * ar + lbi * ai) / den
    ri = (lbi * ar - (lbr - 1.0) * ai) / den
    br, bi = b_re.astype(F32), b_im.astype(F32)
    bbr = rr[..., None] * br - ri[..., None] * bi
    bbi = rr[..., None] * bi + ri[..., None] * br
    cmr, cmi = c_re.astype(F32), c_im.astype(F32)
    tau = jnp.arange(t + 1, dtype=F32)[:, None, None, None]
    pmag = jnp.exp(tau * lr[None])
    pwr, pwi = pmag * jnp.cos(tau * li[None]), pmag * jnp.sin(tau * li[None])

    dd = jnp.pad(jnp.eye(S5_GROUP, dtype=F32)[None] * s5_d.astype(F32).reshape(S5_GROUPS, S5_GROUP, 1),
                 ((0, 0), (0, 0), (128 - S5_GROUP, 0)))

    def lanes4(f0, f1, b0, b1):
        return jnp.concatenate([f0, f1, b0, b1], axis=-1)

    cm = jnp.stack([lanes4(cmr[0], -cmi[0], cmr[1], -cmi[1]),
                    lanes4(-cmi[0], -cmr[0], -cmi[1], -cmr[1])], axis=1)
    fr, fi = jnp.moveaxis(pwr[1:t + 1, 0], 0, 1), jnp.moveaxis(pwi[1:t + 1, 0], 0, 1)
    gr, gi = jnp.moveaxis(pwr[1:t + 1][::-1, 1], 0, 1), jnp.moveaxis(pwi[1:t + 1][::-1, 1], 0, 1)
    bt = lambda z: jnp.swapaxes(z, -1, -2)
    bb = jnp.stack([lanes4(bt(bbr[0]), bt(bbi[0]), bt(bbr[1]), bt(bbi[1])),
                    lanes4(-bt(bbi[0]), bt(bbr[0]), -bt(bbi[1]), bt(bbr[1]))], axis=1)
    wr, wi = jnp.moveaxis(pwr[:t][::-1, 0], 0, 1), jnp.moveaxis(pwi[:t][::-1, 0], 0, 1)
    vr, vi = jnp.moveaxis(pwr[:t, 1], 0, 1), jnp.moveaxis(pwi[:t, 1], 0, 1)
    pw = jnp.stack([lanes4(fr, fr, gr, gr), lanes4(fi, fi, gi, gi),
                    lanes4(wr, wr, vr, vr), lanes4(wi, wi, vi, vi)], axis=1)

    dr, di = pwr[t], pwi[t]
    a_rows, b_rows = [], []
    for _ in range(N_SCAN_LEVELS):
        a_rows.append(lanes4(dr[0], dr[0], dr[1], dr[1]))
        b_rows.append(lanes4(-di[0], di[0], -di[1], di[1]))
        dr, di = dr * dr - di * di, 2.0 * dr * di
    pad = [jnp.zeros_like(a_rows[0])] * (8 - N_SCAN_LEVELS)
    dtab = jnp.stack(a_rows + pad + b_rows + pad, axis=1)
    return cm, bb, pw, dtab, dd


def _na_bias_tables(rpb):
    c = jnp.arange(GRID_W)
    col_start = jnp.clip(c - NA_KW // 2, 0, GRID_W - NA_KW)
    in_win = (c[:, None] >= col_start[None, :]) & (c[:, None] < col_start[None, :] + NA_KW)
    rel_col = jnp.clip(c[:, None] - c[None, :] + (NA_KW - 1), 0, 2 * NA_KW - 2)
    bc = rpb.astype(F32)[:, :, rel_col]
    bias = jnp.concatenate([bc[0::2], bc[1::2]], axis=-1)
    bias = jnp.pad(bias, ((0, 0), (0, 1), (0, 0), (0, 0)))
    win2 = jnp.concatenate([in_win, in_win], axis=-1).astype(F32)
    cmask = jnp.concatenate([jnp.broadcast_to(win2, (NA_RR_INVALID,) + win2.shape),
                             jnp.zeros((1,) + win2.shape, F32)], axis=0)
    return bias, cmask


def kernel(x, mem, g_mix, g_mem, g_ffn, g_final, w_in, w_gate, b_gate, rpb, w_mem_kv,
           a_re, a_im, log_dt, b_re, b_im, c_re, c_im, s5_d, w_glu, w_branch, w_o,
           w_ffn1, w_ffn3, w_ffn2):
    bsz, s, d = x.shape
    n = bsz * s
    x2d = x.reshape(n, d)
    gm = g_mix[0].reshape(1, d).astype(F32)

    wi = w_in[0]
    w_nat = jnp.concatenate([wi[:, 512:1024], wi[:, 2048:2560]], axis=1).astype(BF16)
    w_chm = jnp.concatenate([wi[:, 0:512] * (64 ** -0.5), wi[:, 1024:1536]], axis=1).T.astype(BF16)
    wu_t = wi[:, 1536:2048].T.astype(BF16)
    cm_tab, bb_tab, pw_tab, dtab, dd_tab = _s5_tables(a_re[0], a_im[0], log_dt[0], b_re[0], b_im[0],
                                                      c_re[0], c_im[0], s5_d[0])
    bias, cmask = _na_bias_tables(rpb[0])

    k, qm, qt, vt = _proj(x2d, gm, w_nat, w_chm)

    ut = _proj_s5t(x.reshape(n // CHUNK, CHUNK, d), gm, wu_t)
    zt = _s5_core(ut, cm_tab, bb_tab, pw_tab, dtab, dd_tab)
    y_s5 = _s5_glu(zt, w_glu[0].T.astype(BF16)).reshape(n, 512)

    y_na = _na(qt, k.reshape(bsz, s, 512), vt, bias, cmask).reshape(n, 512)

    k_mem, v_mem = _memkv(mem, g_mem[0].reshape(1, d).astype(F32), w_mem_kv[0].astype(BF16))
    y_mem = _mem_attn(qm.reshape(bsz, s, 512), k_mem, v_mem).reshape(n, 512)

    x1 = _merge(x2d, gm, y_na, y_s5, y_mem, w_gate[0].astype(BF16),
                b_gate[0].reshape(1, 3 * d).astype(F32), w_branch[0].astype(BF16), w_o[0].astype(BF16))
    out = _ffn(x1, g_ffn[0].reshape(1, d).astype(F32), g_final.reshape(1, d).astype(F32),
               w_ffn1[0].astype(BF16), w_ffn3[0].astype(BF16), w_ffn2[0].astype(BF16))
    return out.reshape(bsz, s, d)
```

```python
import functools

import jax
import jax.numpy as jnp
from jax import lax
from jax.experimental import pallas as pl
from jax.experimental.pallas import tpu as pltpu

F32 = jnp.float32
BF16 = jnp.bfloat16

D = 1024
GRID_W = 64
NA_HEADS = 8
NA_KH = 8
NA_KW = 16
S5_GROUPS = 32
S5_GROUP = 16
S5_STATE = 64
CHUNK = 64
MEM_HEADS = 4
MEM_HEAD_DIM = 128
D_FF = 2816
EPS = 1e-6
NEG_INF = -1e30

VMEM_LIMIT = 56 * 1024 * 1024

NT_DIMS = (((1,), (1,)), ((), ()))


def _cparams(n_axes):
    return pltpu.CompilerParams(
        dimension_semantics=("arbitrary",) * n_axes,
        vmem_limit_bytes=VMEM_LIMIT)


def _rms(x, g):
    return x * lax.rsqrt(jnp.mean(x * x, axis=-1, keepdims=True) + EPS) * g


def _const_spec(shape):
    nd = len(shape)
    return pl.BlockSpec(shape, lambda *_: (0,) * nd, pipeline_mode=pl.Buffered(1))


TOK_BLK = 128


def _proj_kernel(x_ref, g_ref, w_ref, wt_ref, k_ref, qm_ref, qt_ref, vt_ref):
    h = _rms(x_ref[...], g_ref[...]).astype(BF16)
    p = jnp.dot(h, w_ref[...], preferred_element_type=F32)
    k_ref[...] = p[:, 0:512].astype(BF16)
    qm_ref[...] = p[:, 512:1024].astype(BF16)
    pt = lax.dot_general(wt_ref[...], h, NT_DIMS, preferred_element_type=F32)
    for i in range(x_ref.shape[0] // TOK_BLK):
        toks = slice(TOK_BLK * i, TOK_BLK * (i + 1))
        qt_ref[i] = pt[0:512, toks].astype(BF16)
        vt_ref[i] = pt[512:1024, toks].astype(BF16)


def _proj(x2d, g, w, wt):
    n = x2d.shape[0]
    tm = 512
    nat = jax.ShapeDtypeStruct((n, 512), BF16)
    chm = jax.ShapeDtypeStruct((n // TOK_BLK, 512, TOK_BLK), BF16)
    nat_spec = pl.BlockSpec((tm, 512), lambda i: (i, 0))
    chm_spec = pl.BlockSpec((tm // TOK_BLK, 512, TOK_BLK), lambda i: (i, 0, 0))
    return pl.pallas_call(
        _proj_kernel,
        grid=(n // tm,),
        in_specs=[pl.BlockSpec((tm, D), lambda i: (i, 0)),
                  _const_spec((1, D)),
                  _const_spec((D, 1024)),
                  _const_spec((1024, D))],
        out_specs=[nat_spec, nat_spec, chm_spec, chm_spec],
        out_shape=[nat, nat, chm, chm],
        compiler_params=_cparams(1),
        name="proj",
    )(x2d, g, w, wt)


S5T_DT = 8


def _proj_s5t_kernel(x_ref, g_ref, wt_ref, o_ref, h_s):
    wt = wt_ref[...]
    nrows = x_ref.shape[0]
    hn = _rms(x_ref[...], g_ref[...]).reshape(nrows * S5T_DT, D)
    for c in range(D // 128):
        h_s[c] = hn[:, 128 * c:128 * (c + 1)]
    for j in range(S5T_DT):
        h = jnp.concatenate([h_s[c, pl.ds(j, nrows, stride=S5T_DT), :] for c in range(D // 128)],
                            axis=1).astype(BF16)
        ut = lax.dot_general(wt, h, NT_DIMS, preferred_element_type=F32)
        o_ref[:, j * S5_GROUP:(j + 1) * S5_GROUP, :] = (
            ut.astype(BF16).reshape(S5_GROUPS, S5_GROUP, 256))


def _proj_s5t(xc, g, wt):
    nrows = xc.shape[0]
    return pl.pallas_call(
        _proj_s5t_kernel,
        grid=(CHUNK // S5T_DT,),
        in_specs=[pl.BlockSpec((nrows, S5T_DT, D), lambda i: (0, i, 0)),
                  _const_spec((1, D)),
                  _const_spec((512, D))],
        out_specs=pl.BlockSpec((S5_GROUPS, S5T_DT * S5_GROUP, nrows), lambda i: (0, i, 0)),
        out_shape=jax.ShapeDtypeStruct((S5_GROUPS, CHUNK * S5_GROUP, nrows), BF16),
        scratch_shapes=[pltpu.VMEM((D // 128, nrows * S5T_DT, 128), F32)],
        compiler_params=_cparams(1),
        name="proj_s5t",
    )(xc, g, wt)


N_SCAN_LEVELS = 6


def _gelu_tanh(x):
    c = 0.7978845608028654
    return 0.5 * x * (1.0 + jnp.tanh(c * (x + 0.044715 * (x * x * x))))


def _split_bf16(x):
    hi = x.astype(BF16)
    return hi, (x - hi.astype(F32)).astype(BF16)


def _nt_f32(a, b):
    ah, al = _split_bf16(a)
    bh, bl = _split_bf16(b)
    nt = lambda u, w: lax.dot_general(u, w, NT_DIMS, preferred_element_type=F32)
    return nt(ah, bh) + nt(ah, bl) + nt(al, bh)


def _s5_kernel(ut_ref, cm_ref, bb_ref, pw_ref, d_ref, dd_ref, z_ref, m_s, n_s, pt_s, ptf_s, rs_s):
    tc = CHUNK * S5_GROUP
    cma, cmb = cm_ref[0, 0], cm_ref[0, 1]
    bba, bbb = bb_ref[0, 0], bb_ref[0, 1]
    for t in range(CHUNK):
        rows = slice(S5_GROUP * t, S5_GROUP * (t + 1))
        n_s[rows, :] = (cma * pw_ref[0, 0, t:t + 1, :] + cmb * pw_ref[0, 1, t:t + 1, :]).astype(BF16)
        ptf_s[rows, :] = bba * pw_ref[0, 2, t:t + 1, :] + bbb * pw_ref[0, 3, t:t + 1, :]
    ptf = ptf_s[...]
    pt_s[...] = ptf.astype(BF16)

    ra = _nt_f32(cma[:, 0:128], ptf[:, 0:128])
    rb = _nt_f32(cma[:, 128:256], ptf[:, 128:256])
    zeros = jnp.zeros((S5_GROUP, tc), F32)
    r = (jnp.concatenate([ra, zeros], axis=1)
         + pltpu.roll(jnp.concatenate([rb, zeros], axis=1), (CHUNK - 1) * S5_GROUP, axis=1)
         + jnp.concatenate([zeros[:, 0:tc - 128], dd_ref[0], zeros], axis=1))
    for k in range(8):
        rk = r if k == 0 else pltpu.roll(r, 2 * tc - S5_GROUP * k, axis=1)
        rs_s[k] = rk[:, 0:RS_LANES].astype(BF16)
    for t in range(CHUNK):
        a, k = divmod(CHUNK - 1 - t, 8)
        m_s[S5_GROUP * t:S5_GROUP * (t + 1), :] = rs_s[k, :, 128 * a:128 * a + tc]

    ut = ut_ref[0]
    nrows = ut.shape[1]
    y = jnp.dot(m_s[...], ut, preferred_element_type=F32)
    eye = jnp.where(lax.broadcasted_iota(jnp.int32, (nrows, nrows), 0)
                    == lax.broadcasted_iota(jnp.int32, (nrows, nrows), 1), 1.0, 0.0).astype(BF16)
    u_rows = lax.dot_general(eye, ut, NT_DIMS, preferred_element_type=F32).astype(BF16)
    v = jnp.dot(u_rows, pt_s[...], preferred_element_type=F32)
    pos = lax.broadcasted_iota(jnp.int32, (nrows, 128), 0) & (CHUNK - 1)

    def shift(x, s, up):
        if up:
            return jnp.where(pos < CHUNK - s, pltpu.roll(x, nrows - s, axis=0), 0.0)
        return jnp.where(pos >= s, pltpu.roll(x, s, axis=0), 0.0)

    def scan(vh, lanes, up):
        x = shift(vh, 1, up)
        for lvl in range(N_SCAN_LEVELS):
            xs = shift(x, 1 << lvl, up)
            a = d_ref[0, lvl:lvl + 1, lanes]
            b = d_ref[0, 8 + lvl:9 + lvl, lanes]
            x = x + a * xs + b * pltpu.roll(xs, 64, axis=1)
        return x

    xf = scan(v[:, 0:128], slice(0, 128), False)
    xb = scan(v[:, 128:256], slice(128, 256), True)
    xin = jnp.concatenate([xf, xb], axis=1).astype(BF16)
    y = y + lax.dot_general(n_s[...], xin, NT_DIMS, preferred_element_type=F32)
    z_ref[0] = _gelu_tanh(y).astype(BF16)


RS_LANES = 128 * 7 + CHUNK * S5_GROUP


def _s5_core(ut, cm, bb, pw, dtab, dd):
    nrows = ut.shape[2]
    tc = CHUNK * S5_GROUP
    return pl.pallas_call(
        _s5_kernel,
        grid=(S5_GROUPS,),
        in_specs=[pl.BlockSpec((1, tc, nrows), lambda g: (g, 0, 0)),
                  pl.BlockSpec((1, 2, S5_GROUP, 256), lambda g: (g, 0, 0, 0)),
                  pl.BlockSpec((1, 2, S5_GROUP, 256), lambda g: (g, 0, 0, 0)),
                  pl.BlockSpec((1, 4, CHUNK, 256), lambda g: (g, 0, 0, 0)),
                  pl.BlockSpec((1, 16, 256), lambda g: (g, 0, 0)),
                  pl.BlockSpec((1, S5_GROUP, 128), lambda g: (g, 0, 0))],
        out_specs=pl.BlockSpec((1, tc, nrows), lambda g: (g, 0, 0)),
        out_shape=jax.ShapeDtypeStruct((S5_GROUPS, tc, nrows), BF16),
        scratch_shapes=[pltpu.VMEM((tc, tc), BF16),
                        pltpu.VMEM((tc, 256), BF16),
                        pltpu.VMEM((tc, 256), BF16),
                        pltpu.VMEM((tc, 256), F32),
                        pltpu.VMEM((8, S5_GROUP, RS_LANES), BF16)],
        compiler_params=_cparams(1),
        name="s5_core",
    )(ut, cm, bb, pw, dtab, dd)


def _s5_glu_kernel(z_ref, wt_ref, o_ref):
    wt = wt_ref[...]
    nrows = z_ref.shape[2]
    eye = jnp.where(lax.broadcasted_iota(jnp.int32, (nrows, nrows), 0)
                    == lax.broadcasted_iota(jnp.int32, (nrows, nrows), 1), 1.0, 0.0).astype(BF16)
    for j in range(S5T_DT):
        zt = z_ref[:, j * S5_GROUP:(j + 1) * S5_GROUP, :].reshape(512, nrows)
        gl = jnp.dot(wt, zt, preferred_element_type=F32)
        o = (zt.astype(F32) * jax.nn.sigmoid(gl)).astype(BF16)
        nat = lax.dot_general(eye, o, NT_DIMS, preferred_element_type=F32)
        o_ref[:, j, :] = nat.astype(BF16)


def _s5_glu(zt, wglu_t):
    nrows = zt.shape[2]
    return pl.pallas_call(
        _s5_glu_kernel,
        grid=(CHUNK // S5T_DT,),
        in_specs=[pl.BlockSpec((S5_GROUPS, S5T_DT * S5_GROUP, nrows), lambda i: (0, i, 0)),
                  _const_spec((512, 512))],
        out_specs=pl.BlockSpec((nrows, S5T_DT, 512), lambda i: (0, i, 0)),
        out_shape=jax.ShapeDtypeStruct((nrows, CHUNK, 512), BF16),
        compiler_params=_cparams(1),
        name="s5_glu",
    )(zt, wglu_t)


NA_ROWS_PER_STEP = 16
NA_PAIRS_PER_STEP = NA_ROWS_PER_STEP // 2
NA_WIN_ROWS = 10
NA_WIN_KEYS = NA_WIN_ROWS * GRID_W
NA_WIN_BLKS = NA_WIN_KEYS // TOK_BLK
NA_RR_OUTSIDE = 2 * NA_KH - 1
LOG2E = 1.4426950408889634


def _na_kernel(qt_ref, k_ref, vt_ref, bias_ref, cmask_ref, o_ref, *scratch):
    st_s, pt_s = (scratch[0:4], scratch[4:8]), scratch[8:12]
    rb = pl.program_id(1)
    lane = lax.broadcasted_iota(jnp.int32, (128, 128), 1)
    row = lax.broadcasted_iota(jnp.int32, (128, 128), 0)
    lo = lane < 64
    lo_q = lax.broadcasted_iota(jnp.int32, (GRID_W, 128), 1) < 64
    same_head = (row < 64) == lo
    col_ok = cmask_ref[...] > 0.0
    eye = jnp.where(lax.broadcasted_iota(jnp.int32, (256, 256), 0)
                    == lax.broadcasted_iota(jnp.int32, (256, 256), 1), 1.0, 0.0).astype(BF16)
    half = NA_KH // 2

    def window(pi):
        r0 = rb * NA_ROWS_PER_STEP + 2 * pi
        return r0, jnp.clip(r0 - half, 0, GRID_W - NA_WIN_ROWS)

    def q_stage(pi, par):
        _, win = window(pi)
        koff = pl.multiple_of(win * GRID_W, TOK_BLK)
        for s in range(4):
            ch = slice(128 * s, 128 * (s + 1))
            a = qt_ref[pi, ch, :].astype(F32)
            ar = pltpu.roll(a, 64, axis=1)
            w = jnp.concatenate([jnp.where(same_head, jnp.where(lo, a, ar), 0.0),
                                 jnp.where(same_head, jnp.where(lo, ar, a), 0.0)], axis=1).astype(BF16)
            kw = k_ref[0, pl.ds(koff, NA_WIN_KEYS), ch]
            st_s[par][s][...] = jnp.dot(kw, w, preferred_element_type=F32)

    def s_stage(pi, par):
        r0, win = window(pi)
        rr = []
        for p in range(2):
            r = r0 + p
            first = jnp.clip(r - half, 0, GRID_W - NA_KH) - win
            rel = win - r + (NA_KH - 1)
            rr.append([jnp.where((wr >= first) & (wr < first + NA_KH), wr + rel, NA_RR_OUTSIDE)
                       for wr in range(NA_WIN_ROWS)])
        sums = []
        for s in range(4):
            st = st_s[par][s]
            slab_sums = []
            for p in range(2):
                lanes = slice(128 * p, 128 * (p + 1))
                m = None
                for wr in range(NA_WIN_ROWS):
                    rows = slice(GRID_W * wr, GRID_W * (wr + 1))
                    t = jnp.where(col_ok, st[rows, lanes] + bias_ref[s, rr[p][wr]], NEG_INF)
                    st[rows, lanes] = t
                    m = t if m is None else jnp.maximum(m, t)
                m = jnp.max(m, axis=0, keepdims=True)
                l = None
                for wr in range(NA_WIN_ROWS):
                    rows = slice(GRID_W * wr, GRID_W * (wr + 1))
                    e = jnp.exp2(st[rows, lanes] - m)
                    pt_s[s][rows, lanes] = e.astype(BF16)
                    l = e if l is None else l + e
                slab_sums.append(jnp.sum(l, axis=0, keepdims=True))
            sums.append(jnp.concatenate(slab_sums, axis=1))
        return tuple(sums)

    def o_stage_pv(pi, sums):
        _, win = window(pi)
        blk0 = lax.shift_right_logical(win, 1)
        ots = []
        for s in range(4):
            ch = slice(128 * s, 128 * (s + 1))
            vw = jnp.concatenate([vt_ref[blk0 + i, ch, :] for i in range(NA_WIN_BLKS)], axis=1)
            ot = jnp.dot(vw, pt_s[s][...], preferred_element_type=F32)
            ots.append((ot / sums[s]).astype(BF16))
        return ots

    def o_stage_store(pi, ots):
        for s in range(4):
            ch = slice(128 * s, 128 * (s + 1))
            tt = lax.dot_general(eye, ots[s], NT_DIMS, preferred_element_type=F32)
            for p in range(2):
                nat = jnp.where(lo_q, tt[128 * p:128 * p + 64], tt[128 * p + 64:128 * p + 128])
                qoff = pl.multiple_of((2 * pi + p) * GRID_W, GRID_W)
                o_ref[0, pl.ds(qoff, GRID_W), ch] = nat.astype(BF16)

    def step(pi, par, sums_prev, has_prev=True):
        ots = o_stage_pv(pi - 1, sums_prev) if has_prev else None
        q_stage(pi, par)
        if has_prev:
            o_stage_store(pi - 1, ots)
        return s_stage(pi, par)

    def two_steps(j, sums):
        sums = step(2 * j + 1, 1, sums)
        return step(2 * j + 2, 0, sums)

    last = NA_PAIRS_PER_STEP - 1
    sums = step(0, 0, None, has_prev=False)
    sums = lax.fori_loop(0, (last - 1) // 2, two_steps, sums)
    sums = step(last, 1, sums)
    o_stage_store(last, o_stage_pv(last, sums))


def _na(qt, k, vt, bias, cmask):
    b, s, _ = k.shape
    tq = NA_ROWS_PER_STEP * GRID_W
    blks = s // TOK_BLK
    return pl.pallas_call(
        _na_kernel,
        grid=(b, s // tq),
        in_specs=[pl.BlockSpec((NA_PAIRS_PER_STEP, 512, TOK_BLK),
                               lambda bi, ri: (bi * (blks // NA_PAIRS_PER_STEP) + ri, 0, 0)),
                  pl.BlockSpec((1, s, 512), lambda bi, ri: (bi, 0, 0)),
                  pl.BlockSpec((blks, 512, TOK_BLK), lambda bi, ri: (bi, 0, 0)),
                  _const_spec(bias.shape),
                  _const_spec(cmask.shape)],
        out_specs=pl.BlockSpec((1, tq, 512), lambda bi, ri: (bi, ri, 0)),
        out_shape=jax.ShapeDtypeStruct((b, s, 512), BF16),
        scratch_shapes=([pltpu.VMEM((NA_WIN_KEYS, 256), F32)] * 8
                        + [pltpu.VMEM((NA_WIN_KEYS, 256), BF16)] * 4),
        compiler_params=_cparams(2),
        name="na",
    )(qt, k, vt, bias, cmask)


def _memkv_kernel(mem_ref, g_ref, w_ref, k_ref, v_ref):
    mn = _rms(mem_ref[0], g_ref[...]).astype(BF16)
    kv = jnp.dot(mn, w_ref[...], preferred_element_type=F32)
    k_ref[0] = kv[:, 0:512].astype(BF16)
    v_ref[0] = kv[:, 512:1024].astype(BF16)


def _memkv(mem, g, w):
    b, m, _ = mem.shape
    out = jax.ShapeDtypeStruct((b, m, 512), BF16)
    ospec = pl.BlockSpec((1, m, 512), lambda i: (i, 0, 0))
    return pl.pallas_call(
        _memkv_kernel,
        grid=(b,),
        in_specs=[pl.BlockSpec((1, m, D), lambda i: (i, 0, 0)),
                  _const_spec((1, D)),
                  _const_spec((D, 1024))],
        out_specs=[ospec, ospec],
        out_shape=[out, out],
        compiler_params=_cparams(1),
        name="memkv",
    )(mem, g, w)


def _mem_kernel(q_ref, k_ref, v_ref, o_ref):
    scale = MEM_HEAD_DIM ** -0.5
    for h in range(MEM_HEADS):
        cols = slice(MEM_HEAD_DIM * h, MEM_HEAD_DIM * (h + 1))
        sc = lax.dot_general(q_ref[0, :, cols], k_ref[0, :, cols], NT_DIMS,
                             preferred_element_type=F32) * scale
        m = jnp.max(sc, axis=-1, keepdims=True)
        p = jnp.exp(sc - m)
        l = jnp.sum(p, axis=-1, keepdims=True)
        o = jnp.dot(p.astype(BF16), v_ref[0, :, cols], preferred_element_type=F32)
        o_ref[0, :, cols] = (o / l).astype(BF16)


def _mem_attn(q, k, v):
    b, s, _ = q.shape
    m = k.shape[1]
    tq = 1024
    return pl.pallas_call(
        _mem_kernel,
        grid=(b, s // tq),
        in_specs=[pl.BlockSpec((1, tq, 512), lambda bi, i: (bi, i, 0)),
                  pl.BlockSpec((1, m, 512), lambda bi, i: (bi, 0, 0)),
                  pl.BlockSpec((1, m, 512), lambda bi, i: (bi, 0, 0))],
        out_specs=pl.BlockSpec((1, tq, 512), lambda bi, i: (bi, i, 0)),
        out_shape=jax.ShapeDtypeStruct((b, s, 512), BF16),
        compiler_params=_cparams(2),
        name="mem_attn",
    )(q, k, v)


def _merge_kernel(x_ref, g_ref, yna_ref, ys5_ref, ymem_ref, wg_ref, bg_ref, wb_ref, wo_ref, o_ref):
    x = x_ref[...]
    h = _rms(x, g_ref[...]).astype(BF16)
    merged = None
    for b, y_ref in enumerate((yna_ref, ys5_ref, ymem_ref)):
        cols = slice(D * b, D * (b + 1))
        gate = jax.nn.sigmoid(jnp.dot(h, wg_ref[:, cols], preferred_element_type=F32) + bg_ref[:, cols])
        up = jnp.dot(y_ref[...], wb_ref[b], preferred_element_type=F32)
        merged = gate * up if merged is None else merged + gate * up
    o_ref[...] = x + jnp.dot(merged.astype(BF16), wo_ref[...], preferred_element_type=F32)


def _merge(x2d, g, yna, ys5, ymem, wg, bg, wb, wo):
    n = x2d.shape[0]
    tm = 512
    yspec = pl.BlockSpec((tm, 512), lambda i: (i, 0))
    return pl.pallas_call(
        _merge_kernel,
        grid=(n // tm,),
        in_specs=[pl.BlockSpec((tm, D), lambda i: (i, 0)),
                  _const_spec((1, D)),
                  yspec, yspec, yspec,
                  _const_spec((D, 3 * D)),
                  _const_spec((1, 3 * D)),
                  _const_spec((3, 512, D)),
                  _const_spec((D, D))],
        out_specs=pl.BlockSpec((tm, D), lambda i: (i, 0)),
        out_shape=jax.ShapeDtypeStruct((n, D), F32),
        compiler_params=_cparams(1),
        name="merge",
    )(x2d, g, yna, ys5, ymem, wg, bg, wb, wo)


def _ffn_kernel(x_ref, g_ref, gf_ref, w1_ref, w3_ref, w2_ref, o_ref):
    x = x_ref[...]
    h = _rms(x, g_ref[...]).astype(BF16)
    a = jnp.dot(h, w1_ref[...], preferred_element_type=F32)
    c = jnp.dot(h, w3_ref[...], preferred_element_type=F32)
    mid = (a * jax.nn.sigmoid(a) * c).astype(BF16)
    x2 = x + jnp.dot(mid, w2_ref[...], preferred_element_type=F32)
    o_ref[...] = _rms(x2, gf_ref[...])


def _ffn(x2d, g, gf, w1, w3, w2):
    n = x2d.shape[0]
    tm = 512
    return pl.pallas_call(
        _ffn_kernel,
        grid=(n // tm,),
        in_specs=[pl.BlockSpec((tm, D), lambda i: (i, 0)),
                  _const_spec((1, D)),
                  _const_spec((1, D)),
                  _const_spec((D, D_FF)),
                  _const_spec((D, D_FF)),
                  _const_spec((D_FF, D))],
        out_specs=pl.BlockSpec((tm, D), lambda i: (i, 0)),
        out_shape=jax.ShapeDtypeStruct((n, D), F32),
        compiler_params=_cparams(1),
        name="ffn",
    )(x2d, g, gf, w1, w3, w2)


def _s5_tables(a_re, a_im, log_dt, b_re, b_im, c_re, c_im, s5_d):
    t = CHUNK
    ar, ai = a_re.astype(F32), a_im.astype(F32)
    dt = jnp.exp(log_dt.astype(F32))[..., None]
    lr, li = ar * dt, ai * dt
    mag = jnp.exp(lr)
    lbr, lbi = mag * jnp.cos(li), mag * jnp.sin(li)
    den = ar * ar + ai * ai
    rr = ((lbr - 1.0) * ar + lbi * ai) / den
    ri = (lbi * ar - (lbr - 1.0) * ai) / den
    br, bi = b_re.astype(F32), b_im.astype(F32)
    bbr = rr[..., None] * br - ri[..., None] * bi
    bbi = rr[..., None] * bi + ri[..., None] * br
    cmr, cmi = c_re.astype(F32), c_im.astype(F32)
    tau = jnp.arange(t + 1, dtype=F32)[:, None, None, None]
    pmag = jnp.exp(tau * lr[None])
    pwr, pwi = pmag * jnp.cos(tau * li[None]), pmag * jnp.sin(tau * li[None])

    dd = jnp.pad(jnp.eye(S5_GROUP, dtype=F32)[None] * s5_d.astype(F32).reshape(S5_GROUPS, S5_GROUP, 1),
                 ((0, 0), (0, 0), (128 - S5_GROUP, 0)))

    def lanes4(f0, f1, b0, b1):
        return jnp.concatenate([f0, f1, b0, b1], axis=-1)

    cm = jnp.stack([lanes4(cmr[0], -cmi[0], cmr[1], -cmi[1]),
                    lanes4(-cmi[0], -cmr[0], -cmi[1], -cmr[1])], axis=1)
    fr, fi = jnp.moveaxis(pwr[1:t + 1, 0], 0, 1), jnp.moveaxis(pwi[1:t + 1, 0], 0, 1)
    gr, gi = jnp.moveaxis(pwr[1:t + 1][::-1, 1], 0, 1), jnp.moveaxis(pwi[1:t + 1][::-1, 1], 0, 1)
    bt = lambda z: jnp.swapaxes(z, -1, -2)
    bb = jnp.stack([lanes4(bt(bbr[0]), bt(bbi[0]), bt(bbr[1]), bt(bbi[1])),
                    lanes4(-bt(bbi[0]), bt(bbr[0]), -bt(bbi[1]), bt(bbr[1]))], axis=1)
    wr, wi = jnp.moveaxis(pwr[:t][::-1, 0], 0, 1), jnp.moveaxis(pwi[:t][::-1, 0], 0, 1)
    vr, vi = jnp.moveaxis(pwr[:t, 1], 0, 1), jnp.moveaxis(pwi[:t, 1], 0, 1)
    pw = jnp.stack([lanes4(fr, fr, gr, gr), lanes4(fi, fi, gi, gi),
                    lanes4(wr, wr, vr, vr), lanes4(wi, wi, vi, vi)], axis=1)

    dr, di = pwr[t], pwi[t]
    a_rows, b_rows = [], []
    for _ in range(N_SCAN_LEVELS):
        a_rows.append(lanes4(dr[0], dr[0], dr[1], dr[1]))
        b_rows.append(lanes4(-di[0], di[0], -di[1], di[1]))
        dr, di = dr * dr - di * di, 2.0 * dr * di
    pad = [jnp.zeros_like(a_rows[0])] * (8 - N_SCAN_LEVELS)
    dtab = jnp.stack(a_rows + pad + b_rows + pad, axis=1)
    return cm, bb, pw, dtab, dd


def _na_bias_tables(rpb):
    c = jnp.arange(GRID_W)
    col_start = jnp.clip(c - NA_KW // 2, 0, GRID_W - NA_KW)
    in_win = (c[:, None] >= col_start[None, :]) & (c[:, None] < col_start[None, :] + NA_KW)
    rel_col = jnp.clip(c[:, None] - c[None, :] + (NA_KW - 1), 0, 2 * NA_KW - 2)
    bc = rpb.astype(F32)[:, :, rel_col]
    bias = jnp.concatenate([bc[0::2], bc[1::2]], axis=-1) * LOG2E
    bias = jnp.pad(bias, ((0, 0), (0, 1), (0, 0), (0, 0)), constant_values=-jnp.inf)
    cmask = jnp.concatenate([in_win, in_win], axis=-1).astype(F32)
    return bias, cmask


def kernel(x, mem, g_mix, g_mem, g_ffn, g_final, w_in, w_gate, b_gate, rpb, w_mem_kv,
           a_re, a_im, log_dt, b_re, b_im, c_re, c_im, s5_d, w_glu, w_branch, w_o,
           w_ffn1, w_ffn3, w_ffn2):
    bsz, s, d = x.shape
    n = bsz * s
    x2d = x.reshape(n, d)
    gm = g_mix[0].reshape(1, d).astype(F32)

    wi = w_in[0]
    w_nat = jnp.concatenate([wi[:, 512:1024], wi[:, 2048:2560]], axis=1).astype(BF16)
    w_chm = jnp.concatenate([wi[:, 0:512] * (64 ** -0.5 * LOG2E), wi[:, 1024:1536]], axis=1).T.astype(BF16)
    wu_t = wi[:, 1536:2048].T.astype(BF16)
    cm_tab, bb_tab, pw_tab, dtab, dd_tab = _s5_tables(a_re[0], a_im[0], log_dt[0], b_re[0], b_im[0],
                                                      c_re[0], c_im[0], s5_d[0])
    bias, cmask = _na_bias_tables(rpb[0])

    k, qm, qt, vt = _proj(x2d, gm, w_nat, w_chm)

    ut = _proj_s5t(x.reshape(n // CHUNK, CHUNK, d), gm, wu_t)
    zt = _s5_core(ut, cm_tab, bb_tab, pw_tab, dtab, dd_tab)
    y_s5 = _s5_glu(zt, w_glu[0].T.astype(BF16)).reshape(n, 512)

    y_na = _na(qt, k.reshape(bsz, s, 512), vt, bias, cmask).reshape(n, 512)

    k_mem, v_mem = _memkv(mem, g_mem[0].reshape(1, d).astype(F32), w_mem_kv[0].astype(BF16))
    y_mem = _mem_attn(qm.reshape(bsz, s, 512), k_mem, v_mem).reshape(n, 512)

    x1 = _merge(x2d, gm, y_na, y_s5, y_mem, w_gate[0].astype(BF16),
                b_gate[0].reshape(1, 3 * d).astype(F32), w_branch[0].astype(BF16), w_o[0].astype(BF16))
    out = _ffn(x1, g_ffn[0].reshape(1, d).astype(F32), g_final.reshape(1, d).astype(F32),
               w_ffn1[0].astype(BF16), w_ffn3[0].astype(BF16), w_ffn2[0].astype(BF16))
    return out.reshape(bsz, s, d)
```

```python
import functools

import jax
import jax.numpy as jnp
from jax import lax
from jax.experimental import pallas as pl
from jax.experimental.pallas import tpu as pltpu

F32 = jnp.float32
BF16 = jnp.bfloat16

D = 1024
GRID_W = 64
NA_HEADS = 8
NA_KH = 8
NA_KW = 16
S5_GROUPS = 32
S5_GROUP = 16
S5_STATE = 64
CHUNK = 64
MEM_HEADS = 4
MEM_HEAD_DIM = 128
D_FF = 2816
EPS = 1e-6
NEG_INF = -1e30

VMEM_LIMIT = 56 * 1024 * 1024

NT_DIMS = (((1,), (1,)), ((), ()))


def _cparams(n_axes):
    return pltpu.CompilerParams(
        dimension_semantics=("arbitrary",) * n_axes,
        vmem_limit_bytes=VMEM_LIMIT)


def _rms(x, g):
    return x * lax.rsqrt(jnp.mean(x * x, axis=-1, keepdims=True) + EPS) * g


def _const_spec(shape):
    nd = len(shape)
    return pl.BlockSpec(shape, lambda *_: (0,) * nd, pipeline_mode=pl.Buffered(1))


TOK_BLK = 128


def _proj_kernel(x_ref, g_ref, w_ref, wt_ref, k_ref, qm_ref, qt_ref, vt_ref):
    h = _rms(x_ref[...], g_ref[...]).astype(BF16)
    p = jnp.dot(h, w_ref[...], preferred_element_type=F32)
    k_ref[...] = p[:, 0:512].astype(BF16)
    qm_ref[...] = p[:, 512:1024].astype(BF16)
    pt = lax.dot_general(wt_ref[...], h, NT_DIMS, preferred_element_type=F32)
    for i in range(x_ref.shape[0] // TOK_BLK):
        toks = slice(TOK_BLK * i, TOK_BLK * (i + 1))
        qt_ref[i] = pt[0:512, toks].astype(BF16)
        vt_ref[i] = pt[512:1024, toks].astype(BF16)


def _proj(x2d, g, w, wt):
    n = x2d.shape[0]
    tm = 512
    nat = jax.ShapeDtypeStruct((n, 512), BF16)
    chm = jax.ShapeDtypeStruct((n // TOK_BLK, 512, TOK_BLK), BF16)
    nat_spec = pl.BlockSpec((tm, 512), lambda i: (i, 0))
    chm_spec = pl.BlockSpec((tm // TOK_BLK, 512, TOK_BLK), lambda i: (i, 0, 0))
    return pl.pallas_call(
        _proj_kernel,
        grid=(n // tm,),
        in_specs=[pl.BlockSpec((tm, D), lambda i: (i, 0)),
                  _const_spec((1, D)),
                  _const_spec((D, 1024)),
                  _const_spec((1024, D))],
        out_specs=[nat_spec, nat_spec, chm_spec, chm_spec],
        out_shape=[nat, nat, chm, chm],
        compiler_params=_cparams(1),
        name="proj",
    )(x2d, g, w, wt)


S5T_DT = 8


def _proj_s5t_kernel(x_ref, g_ref, wt_ref, o_ref, h_s):
    wt = wt_ref[...]
    nrows = x_ref.shape[0]
    hn = _rms(x_ref[...], g_ref[...]).reshape(nrows * S5T_DT, D)
    for c in range(D // 128):
        h_s[c] = hn[:, 128 * c:128 * (c + 1)]
    for j in range(S5T_DT):
        h = jnp.concatenate([h_s[c, pl.ds(j, nrows, stride=S5T_DT), :] for c in range(D // 128)],
                            axis=1).astype(BF16)
        ut = lax.dot_general(wt, h, NT_DIMS, preferred_element_type=F32)
        o_ref[:, j * S5_GROUP:(j + 1) * S5_GROUP, :] = (
            ut.astype(BF16).reshape(S5_GROUPS, S5_GROUP, 256))


def _proj_s5t(xc, g, wt):
    nrows = xc.shape[0]
    return pl.pallas_call(
        _proj_s5t_kernel,
        grid=(CHUNK // S5T_DT,),
        in_specs=[pl.BlockSpec((nrows, S5T_DT, D), lambda i: (0, i, 0)),
                  _const_spec((1, D)),
                  _const_spec((512, D))],
        out_specs=pl.BlockSpec((S5_GROUPS, S5T_DT * S5_GROUP, nrows), lambda i: (0, i, 0)),
        out_shape=jax.ShapeDtypeStruct((S5_GROUPS, CHUNK * S5_GROUP, nrows), BF16),
        scratch_shapes=[pltpu.VMEM((D // 128, nrows * S5T_DT, 128), F32)],
        compiler_params=_cparams(1),
        name="proj_s5t",
    )(xc, g, wt)


N_SCAN_LEVELS = 6


def _gelu_tanh(x):
    c = 0.7978845608028654
    return 0.5 * x * (1.0 + jnp.tanh(c * (x + 0.044715 * (x * x * x))))


def _split_bf16(x):
    hi = x.astype(BF16)
    return hi, (x - hi.astype(F32)).astype(BF16)


def _nt_f32(a, b):
    ah, al = _split_bf16(a)
    bh, bl = _split_bf16(b)
    nt = lambda u, w: lax.dot_general(u, w, NT_DIMS, preferred_element_type=F32)
    return nt(ah, bh) + nt(ah, bl) + nt(al, bh)


def _s5_kernel(ut_ref, cm_ref, bb_ref, pw_ref, d_ref, dd_ref, z_ref, m_s, n_s, pt_s, ptf_s, rs_s):
    tc = CHUNK * S5_GROUP
    cma, cmb = cm_ref[0, 0], cm_ref[0, 1]
    bba, bbb = bb_ref[0, 0], bb_ref[0, 1]
    for t in range(CHUNK):
        rows = slice(S5_GROUP * t, S5_GROUP * (t + 1))
        n_s[rows, :] = (cma * pw_ref[0, 0, t:t + 1, :] + cmb * pw_ref[0, 1, t:t + 1, :]).astype(BF16)
        ptf_s[rows, :] = bba * pw_ref[0, 2, t:t + 1, :] + bbb * pw_ref[0, 3, t:t + 1, :]
    ptf = ptf_s[...]
    pt_s[...] = ptf.astype(BF16)

    ra = _nt_f32(cma[:, 0:128], ptf[:, 0:128])
    rb = _nt_f32(cma[:, 128:256], ptf[:, 128:256])
    zeros = jnp.zeros((S5_GROUP, tc), F32)
    r = (jnp.concatenate([ra, zeros], axis=1)
         + pltpu.roll(jnp.concatenate([rb, zeros], axis=1), (CHUNK - 1) * S5_GROUP, axis=1)
         + jnp.concatenate([zeros[:, 0:tc - 128], dd_ref[0], zeros], axis=1))
    for k in range(8):
        rk = r if k == 0 else pltpu.roll(r, 2 * tc - S5_GROUP * k, axis=1)
        rs_s[k] = rk[:, 0:RS_LANES].astype(BF16)
    for t in range(CHUNK):
        a, k = divmod(CHUNK - 1 - t, 8)
        m_s[S5_GROUP * t:S5_GROUP * (t + 1), :] = rs_s[k, :, 128 * a:128 * a + tc]

    ut = ut_ref[0]
    nrows = ut.shape[1]
    y = jnp.dot(m_s[...], ut, preferred_element_type=F32)
    eye = jnp.where(lax.broadcasted_iota(jnp.int32, (nrows, nrows), 0)
                    == lax.broadcasted_iota(jnp.int32, (nrows, nrows), 1), 1.0, 0.0).astype(BF16)
    u_rows = lax.dot_general(eye, ut, NT_DIMS, preferred_element_type=F32).astype(BF16)
    v = jnp.dot(u_rows, pt_s[...], preferred_element_type=F32)
    pos = lax.broadcasted_iota(jnp.int32, (nrows, 128), 0) & (CHUNK - 1)

    def shift(x, s, up):
        if up:
            return jnp.where(pos < CHUNK - s, pltpu.roll(x, nrows - s, axis=0), 0.0)
        return jnp.where(pos >= s, pltpu.roll(x, s, axis=0), 0.0)

    def scan(vh, lanes, up):
        x = shift(vh, 1, up)
        for lvl in range(N_SCAN_LEVELS):
            xs = shift(x, 1 << lvl, up)
            a = d_ref[0, lvl:lvl + 1, lanes]
            b = d_ref[0, 8 + lvl:9 + lvl, lanes]
            x = x + a * xs + b * pltpu.roll(xs, 64, axis=1)
        return x

    xf = scan(v[:, 0:128], slice(0, 128), False)
    xb = scan(v[:, 128:256], slice(128, 256), True)
    xin = jnp.concatenate([xf, xb], axis=1).astype(BF16)
    y = y + lax.dot_general(n_s[...], xin, NT_DIMS, preferred_element_type=F32)
    z_ref[0] = _gelu_tanh(y).astype(BF16)


RS_LANES = 128 * 7 + CHUNK * S5_GROUP


def _s5_core(ut, cm, bb, pw, dtab, dd):
    nrows = ut.shape[2]
    tc = CHUNK * S5_GROUP
    return pl.pallas_call(
        _s5_kernel,
        grid=(S5_GROUPS,),
        in_specs=[pl.BlockSpec((1, tc, nrows), lambda g: (g, 0, 0)),
                  pl.BlockSpec((1, 2, S5_GROUP, 256), lambda g: (g, 0, 0, 0)),
                  pl.BlockSpec((1, 2, S5_GROUP, 256), lambda g: (g, 0, 0, 0)),
                  pl.BlockSpec((1, 4, CHUNK, 256), lambda g: (g, 0, 0, 0)),
                  pl.BlockSpec((1, 16, 256), lambda g: (g, 0, 0)),
                  pl.BlockSpec((1, S5_GROUP, 128), lambda g: (g, 0, 0))],
        out_specs=pl.BlockSpec((1, tc, nrows), lambda g: (g, 0, 0)),
        out_shape=jax.ShapeDtypeStruct((S5_GROUPS, tc, nrows), BF16),
        scratch_shapes=[pltpu.VMEM((tc, tc), BF16),
                        pltpu.VMEM((tc, 256), BF16),
                        pltpu.VMEM((tc, 256), BF16),
                        pltpu.VMEM((tc, 256), F32),
                        pltpu.VMEM((8, S5_GROUP, RS_LANES), BF16)],
        compiler_params=_cparams(1),
        name="s5_core",
    )(ut, cm, bb, pw, dtab, dd)


def _s5_glu_kernel(z_ref, wt_ref, o_ref):
    wt = wt_ref[...]
    nrows = z_ref.shape[2]
    eye = jnp.where(lax.broadcasted_iota(jnp.int32, (nrows, nrows), 0)
                    == lax.broadcasted_iota(jnp.int32, (nrows, nrows), 1), 1.0, 0.0).astype(BF16)
    for j in range(S5T_DT):
        zt = z_ref[:, j * S5_GROUP:(j + 1) * S5_GROUP, :].reshape(512, nrows)
        gl = jnp.dot(wt, zt, preferred_element_type=F32)
        o = (zt.astype(F32) * jax.nn.sigmoid(gl)).astype(BF16)
        nat = lax.dot_general(eye, o, NT_DIMS, preferred_element_type=F32)
        o_ref[:, j, :] = nat.astype(BF16)


def _s5_glu(zt, wglu_t):
    nrows = zt.shape[2]
    return pl.pallas_call(
        _s5_glu_kernel,
        grid=(CHUNK // S5T_DT,),
        in_specs=[pl.BlockSpec((S5_GROUPS, S5T_DT * S5_GROUP, nrows), lambda i: (0, i, 0)),
                  _const_spec((512, 512))],
        out_specs=pl.BlockSpec((nrows, S5T_DT, 512), lambda i: (0, i, 0)),
        out_shape=jax.ShapeDtypeStruct((nrows, CHUNK, 512), BF16),
        compiler_params=_cparams(1),
        name="s5_glu",
    )(zt, wglu_t)


NA_ROWS_PER_STEP = 16
NA_PAIRS_PER_STEP = NA_ROWS_PER_STEP // 2
NA_WIN_ROWS = 10
NA_WIN_KEYS = NA_WIN_ROWS * GRID_W
NA_WIN_BLKS = NA_WIN_KEYS // TOK_BLK
NA_RR_OUTSIDE = 2 * NA_KH - 1
LOG2E = 1.4426950408889634


def _na_kernel(rpb_ref, qt_ref, k_ref, vt_ref, oob_ref, o_ref, *scratch):
    st_s, pt_s, bias_s = (scratch[0:4], scratch[4:8]), scratch[8:12], scratch[12]
    rb = pl.program_id(1)
    lane = lax.broadcasted_iota(jnp.int32, (128, 128), 1)
    row = lax.broadcasted_iota(jnp.int32, (128, 128), 0)
    lo = lane < 64
    lo_q = lax.broadcasted_iota(jnp.int32, (GRID_W, 128), 1) < 64
    same_head = (row < 64) == lo
    eye = jnp.where(lax.broadcasted_iota(jnp.int32, (256, 256), 0)
                    == lax.broadcasted_iota(jnp.int32, (256, 256), 1), 1.0, 0.0).astype(BF16)
    half = NA_KH // 2

    kcol = lax.broadcasted_iota(jnp.int32, (GRID_W, 128), 0)
    qcol = lax.broadcasted_iota(jnp.int32, (GRID_W, 128), 1) & (GRID_W - 1)
    qstart = jnp.clip(qcol - NA_KW // 2, 0, GRID_W - NA_KW)
    col_ok = (kcol >= qstart) & (kcol < qstart + NA_KW)

    @pl.when((pl.program_id(0) == 0) & (rb == 0))
    def _build_bias():
        rel = jnp.clip(kcol - qcol + (NA_KW - 1), 0, 2 * NA_KW - 2)

        def rr_body(rr, carry):
            for s in range(4):
                acc = jnp.zeros((GRID_W, 128), F32)
                for j in range(2 * NA_KW - 1):
                    acc = jnp.where(rel == j, jnp.where(lo_q, rpb_ref[2 * s, rr, j], rpb_ref[2 * s + 1, rr, j]), acc)
                bias_s[s, rr] = acc * LOG2E
            return carry

        lax.fori_loop(0, NA_RR_OUTSIDE, rr_body, 0)
        for s in range(4):
            bias_s[s, NA_RR_OUTSIDE] = oob_ref[...]

    def window(pi):
        r0 = rb * NA_ROWS_PER_STEP + 2 * pi
        return r0, jnp.clip(r0 - half, 0, GRID_W - NA_WIN_ROWS)

    def q_stage(pi, par):
        _, win = window(pi)
        koff = pl.multiple_of(win * GRID_W, TOK_BLK)
        for s in range(4):
            ch = slice(128 * s, 128 * (s + 1))
            a = qt_ref[pi, ch, :].astype(F32)
            ar = pltpu.roll(a, 64, axis=1)
            w = jnp.concatenate([jnp.where(same_head, jnp.where(lo, a, ar), 0.0),
                                 jnp.where(same_head, jnp.where(lo, ar, a), 0.0)], axis=1).astype(BF16)
            kw = k_ref[0, pl.ds(koff, NA_WIN_KEYS), ch]
            st_s[par][s][...] = jnp.dot(kw, w, preferred_element_type=F32)

    def s_stage(pi, par):
        r0, win = window(pi)
        rr = []
        for p in range(2):
            r = r0 + p
            first = jnp.clip(r - half, 0, GRID_W - NA_KH) - win
            rel = win - r + (NA_KH - 1)
            rr.append([jnp.where((wr >= first) & (wr < first + NA_KH), wr + rel, NA_RR_OUTSIDE)
                       for wr in range(NA_WIN_ROWS)])
        sums = []
        for s in range(4):
            st = st_s[par][s]
            slab_sums = []
            for p in range(2):
                lanes = slice(128 * p, 128 * (p + 1))
                m = None
                for wr in range(NA_WIN_ROWS):
                    rows = slice(GRID_W * wr, GRID_W * (wr + 1))
                    t = jnp.where(col_ok, st[rows, lanes] + bias_s[s, rr[p][wr]], NEG_INF)
                    st[rows, lanes] = t
                    m = t if m is None else jnp.maximum(m, t)
                m = jnp.max(m, axis=0, keepdims=True)
                l = None
                for wr in range(NA_WIN_ROWS):
                    rows = slice(GRID_W * wr, GRID_W * (wr + 1))
                    e = jnp.exp2(st[rows, lanes] - m)
                    pt_s[s][rows, lanes] = e.astype(BF16)
                    l = e if l is None else l + e
                slab_sums.append(jnp.sum(l, axis=0, keepdims=True))
            sums.append(jnp.concatenate(slab_sums, axis=1))
        return tuple(sums)

    def o_stage_pv(pi, sums):
        _, win = window(pi)
        blk0 = lax.shift_right_logical(win, 1)
        ots = []
        for s in range(4):
            ch = slice(128 * s, 128 * (s + 1))
            vw = jnp.concatenate([vt_ref[blk0 + i, ch, :] for i in range(NA_WIN_BLKS)], axis=1)
            ot = jnp.dot(vw, pt_s[s][...], preferred_element_type=F32)
            ots.append((ot / sums[s]).astype(BF16))
        return ots

    def o_stage_store(pi, ots):
        for s in range(4):
            ch = slice(128 * s, 128 * (s + 1))
            tt = lax.dot_general(eye, ots[s], NT_DIMS, preferred_element_type=F32)
            for p in range(2):
                nat = jnp.where(lo_q, tt[128 * p:128 * p + 64], tt[128 * p + 64:128 * p + 128])
                qoff = pl.multiple_of((2 * pi + p) * GRID_W, GRID_W)
                o_ref[0, pl.ds(qoff, GRID_W), ch] = nat.astype(BF16)

    def step(pi, par, sums_prev, has_prev=True):
        ots = o_stage_pv(pi - 1, sums_prev) if has_prev else None
        q_stage(pi, par)
        if has_prev:
            o_stage_store(pi - 1, ots)
        return s_stage(pi, par)

    def two_steps(j, sums):
        sums = step(2 * j + 1, 1, sums)
        return step(2 * j + 2, 0, sums)

    last = NA_PAIRS_PER_STEP - 1
    sums = step(0, 0, None, has_prev=False)
    sums = lax.fori_loop(0, (last - 1) // 2, two_steps, sums)
    sums = step(last, 1, sums)
    o_stage_store(last, o_stage_pv(last, sums))


def _na(rpb, qt, k, vt):
    b, s, _ = k.shape
    tq = NA_ROWS_PER_STEP * GRID_W
    blks = s // TOK_BLK
    oob = jnp.full((GRID_W, 128), -jnp.inf, F32)
    return pl.pallas_call(
        _na_kernel,
        grid=(b, s // tq),
        in_specs=[pl.BlockSpec(memory_space=pltpu.SMEM),
                  pl.BlockSpec((NA_PAIRS_PER_STEP, 512, TOK_BLK),
                               lambda bi, ri: (bi * (blks // NA_PAIRS_PER_STEP) + ri, 0, 0)),
                  pl.BlockSpec((1, s, 512), lambda bi, ri: (bi, 0, 0)),
                  pl.BlockSpec((blks, 512, TOK_BLK), lambda bi, ri: (bi, 0, 0)),
                  _const_spec(oob.shape)],
        out_specs=pl.BlockSpec((1, tq, 512), lambda bi, ri: (bi, ri, 0)),
        out_shape=jax.ShapeDtypeStruct((b, s, 512), BF16),
        scratch_shapes=([pltpu.VMEM((NA_WIN_KEYS, 256), F32)] * 8
                        + [pltpu.VMEM((NA_WIN_KEYS, 256), BF16)] * 4
                        + [pltpu.VMEM((NA_HEADS // 2, 2 * NA_KH, GRID_W, 128), F32)]),
        compiler_params=_cparams(2),
        name="na",
    )(rpb, qt, k, vt, oob)


def _memkv_kernel(mem_ref, g_ref, w_ref, k_ref, v_ref):
    mn = _rms(mem_ref[0], g_ref[...]).astype(BF16)
    kv = jnp.dot(mn, w_ref[...], preferred_element_type=F32)
    k_ref[0] = kv[:, 0:512].astype(BF16)
    v_ref[0] = kv[:, 512:1024].astype(BF16)


def _memkv(mem, g, w):
    b, m, _ = mem.shape
    out = jax.ShapeDtypeStruct((b, m, 512), BF16)
    ospec = pl.BlockSpec((1, m, 512), lambda i: (i, 0, 0))
    return pl.pallas_call(
        _memkv_kernel,
        grid=(b,),
        in_specs=[pl.BlockSpec((1, m, D), lambda i: (i, 0, 0)),
                  _const_spec((1, D)),
                  _const_spec((D, 1024))],
        out_specs=[ospec, ospec],
        out_shape=[out, out],
        compiler_params=_cparams(1),
        name="memkv",
    )(mem, g, w)


def _mem_kernel(q_ref, k_ref, v_ref, o_ref):
    scale = MEM_HEAD_DIM ** -0.5
    for h in range(MEM_HEADS):
        cols = slice(MEM_HEAD_DIM * h, MEM_HEAD_DIM * (h + 1))
        sc = lax.dot_general(q_ref[0, :, cols], k_ref[0, :, cols], NT_DIMS,
                             preferred_element_type=F32) * scale
        m = jnp.max(sc, axis=-1, keepdims=True)
        p = jnp.exp(sc - m)
        l = jnp.sum(p, axis=-1, keepdims=True)
        o = jnp.dot(p.astype(BF16), v_ref[0, :, cols], preferred_element_type=F32)
        o_ref[0, :, cols] = (o / l).astype(BF16)


def _mem_attn(q, k, v):
    b, s, _ = q.shape
    m = k.shape[1]
    tq = 1024
    return pl.pallas_call(
        _mem_kernel,
        grid=(b, s // tq),
        in_specs=[pl.BlockSpec((1, tq, 512), lambda bi, i: (bi, i, 0)),
                  pl.BlockSpec((1, m, 512), lambda bi, i: (bi, 0, 0)),
                  pl.BlockSpec((1, m, 512), lambda bi, i: (bi, 0, 0))],
        out_specs=pl.BlockSpec((1, tq, 512), lambda bi, i: (bi, i, 0)),
        out_shape=jax.ShapeDtypeStruct((b, s, 512), BF16),
        compiler_params=_cparams(2),
        name="mem_attn",
    )(q, k, v)


def _merge_kernel(x_ref, g_ref, yna_ref, ys5_ref, ymem_ref, wg_ref, bg_ref, wb_ref, wo_ref, o_ref):
    x = x_ref[...]
    h = _rms(x, g_ref[...]).astype(BF16)
    merged = None
    for b, y_ref in enumerate((yna_ref, ys5_ref, ymem_ref)):
        cols = slice(D * b, D * (b + 1))
        gate = jax.nn.sigmoid(jnp.dot(h, wg_ref[:, cols], preferred_element_type=F32) + bg_ref[:, cols])
        up = jnp.dot(y_ref[...], wb_ref[b], preferred_element_type=F32)
        merged = gate * up if merged is None else merged + gate * up
    o_ref[...] = x + jnp.dot(merged.astype(BF16), wo_ref[...], preferred_element_type=F32)


def _merge(x2d, g, yna, ys5, ymem, wg, bg, wb, wo):
    n = x2d.shape[0]
    tm = 512
    yspec = pl.BlockSpec((tm, 512), lambda i: (i, 0))
    return pl.pallas_call(
        _merge_kernel,
        grid=(n // tm,),
        in_specs=[pl.BlockSpec((tm, D), lambda i: (i, 0)),
                  _const_spec((1, D)),
                  yspec, yspec, yspec,
                  _const_spec((D, 3 * D)),
                  _const_spec((1, 3 * D)),
                  _const_spec((3, 512, D)),
                  _const_spec((D, D))],
        out_specs=pl.BlockSpec((tm, D), lambda i: (i, 0)),
        out_shape=jax.ShapeDtypeStruct((n, D), F32),
        compiler_params=_cparams(1),
        name="merge",
    )(x2d, g, yna, ys5, ymem, wg, bg, wb, wo)


def _ffn_kernel(x_ref, g_ref, gf_ref, w1_ref, w3_ref, w2_ref, o_ref):
    x = x_ref[...]
    h = _rms(x, g_ref[...]).astype(BF16)
    a = jnp.dot(h, w1_ref[...], preferred_element_type=F32)
    c = jnp.dot(h, w3_ref[...], preferred_element_type=F32)
    mid = (a * jax.nn.sigmoid(a) * c).astype(BF16)
    x2 = x + jnp.dot(mid, w2_ref[...], preferred_element_type=F32)
    o_ref[...] = _rms(x2, gf_ref[...])


def _ffn(x2d, g, gf, w1, w3, w2):
    n = x2d.shape[0]
    tm = 512
    return pl.pallas_call(
        _ffn_kernel,
        grid=(n // tm,),
        in_specs=[pl.BlockSpec((tm, D), lambda i: (i, 0)),
                  _const_spec((1, D)),
                  _const_spec((1, D)),
                  _const_spec((D, D_FF)),
                  _const_spec((D, D_FF)),
                  _const_spec((D_FF, D))],
        out_specs=pl.BlockSpec((tm, D), lambda i: (i, 0)),
        out_shape=jax.ShapeDtypeStruct((n, D), F32),
        compiler_params=_cparams(1),
        name="ffn",
    )(x2d, g, gf, w1, w3, w2)


def _s5_tables(a_re, a_im, log_dt, b_re, b_im, c_re, c_im, s5_d):
    t = CHUNK
    ar, ai = a_re.astype(F32), a_im.astype(F32)
    dt = jnp.exp(log_dt.astype(F32))[..., None]
    lr, li = ar * dt, ai * dt
    mag = jnp.exp(lr)
    lbr, lbi = mag * jnp.cos(li), mag * jnp.sin(li)
    den = ar * ar + ai * ai
    rr = ((lbr - 1.0) * ar + lbi * ai) / den
    ri = (lbi * ar - (lbr - 1.0) * ai) / den
    br, bi = b_re.astype(F32), b_im.astype(F32)
    bbr = rr[..., None] * br - ri[..., None] * bi
    bbi = rr[..., None] * bi + ri[..., None] * br
    cmr, cmi = c_re.astype(F32), c_im.astype(F32)
    tau = jnp.arange(t + 1, dtype=F32)[:, None, None, None]
    pmag = jnp.exp(tau * lr[None])
    pwr, pwi = pmag * jnp.cos(tau * li[None]), pmag * jnp.sin(tau * li[None])

    dd = jnp.pad(jnp.eye(S5_GROUP, dtype=F32)[None] * s5_d.astype(F32).reshape(S5_GROUPS, S5_GROUP, 1),
                 ((0, 0), (0, 0), (128 - S5_GROUP, 0)))

    def lanes4(f0, f1, b0, b1):
        return jnp.concatenate([f0, f1, b0, b1], axis=-1)

    cm = jnp.stack([lanes4(cmr[0], -cmi[0], cmr[1], -cmi[1]),
                    lanes4(-cmi[0], -cmr[0], -cmi[1], -cmr[1])], axis=1)
    fr, fi = jnp.moveaxis(pwr[1:t + 1, 0], 0, 1), jnp.moveaxis(pwi[1:t + 1, 0], 0, 1)
    gr, gi = jnp.moveaxis(pwr[1:t + 1][::-1, 1], 0, 1), jnp.moveaxis(pwi[1:t + 1][::-1, 1], 0, 1)
    bt = lambda z: jnp.swapaxes(z, -1, -2)
    bb = jnp.stack([lanes4(bt(bbr[0]), bt(bbi[0]), bt(bbr[1]), bt(bbi[1])),
                    lanes4(-bt(bbi[0]), bt(bbr[0]), -bt(bbi[1]), bt(bbr[1]))], axis=1)
    wr, wi = jnp.moveaxis(pwr[:t][::-1, 0], 0, 1), jnp.moveaxis(pwi[:t][::-1, 0], 0, 1)
    vr, vi = jnp.moveaxis(pwr[:t, 1], 0, 1), jnp.moveaxis(pwi[:t, 1], 0, 1)
    pw = jnp.stack([lanes4(fr, fr, gr, gr), lanes4(fi, fi, gi, gi),
                    lanes4(wr, wr, vr, vr), lanes4(wi, wi, vi, vi)], axis=1)

    dr, di = pwr[t], pwi[t]
    a_rows, b_rows = [], []
    for _ in range(N_SCAN_LEVELS):
        a_rows.append(lanes4(dr[0], dr[0], dr[1], dr[1]))
        b_rows.append(lanes4(-di[0], di[0], -di[1], di[1]))
        dr, di = dr * dr - di * di, 2.0 * dr * di
    pad = [jnp.zeros_like(a_rows[0])] * (8 - N_SCAN_LEVELS)
    dtab = jnp.stack(a_rows + pad + b_rows + pad, axis=1)
    return cm, bb, pw, dtab, dd


def kernel(x, mem, g_mix, g_mem, g_ffn, g_final, w_in, w_gate, b_gate, rpb, w_mem_kv,
           a_re, a_im, log_dt, b_re, b_im, c_re, c_im, s5_d, w_glu, w_branch, w_o,
           w_ffn1, w_ffn3, w_ffn2):
    bsz, s, d = x.shape
    n = bsz * s
    x2d = x.reshape(n, d)
    gm = g_mix[0].reshape(1, d).astype(F32)

    wi = w_in[0]
    w_nat = jnp.concatenate([wi[:, 512:1024], wi[:, 2048:2560]], axis=1).astype(BF16)
    w_chm = jnp.concatenate([wi[:, 0:512] * (64 ** -0.5 * LOG2E), wi[:, 1024:1536]], axis=1).T.astype(BF16)
    wu_t = wi[:, 1536:2048].T.astype(BF16)
    cm_tab, bb_tab, pw_tab, dtab, dd_tab = _s5_tables(a_re[0], a_im[0], log_dt[0], b_re[0], b_im[0],
                                                      c_re[0], c_im[0], s5_d[0])

    k, qm, qt, vt = _proj(x2d, gm, w_nat, w_chm)

    ut = _proj_s5t(x.reshape(n // CHUNK, CHUNK, d), gm, wu_t)
    zt = _s5_core(ut, cm_tab, bb_tab, pw_tab, dtab, dd_tab)
    y_s5 = _s5_glu(zt, w_glu[0].T.astype(BF16)).reshape(n, 512)

    y_na = _na(rpb[0].astype(F32), qt, k.reshape(bsz, s, 512), vt).reshape(n, 512)

    k_mem, v_mem = _memkv(mem, g_mem[0].reshape(1, d).astype(F32), w_mem_kv[0].astype(BF16))
    y_mem = _mem_attn(qm.reshape(bsz, s, 512), k_mem, v_mem).reshape(n, 512)

    x1 = _merge(x2d, gm, y_na, y_s5, y_mem, w_gate[0].astype(BF16),
                b_gate[0].reshape(1, 3 * d).astype(F32), w_branch[0].astype(BF16), w_o[0].astype(BF16))
    out = _ffn(x1, g_ffn[0].reshape(1, d).astype(F32), g_final.reshape(1, d).astype(F32),
               w_ffn1[0].astype(BF16), w_ffn3[0].astype(BF16), w_ffn2[0].astype(BF16))
    return out.reshape(bsz, s, d)
```

```python
import functools

import jax
import jax.numpy as jnp
from jax import lax
from jax.experimental import pallas as pl
from jax.experimental.pallas import tpu as pltpu

F32 = jnp.float32
BF16 = jnp.bfloat16

D = 1024
GRID_W = 64
NA_HEADS = 8
NA_KH = 8
NA_KW = 16
S5_GROUPS = 32
S5_GROUP = 16
S5_STATE = 64
CHUNK = 64
MEM_HEADS = 4
MEM_HEAD_DIM = 128
D_FF = 2816
EPS = 1e-6
NEG_INF = -1e30

VMEM_LIMIT = 56 * 1024 * 1024

NT_DIMS = (((1,), (1,)), ((), ()))


def _cparams(n_axes):
    return pltpu.CompilerParams(
        dimension_semantics=("arbitrary",) * n_axes,
        vmem_limit_bytes=VMEM_LIMIT)


def _rms(x, g):
    return x * lax.rsqrt(jnp.mean(x * x, axis=-1, keepdims=True) + EPS) * g


def _const_spec(shape):
    nd = len(shape)
    return pl.BlockSpec(shape, lambda *_: (0,) * nd, pipeline_mode=pl.Buffered(1))


TOK_BLK = 128


def _proj_kernel(x_ref, g_ref, w_ref, wt_ref, k_ref, qm_ref, qw_ref, vt_ref):
    h = _rms(x_ref[...], g_ref[...]).astype(BF16)
    p = jnp.dot(h, w_ref[...], preferred_element_type=F32)
    k_ref[...] = p[:, 0:512].astype(BF16)
    qm_ref[...] = p[:, 512:1024].astype(BF16)
    pt = lax.dot_general(wt_ref[...], h, NT_DIMS, preferred_element_type=F32)
    lo = lax.broadcasted_iota(jnp.int32, (128, 128), 1) < 64
    same_head = (lax.broadcasted_iota(jnp.int32, (128, 128), 0) < 64) == lo
    for i in range(x_ref.shape[0] // TOK_BLK):
        toks = slice(TOK_BLK * i, TOK_BLK * (i + 1))
        vt_ref[i] = pt[512:1024, toks].astype(BF16)
        for s in range(4):
            a = pt[128 * s:128 * (s + 1), toks]
            ar = pltpu.roll(a, 64, axis=1)
            qw_ref[i, 128 * s:128 * (s + 1), :] = jnp.concatenate(
                [jnp.where(same_head, jnp.where(lo, a, ar), 0.0),
                 jnp.where(same_head, jnp.where(lo, ar, a), 0.0)], axis=1).astype(BF16)


def _proj(x2d, g, w, wt):
    n = x2d.shape[0]
    tm = 512
    nat = jax.ShapeDtypeStruct((n, 512), BF16)
    nat_spec = pl.BlockSpec((tm, 512), lambda i: (i, 0))
    chm = lambda width: (jax.ShapeDtypeStruct((n // TOK_BLK, 512, width), BF16),
                         pl.BlockSpec((tm // TOK_BLK, 512, width), lambda i: (i, 0, 0)))
    (qw_shape, qw_spec), (vt_shape, vt_spec) = chm(2 * TOK_BLK), chm(TOK_BLK)
    return pl.pallas_call(
        _proj_kernel,
        grid=(n // tm,),
        in_specs=[pl.BlockSpec((tm, D), lambda i: (i, 0)),
                  _const_spec((1, D)),
                  _const_spec((D, 1024)),
                  _const_spec((1024, D))],
        out_specs=[nat_spec, nat_spec, qw_spec, vt_spec],
        out_shape=[nat, nat, qw_shape, vt_shape],
        compiler_params=_cparams(1),
        name="proj",
    )(x2d, g, w, wt)


S5T_DT = 8


def _proj_s5t_kernel(x_ref, g_ref, wt_ref, o_ref, h_s):
    wt = wt_ref[...]
    nrows = x_ref.shape[0]
    hn = _rms(x_ref[...], g_ref[...]).reshape(nrows * S5T_DT, D)
    for c in range(D // 128):
        h_s[c] = hn[:, 128 * c:128 * (c + 1)]
    for j in range(S5T_DT):
        h = jnp.concatenate([h_s[c, pl.ds(j, nrows, stride=S5T_DT), :] for c in range(D // 128)],
                            axis=1).astype(BF16)
        ut = lax.dot_general(wt, h, NT_DIMS, preferred_element_type=F32)
        o_ref[:, j * S5_GROUP:(j + 1) * S5_GROUP, :] = (
            ut.astype(BF16).reshape(S5_GROUPS, S5_GROUP, 256))


def _proj_s5t(xc, g, wt):
    nrows = xc.shape[0]
    return pl.pallas_call(
        _proj_s5t_kernel,
        grid=(CHUNK // S5T_DT,),
        in_specs=[pl.BlockSpec((nrows, S5T_DT, D), lambda i: (0, i, 0)),
                  _const_spec((1, D)),
                  _const_spec((512, D))],
        out_specs=pl.BlockSpec((S5_GROUPS, S5T_DT * S5_GROUP, nrows), lambda i: (0, i, 0)),
        out_shape=jax.ShapeDtypeStruct((S5_GROUPS, CHUNK * S5_GROUP, nrows), BF16),
        scratch_shapes=[pltpu.VMEM((D // 128, nrows * S5T_DT, 128), F32)],
        compiler_params=_cparams(1),
        name="proj_s5t",
    )(xc, g, wt)


N_SCAN_LEVELS = 6


def _gelu_tanh(x):
    c = 0.7978845608028654
    return 0.5 * x * (1.0 + jnp.tanh(c * (x + 0.044715 * (x * x * x))))


def _split_bf16(x):
    hi = x.astype(BF16)
    return hi, (x - hi.astype(F32)).astype(BF16)


def _nt_f32(a, b):
    ah, al = _split_bf16(a)
    bh, bl = _split_bf16(b)
    nt = lambda u, w: lax.dot_general(u, w, NT_DIMS, preferred_element_type=F32)
    return nt(ah, bh) + nt(ah, bl) + nt(al, bh)


S5_GROUPS_PER_STEP = 2
S5_SCRATCH_PER_GROUP = 5


def _s5_kernel(ut_ref, cm_ref, bb_ref, pw_ref, d_ref, dd_ref, z_ref, *scratch):
    groups = [scratch[S5_SCRATCH_PER_GROUP * gi:S5_SCRATCH_PER_GROUP * (gi + 1)]
              for gi in range(S5_GROUPS_PER_STEP)]
    for gi, (m_s, n_s, pt_s, ptf_s, rs_s) in enumerate(groups):
        _s5_operators(gi, cm_ref, bb_ref, pw_ref, dd_ref, m_s, n_s, pt_s, ptf_s, rs_s)
    yv = [_s5_chunk_matmuls(gi, ut_ref, m_s, pt_s) for gi, (m_s, _, pt_s, _, _) in enumerate(groups)]
    for gi, (_, n_s, _, _, _) in enumerate(groups):
        _s5_finish(gi, d_ref, z_ref, n_s, *yv[gi])


def _s5_operators(gi, cm_ref, bb_ref, pw_ref, dd_ref, m_s, n_s, pt_s, ptf_s, rs_s):
    tc = CHUNK * S5_GROUP
    cma, cmb = cm_ref[gi,0], cm_ref[gi,1]
    bba, bbb = bb_ref[gi,0], bb_ref[gi,1]
    for t in range(CHUNK):
        rows = slice(S5_GROUP * t, S5_GROUP * (t + 1))
        n_s[rows, :] = (cma * pw_ref[gi,0, t:t + 1, :] + cmb * pw_ref[gi,1, t:t + 1, :]).astype(BF16)
        ptf_s[rows, :] = bba * pw_ref[gi,2, t:t + 1, :] + bbb * pw_ref[gi,3, t:t + 1, :]
    ptf = ptf_s[...]
    pt_s[...] = ptf.astype(BF16)

    ra = _nt_f32(cma[:, 0:128], ptf[:, 0:128])
    rb = _nt_f32(cma[:, 128:256], ptf[:, 128:256])
    zeros = jnp.zeros((S5_GROUP, tc), F32)
    r = (jnp.concatenate([ra, zeros], axis=1)
         + pltpu.roll(jnp.concatenate([rb, zeros], axis=1), (CHUNK - 1) * S5_GROUP, axis=1)
         + jnp.concatenate([zeros[:, 0:tc - 128], dd_ref[gi], zeros], axis=1))
    for k in range(8):
        rk = r if k == 0 else pltpu.roll(r, 2 * tc - S5_GROUP * k, axis=1)
        rs_s[k] = rk[:, 0:RS_LANES].astype(BF16)
    for t in range(CHUNK):
        a, k = divmod(CHUNK - 1 - t, 8)
        m_s[S5_GROUP * t:S5_GROUP * (t + 1), :] = rs_s[k, :, 128 * a:128 * a + tc]


def _s5_chunk_matmuls(gi, ut_ref, m_s, pt_s):
    ut = ut_ref[gi]
    nrows = ut.shape[1]
    eye = jnp.where(lax.broadcasted_iota(jnp.int32, (nrows, nrows), 0)
                    == lax.broadcasted_iota(jnp.int32, (nrows, nrows), 1), 1.0, 0.0).astype(BF16)
    u_rows = lax.dot_general(eye, ut, NT_DIMS, preferred_element_type=F32).astype(BF16)
    y = jnp.dot(m_s[...], ut, preferred_element_type=F32)
    v = jnp.dot(u_rows, pt_s[...], preferred_element_type=F32)
    return y, v


def _s5_finish(gi, d_ref, z_ref, n_s, y, v):
    nrows = v.shape[0]
    pos = lax.broadcasted_iota(jnp.int32, (nrows, 128), 0) & (CHUNK - 1)

    def shift(x, s, up):
        if up:
            return jnp.where(pos < CHUNK - s, pltpu.roll(x, nrows - s, axis=0), 0.0)
        return jnp.where(pos >= s, pltpu.roll(x, s, axis=0), 0.0)

    def scan(vh, lanes, up):
        x = shift(vh, 1, up)
        for lvl in range(N_SCAN_LEVELS):
            xs = shift(x, 1 << lvl, up)
            a = d_ref[gi,lvl:lvl + 1, lanes]
            b = d_ref[gi,8 + lvl:9 + lvl, lanes]
            x = x + a * xs + b * pltpu.roll(xs, 64, axis=1)
        return x

    xf = scan(v[:, 0:128], slice(0, 128), False)
    xb = scan(v[:, 128:256], slice(128, 256), True)
    xin = jnp.concatenate([xf, xb], axis=1).astype(BF16)
    y = y + lax.dot_general(n_s[...], xin, NT_DIMS, preferred_element_type=F32)
    z_ref[gi] =_gelu_tanh(y).astype(BF16)


RS_LANES = 128 * 7 + CHUNK * S5_GROUP


def _s5_core(ut, cm, bb, pw, dtab, dd):
    nrows = ut.shape[2]
    tc = CHUNK * S5_GROUP
    gs = S5_GROUPS_PER_STEP
    return pl.pallas_call(
        _s5_kernel,
        grid=(S5_GROUPS // gs,),
        in_specs=[pl.BlockSpec((gs, tc, nrows), lambda g: (g, 0, 0)),
                  pl.BlockSpec((gs, 2, S5_GROUP, 256), lambda g: (g, 0, 0, 0)),
                  pl.BlockSpec((gs, 2, S5_GROUP, 256), lambda g: (g, 0, 0, 0)),
                  pl.BlockSpec((gs, 4, CHUNK, 256), lambda g: (g, 0, 0, 0)),
                  pl.BlockSpec((gs, 16, 256), lambda g: (g, 0, 0)),
                  pl.BlockSpec((gs, S5_GROUP, 128), lambda g: (g, 0, 0))],
        out_specs=pl.BlockSpec((gs, tc, nrows), lambda g: (g, 0, 0)),
        out_shape=jax.ShapeDtypeStruct((S5_GROUPS, tc, nrows), BF16),
        scratch_shapes=[pltpu.VMEM((tc, tc), BF16),
                        pltpu.VMEM((tc, 256), BF16),
                        pltpu.VMEM((tc, 256), BF16),
                        pltpu.VMEM((tc, 256), F32),
                        pltpu.VMEM((8, S5_GROUP, RS_LANES), BF16),
                        ] * gs,
        compiler_params=_cparams(1),
        name="s5_core",
    )(ut, cm, bb, pw, dtab, dd)


def _s5_glu_kernel(z_ref, wt_ref, o_ref):
    wt = wt_ref[...]
    nrows = z_ref.shape[2]
    eye = jnp.where(lax.broadcasted_iota(jnp.int32, (nrows, nrows), 0)
                    == lax.broadcasted_iota(jnp.int32, (nrows, nrows), 1), 1.0, 0.0).astype(BF16)
    for j in range(S5T_DT):
        zt = z_ref[:, j * S5_GROUP:(j + 1) * S5_GROUP, :].reshape(512, nrows)
        gl = jnp.dot(wt, zt, preferred_element_type=F32)
        o = (zt.astype(F32) * jax.nn.sigmoid(gl)).astype(BF16)
        nat = lax.dot_general(eye, o, NT_DIMS, preferred_element_type=F32)
        o_ref[:, j, :] = nat.astype(BF16)


def _s5_glu(zt, wglu_t):
    nrows = zt.shape[2]
    return pl.pallas_call(
        _s5_glu_kernel,
        grid=(CHUNK // S5T_DT,),
        in_specs=[pl.BlockSpec((S5_GROUPS, S5T_DT * S5_GROUP, nrows), lambda i: (0, i, 0)),
                  _const_spec((512, 512))],
        out_specs=pl.BlockSpec((nrows, S5T_DT, 512), lambda i: (0, i, 0)),
        out_shape=jax.ShapeDtypeStruct((nrows, CHUNK, 512), BF16),
        compiler_params=_cparams(1),
        name="s5_glu",
    )(zt, wglu_t)


NA_ROWS_PER_STEP = 16
NA_PAIRS_PER_STEP = NA_ROWS_PER_STEP // 2
NA_WIN_ROWS = 10
NA_WIN_KEYS = NA_WIN_ROWS * GRID_W
NA_WIN_BLKS = NA_WIN_KEYS // TOK_BLK
NA_RR_OUTSIDE = 2 * NA_KH - 1
LOG2E = 1.4426950408889634


def _na_kernel(rpb_ref, qw_ref, k_ref, vt_ref, oob_ref, o_ref, *scratch):
    st_s, pt_s, bias_s = (scratch[0:4], scratch[4:8]), scratch[8:12], scratch[12]
    rb = pl.program_id(1)
    lo_q = lax.broadcasted_iota(jnp.int32, (GRID_W, 128), 1) < 64
    same_head2 = ((lax.broadcasted_iota(jnp.int32, (128, 256), 0) < 64)
                  == ((lax.broadcasted_iota(jnp.int32, (128, 256), 1) & 64) == 0))
    esum = jnp.where(lax.broadcasted_iota(jnp.int32, (GRID_W, 128), 0)
                     == (lax.broadcasted_iota(jnp.int32, (GRID_W, 128), 1) & (GRID_W - 1)),
                     1.0, 0.0).astype(BF16)
    half = NA_KH // 2

    kcol = lax.broadcasted_iota(jnp.int32, (GRID_W, 128), 0)
    qcol = lax.broadcasted_iota(jnp.int32, (GRID_W, 128), 1) & (GRID_W - 1)
    qstart = jnp.clip(qcol - NA_KW // 2, 0, GRID_W - NA_KW)
    col_ok = (kcol >= qstart) & (kcol < qstart + NA_KW)

    @pl.when((pl.program_id(0) == 0) & (rb == 0))
    def _build_bias():
        rel = jnp.clip(kcol - qcol + (NA_KW - 1), 0, 2 * NA_KW - 2)

        def rr_body(rr, carry):
            for s in range(4):
                acc = jnp.zeros((GRID_W, 128), F32)
                for j in range(2 * NA_KW - 1):
                    acc = jnp.where(rel == j, jnp.where(lo_q, rpb_ref[2 * s, rr, j], rpb_ref[2 * s + 1, rr, j]), acc)
                bias_s[s, rr] = acc * LOG2E
            return carry

        lax.fori_loop(0, NA_RR_OUTSIDE, rr_body, 0)
        for s in range(4):
            bias_s[s, NA_RR_OUTSIDE] = oob_ref[...]

    def window(pi):
        r0 = rb * NA_ROWS_PER_STEP + 2 * pi
        return r0, jnp.clip(r0 - half, 0, GRID_W - NA_WIN_ROWS)

    def q_stage(pi, par):
        _, win = window(pi)
        koff = pl.multiple_of(win * GRID_W, TOK_BLK)
        for s in range(4):
            ch = slice(128 * s, 128 * (s + 1))
            kw = k_ref[0, pl.ds(koff, NA_WIN_KEYS), ch]
            st_s[par][s][...] = jnp.dot(kw, qw_ref[pi, ch, :], preferred_element_type=F32)

    def s_stage(pi, par):
        r0, win = window(pi)
        rr = []
        for p in range(2):
            r = r0 + p
            first = jnp.clip(r - half, 0, GRID_W - NA_KH) - win
            rel = win - r + (NA_KH - 1)
            rr.append([jnp.where((wr >= first) & (wr < first + NA_KH), wr + rel, NA_RR_OUTSIDE)
                       for wr in range(NA_WIN_ROWS)])
        sums = []
        for s in range(4):
            st = st_s[par][s]
            slab_sums = []
            for p in range(2):
                lanes = slice(128 * p, 128 * (p + 1))
                m = None
                for wr in range(NA_WIN_ROWS):
                    rows = slice(GRID_W * wr, GRID_W * (wr + 1))
                    t = jnp.where(col_ok, st[rows, lanes] + bias_s[s, rr[p][wr]], NEG_INF)
                    st[rows, lanes] = t
                    m = t if m is None else jnp.maximum(m, t)
                m = jnp.max(m, axis=0, keepdims=True)
                l = None
                for wr in range(NA_WIN_ROWS):
                    rows = slice(GRID_W * wr, GRID_W * (wr + 1))
                    e = jnp.exp2(st[rows, lanes] - m)
                    pt_s[s][rows, lanes] = e.astype(BF16)
                    l = e if l is None else l + e
                slab_sums.append(jnp.sum(l, axis=0, keepdims=True))
            sums.append(jnp.concatenate(slab_sums, axis=1))
        return tuple(sums)

    def o_stage_pv(pi, sums):
        _, win = window(pi)
        blk0 = lax.shift_right_logical(win, 1)
        ots = []
        for s in range(4):
            ch = slice(128 * s, 128 * (s + 1))
            vw = jnp.concatenate([vt_ref[blk0 + i, ch, :] for i in range(NA_WIN_BLKS)], axis=1)
            ot = jnp.dot(vw, pt_s[s][...], preferred_element_type=F32)
            ots.append(jnp.where(same_head2, (ot / sums[s]).astype(BF16), jnp.zeros((), BF16)))
        return ots

    def o_stage_store(pi, ots):
        for s in range(4):
            ch = slice(128 * s, 128 * (s + 1))
            for p in range(2):
                nat = lax.dot_general(esum, ots[s][:, 128 * p:128 * (p + 1)], NT_DIMS,
                                      preferred_element_type=F32)
                qoff = pl.multiple_of((2 * pi + p) * GRID_W, GRID_W)
                o_ref[0, pl.ds(qoff, GRID_W), ch] = nat.astype(BF16)

    def step(pi, par, sums_prev, has_prev=True):
        ots = o_stage_pv(pi - 1, sums_prev) if has_prev else None
        q_stage(pi, par)
        if has_prev:
            o_stage_store(pi - 1, ots)
        return s_stage(pi, par)

    def two_steps(j, sums):
        sums = step(2 * j + 1, 1, sums)
        return step(2 * j + 2, 0, sums)

    last = NA_PAIRS_PER_STEP - 1
    sums = step(0, 0, None, has_prev=False)
    sums = lax.fori_loop(0, (last - 1) // 2, two_steps, sums)
    sums = step(last, 1, sums)
    o_stage_store(last, o_stage_pv(last, sums))


def _na(rpb, qw, k, vt):
    b, s, _ = k.shape
    tq = NA_ROWS_PER_STEP * GRID_W
    blks = s // TOK_BLK
    oob = jnp.full((GRID_W, 128), -jnp.inf, F32)
    return pl.pallas_call(
        _na_kernel,
        grid=(b, s // tq),
        in_specs=[pl.BlockSpec(memory_space=pltpu.SMEM),
                  pl.BlockSpec((NA_PAIRS_PER_STEP, 512, 2 * TOK_BLK),
                               lambda bi, ri: (bi * (blks // NA_PAIRS_PER_STEP) + ri, 0, 0)),
                  pl.BlockSpec((1, s, 512), lambda bi, ri: (bi, 0, 0)),
                  pl.BlockSpec((blks, 512, TOK_BLK), lambda bi, ri: (bi, 0, 0)),
                  _const_spec(oob.shape)],
        out_specs=pl.BlockSpec((1, tq, 512), lambda bi, ri: (bi, ri, 0)),
        out_shape=jax.ShapeDtypeStruct((b, s, 512), BF16),
        scratch_shapes=([pltpu.VMEM((NA_WIN_KEYS, 256), F32)] * 8
                        + [pltpu.VMEM((NA_WIN_KEYS, 256), BF16)] * 4
                        + [pltpu.VMEM((NA_HEADS // 2, 2 * NA_KH, GRID_W, 128), F32)]),
        compiler_params=_cparams(2),
        name="na",
    )(rpb, qw, k, vt, oob)


def _memkv_kernel(mem_ref, g_ref, w_ref, k_ref, v_ref):
    mn = _rms(mem_ref[0], g_ref[...]).astype(BF16)
    kv = jnp.dot(mn, w_ref[...], preferred_element_type=F32)
    k_ref[0] = kv[:, 0:512].astype(BF16)
    v_ref[0] = kv[:, 512:1024].astype(BF16)


def _memkv(mem, g, w):
    b, m, _ = mem.shape
    out = jax.ShapeDtypeStruct((b, m, 512), BF16)
    ospec = pl.BlockSpec((1, m, 512), lambda i: (i, 0, 0))
    return pl.pallas_call(
        _memkv_kernel,
        grid=(b,),
        in_specs=[pl.BlockSpec((1, m, D), lambda i: (i, 0, 0)),
                  _const_spec((1, D)),
                  _const_spec((D, 1024))],
        out_specs=[ospec, ospec],
        out_shape=[out, out],
        compiler_params=_cparams(1),
        name="memkv",
    )(mem, g, w)


def _mem_kernel(q_ref, k_ref, v_ref, o_ref):
    scale = MEM_HEAD_DIM ** -0.5
    for h in range(MEM_HEADS):
        cols = slice(MEM_HEAD_DIM * h, MEM_HEAD_DIM * (h + 1))
        sc = lax.dot_general(q_ref[0, :, cols], k_ref[0, :, cols], NT_DIMS,
                             preferred_element_type=F32) * scale
        m = jnp.max(sc, axis=-1, keepdims=True)
        p = jnp.exp(sc - m)
        l = jnp.sum(p, axis=-1, keepdims=True)
        o = jnp.dot(p.astype(BF16), v_ref[0, :, cols], preferred_element_type=F32)
        o_ref[0, :, cols] = (o / l).astype(BF16)


def _mem_attn(q, k, v):
    b, s, _ = q.shape
    m = k.shape[1]
    tq = 1024
    return pl.pallas_call(
        _mem_kernel,
        grid=(b, s // tq),
        in_specs=[pl.BlockSpec((1, tq, 512), lambda bi, i: (bi, i, 0)),
                  pl.BlockSpec((1, m, 512), lambda bi, i: (bi, 0, 0)),
                  pl.BlockSpec((1, m, 512), lambda bi, i: (bi, 0, 0))],
        out_specs=pl.BlockSpec((1, tq, 512), lambda bi, i: (bi, i, 0)),
        out_shape=jax.ShapeDtypeStruct((b, s, 512), BF16),
        compiler_params=_cparams(2),
        name="mem_attn",
    )(q, k, v)


def _merge_kernel(x_ref, g_ref, yna_ref, ys5_ref, ymem_ref, wg_ref, bg_ref, wb_ref, wo_ref, o_ref):
    x = x_ref[...]
    h = _rms(x, g_ref[...]).astype(BF16)
    merged = None
    for b, y_ref in enumerate((yna_ref, ys5_ref, ymem_ref)):
        cols = slice(D * b, D * (b + 1))
        gate = jax.nn.sigmoid(jnp.dot(h, wg_ref[:, cols], preferred_element_type=F32) + bg_ref[:, cols])
        up = jnp.dot(y_ref[...], wb_ref[b], preferred_element_type=F32)
        merged = gate * up if merged is None else merged + gate * up
    o_ref[...] = x + jnp.dot(merged.astype(BF16), wo_ref[...], preferred_element_type=F32)


def _merge(x2d, g, yna, ys5, ymem, wg, bg, wb, wo):
    n = x2d.shape[0]
    tm = 512
    yspec = pl.BlockSpec((tm, 512), lambda i: (i, 0))
    return pl.pallas_call(
        _merge_kernel,
        grid=(n // tm,),
        in_specs=[pl.BlockSpec((tm, D), lambda i: (i, 0)),
                  _const_spec((1, D)),
                  yspec, yspec, yspec,
                  _const_spec((D, 3 * D)),
                  _const_spec((1, 3 * D)),
                  _const_spec((3, 512, D)),
                  _const_spec((D, D))],
        out_specs=pl.BlockSpec((tm, D), lambda i: (i, 0)),
        out_shape=jax.ShapeDtypeStruct((n, D), F32),
        compiler_params=_cparams(1),
        name="merge",
    )(x2d, g, yna, ys5, ymem, wg, bg, wb, wo)


def _ffn_kernel(x_ref, g_ref, gf_ref, w1_ref, w3_ref, w2_ref, o_ref):
    x = x_ref[...]
    h = _rms(x, g_ref[...]).astype(BF16)
    a = jnp.dot(h, w1_ref[...], preferred_element_type=F32)
    c = jnp.dot(h, w3_ref[...], preferred_element_type=F32)
    mid = (a * jax.nn.sigmoid(a) * c).astype(BF16)
    x2 = x + jnp.dot(mid, w2_ref[...], preferred_element_type=F32)
    o_ref[...] = _rms(x2, gf_ref[...])


def _ffn(x2d, g, gf, w1, w3, w2):
    n = x2d.shape[0]
    tm = 512
    return pl.pallas_call(
        _ffn_kernel,
        grid=(n // tm,),
        in_specs=[pl.BlockSpec((tm, D), lambda i: (i, 0)),
                  _const_spec((1, D)),
                  _const_spec((1, D)),
                  _const_spec((D, D_FF)),
                  _const_spec((D, D_FF)),
                  _const_spec((D_FF, D))],
        out_specs=pl.BlockSpec((tm, D), lambda i: (i, 0)),
        out_shape=jax.ShapeDtypeStruct((n, D), F32),
        compiler_params=_cparams(1),
        name="ffn",
    )(x2d, g, gf, w1, w3, w2)


def _s5_tables(a_re, a_im, log_dt, b_re, b_im, c_re, c_im, s5_d):
    t = CHUNK
    ar, ai = a_re.astype(F32), a_im.astype(F32)
    dt = jnp.exp(log_dt.astype(F32))[..., None]
    lr, li = ar * dt, ai * dt
    mag = jnp.exp(lr)
    lbr, lbi = mag * jnp.cos(li), mag * jnp.sin(li)
    den = ar * ar + ai * ai
    rr = ((lbr - 1.0) * ar + lbi * ai) / den
    ri = (lbi * ar - (lbr - 1.0) * ai) / den
    br, bi = b_re.astype(F32), b_im.astype(F32)
    bbr = rr[..., None] * br - ri[..., None] * bi
    bbi = rr[..., None] * bi + ri[..., None] * br
    cmr, cmi = c_re.astype(F32), c_im.astype(F32)
    tau = jnp.arange(t + 1, dtype=F32)[:, None, None, None]
    pmag = jnp.exp(tau * lr[None])
    pwr, pwi = pmag * jnp.cos(tau * li[None]), pmag * jnp.sin(tau * li[None])

    dd = jnp.pad(jnp.eye(S5_GROUP, dtype=F32)[None] * s5_d.astype(F32).reshape(S5_GROUPS, S5_GROUP, 1),
                 ((0, 0), (0, 0), (128 - S5_GROUP, 0)))

    def lanes4(f0, f1, b0, b1):
        return jnp.concatenate([f0, f1, b0, b1], axis=-1)

    cm = jnp.stack([lanes4(cmr[0], -cmi[0], cmr[1], -cmi[1]),
                    lanes4(-cmi[0], -cmr[0], -cmi[1], -cmr[1])], axis=1)
    fr, fi = jnp.moveaxis(pwr[1:t + 1, 0], 0, 1), jnp.moveaxis(pwi[1:t + 1, 0], 0, 1)
    gr, gi = jnp.moveaxis(pwr[1:t + 1][::-1, 1], 0, 1), jnp.moveaxis(pwi[1:t + 1][::-1, 1], 0, 1)
    bt = lambda z: jnp.swapaxes(z, -1, -2)
    bb = jnp.stack([lanes4(bt(bbr[0]), bt(bbi[0]), bt(bbr[1]), bt(bbi[1])),
                    lanes4(-bt(bbi[0]), bt(bbr[0]), -bt(bbi[1]), bt(bbr[1]))], axis=1)
    wr, wi = jnp.moveaxis(pwr[:t][::-1, 0], 0, 1), jnp.moveaxis(pwi[:t][::-1, 0], 0, 1)
    vr, vi = jnp.moveaxis(pwr[:t, 1], 0, 1), jnp.moveaxis(pwi[:t, 1], 0, 1)
    pw = jnp.stack([lanes4(fr, fr, gr, gr), lanes4(fi, fi, gi, gi),
                    lanes4(wr, wr, vr, vr), lanes4(wi, wi, vi, vi)], axis=1)

    dr, di = pwr[t], pwi[t]
    a_rows, b_rows = [], []
    for _ in range(N_SCAN_LEVELS):
        a_rows.append(lanes4(dr[0], dr[0], dr[1], dr[1]))
        b_rows.append(lanes4(-di[0], di[0], -di[1], di[1]))
        dr, di = dr * dr - di * di, 2.0 * dr * di
    pad = [jnp.zeros_like(a_rows[0])] * (8 - N_SCAN_LEVELS)
    dtab = jnp.stack(a_rows + pad + b_rows + pad, axis=1)
    return cm, bb, pw, dtab, dd


def kernel(x, mem, g_mix, g_mem, g_ffn, g_final, w_in, w_gate, b_gate, rpb, w_mem_kv,
           a_re, a_im, log_dt, b_re, b_im, c_re, c_im, s5_d, w_glu, w_branch, w_o,
           w_ffn1, w_ffn3, w_ffn2):
    bsz, s, d = x.shape
    n = bsz * s
    x2d = x.reshape(n, d)
    gm = g_mix[0].reshape(1, d).astype(F32)

    wi = w_in[0]
    w_nat = jnp.concatenate([wi[:, 512:1024], wi[:, 2048:2560]], axis=1).astype(BF16)
    w_chm = jnp.concatenate([wi[:, 0:512] * (64 ** -0.5 * LOG2E), wi[:, 1024:1536]], axis=1).T.astype(BF16)
    wu_t = wi[:, 1536:2048].T.astype(BF16)
    cm_tab, bb_tab, pw_tab, dtab, dd_tab = _s5_tables(a_re[0], a_im[0], log_dt[0], b_re[0], b_im[0],
                                                      c_re[0], c_im[0], s5_d[0])

    k, qm, qw, vt = _proj(x2d, gm, w_nat, w_chm)

    ut = _proj_s5t(x.reshape(n // CHUNK, CHUNK, d), gm, wu_t)
    zt = _s5_core(ut, cm_tab, bb_tab, pw_tab, dtab, dd_tab)
    y_s5 = _s5_glu(zt, w_glu[0].T.astype(BF16)).reshape(n, 512)

    y_na = _na(rpb[0].astype(F32), qw, k.reshape(bsz, s, 512), vt).reshape(n, 512)

    k_mem, v_mem = _memkv(mem, g_mem[0].reshape(1, d).astype(F32), w_mem_kv[0].astype(BF16))
    y_mem = _mem_attn(qm.reshape(bsz, s, 512), k_mem, v_mem).reshape(n, 512)

    x1 = _merge(x2d, gm, y_na, y_s5, y_mem, w_gate[0].astype(BF16),
                b_gate[0].reshape(1, 3 * d).astype(F32), w_branch[0].astype(BF16), w_o[0].astype(BF16))
    out = _ffn(x1, g_ffn[0].reshape(1, d).astype(F32), g_final.reshape(1, d).astype(F32),
               w_ffn1[0].astype(BF16), w_ffn3[0].astype(BF16), w_ffn2[0].astype(BF16))
    return out.reshape(bsz, s, d)
```

```python
import functools

import jax
import jax.numpy as jnp
from jax import lax
from jax.experimental import pallas as pl
from jax.experimental.pallas import tpu as pltpu

F32 = jnp.float32
BF16 = jnp.bfloat16

D = 1024
GRID_W = 64
NA_HEADS = 8
NA_KH = 8
NA_KW = 16
S5_GROUPS = 32
S5_GROUP = 16
S5_STATE = 64
CHUNK = 64
MEM_HEADS = 4
MEM_HEAD_DIM = 128
D_FF = 2816
EPS = 1e-6
NEG_INF = -1e30

VMEM_LIMIT = 56 * 1024 * 1024

NT_DIMS = (((1,), (1,)), ((), ()))


def _cparams(n_axes):
    return pltpu.CompilerParams(
        dimension_semantics=("arbitrary",) * n_axes,
        vmem_limit_bytes=VMEM_LIMIT)


def _rms(x, g):
    return x * lax.rsqrt(jnp.mean(x * x, axis=-1, keepdims=True) + EPS) * g


def _const_spec(shape):
    nd = len(shape)
    return pl.BlockSpec(shape, lambda *_: (0,) * nd, pipeline_mode=pl.Buffered(1))


TOK_BLK = 128


LOG2E = 1.4426950408889634
NA_Q_SCALE = 64 ** -0.5 * LOG2E


def _proj_kernel(x_ref, g_ref, wk_ref, wqm_ref, wqt_ref, wvt_ref, k_ref, qm_ref, qw_ref, vt_ref):
    h = _rms(x_ref[...], g_ref[...]).astype(BF16)
    k_ref[...] = jnp.dot(h, wk_ref[...], preferred_element_type=F32).astype(BF16)
    qm_ref[...] = jnp.dot(h, wqm_ref[...], preferred_element_type=F32).astype(BF16)
    qt = lax.dot_general(wqt_ref[...], h, NT_DIMS, preferred_element_type=F32) * NA_Q_SCALE
    vt = lax.dot_general(wvt_ref[...], h, NT_DIMS, preferred_element_type=F32)
    lo = lax.broadcasted_iota(jnp.int32, (128, 128), 1) < 64
    same_head = (lax.broadcasted_iota(jnp.int32, (128, 128), 0) < 64) == lo
    for i in range(x_ref.shape[0] // TOK_BLK):
        toks = slice(TOK_BLK * i, TOK_BLK * (i + 1))
        vt_ref[i] = vt[:, toks].astype(BF16)
        for s in range(4):
            a = qt[128 * s:128 * (s + 1), toks]
            ar = pltpu.roll(a, 64, axis=1)
            qw_ref[i, 128 * s:128 * (s + 1), :] = jnp.concatenate(
                [jnp.where(same_head, jnp.where(lo, a, ar), 0.0),
                 jnp.where(same_head, jnp.where(lo, ar, a), 0.0)], axis=1).astype(BF16)


W_IN_BLK = 512


def _w_in_spec(blk, transposed):
    if transposed:
        return pl.BlockSpec((W_IN_BLK, D), lambda *_: (blk, 0), pipeline_mode=pl.Buffered(1))
    return pl.BlockSpec((D, W_IN_BLK), lambda *_: (0, blk), pipeline_mode=pl.Buffered(1))


def _proj(x2d, g, w, wt):
    n = x2d.shape[0]
    tm = 512
    nat = jax.ShapeDtypeStruct((n, 512), BF16)
    nat_spec = pl.BlockSpec((tm, 512), lambda i: (i, 0))
    chm = lambda width: (jax.ShapeDtypeStruct((n // TOK_BLK, 512, width), BF16),
                         pl.BlockSpec((tm // TOK_BLK, 512, width), lambda i: (i, 0, 0)))
    (qw_shape, qw_spec), (vt_shape, vt_spec) = chm(2 * TOK_BLK), chm(TOK_BLK)
    return pl.pallas_call(
        _proj_kernel,
        grid=(n // tm,),
        in_specs=[pl.BlockSpec((tm, D), lambda i: (i, 0)),
                  _const_spec((1, D)),
                  _w_in_spec(1, False),
                  _w_in_spec(4, False),
                  _w_in_spec(0, True),
                  _w_in_spec(2, True)],
        out_specs=[nat_spec, nat_spec, qw_spec, vt_spec],
        out_shape=[nat, nat, qw_shape, vt_shape],
        compiler_params=_cparams(1),
        name="proj",
    )(x2d, g, w, w, wt, wt)


S5T_DT = 8


def _proj_s5t_kernel(x_ref, g_ref, wt_ref, o_ref, h_s):
    wt = wt_ref[...]
    nrows = x_ref.shape[0]
    hn = _rms(x_ref[...], g_ref[...]).reshape(nrows * S5T_DT, D)
    for c in range(D // 128):
        h_s[c] = hn[:, 128 * c:128 * (c + 1)]
    for j in range(S5T_DT):
        h = jnp.concatenate([h_s[c, pl.ds(j, nrows, stride=S5T_DT), :] for c in range(D // 128)],
                            axis=1).astype(BF16)
        ut = lax.dot_general(wt, h, NT_DIMS, preferred_element_type=F32)
        o_ref[:, j * S5_GROUP:(j + 1) * S5_GROUP, :] = (
            ut.astype(BF16).reshape(S5_GROUPS, S5_GROUP, 256))


def _proj_s5t(xc, g, wt):
    nrows = xc.shape[0]
    return pl.pallas_call(
        _proj_s5t_kernel,
        grid=(CHUNK // S5T_DT,),
        in_specs=[pl.BlockSpec((nrows, S5T_DT, D), lambda i: (0, i, 0)),
                  _const_spec((1, D)),
                  _w_in_spec(3, True)],
        out_specs=pl.BlockSpec((S5_GROUPS, S5T_DT * S5_GROUP, nrows), lambda i: (0, i, 0)),
        out_shape=jax.ShapeDtypeStruct((S5_GROUPS, CHUNK * S5_GROUP, nrows), BF16),
        scratch_shapes=[pltpu.VMEM((D // 128, nrows * S5T_DT, 128), F32)],
        compiler_params=_cparams(1),
        name="proj_s5t",
    )(xc, g, wt)


N_SCAN_LEVELS = 6


def _gelu_tanh(x):
    c = 0.7978845608028654
    return 0.5 * x * (1.0 + jnp.tanh(c * (x + 0.044715 * (x * x * x))))


def _split_bf16(x):
    hi = x.astype(BF16)
    return hi, (x - hi.astype(F32)).astype(BF16)


def _nt_f32(a, b):
    ah, al = _split_bf16(a)
    bh, bl = _split_bf16(b)
    nt = lambda u, w: lax.dot_general(u, w, NT_DIMS, preferred_element_type=F32)
    return nt(ah, bh) + nt(ah, bl) + nt(al, bh)


S5_GROUPS_PER_STEP = 2
S5_SCRATCH_PER_GROUP = 5


def _s5_kernel(ut_ref, cm_ref, bb_ref, pw_ref, d_ref, dd_ref, z_ref, *scratch):
    groups = [scratch[S5_SCRATCH_PER_GROUP * gi:S5_SCRATCH_PER_GROUP * (gi + 1)]
              for gi in range(S5_GROUPS_PER_STEP)]
    for gi, (m_s, n_s, pt_s, ptf_s, rs_s) in enumerate(groups):
        _s5_operators(gi, cm_ref, bb_ref, pw_ref, dd_ref, m_s, n_s, pt_s, ptf_s, rs_s)
    yv = [_s5_chunk_matmuls(gi, ut_ref, m_s, pt_s) for gi, (m_s, _, pt_s, _, _) in enumerate(groups)]
    for gi, (_, n_s, _, _, _) in enumerate(groups):
        _s5_finish(gi, d_ref, z_ref, n_s, *yv[gi])


def _s5_operators(gi, cm_ref, bb_ref, pw_ref, dd_ref, m_s, n_s, pt_s, ptf_s, rs_s):
    tc = CHUNK * S5_GROUP
    cma, cmb = cm_ref[gi,0], cm_ref[gi,1]
    bba, bbb = bb_ref[gi,0], bb_ref[gi,1]
    for t in range(CHUNK):
        rows = slice(S5_GROUP * t, S5_GROUP * (t + 1))
        n_s[rows, :] = (cma * pw_ref[gi,0, t:t + 1, :] + cmb * pw_ref[gi,1, t:t + 1, :]).astype(BF16)
        ptf_s[rows, :] = bba * pw_ref[gi,2, t:t + 1, :] + bbb * pw_ref[gi,3, t:t + 1, :]
    ptf = ptf_s[...]
    pt_s[...] = ptf.astype(BF16)

    ra = _nt_f32(cma[:, 0:128], ptf[:, 0:128])
    rb = _nt_f32(cma[:, 128:256], ptf[:, 128:256])
    zeros = jnp.zeros((S5_GROUP, tc), F32)
    r = (jnp.concatenate([ra, zeros], axis=1)
         + pltpu.roll(jnp.concatenate([rb, zeros], axis=1), (CHUNK - 1) * S5_GROUP, axis=1)
         + jnp.concatenate([zeros[:, 0:tc - 128], dd_ref[gi], zeros], axis=1))
    for k in range(8):
        rk = r if k == 0 else pltpu.roll(r, 2 * tc - S5_GROUP * k, axis=1)
        rs_s[k] = rk[:, 0:RS_LANES].astype(BF16)
    for t in range(CHUNK):
        a, k = divmod(CHUNK - 1 - t, 8)
        m_s[S5_GROUP * t:S5_GROUP * (t + 1), :] = rs_s[k, :, 128 * a:128 * a + tc]


def _s5_chunk_matmuls(gi, ut_ref, m_s, pt_s):
    ut = ut_ref[gi]
    nrows = ut.shape[1]
    eye = jnp.where(lax.broadcasted_iota(jnp.int32, (nrows, nrows), 0)
                    == lax.broadcasted_iota(jnp.int32, (nrows, nrows), 1), 1.0, 0.0).astype(BF16)
    u_rows = lax.dot_general(eye, ut, NT_DIMS, preferred_element_type=F32).astype(BF16)
    y = jnp.dot(m_s[...], ut, preferred_element_type=F32)
    v = jnp.dot(u_rows, pt_s[...], preferred_element_type=F32)
    return y, v


def _s5_finish(gi, d_ref, z_ref, n_s, y, v):
    nrows = v.shape[0]
    pos = lax.broadcasted_iota(jnp.int32, (nrows, 128), 0) & (CHUNK - 1)

    def shift(x, s, up):
        if up:
            return jnp.where(pos < CHUNK - s, pltpu.roll(x, nrows - s, axis=0), 0.0)
        return jnp.where(pos >= s, pltpu.roll(x, s, axis=0), 0.0)

    def scan(vh, lanes, up):
        x = shift(vh, 1, up)
        for lvl in range(N_SCAN_LEVELS):
            xs = shift(x, 1 << lvl, up)
            a = d_ref[gi,lvl:lvl + 1, lanes]
            b = d_ref[gi,8 + lvl:9 + lvl, lanes]
            x = x + a * xs + b * pltpu.roll(xs, 64, axis=1)
        return x

    xf = scan(v[:, 0:128], slice(0, 128), False)
    xb = scan(v[:, 128:256], slice(128, 256), True)
    xin = jnp.concatenate([xf, xb], axis=1).astype(BF16)
    y = y + lax.dot_general(n_s[...], xin, NT_DIMS, preferred_element_type=F32)
    z_ref[gi] =_gelu_tanh(y).astype(BF16)


RS_LANES = 128 * 7 + CHUNK * S5_GROUP


def _s5_core(ut, cm, bb, pw, dtab, dd):
    nrows = ut.shape[2]
    tc = CHUNK * S5_GROUP
    gs = S5_GROUPS_PER_STEP
    return pl.pallas_call(
        _s5_kernel,
        grid=(S5_GROUPS // gs,),
        in_specs=[pl.BlockSpec((gs, tc, nrows), lambda g: (g, 0, 0)),
                  pl.BlockSpec((gs, 2, S5_GROUP, 256), lambda g: (g, 0, 0, 0)),
                  pl.BlockSpec((gs, 2, S5_GROUP, 256), lambda g: (g, 0, 0, 0)),
                  pl.BlockSpec((gs, 4, CHUNK, 256), lambda g: (g, 0, 0, 0)),
                  pl.BlockSpec((gs, 16, 256), lambda g: (g, 0, 0)),
                  pl.BlockSpec((gs, S5_GROUP, 128), lambda g: (g, 0, 0))],
        out_specs=pl.BlockSpec((gs, tc, nrows), lambda g: (g, 0, 0)),
        out_shape=jax.ShapeDtypeStruct((S5_GROUPS, tc, nrows), BF16),
        scratch_shapes=[pltpu.VMEM((tc, tc), BF16),
                        pltpu.VMEM((tc, 256), BF16),
                        pltpu.VMEM((tc, 256), BF16),
                        pltpu.VMEM((tc, 256), F32),
                        pltpu.VMEM((8, S5_GROUP, RS_LANES), BF16),
                        ] * gs,
        compiler_params=_cparams(1),
        name="s5_core",
    )(ut, cm, bb, pw, dtab, dd)


def _s5_glu_kernel(z_ref, wt_ref, o_ref):
    wt = wt_ref[...]
    nrows = z_ref.shape[2]
    eye = jnp.where(lax.broadcasted_iota(jnp.int32, (nrows, nrows), 0)
                    == lax.broadcasted_iota(jnp.int32, (nrows, nrows), 1), 1.0, 0.0).astype(BF16)
    for j in range(S5T_DT):
        zt = z_ref[:, j * S5_GROUP:(j + 1) * S5_GROUP, :].reshape(512, nrows)
        gl = jnp.dot(wt, zt, preferred_element_type=F32)
        o = (zt.astype(F32) * jax.nn.sigmoid(gl)).astype(BF16)
        nat = lax.dot_general(eye, o, NT_DIMS, preferred_element_type=F32)
        o_ref[:, j, :] = nat.astype(BF16)


def _s5_glu(zt, wglu_t):
    nrows = zt.shape[2]
    return pl.pallas_call(
        _s5_glu_kernel,
        grid=(CHUNK // S5T_DT,),
        in_specs=[pl.BlockSpec((S5_GROUPS, S5T_DT * S5_GROUP, nrows), lambda i: (0, i, 0)),
                  _const_spec((512, 512))],
        out_specs=pl.BlockSpec((nrows, S5T_DT, 512), lambda i: (0, i, 0)),
        out_shape=jax.ShapeDtypeStruct((nrows, CHUNK, 512), BF16),
        compiler_params=_cparams(1),
        name="s5_glu",
    )(zt, wglu_t)


NA_ROWS_PER_STEP = 16
NA_PAIRS_PER_STEP = NA_ROWS_PER_STEP // 2
NA_WIN_ROWS = 10
NA_WIN_KEYS = NA_WIN_ROWS * GRID_W
NA_WIN_BLKS = NA_WIN_KEYS // TOK_BLK
NA_RR_OUTSIDE = 2 * NA_KH - 1


def _na_kernel(rpb_ref, qw_ref, k_ref, vt_ref, oob_ref, o_ref, *scratch):
    st_s, pt_s, bias_s = (scratch[0:4], scratch[4:8]), (scratch[8:12], scratch[12:16]), scratch[16]
    rb = pl.program_id(1)
    lo_q = lax.broadcasted_iota(jnp.int32, (GRID_W, 128), 1) < 64
    same_head2 = ((lax.broadcasted_iota(jnp.int32, (128, 256), 0) < 64)
                  == ((lax.broadcasted_iota(jnp.int32, (128, 256), 1) & 64) == 0))
    esum = jnp.where(lax.broadcasted_iota(jnp.int32, (GRID_W, 128), 0)
                     == (lax.broadcasted_iota(jnp.int32, (GRID_W, 128), 1) & (GRID_W - 1)),
                     1.0, 0.0).astype(BF16)
    half = NA_KH // 2

    kcol = lax.broadcasted_iota(jnp.int32, (GRID_W, 128), 0)
    qcol = lax.broadcasted_iota(jnp.int32, (GRID_W, 128), 1) & (GRID_W - 1)
    qstart = jnp.clip(qcol - NA_KW // 2, 0, GRID_W - NA_KW)
    col_ok = (kcol >= qstart) & (kcol < qstart + NA_KW)

    @pl.when((pl.program_id(0) == 0) & (rb == 0))
    def _build_bias():
        rel = jnp.clip(kcol - qcol + (NA_KW - 1), 0, 2 * NA_KW - 2)

        def rr_body(rr, carry):
            for s in range(4):
                acc = jnp.zeros((GRID_W, 128), F32)
                for j in range(2 * NA_KW - 1):
                    acc = jnp.where(rel == j, jnp.where(lo_q, rpb_ref[2 * s, rr, j], rpb_ref[2 * s + 1, rr, j]), acc)
                bias_s[s, rr] = acc * LOG2E
            return carry

        lax.fori_loop(0, NA_RR_OUTSIDE, rr_body, 0)
        for s in range(4):
            bias_s[s, NA_RR_OUTSIDE] = oob_ref[...]

    def window(pi):
        r0 = rb * NA_ROWS_PER_STEP + 2 * pi
        return r0, jnp.clip(r0 - half, 0, GRID_W - NA_WIN_ROWS)

    def q_stage(pi):
        _, win = window(pi)
        koff = pl.multiple_of(win * GRID_W, TOK_BLK)
        scs = []
        for s in range(4):
            ch = slice(128 * s, 128 * (s + 1))
            kw = k_ref[0, pl.ds(koff, NA_WIN_KEYS), ch]
            scs.append(jnp.dot(kw, qw_ref[pi, ch, :], preferred_element_type=F32))
        return scs

    def s_stage(pi, par, scs):
        r0, win = window(pi)
        rr = []
        for p in range(2):
            r = r0 + p
            first = jnp.clip(r - half, 0, GRID_W - NA_KH) - win
            rel = win - r + (NA_KH - 1)
            rr.append([jnp.where((wr >= first) & (wr < first + NA_KH), wr + rel, NA_RR_OUTSIDE)
                       for wr in range(NA_WIN_ROWS)])
        sums = []
        for s in range(4):
            st = st_s[par][s]
            slab_sums = []
            for p in range(2):
                lanes = slice(128 * p, 128 * (p + 1))
                m = None
                for wr in range(NA_WIN_ROWS):
                    rows = slice(GRID_W * wr, GRID_W * (wr + 1))
                    t = jnp.where(col_ok, scs[s][rows, lanes] + bias_s[s, rr[p][wr]], NEG_INF)
                    st[rows, lanes] = t
                    m = t if m is None else jnp.maximum(m, t)
                m = jnp.max(m, axis=0, keepdims=True)
                l = None
                for wr in range(NA_WIN_ROWS):
                    rows = slice(GRID_W * wr, GRID_W * (wr + 1))
                    e = jnp.exp2(st[rows, lanes] - m)
                    pt_s[par][s][rows, lanes] = e.astype(BF16)
                    l = e if l is None else l + e
                slab_sums.append(jnp.sum(l, axis=0, keepdims=True))
            sums.append(jnp.concatenate(slab_sums, axis=1))
        return tuple(sums)

    def o_stage_pv(pi, par, sums):
        _, win = window(pi)
        blk0 = lax.shift_right_logical(win, 1)
        ots = []
        for s in range(4):
            ch = slice(128 * s, 128 * (s + 1))
            vw = jnp.concatenate([vt_ref[blk0 + i, ch, :] for i in range(NA_WIN_BLKS)], axis=1)
            ot = jnp.dot(vw, pt_s[par][s][...], preferred_element_type=F32)
            ots.append(jnp.where(same_head2, (ot / sums[s]).astype(BF16), jnp.zeros((), BF16)))
        return ots

    def o_stage_store(pi, ots):
        for s in range(4):
            ch = slice(128 * s, 128 * (s + 1))
            for p in range(2):
                nat = lax.dot_general(esum, ots[s][:, 128 * p:128 * (p + 1)], NT_DIMS,
                                      preferred_element_type=F32)
                qoff = pl.multiple_of((2 * pi + p) * GRID_W, GRID_W)
                o_ref[0, pl.ds(qoff, GRID_W), ch] = nat.astype(BF16)

    def step(j, prev_sums, has_prev=True):
        a, b = 2 * j, 2 * j + 1
        scs_a = q_stage(a)
        if has_prev:
            ots_a = o_stage_pv(a - 2, 0, prev_sums[0])
            ots_b = o_stage_pv(b - 2, 1, prev_sums[1])
        scs_b = q_stage(b)
        if has_prev:
            o_stage_store(a - 2, ots_a)
            o_stage_store(b - 2, ots_b)
        return s_stage(a, 0, scs_a), s_stage(b, 1, scs_b)

    sums = step(0, None, has_prev=False)
    sums = lax.fori_loop(1, NA_PAIRS_PER_STEP // 2, step, sums)
    for u in range(2):
        pi = NA_PAIRS_PER_STEP - 2 + u
        o_stage_store(pi, o_stage_pv(pi, u, sums[u]))


def _na(rpb, qw, k, vt):
    b, s, _ = k.shape
    tq = NA_ROWS_PER_STEP * GRID_W
    blks = s // TOK_BLK
    oob = jnp.full((GRID_W, 128), -jnp.inf, F32)
    return pl.pallas_call(
        _na_kernel,
        grid=(b, s // tq),
        in_specs=[pl.BlockSpec(memory_space=pltpu.SMEM),
                  pl.BlockSpec((NA_PAIRS_PER_STEP, 512, 2 * TOK_BLK),
                               lambda bi, ri: (bi * (blks // NA_PAIRS_PER_STEP) + ri, 0, 0)),
                  pl.BlockSpec((1, s, 512), lambda bi, ri: (bi, 0, 0)),
                  pl.BlockSpec((blks, 512, TOK_BLK), lambda bi, ri: (bi, 0, 0)),
                  _const_spec(oob.shape)],
        out_specs=pl.BlockSpec((1, tq, 512), lambda bi, ri: (bi, ri, 0)),
        out_shape=jax.ShapeDtypeStruct((b, s, 512), BF16),
        scratch_shapes=([pltpu.VMEM((NA_WIN_KEYS, 256), F32)] * 8
                        + [pltpu.VMEM((NA_WIN_KEYS, 256), BF16)] * 8
                        + [pltpu.VMEM((NA_HEADS // 2, 2 * NA_KH, GRID_W, 128), F32)]),
        compiler_params=_cparams(2),
        name="na",
    )(rpb, qw, k, vt, oob)


def _memkv_kernel(mem_ref, g_ref, w_ref, k_ref, v_ref):
    mn = _rms(mem_ref[0], g_ref[...]).astype(BF16)
    kv = jnp.dot(mn, w_ref[...], preferred_element_type=F32)
    k_ref[0] = kv[:, 0:512].astype(BF16)
    v_ref[0] = kv[:, 512:1024].astype(BF16)


def _memkv(mem, g, w):
    b, m, _ = mem.shape
    out = jax.ShapeDtypeStruct((b, m, 512), BF16)
    ospec = pl.BlockSpec((1, m, 512), lambda i: (i, 0, 0))
    return pl.pallas_call(
        _memkv_kernel,
        grid=(b,),
        in_specs=[pl.BlockSpec((1, m, D), lambda i: (i, 0, 0)),
                  _const_spec((1, D)),
                  _const_spec((D, 1024))],
        out_specs=[ospec, ospec],
        out_shape=[out, out],
        compiler_params=_cparams(1),
        name="memkv",
    )(mem, g, w)


def _mem_kernel(q_ref, k_ref, v_ref, o_ref):
    scale = MEM_HEAD_DIM ** -0.5
    for h in range(MEM_HEADS):
        cols = slice(MEM_HEAD_DIM * h, MEM_HEAD_DIM * (h + 1))
        sc = lax.dot_general(q_ref[0, :, cols], k_ref[0, :, cols], NT_DIMS,
                             preferred_element_type=F32) * scale
        m = jnp.max(sc, axis=-1, keepdims=True)
        p = jnp.exp(sc - m)
        l = jnp.sum(p, axis=-1, keepdims=True)
        o = jnp.dot(p.astype(BF16), v_ref[0, :, cols], preferred_element_type=F32)
        o_ref[0, :, cols] = (o / l).astype(BF16)


def _mem_attn(q, k, v):
    b, s, _ = q.shape
    m = k.shape[1]
    tq = 1024
    return pl.pallas_call(
        _mem_kernel,
        grid=(b, s // tq),
        in_specs=[pl.BlockSpec((1, tq, 512), lambda bi, i: (bi, i, 0)),
                  pl.BlockSpec((1, m, 512), lambda bi, i: (bi, 0, 0)),
                  pl.BlockSpec((1, m, 512), lambda bi, i: (bi, 0, 0))],
        out_specs=pl.BlockSpec((1, tq, 512), lambda bi, i: (bi, i, 0)),
        out_shape=jax.ShapeDtypeStruct((b, s, 512), BF16),
        compiler_params=_cparams(2),
        name="mem_attn",
    )(q, k, v)


def _merge_kernel(x_ref, g_ref, yna_ref, ys5_ref, ymem_ref, wg_ref, bg_ref, wb_ref, wo_ref, o_ref):
    x = x_ref[...]
    h = _rms(x, g_ref[...]).astype(BF16)
    merged = None
    for b, y_ref in enumerate((yna_ref, ys5_ref, ymem_ref)):
        cols = slice(D * b, D * (b + 1))
        gate = jax.nn.sigmoid(jnp.dot(h, wg_ref[:, cols], preferred_element_type=F32) + bg_ref[:, cols])
        up = jnp.dot(y_ref[...], wb_ref[b], preferred_element_type=F32)
        merged = gate * up if merged is None else merged + gate * up
    o_ref[...] = x + jnp.dot(merged.astype(BF16), wo_ref[...], preferred_element_type=F32)


def _merge(x2d, g, yna, ys5, ymem, wg, bg, wb, wo):
    n = x2d.shape[0]
    tm = 512
    yspec = pl.BlockSpec((tm, 512), lambda i: (i, 0))
    return pl.pallas_call(
        _merge_kernel,
        grid=(n // tm,),
        in_specs=[pl.BlockSpec((tm, D), lambda i: (i, 0)),
                  _const_spec((1, D)),
                  yspec, yspec, yspec,
                  _const_spec((D, 3 * D)),
                  _const_spec((1, 3 * D)),
                  _const_spec((3, 512, D)),
                  _const_spec((D, D))],
        out_specs=pl.BlockSpec((tm, D), lambda i: (i, 0)),
        out_shape=jax.ShapeDtypeStruct((n, D), F32),
        compiler_params=_cparams(1),
        name="merge",
    )(x2d, g, yna, ys5, ymem, wg, bg, wb, wo)


def _ffn_kernel(x_ref, g_ref, gf_ref, w1_ref, w3_ref, w2_ref, o_ref):
    x = x_ref[...]
    h = _rms(x, g_ref[...]).astype(BF16)
    a = jnp.dot(h, w1_ref[...], preferred_element_type=F32)
    c = jnp.dot(h, w3_ref[...], preferred_element_type=F32)
    mid = (a * jax.nn.sigmoid(a) * c).astype(BF16)
    x2 = x + jnp.dot(mid, w2_ref[...], preferred_element_type=F32)
    o_ref[...] = _rms(x2, gf_ref[...])


def _ffn(x2d, g, gf, w1, w3, w2):
    n = x2d.shape[0]
    tm = 512
    return pl.pallas_call(
        _ffn_kernel,
        grid=(n // tm,),
        in_specs=[pl.BlockSpec((tm, D), lambda i: (i, 0)),
                  _const_spec((1, D)),
                  _const_spec((1, D)),
                  _const_spec((D, D_FF)),
                  _const_spec((D, D_FF)),
                  _const_spec((D_FF, D))],
        out_specs=pl.BlockSpec((tm, D), lambda i: (i, 0)),
        out_shape=jax.ShapeDtypeStruct((n, D), F32),
        compiler_params=_cparams(1),
        name="ffn",
    )(x2d, g, gf, w1, w3, w2)


def _s5_tables(a_re, a_im, log_dt, b_re, b_im, c_re, c_im, s5_d):
    t = CHUNK
    ar, ai = a_re.astype(F32), a_im.astype(F32)
    dt = jnp.exp(log_dt.astype(F32))[..., None]
    lr, li = ar * dt, ai * dt
    mag = jnp.exp(lr)
    lbr, lbi = mag * jnp.cos(li), mag * jnp.sin(li)
    den = ar * ar + ai * ai
    rr = ((lbr - 1.0) * ar + lbi * ai) / den
    ri = (lbi * ar - (lbr - 1.0) * ai) / den
    br, bi = b_re.astype(F32), b_im.astype(F32)
    bbr = rr[..., None] * br - ri[..., None] * bi
    bbi = rr[..., None] * bi + ri[..., None] * br
    cmr, cmi = c_re.astype(F32), c_im.astype(F32)
    tau = jnp.arange(t + 1, dtype=F32)[:, None, None, None]
    pmag = jnp.exp(tau * lr[None])
    pwr, pwi = pmag * jnp.cos(tau * li[None]), pmag * jnp.sin(tau * li[None])

    dd = jnp.pad(jnp.eye(S5_GROUP, dtype=F32)[None] * s5_d.astype(F32).reshape(S5_GROUPS, S5_GROUP, 1),
                 ((0, 0), (0, 0), (128 - S5_GROUP, 0)))

    def lanes4(f0, f1, b0, b1):
        return jnp.concatenate([f0, f1, b0, b1], axis=-1)

    cm = jnp.stack([lanes4(cmr[0], -cmi[0], cmr[1], -cmi[1]),
                    lanes4(-cmi[0], -cmr[0], -cmi[1], -cmr[1])], axis=1)
    fr, fi = jnp.moveaxis(pwr[1:t + 1, 0], 0, 1), jnp.moveaxis(pwi[1:t + 1, 0], 0, 1)
    gr, gi = jnp.moveaxis(pwr[1:t + 1][::-1, 1], 0, 1), jnp.moveaxis(pwi[1:t + 1][::-1, 1], 0, 1)
    bt = lambda z: jnp.swapaxes(z, -1, -2)
    bb = jnp.stack([lanes4(bt(bbr[0]), bt(bbi[0]), bt(bbr[1]), bt(bbi[1])),
                    lanes4(-bt(bbi[0]), bt(bbr[0]), -bt(bbi[1]), bt(bbr[1]))], axis=1)
    wr, wi = jnp.moveaxis(pwr[:t][::-1, 0], 0, 1), jnp.moveaxis(pwi[:t][::-1, 0], 0, 1)
    vr, vi = jnp.moveaxis(pwr[:t, 1], 0, 1), jnp.moveaxis(pwi[:t, 1], 0, 1)
    pw = jnp.stack([lanes4(fr, fr, gr, gr), lanes4(fi, fi, gi, gi),
                    lanes4(wr, wr, vr, vr), lanes4(wi, wi, vi, vi)], axis=1)

    dr, di = pwr[t], pwi[t]
    a_rows, b_rows = [], []
    for _ in range(N_SCAN_LEVELS):
        a_rows.append(lanes4(dr[0], dr[0], dr[1], dr[1]))
        b_rows.append(lanes4(-di[0], di[0], -di[1], di[1]))
        dr, di = dr * dr - di * di, 2.0 * dr * di
    pad = [jnp.zeros_like(a_rows[0])] * (8 - N_SCAN_LEVELS)
    dtab = jnp.stack(a_rows + pad + b_rows + pad, axis=1)
    return cm, bb, pw, dtab, dd


def kernel(x, mem, g_mix, g_mem, g_ffn, g_final, w_in, w_gate, b_gate, rpb, w_mem_kv,
           a_re, a_im, log_dt, b_re, b_im, c_re, c_im, s5_d, w_glu, w_branch, w_o,
           w_ffn1, w_ffn3, w_ffn2):
    bsz, s, d = x.shape
    n = bsz * s
    x2d = x.reshape(n, d)
    gm = g_mix[0].reshape(1, d).astype(F32)

    w_in_b = w_in[0].astype(BF16)
    w_in_t = w_in[0].T.astype(BF16)
    cm_tab, bb_tab, pw_tab, dtab, dd_tab = _s5_tables(a_re[0], a_im[0], log_dt[0], b_re[0], b_im[0],
                                                      c_re[0], c_im[0], s5_d[0])

    k, qm, qw, vt = _proj(x2d, gm, w_in_b, w_in_t)

    ut = _proj_s5t(x.reshape(n // CHUNK, CHUNK, d), gm, w_in_t)
    zt = _s5_core(ut, cm_tab, bb_tab, pw_tab, dtab, dd_tab)
    y_s5 = _s5_glu(zt, w_glu[0].T.astype(BF16)).reshape(n, 512)

    y_na = _na(rpb[0].astype(F32), qw, k.reshape(bsz, s, 512), vt).reshape(n, 512)

    k_mem, v_mem = _memkv(mem, g_mem[0].reshape(1, d).astype(F32), w_mem_kv[0].astype(BF16))
    y_mem = _mem_attn(qm.reshape(bsz, s, 512), k_mem, v_mem).reshape(n, 512)

    x1 = _merge(x2d, gm, y_na, y_s5, y_mem, w_gate[0].astype(BF16),
                b_gate[0].reshape(1, 3 * d).astype(F32), w_branch[0].astype(BF16), w_o[0].astype(BF16))
    out = _ffn(x1, g_ffn[0].reshape(1, d).astype(F32), g_final.reshape(1, d).astype(F32),
               w_ffn1[0].astype(BF16), w_ffn3[0].astype(BF16), w_ffn2[0].astype(BF16))
    return out.reshape(bsz, s, d)
```

```python
import functools

import jax
import jax.numpy as jnp
from jax import lax
from jax.experimental import pallas as pl
from jax.experimental.pallas import tpu as pltpu

F32 = jnp.float32
BF16 = jnp.bfloat16

D = 1024
GRID_W = 64
NA_HEADS = 8
NA_KH = 8
NA_KW = 16
S5_GROUPS = 32
S5_GROUP = 16
S5_STATE = 64
CHUNK = 64
MEM_HEADS = 4
MEM_HEAD_DIM = 128
D_FF = 2816
EPS = 1e-6
NEG_INF = -1e30

VMEM_LIMIT = 56 * 1024 * 1024

NT_DIMS = (((1,), (1,)), ((), ()))


def _cparams(n_axes):
    return pltpu.CompilerParams(
        dimension_semantics=("arbitrary",) * n_axes,
        vmem_limit_bytes=VMEM_LIMIT)


def _rms(x, g):
    return x * lax.rsqrt(jnp.mean(x * x, axis=-1, keepdims=True) + EPS) * g


def _const_spec(shape):
    nd = len(shape)
    return pl.BlockSpec(shape, lambda *_: (0,) * nd, pipeline_mode=pl.Buffered(1))


TOK_BLK = 128


LOG2E = 1.4426950408889634
NA_Q_SCALE = 64 ** -0.5 * LOG2E


def _proj_kernel(x_ref, g_ref, wk_ref, wqm_ref, wqt_ref, wvt_ref, k_ref, qm_ref, qw_ref, vt_ref):
    h = _rms(x_ref[...], g_ref[...]).astype(BF16)
    k_ref[...] = jnp.dot(h, wk_ref[...], preferred_element_type=F32).astype(BF16)
    qm_ref[...] = jnp.dot(h, wqm_ref[...], preferred_element_type=F32).astype(BF16)
    qt = lax.dot_general(wqt_ref[...], h, NT_DIMS, preferred_element_type=F32) * NA_Q_SCALE
    vt = lax.dot_general(wvt_ref[...], h, NT_DIMS, preferred_element_type=F32)
    lo = lax.broadcasted_iota(jnp.int32, (128, 128), 1) < 64
    same_head = (lax.broadcasted_iota(jnp.int32, (128, 128), 0) < 64) == lo
    for i in range(x_ref.shape[0] // TOK_BLK):
        toks = slice(TOK_BLK * i, TOK_BLK * (i + 1))
        vt_ref[i] = vt[:, toks].astype(BF16)
        for s in range(4):
            a = qt[128 * s:128 * (s + 1), toks]
            ar = pltpu.roll(a, 64, axis=1)
            qw_ref[i, 128 * s:128 * (s + 1), :] = jnp.concatenate(
                [jnp.where(same_head, jnp.where(lo, a, ar), 0.0),
                 jnp.where(same_head, jnp.where(lo, ar, a), 0.0)], axis=1).astype(BF16)


W_IN_BLK = 512


def _w_in_spec(blk, transposed):
    if transposed:
        return pl.BlockSpec((W_IN_BLK, D), lambda *_: (blk, 0), pipeline_mode=pl.Buffered(1))
    return pl.BlockSpec((D, W_IN_BLK), lambda *_: (0, blk), pipeline_mode=pl.Buffered(1))


def _proj(x2d, g, w, wt):
    n = x2d.shape[0]
    tm = 1024
    nat = jax.ShapeDtypeStruct((n, 512), BF16)
    nat_spec = pl.BlockSpec((tm, 512), lambda i: (i, 0))
    chm = lambda width: (jax.ShapeDtypeStruct((n // TOK_BLK, 512, width), BF16),
                         pl.BlockSpec((tm // TOK_BLK, 512, width), lambda i: (i, 0, 0)))
    (qw_shape, qw_spec), (vt_shape, vt_spec) = chm(2 * TOK_BLK), chm(TOK_BLK)
    return pl.pallas_call(
        _proj_kernel,
        grid=(n // tm,),
        in_specs=[pl.BlockSpec((tm, D), lambda i: (i, 0)),
                  _const_spec((1, D)),
                  _w_in_spec(1, False),
                  _w_in_spec(4, False),
                  _w_in_spec(0, True),
                  _w_in_spec(2, True)],
        out_specs=[nat_spec, nat_spec, qw_spec, vt_spec],
        out_shape=[nat, nat, qw_shape, vt_shape],
        compiler_params=_cparams(1),
        name="proj",
    )(x2d, g, w, w, wt, wt)


S5T_DT = 8


def _proj_s5t_kernel(x_ref, g_ref, wt_ref, o_ref, h_s):
    wt = wt_ref[...]
    nrows = x_ref.shape[0]
    hn = _rms(x_ref[...], g_ref[...]).reshape(nrows * S5T_DT, D)
    for c in range(D // 128):
        h_s[c] = hn[:, 128 * c:128 * (c + 1)]
    for j in range(S5T_DT):
        h = jnp.concatenate([h_s[c, pl.ds(j, nrows, stride=S5T_DT), :] for c in range(D // 128)],
                            axis=1).astype(BF16)
        ut = lax.dot_general(wt, h, NT_DIMS, preferred_element_type=F32)
        o_ref[:, j * S5_GROUP:(j + 1) * S5_GROUP, :] = (
            ut.astype(BF16).reshape(S5_GROUPS, S5_GROUP, 256))


def _proj_s5t(xc, g, wt):
    nrows = xc.shape[0]
    return pl.pallas_call(
        _proj_s5t_kernel,
        grid=(CHUNK // S5T_DT,),
        in_specs=[pl.BlockSpec((nrows, S5T_DT, D), lambda i: (0, i, 0)),
                  _const_spec((1, D)),
                  _w_in_spec(3, True)],
        out_specs=pl.BlockSpec((S5_GROUPS, S5T_DT * S5_GROUP, nrows), lambda i: (0, i, 0)),
        out_shape=jax.ShapeDtypeStruct((S5_GROUPS, CHUNK * S5_GROUP, nrows), BF16),
        scratch_shapes=[pltpu.VMEM((D // 128, nrows * S5T_DT, 128), F32)],
        compiler_params=_cparams(1),
        name="proj_s5t",
    )(xc, g, wt)


N_SCAN_LEVELS = 6


def _gelu_tanh(x):
    c = 0.7978845608028654
    return 0.5 * x * (1.0 + jnp.tanh(c * (x + 0.044715 * (x * x * x))))


def _split_bf16(x):
    hi = x.astype(BF16)
    return hi, (x - hi.astype(F32)).astype(BF16)


def _nt_f32(a, b):
    ah, al = _split_bf16(a)
    bh, bl = _split_bf16(b)
    nt = lambda u, w: lax.dot_general(u, w, NT_DIMS, preferred_element_type=F32)
    return nt(ah, bh) + nt(ah, bl) + nt(al, bh)


S5_GROUPS_PER_STEP = 2
S5_SCRATCH_PER_GROUP = 5


def _s5_kernel(ut_ref, cm_ref, bb_ref, pw_ref, d_ref, dd_ref, z_ref, *scratch):
    groups = [scratch[S5_SCRATCH_PER_GROUP * gi:S5_SCRATCH_PER_GROUP * (gi + 1)]
              for gi in range(S5_GROUPS_PER_STEP)]
    for gi, (m_s, n_s, pt_s, ptf_s, rs_s) in enumerate(groups):
        _s5_operators(gi, cm_ref, bb_ref, pw_ref, dd_ref, m_s, n_s, pt_s, ptf_s, rs_s)
    yv = [_s5_chunk_matmuls(gi, ut_ref, m_s, pt_s) for gi, (m_s, _, pt_s, _, _) in enumerate(groups)]
    for gi, (_, n_s, _, _, _) in enumerate(groups):
        _s5_finish(gi, d_ref, z_ref, n_s, *yv[gi])


def _s5_operators(gi, cm_ref, bb_ref, pw_ref, dd_ref, m_s, n_s, pt_s, ptf_s, rs_s):
    tc = CHUNK * S5_GROUP
    cma, cmb = cm_ref[gi,0], cm_ref[gi,1]
    bba, bbb = bb_ref[gi,0], bb_ref[gi,1]
    for t in range(CHUNK):
        rows = slice(S5_GROUP * t, S5_GROUP * (t + 1))
        n_s[rows, :] = (cma * pw_ref[gi,0, t:t + 1, :] + cmb * pw_ref[gi,1, t:t + 1, :]).astype(BF16)
        ptf_s[rows, :] = bba * pw_ref[gi,2, t:t + 1, :] + bbb * pw_ref[gi,3, t:t + 1, :]
    ptf = ptf_s[...]
    pt_s[...] = ptf.astype(BF16)

    ra = _nt_f32(cma[:, 0:128], ptf[:, 0:128])
    rb = _nt_f32(cma[:, 128:256], ptf[:, 128:256])
    zeros = jnp.zeros((S5_GROUP, tc), F32)
    r = (jnp.concatenate([ra, zeros], axis=1)
         + pltpu.roll(jnp.concatenate([rb, zeros], axis=1), (CHUNK - 1) * S5_GROUP, axis=1)
         + jnp.concatenate([zeros[:, 0:tc - 128], dd_ref[gi], zeros], axis=1))
    for k in range(8):
        rk = r if k == 0 else pltpu.roll(r, 2 * tc - S5_GROUP * k, axis=1)
        rs_s[k] = rk[:, 0:RS_LANES].astype(BF16)
    for t in range(CHUNK):
        a, k = divmod(CHUNK - 1 - t, 8)
        m_s[S5_GROUP * t:S5_GROUP * (t + 1), :] = rs_s[k, :, 128 * a:128 * a + tc]


def _s5_chunk_matmuls(gi, ut_ref, m_s, pt_s):
    ut = ut_ref[gi]
    nrows = ut.shape[1]
    eye = jnp.where(lax.broadcasted_iota(jnp.int32, (nrows, nrows), 0)
                    == lax.broadcasted_iota(jnp.int32, (nrows, nrows), 1), 1.0, 0.0).astype(BF16)
    u_rows = lax.dot_general(eye, ut, NT_DIMS, preferred_element_type=F32).astype(BF16)
    y = jnp.dot(m_s[...], ut, preferred_element_type=F32)
    v = jnp.dot(u_rows, pt_s[...], preferred_element_type=F32)
    return y, v


def _s5_finish(gi, d_ref, z_ref, n_s, y, v):
    nrows = v.shape[0]
    pos = lax.broadcasted_iota(jnp.int32, (nrows, 128), 0) & (CHUNK - 1)

    def shift(x, s, up):
        if up:
            return jnp.where(pos < CHUNK - s, pltpu.roll(x, nrows - s, axis=0), 0.0)
        return jnp.where(pos >= s, pltpu.roll(x, s, axis=0), 0.0)

    def scan(vh, lanes, up):
        x = shift(vh, 1, up)
        for lvl in range(N_SCAN_LEVELS):
            xs = shift(x, 1 << lvl, up)
            a = d_ref[gi,lvl:lvl + 1, lanes]
            b = d_ref[gi,8 + lvl:9 + lvl, lanes]
            x = x + a * xs + b * pltpu.roll(xs, 64, axis=1)
        return x

    xf = scan(v[:, 0:128], slice(0, 128), False)
    xb = scan(v[:, 128:256], slice(128, 256), True)
    xin = jnp.concatenate([xf, xb], axis=1).astype(BF16)
    y = y + lax.dot_general(n_s[...], xin, NT_DIMS, preferred_element_type=F32)
    z_ref[gi] =_gelu_tanh(y).astype(BF16)


RS_LANES = 128 * 7 + CHUNK * S5_GROUP


def _s5_core(ut, cm, bb, pw, dtab, dd):
    nrows = ut.shape[2]
    tc = CHUNK * S5_GROUP
    gs = S5_GROUPS_PER_STEP
    return pl.pallas_call(
        _s5_kernel,
        grid=(S5_GROUPS // gs,),
        in_specs=[pl.BlockSpec((gs, tc, nrows), lambda g: (g, 0, 0)),
                  pl.BlockSpec((gs, 2, S5_GROUP, 256), lambda g: (g, 0, 0, 0)),
                  pl.BlockSpec((gs, 2, S5_GROUP, 256), lambda g: (g, 0, 0, 0)),
                  pl.BlockSpec((gs, 4, CHUNK, 256), lambda g: (g, 0, 0, 0)),
                  pl.BlockSpec((gs, 16, 256), lambda g: (g, 0, 0)),
                  pl.BlockSpec((gs, S5_GROUP, 128), lambda g: (g, 0, 0))],
        out_specs=pl.BlockSpec((gs, tc, nrows), lambda g: (g, 0, 0)),
        out_shape=jax.ShapeDtypeStruct((S5_GROUPS, tc, nrows), BF16),
        scratch_shapes=[pltpu.VMEM((tc, tc), BF16),
                        pltpu.VMEM((tc, 256), BF16),
                        pltpu.VMEM((tc, 256), BF16),
                        pltpu.VMEM((tc, 256), F32),
                        pltpu.VMEM((8, S5_GROUP, RS_LANES), BF16),
                        ] * gs,
        compiler_params=_cparams(1),
        name="s5_core",
    )(ut, cm, bb, pw, dtab, dd)


def _s5_glu_kernel(z_ref, wt_ref, o_ref):
    wt = wt_ref[...]
    nrows = z_ref.shape[2]
    eye = jnp.where(lax.broadcasted_iota(jnp.int32, (nrows, nrows), 0)
                    == lax.broadcasted_iota(jnp.int32, (nrows, nrows), 1), 1.0, 0.0).astype(BF16)
    for j in range(S5T_DT):
        zt = z_ref[:, j * S5_GROUP:(j + 1) * S5_GROUP, :].reshape(512, nrows)
        gl = jnp.dot(wt, zt, preferred_element_type=F32)
        o = (zt.astype(F32) * jax.nn.sigmoid(gl)).astype(BF16)
        nat = lax.dot_general(eye, o, NT_DIMS, preferred_element_type=F32)
        o_ref[:, j, :] = nat.astype(BF16)


def _s5_glu(zt, wglu_t):
    nrows = zt.shape[2]
    return pl.pallas_call(
        _s5_glu_kernel,
        grid=(CHUNK // S5T_DT,),
        in_specs=[pl.BlockSpec((S5_GROUPS, S5T_DT * S5_GROUP, nrows), lambda i: (0, i, 0)),
                  _const_spec((512, 512))],
        out_specs=pl.BlockSpec((nrows, S5T_DT, 512), lambda i: (0, i, 0)),
        out_shape=jax.ShapeDtypeStruct((nrows, CHUNK, 512), BF16),
        compiler_params=_cparams(1),
        name="s5_glu",
    )(zt, wglu_t)


NA_ROWS_PER_STEP = 32
NA_PAIRS_PER_STEP = NA_ROWS_PER_STEP // 2
NA_WIN_ROWS = 10
NA_WIN_KEYS = NA_WIN_ROWS * GRID_W
NA_WIN_BLKS = NA_WIN_KEYS // TOK_BLK
NA_RR_OUTSIDE = 2 * NA_KH - 1


def _na_kernel(rpb_ref, qw_ref, k_ref, vt_ref, oob_ref, o_ref, *scratch):
    st_s, pt_s, bias_s = (scratch[0:4], scratch[4:8]), (scratch[8:12], scratch[12:16]), scratch[16]
    rb = pl.program_id(1)
    lo_q = lax.broadcasted_iota(jnp.int32, (GRID_W, 128), 1) < 64
    same_head2 = ((lax.broadcasted_iota(jnp.int32, (128, 256), 0) < 64)
                  == ((lax.broadcasted_iota(jnp.int32, (128, 256), 1) & 64) == 0))
    esum = jnp.where(lax.broadcasted_iota(jnp.int32, (GRID_W, 128), 0)
                     == (lax.broadcasted_iota(jnp.int32, (GRID_W, 128), 1) & (GRID_W - 1)),
                     1.0, 0.0).astype(BF16)
    half = NA_KH // 2

    kcol = lax.broadcasted_iota(jnp.int32, (GRID_W, 128), 0)
    qcol = lax.broadcasted_iota(jnp.int32, (GRID_W, 128), 1) & (GRID_W - 1)
    qstart = jnp.clip(qcol - NA_KW // 2, 0, GRID_W - NA_KW)
    col_ok = (kcol >= qstart) & (kcol < qstart + NA_KW)

    @pl.when((pl.program_id(0) == 0) & (rb == 0))
    def _build_bias():
        rel = jnp.clip(kcol - qcol + (NA_KW - 1), 0, 2 * NA_KW - 2)

        def rr_body(rr, carry):
            for s in range(4):
                acc = jnp.zeros((GRID_W, 128), F32)
                for j in range(2 * NA_KW - 1):
                    acc = jnp.where(rel == j, jnp.where(lo_q, rpb_ref[2 * s, rr, j], rpb_ref[2 * s + 1, rr, j]), acc)
                bias_s[s, rr] = acc * LOG2E
            return carry

        lax.fori_loop(0, NA_RR_OUTSIDE, rr_body, 0)
        for s in range(4):
            bias_s[s, NA_RR_OUTSIDE] = oob_ref[...]

    def window(pi):
        r0 = rb * NA_ROWS_PER_STEP + 2 * pi
        return r0, jnp.clip(r0 - half, 0, GRID_W - NA_WIN_ROWS)

    def q_stage(pi):
        _, win = window(pi)
        koff = pl.multiple_of(win * GRID_W, TOK_BLK)
        scs = []
        for s in range(4):
            ch = slice(128 * s, 128 * (s + 1))
            kw = k_ref[0, pl.ds(koff, NA_WIN_KEYS), ch]
            scs.append(jnp.dot(kw, qw_ref[pi, ch, :], preferred_element_type=F32))
        return scs

    def s_stage(pi, par, scs):
        r0, win = window(pi)
        rr = []
        for p in range(2):
            r = r0 + p
            first = jnp.clip(r - half, 0, GRID_W - NA_KH) - win
            rel = win - r + (NA_KH - 1)
            rr.append([jnp.where((wr >= first) & (wr < first + NA_KH), wr + rel, NA_RR_OUTSIDE)
                       for wr in range(NA_WIN_ROWS)])
        sums = []
        for s in range(4):
            st = st_s[par][s]
            slab_sums = []
            for p in range(2):
                lanes = slice(128 * p, 128 * (p + 1))
                m = None
                for wr in range(NA_WIN_ROWS):
                    rows = slice(GRID_W * wr, GRID_W * (wr + 1))
                    t = jnp.where(col_ok, scs[s][rows, lanes] + bias_s[s, rr[p][wr]], NEG_INF)
                    st[rows, lanes] = t
                    m = t if m is None else jnp.maximum(m, t)
                m = jnp.max(m, axis=0, keepdims=True)
                l = None
                for wr in range(NA_WIN_ROWS):
                    rows = slice(GRID_W * wr, GRID_W * (wr + 1))
                    e = jnp.exp2(st[rows, lanes] - m)
                    pt_s[par][s][rows, lanes] = e.astype(BF16)
                    l = e if l is None else l + e
                slab_sums.append(jnp.sum(l, axis=0, keepdims=True))
            sums.append(jnp.concatenate(slab_sums, axis=1))
        return tuple(sums)

    def o_stage_pv(pi, par, sums):
        _, win = window(pi)
        blk0 = lax.shift_right_logical(win, 1)
        ots = []
        for s in range(4):
            ch = slice(128 * s, 128 * (s + 1))
            vw = jnp.concatenate([vt_ref[blk0 + i, ch, :] for i in range(NA_WIN_BLKS)], axis=1)
            ot = jnp.dot(vw, pt_s[par][s][...], preferred_element_type=F32)
            ots.append(jnp.where(same_head2, (ot / sums[s]).astype(BF16), jnp.zeros((), BF16)))
        return ots

    def o_stage_store(pi, ots):
        for s in range(4):
            ch = slice(128 * s, 128 * (s + 1))
            for p in range(2):
                nat = lax.dot_general(esum, ots[s][:, 128 * p:128 * (p + 1)], NT_DIMS,
                                      preferred_element_type=F32)
                qoff = pl.multiple_of((2 * pi + p) * GRID_W, GRID_W)
                o_ref[0, pl.ds(qoff, GRID_W), ch] = nat.astype(BF16)

    def step(j, prev_sums, has_prev=True):
        a, b = 2 * j, 2 * j + 1
        scs_a = q_stage(a)
        if has_prev:
            ots_a = o_stage_pv(a - 2, 0, prev_sums[0])
            ots_b = o_stage_pv(b - 2, 1, prev_sums[1])
        scs_b = q_stage(b)
        if has_prev:
            o_stage_store(a - 2, ots_a)
            o_stage_store(b - 2, ots_b)
        return s_stage(a, 0, scs_a), s_stage(b, 1, scs_b)

    sums = step(0, None, has_prev=False)
    sums = lax.fori_loop(1, NA_PAIRS_PER_STEP // 2, step, sums)
    for u in range(2):
        pi = NA_PAIRS_PER_STEP - 2 + u
        o_stage_store(pi, o_stage_pv(pi, u, sums[u]))


def _na(rpb, qw, k, vt):
    b, s, _ = k.shape
    tq = NA_ROWS_PER_STEP * GRID_W
    blks = s // TOK_BLK
    oob = jnp.full((GRID_W, 128), -jnp.inf, F32)
    return pl.pallas_call(
        _na_kernel,
        grid=(b, s // tq),
        in_specs=[pl.BlockSpec(memory_space=pltpu.SMEM),
                  pl.BlockSpec((NA_PAIRS_PER_STEP, 512, 2 * TOK_BLK),
                               lambda bi, ri: (bi * (blks // NA_PAIRS_PER_STEP) + ri, 0, 0)),
                  pl.BlockSpec((1, s, 512), lambda bi, ri: (bi, 0, 0)),
                  pl.BlockSpec((blks, 512, TOK_BLK), lambda bi, ri: (bi, 0, 0)),
                  _const_spec(oob.shape)],
        out_specs=pl.BlockSpec((1, tq, 512), lambda bi, ri: (bi, ri, 0)),
        out_shape=jax.ShapeDtypeStruct((b, s, 512), BF16),
        scratch_shapes=([pltpu.VMEM((NA_WIN_KEYS, 256), F32)] * 8
                        + [pltpu.VMEM((NA_WIN_KEYS, 256), BF16)] * 8
                        + [pltpu.VMEM((NA_HEADS // 2, 2 * NA_KH, GRID_W, 128), F32)]),
        compiler_params=_cparams(2),
        name="na",
    )(rpb, qw, k, vt, oob)


def _memkv_kernel(mem_ref, g_ref, w_ref, k_ref, v_ref):
    mn = _rms(mem_ref[0], g_ref[...]).astype(BF16)
    kv = jnp.dot(mn, w_ref[...], preferred_element_type=F32)
    k_ref[0] = kv[:, 0:512].astype(BF16)
    v_ref[0] = kv[:, 512:1024].astype(BF16)


def _memkv(mem, g, w):
    b, m, _ = mem.shape
    out = jax.ShapeDtypeStruct((b, m, 512), BF16)
    ospec = pl.BlockSpec((1, m, 512), lambda i: (i, 0, 0))
    return pl.pallas_call(
        _memkv_kernel,
        grid=(b,),
        in_specs=[pl.BlockSpec((1, m, D), lambda i: (i, 0, 0)),
                  _const_spec((1, D)),
                  _const_spec((D, 1024))],
        out_specs=[ospec, ospec],
        out_shape=[out, out],
        compiler_params=_cparams(1),
        name="memkv",
    )(mem, g, w)


def _mem_kernel(q_ref, k_ref, v_ref, o_ref):
    scale = MEM_HEAD_DIM ** -0.5
    for h in range(MEM_HEADS):
        cols = slice(MEM_HEAD_DIM * h, MEM_HEAD_DIM * (h + 1))
        sc = lax.dot_general(q_ref[0, :, cols], k_ref[0, :, cols], NT_DIMS,
                             preferred_element_type=F32) * scale
        m = jnp.max(sc, axis=-1, keepdims=True)
        p = jnp.exp(sc - m)
        l = jnp.sum(p, axis=-1, keepdims=True)
        o = jnp.dot(p.astype(BF16), v_ref[0, :, cols], preferred_element_type=F32)
        o_ref[0, :, cols] = (o / l).astype(BF16)


def _mem_attn(q, k, v):
    b, s, _ = q.shape
    m = k.shape[1]
    tq = 1024
    return pl.pallas_call(
        _mem_kernel,
        grid=(b, s // tq),
        in_specs=[pl.BlockSpec((1, tq, 512), lambda bi, i: (bi, i, 0)),
                  pl.BlockSpec((1, m, 512), lambda bi, i: (bi, 0, 0)),
                  pl.BlockSpec((1, m, 512), lambda bi, i: (bi, 0, 0))],
        out_specs=pl.BlockSpec((1, tq, 512), lambda bi, i: (bi, i, 0)),
        out_shape=jax.ShapeDtypeStruct((b, s, 512), BF16),
        compiler_params=_cparams(2),
        name="mem_attn",
    )(q, k, v)


def _merge_kernel(x_ref, g_ref, yna_ref, ys5_ref, ymem_ref, wg_ref, bg_ref, wb_ref, wo_ref, o_ref):
    x = x_ref[...]
    h = _rms(x, g_ref[...]).astype(BF16)
    merged = None
    for b, y_ref in enumerate((yna_ref, ys5_ref, ymem_ref)):
        cols = slice(D * b, D * (b + 1))
        gate = jax.nn.sigmoid(jnp.dot(h, wg_ref[:, cols], preferred_element_type=F32) + bg_ref[:, cols])
        up = jnp.dot(y_ref[...], wb_ref[b], preferred_element_type=F32)
        merged = gate * up if merged is None else merged + gate * up
    o_ref[...] = x + jnp.dot(merged.astype(BF16), wo_ref[...], preferred_element_type=F32)


def _merge(x2d, g, yna, ys5, ymem, wg, bg, wb, wo):
    n = x2d.shape[0]
    tm = 1024
    yspec = pl.BlockSpec((tm, 512), lambda i: (i, 0))
    return pl.pallas_call(
        _merge_kernel,
        grid=(n // tm,),
        in_specs=[pl.BlockSpec((tm, D), lambda i: (i, 0)),
                  _const_spec((1, D)),
                  yspec, yspec, yspec,
                  _const_spec((D, 3 * D)),
                  _const_spec((1, 3 * D)),
                  _const_spec((3, 512, D)),
                  _const_spec((D, D))],
        out_specs=pl.BlockSpec((tm, D), lambda i: (i, 0)),
        out_shape=jax.ShapeDtypeStruct((n, D), F32),
        compiler_params=_cparams(1),
        name="merge",
    )(x2d, g, yna, ys5, ymem, wg, bg, wb, wo)


def _ffn_kernel(x_ref, g_ref, gf_ref, w1_ref, w3_ref, w2_ref, o_ref):
    x = x_ref[...]
    h = _rms(x, g_ref[...]).astype(BF16)
    a = jnp.dot(h, w1_ref[...], preferred_element_type=F32)
    c = jnp.dot(h, w3_ref[...], preferred_element_type=F32)
    mid = (a * jax.nn.sigmoid(a) * c).astype(BF16)
    x2 = x + jnp.dot(mid, w2_ref[...], preferred_element_type=F32)
    o_ref[...] = _rms(x2, gf_ref[...])


def _ffn(x2d, g, gf, w1, w3, w2):
    n = x2d.shape[0]
    tm = 512
    return pl.pallas_call(
        _ffn_kernel,
        grid=(n // tm,),
        in_specs=[pl.BlockSpec((tm, D), lambda i: (i, 0)),
                  _const_spec((1, D)),
                  _const_spec((1, D)),
                  _const_spec((D, D_FF)),
                  _const_spec((D, D_FF)),
                  _const_spec((D_FF, D))],
        out_specs=pl.BlockSpec((tm, D), lambda i: (i, 0)),
        out_shape=jax.ShapeDtypeStruct((n, D), F32),
        compiler_params=_cparams(1),
        name="ffn",
    )(x2d, g, gf, w1, w3, w2)


def _s5_tables(a_re, a_im, log_dt, b_re, b_im, c_re, c_im, s5_d):
    t = CHUNK
    ar, ai = a_re.astype(F32), a_im.astype(F32)
    dt = jnp.exp(log_dt.astype(F32))[..., None]
    lr, li = ar * dt, ai * dt
    mag = jnp.exp(lr)
    lbr, lbi = mag * jnp.cos(li), mag * jnp.sin(li)
    den = ar * ar + ai * ai
    rr = ((lbr - 1.0) * ar + lbi * ai) / den
    ri = (lbi * ar - (lbr - 1.0) * ai) / den
    br, bi = b_re.astype(F32), b_im.astype(F32)
    bbr = rr[..., None] * br - ri[..., None] * bi
    bbi = rr[..., None] * bi + ri[..., None] * br
    cmr, cmi = c_re.astype(F32), c_im.astype(F32)
    tau = jnp.arange(t + 1, dtype=F32)[:, None, None, None]
    pmag = jnp.exp(tau * lr[None])
    pwr, pwi = pmag * jnp.cos(tau * li[None]), pmag * jnp.sin(tau * li[None])

    dd = jnp.pad(jnp.eye(S5_GROUP, dtype=F32)[None] * s5_d.astype(F32).reshape(S5_GROUPS, S5_GROUP, 1),
                 ((0, 0), (0, 0), (128 - S5_GROUP, 0)))

    def lanes4(f0, f1, b0, b1):
        return jnp.concatenate([f0, f1, b0, b1], axis=-1)

    cm = jnp.stack([lanes4(cmr[0], -cmi[0], cmr[1], -cmi[1]),
                    lanes4(-cmi[0], -cmr[0], -cmi[1], -cmr[1])], axis=1)
    fr, fi = jnp.moveaxis(pwr[1:t + 1, 0], 0, 1), jnp.moveaxis(pwi[1:t + 1, 0], 0, 1)
    gr, gi = jnp.moveaxis(pwr[1:t + 1][::-1, 1], 0, 1), jnp.moveaxis(pwi[1:t + 1][::-1, 1], 0, 1)
    bt = lambda z: jnp.swapaxes(z, -1, -2)
    bb = jnp.stack([lanes4(bt(bbr[0]), bt(bbi[0]), bt(bbr[1]), bt(bbi[1])),
                    lanes4(-bt(bbi[0]), bt(bbr[0]), -bt(bbi[1]), bt(bbr[1]))], axis=1)
    wr, wi = jnp.moveaxis(pwr[:t][::-1, 0], 0, 1), jnp.moveaxis(pwi[:t][::-1, 0], 0, 1)
    vr, vi = jnp.moveaxis(pwr[:t, 1], 0, 1), jnp.moveaxis(pwi[:t, 1], 0, 1)
    pw = jnp.stack([lanes4(fr, fr, gr, gr), lanes4(fi, fi, gi, gi),
                    lanes4(wr, wr, vr, vr), lanes4(wi, wi, vi, vi)], axis=1)

    dr, di = pwr[t], pwi[t]
    a_rows, b_rows = [], []
    for _ in range(N_SCAN_LEVELS):
        a_rows.append(lanes4(dr[0], dr[0], dr[1], dr[1]))
        b_rows.append(lanes4(-di[0], di[0], -di[1], di[1]))
        dr, di = dr * dr - di * di, 2.0 * dr * di
    pad = [jnp.zeros_like(a_rows[0])] * (8 - N_SCAN_LEVELS)
    dtab = jnp.stack(a_rows + pad + b_rows + pad, axis=1)
    return cm, bb, pw, dtab, dd


def kernel(x, mem, g_mix, g_mem, g_ffn, g_final, w_in, w_gate, b_gate, rpb, w_mem_kv,
           a_re, a_im, log_dt, b_re, b_im, c_re, c_im, s5_d, w_glu, w_branch, w_o,
           w_ffn1, w_ffn3, w_ffn2):
    bsz, s, d = x.shape
    n = bsz * s
    x2d = x.reshape(n, d)
    gm = g_mix[0].reshape(1, d).astype(F32)

    w_in_b = w_in[0].astype(BF16)
    w_in_t = w_in[0].T.astype(BF16)
    cm_tab, bb_tab, pw_tab, dtab, dd_tab = _s5_tables(a_re[0], a_im[0], log_dt[0], b_re[0], b_im[0],
                                                      c_re[0], c_im[0], s5_d[0])

    k, qm, qw, vt = _proj(x2d, gm, w_in_b, w_in_t)

    ut = _proj_s5t(x.reshape(n // CHUNK, CHUNK, d), gm, w_in_t)
    zt = _s5_core(ut, cm_tab, bb_tab, pw_tab, dtab, dd_tab)
    y_s5 = _s5_glu(zt, w_glu[0].T.astype(BF16)).reshape(n, 512)

    y_na = _na(rpb[0].astype(F32), qw, k.reshape(bsz, s, 512), vt).reshape(n, 512)

    k_mem, v_mem = _memkv(mem, g_mem[0].reshape(1, d).astype(F32), w_mem_kv[0].astype(BF16))
    y_mem = _mem_attn(qm.reshape(bsz, s, 512), k_mem, v_mem).reshape(n, 512)

    x1 = _merge(x2d, gm, y_na, y_s5, y_mem, w_gate[0].astype(BF16),
                b_gate[0].reshape(1, 3 * d).astype(F32), w_branch[0].astype(BF16), w_o[0].astype(BF16))
    out = _ffn(x1, g_ffn[0].reshape(1, d).astype(F32), g_final.reshape(1, d).astype(F32),
               w_ffn1[0].astype(BF16), w_ffn3[0].astype(BF16), w_ffn2[0].astype(BF16))
    return out.reshape(bsz, s, d)
```

```python
import functools

import jax
import jax.numpy as jnp
from jax import lax
from jax.experimental import pallas as pl
from jax.experimental.pallas import tpu as pltpu

F32 = jnp.float32
BF16 = jnp.bfloat16

D = 1024
GRID_W = 64
NA_HEADS = 8
NA_KH = 8
NA_KW = 16
S5_GROUPS = 32
S5_GROUP = 16
S5_STATE = 64
CHUNK = 64
MEM_HEADS = 4
MEM_HEAD_DIM = 128
D_FF = 2816
EPS = 1e-6
NEG_INF = -1e30

VMEM_LIMIT = 56 * 1024 * 1024

NT_DIMS = (((1,), (1,)), ((), ()))


def _cparams(n_axes):
    return pltpu.CompilerParams(
        dimension_semantics=("arbitrary",) * n_axes,
        vmem_limit_bytes=VMEM_LIMIT)


def _rms(x, g):
    return x * lax.rsqrt(jnp.mean(x * x, axis=-1, keepdims=True) + EPS) * g


def _const_spec(shape):
    nd = len(shape)
    return pl.BlockSpec(shape, lambda *_: (0,) * nd, pipeline_mode=pl.Buffered(1))


TOK_BLK = 128


LOG2E = 1.4426950408889634
NA_Q_SCALE = 64 ** -0.5 * LOG2E


def _proj_kernel(x_ref, g_ref, wk_ref, wqm_ref, wqt_ref, wvt_ref, k_ref, qm_ref, qw_ref, vt_ref):
    h = _rms(x_ref[...], g_ref[...]).astype(BF16)
    k_ref[...] = jnp.dot(h, wk_ref[...], preferred_element_type=F32).astype(BF16)
    qm_ref[...] = jnp.dot(h, wqm_ref[...], preferred_element_type=F32).astype(BF16)
    qt = lax.dot_general(wqt_ref[...], h, NT_DIMS, preferred_element_type=F32) * NA_Q_SCALE
    vt = lax.dot_general(wvt_ref[...], h, NT_DIMS, preferred_element_type=F32)
    lo = lax.broadcasted_iota(jnp.int32, (128, 128), 1) < 64
    same_head = (lax.broadcasted_iota(jnp.int32, (128, 128), 0) < 64) == lo
    for i in range(x_ref.shape[0] // TOK_BLK):
        toks = slice(TOK_BLK * i, TOK_BLK * (i + 1))
        vt_ref[i] = vt[:, toks].astype(BF16)
        for s in range(4):
            a = qt[128 * s:128 * (s + 1), toks]
            ar = pltpu.roll(a, 64, axis=1)
            qw_ref[i, 128 * s:128 * (s + 1), :] = jnp.concatenate(
                [jnp.where(same_head, jnp.where(lo, a, ar), 0.0),
                 jnp.where(same_head, jnp.where(lo, ar, a), 0.0)], axis=1).astype(BF16)


W_IN_BLK = 512


def _w_in_spec(blk, transposed):
    if transposed:
        return pl.BlockSpec((W_IN_BLK, D), lambda *_: (blk, 0), pipeline_mode=pl.Buffered(1))
    return pl.BlockSpec((D, W_IN_BLK), lambda *_: (0, blk), pipeline_mode=pl.Buffered(1))


def _proj(x2d, g, w, wt):
    n = x2d.shape[0]
    tm = 1024
    nat = jax.ShapeDtypeStruct((n, 512), BF16)
    nat_spec = pl.BlockSpec((tm, 512), lambda i: (i, 0))
    chm = lambda width: (jax.ShapeDtypeStruct((n // TOK_BLK, 512, width), BF16),
                         pl.BlockSpec((tm // TOK_BLK, 512, width), lambda i: (i, 0, 0)))
    (qw_shape, qw_spec), (vt_shape, vt_spec) = chm(2 * TOK_BLK), chm(TOK_BLK)
    return pl.pallas_call(
        _proj_kernel,
        grid=(n // tm,),
        in_specs=[pl.BlockSpec((tm, D), lambda i: (i, 0)),
                  _const_spec((1, D)),
                  _w_in_spec(1, False),
                  _w_in_spec(4, False),
                  _w_in_spec(0, True),
                  _w_in_spec(2, True)],
        out_specs=[nat_spec, nat_spec, qw_spec, vt_spec],
        out_shape=[nat, nat, qw_shape, vt_shape],
        compiler_params=_cparams(1),
        name="proj",
    )(x2d, g, w, w, wt, wt)


S5T_DT = 8


def _proj_s5t_kernel(x_ref, g_ref, wt_ref, o_ref, h_s):
    wt = wt_ref[...]
    nrows = x_ref.shape[0]
    hn = _rms(x_ref[...], g_ref[...]).reshape(nrows * S5T_DT, D)
    for c in range(D // 128):
        h_s[c] = hn[:, 128 * c:128 * (c + 1)]
    for j in range(S5T_DT):
        h = jnp.concatenate([h_s[c, pl.ds(j, nrows, stride=S5T_DT), :] for c in range(D // 128)],
                            axis=1).astype(BF16)
        ut = lax.dot_general(wt, h, NT_DIMS, preferred_element_type=F32)
        o_ref[:, j * S5_GROUP:(j + 1) * S5_GROUP, :] = (
            ut.astype(BF16).reshape(S5_GROUPS, S5_GROUP, 256))


def _proj_s5t(xc, g, wt):
    nrows = xc.shape[0]
    return pl.pallas_call(
        _proj_s5t_kernel,
        grid=(CHUNK // S5T_DT,),
        in_specs=[pl.BlockSpec((nrows, S5T_DT, D), lambda i: (0, i, 0)),
                  _const_spec((1, D)),
                  _w_in_spec(3, True)],
        out_specs=pl.BlockSpec((S5_GROUPS, S5T_DT * S5_GROUP, nrows), lambda i: (0, i, 0)),
        out_shape=jax.ShapeDtypeStruct((S5_GROUPS, CHUNK * S5_GROUP, nrows), BF16),
        scratch_shapes=[pltpu.VMEM((D // 128, nrows * S5T_DT, 128), F32)],
        compiler_params=_cparams(1),
        name="proj_s5t",
    )(xc, g, wt)


N_SCAN_LEVELS = 6


def _gelu_tanh(x):
    c = 0.7978845608028654
    return 0.5 * x * (1.0 + jnp.tanh(c * (x + 0.044715 * (x * x * x))))


def _split_bf16(x):
    hi = x.astype(BF16)
    return hi, (x - hi.astype(F32)).astype(BF16)


def _nt_f32(a, b):
    ah, al = _split_bf16(a)
    bh, bl = _split_bf16(b)
    nt = lambda u, w: lax.dot_general(u, w, NT_DIMS, preferred_element_type=F32)
    return nt(ah, bh) + nt(ah, bl) + nt(al, bh)


S5_GROUPS_PER_STEP = 4
S5_SCRATCH_PER_GROUP = 5


def _s5_kernel(ut_ref, cm_ref, bb_ref, pw_ref, d_ref, dd_ref, z_ref, *scratch):
    groups = [scratch[S5_SCRATCH_PER_GROUP * gi:S5_SCRATCH_PER_GROUP * (gi + 1)]
              for gi in range(S5_GROUPS_PER_STEP)]
    for gi, (m_s, n_s, pt_s, ptf_s, rs_s) in enumerate(groups):
        _s5_operators(gi, cm_ref, bb_ref, pw_ref, dd_ref, m_s, n_s, pt_s, ptf_s, rs_s)
    yv = [_s5_chunk_matmuls(gi, ut_ref, m_s, pt_s) for gi, (m_s, _, pt_s, _, _) in enumerate(groups)]
    for gi, (_, n_s, _, _, _) in enumerate(groups):
        _s5_finish(gi, d_ref, z_ref, n_s, *yv[gi])


def _s5_operators(gi, cm_ref, bb_ref, pw_ref, dd_ref, m_s, n_s, pt_s, ptf_s, rs_s):
    tc = CHUNK * S5_GROUP
    cma, cmb = cm_ref[gi,0], cm_ref[gi,1]
    bba, bbb = bb_ref[gi,0], bb_ref[gi,1]
    for t in range(CHUNK):
        rows = slice(S5_GROUP * t, S5_GROUP * (t + 1))
        n_s[rows, :] = (cma * pw_ref[gi,0, t:t + 1, :] + cmb * pw_ref[gi,1, t:t + 1, :]).astype(BF16)
        ptf_s[rows, :] = bba * pw_ref[gi,2, t:t + 1, :] + bbb * pw_ref[gi,3, t:t + 1, :]
    ptf = ptf_s[...]
    pt_s[...] = ptf.astype(BF16)

    ra = _nt_f32(cma[:, 0:128], ptf[:, 0:128])
    rb = _nt_f32(cma[:, 128:256], ptf[:, 128:256])
    zeros = jnp.zeros((S5_GROUP, tc), F32)
    r = (jnp.concatenate([ra, zeros], axis=1)
         + pltpu.roll(jnp.concatenate([rb, zeros], axis=1), (CHUNK - 1) * S5_GROUP, axis=1)
         + jnp.concatenate([zeros[:, 0:tc - 128], dd_ref[gi], zeros], axis=1))
    for k in range(8):
        rk = r if k == 0 else pltpu.roll(r, 2 * tc - S5_GROUP * k, axis=1)
        rs_s[k] = rk[:, 0:RS_LANES].astype(BF16)
    for t in range(CHUNK):
        a, k = divmod(CHUNK - 1 - t, 8)
        m_s[S5_GROUP * t:S5_GROUP * (t + 1), :] = rs_s[k, :, 128 * a:128 * a + tc]


def _s5_chunk_matmuls(gi, ut_ref, m_s, pt_s):
    ut = ut_ref[gi]
    nrows = ut.shape[1]
    eye = jnp.where(lax.broadcasted_iota(jnp.int32, (nrows, nrows), 0)
                    == lax.broadcasted_iota(jnp.int32, (nrows, nrows), 1), 1.0, 0.0).astype(BF16)
    u_rows = lax.dot_general(eye, ut, NT_DIMS, preferred_element_type=F32).astype(BF16)
    y = jnp.dot(m_s[...], ut, preferred_element_type=F32)
    v = jnp.dot(u_rows, pt_s[...], preferred_element_type=F32)
    return y, v


def _s5_finish(gi, d_ref, z_ref, n_s, y, v):
    nrows = v.shape[0]
    pos = lax.broadcasted_iota(jnp.int32, (nrows, 128), 0) & (CHUNK - 1)

    def shift(x, s, up):
        if up:
            return jnp.where(pos < CHUNK - s, pltpu.roll(x, nrows - s, axis=0), 0.0)
        return jnp.where(pos >= s, pltpu.roll(x, s, axis=0), 0.0)

    def scan(vh, lanes, up):
        x = shift(vh, 1, up)
        for lvl in range(N_SCAN_LEVELS):
            xs = shift(x, 1 << lvl, up)
            a = d_ref[gi,lvl:lvl + 1, lanes]
            b = d_ref[gi,8 + lvl:9 + lvl, lanes]
            x = x + a * xs + b * pltpu.roll(xs, 64, axis=1)
        return x

    xf = scan(v[:, 0:128], slice(0, 128), False)
    xb = scan(v[:, 128:256], slice(128, 256), True)
    xin = jnp.concatenate([xf, xb], axis=1).astype(BF16)
    y = y + lax.dot_general(n_s[...], xin, NT_DIMS, preferred_element_type=F32)
    z_ref[gi] =_gelu_tanh(y).astype(BF16)


RS_LANES = 128 * 7 + CHUNK * S5_GROUP


def _s5_core(ut, cm, bb, pw, dtab, dd):
    nrows = ut.shape[2]
    tc = CHUNK * S5_GROUP
    gs = S5_GROUPS_PER_STEP
    return pl.pallas_call(
        _s5_kernel,
        grid=(S5_GROUPS // gs,),
        in_specs=[pl.BlockSpec((gs, tc, nrows), lambda g: (g, 0, 0)),
                  pl.BlockSpec((gs, 2, S5_GROUP, 256), lambda g: (g, 0, 0, 0)),
                  pl.BlockSpec((gs, 2, S5_GROUP, 256), lambda g: (g, 0, 0, 0)),
                  pl.BlockSpec((gs, 4, CHUNK, 256), lambda g: (g, 0, 0, 0)),
                  pl.BlockSpec((gs, 16, 256), lambda g: (g, 0, 0)),
                  pl.BlockSpec((gs, S5_GROUP, 128), lambda g: (g, 0, 0))],
        out_specs=pl.BlockSpec((gs, tc, nrows), lambda g: (g, 0, 0)),
        out_shape=jax.ShapeDtypeStruct((S5_GROUPS, tc, nrows), BF16),
        scratch_shapes=[pltpu.VMEM((tc, tc), BF16),
                        pltpu.VMEM((tc, 256), BF16),
                        pltpu.VMEM((tc, 256), BF16),
                        pltpu.VMEM((tc, 256), F32),
                        pltpu.VMEM((8, S5_GROUP, RS_LANES), BF16),
                        ] * gs,
        compiler_params=_cparams(1),
        name="s5_core",
    )(ut, cm, bb, pw, dtab, dd)


def _s5_glu_kernel(z_ref, wt_ref, o_ref):
    wt = wt_ref[...]
    nrows = z_ref.shape[2]
    eye = jnp.where(lax.broadcasted_iota(jnp.int32, (nrows, nrows), 0)
                    == lax.broadcasted_iota(jnp.int32, (nrows, nrows), 1), 1.0, 0.0).astype(BF16)
    for j in range(S5T_DT):
        zt = z_ref[:, j * S5_GROUP:(j + 1) * S5_GROUP, :].reshape(512, nrows)
        gl = jnp.dot(wt, zt, preferred_element_type=F32)
        o = (zt.astype(F32) * jax.nn.sigmoid(gl)).astype(BF16)
        nat = lax.dot_general(eye, o, NT_DIMS, preferred_element_type=F32)
        o_ref[:, j, :] = nat


def _s5_glu(zt, wglu_t):
    nrows = zt.shape[2]
    return pl.pallas_call(
        _s5_glu_kernel,
        grid=(CHUNK // S5T_DT,),
        in_specs=[pl.BlockSpec((S5_GROUPS, S5T_DT * S5_GROUP, nrows), lambda i: (0, i, 0)),
                  _const_spec((512, 512))],
        out_specs=pl.BlockSpec((nrows, S5T_DT, 512), lambda i: (0, i, 0)),
        out_shape=jax.ShapeDtypeStruct((nrows, CHUNK, 512), F32),
        compiler_params=_cparams(1),
        name="s5_glu",
    )(zt, wglu_t)


NA_ROWS_PER_STEP = 32
NA_PAIRS_PER_STEP = NA_ROWS_PER_STEP // 2
NA_WIN_ROWS = 10
NA_WIN_KEYS = NA_WIN_ROWS * GRID_W
NA_WIN_BLKS = NA_WIN_KEYS // TOK_BLK
NA_RR_OUTSIDE = 2 * NA_KH - 1


def _na_kernel(rpb_ref, qw_ref, k_ref, vt_ref, oob_ref, o_ref, *scratch):
    st_s, pt_s, bias_s = (scratch[0:4], scratch[4:8]), (scratch[8:12], scratch[12:16]), scratch[16]
    rb = pl.program_id(1)
    lo_q = lax.broadcasted_iota(jnp.int32, (GRID_W, 128), 1) < 64
    same_head2 = ((lax.broadcasted_iota(jnp.int32, (128, 256), 0) < 64)
                  == ((lax.broadcasted_iota(jnp.int32, (128, 256), 1) & 64) == 0))
    esum = jnp.where(lax.broadcasted_iota(jnp.int32, (GRID_W, 128), 0)
                     == (lax.broadcasted_iota(jnp.int32, (GRID_W, 128), 1) & (GRID_W - 1)),
                     1.0, 0.0).astype(BF16)
    half = NA_KH // 2

    kcol = lax.broadcasted_iota(jnp.int32, (GRID_W, 128), 0)
    qcol = lax.broadcasted_iota(jnp.int32, (GRID_W, 128), 1) & (GRID_W - 1)
    qstart = jnp.clip(qcol - NA_KW // 2, 0, GRID_W - NA_KW)
    col_ok = (kcol >= qstart) & (kcol < qstart + NA_KW)

    @pl.when((pl.program_id(0) == 0) & (rb == 0))
    def _build_bias():
        rel = jnp.clip(kcol - qcol + (NA_KW - 1), 0, 2 * NA_KW - 2)

        def rr_body(rr, carry):
            for s in range(4):
                acc = jnp.zeros((GRID_W, 128), F32)
                for j in range(2 * NA_KW - 1):
                    acc = jnp.where(rel == j, jnp.where(lo_q, rpb_ref[2 * s, rr, j], rpb_ref[2 * s + 1, rr, j]), acc)
                bias_s[s, rr] = acc * LOG2E
            return carry

        lax.fori_loop(0, NA_RR_OUTSIDE, rr_body, 0)
        for s in range(4):
            bias_s[s, NA_RR_OUTSIDE] = oob_ref[...]

    def window(pi):
        r0 = rb * NA_ROWS_PER_STEP + 2 * pi
        return r0, jnp.clip(r0 - half, 0, GRID_W - NA_WIN_ROWS)

    def q_stage(pi):
        _, win = window(pi)
        koff = pl.multiple_of(win * GRID_W, TOK_BLK)
        scs = []
        for s in range(4):
            ch = slice(128 * s, 128 * (s + 1))
            kw = k_ref[0, pl.ds(koff, NA_WIN_KEYS), ch]
            scs.append(jnp.dot(kw, qw_ref[pi, ch, :], preferred_element_type=F32))
        return scs

    def s_stage(pi, par, scs):
        r0, win = window(pi)
        rr = []
        for p in range(2):
            r = r0 + p
            first = jnp.clip(r - half, 0, GRID_W - NA_KH) - win
            rel = win - r + (NA_KH - 1)
            rr.append([jnp.where((wr >= first) & (wr < first + NA_KH), wr + rel, NA_RR_OUTSIDE)
                       for wr in range(NA_WIN_ROWS)])
        sums = []
        for s in range(4):
            st = st_s[par][s]
            slab_sums = []
            for p in range(2):
                lanes = slice(128 * p, 128 * (p + 1))
                m = None
                for wr in range(NA_WIN_ROWS):
                    rows = slice(GRID_W * wr, GRID_W * (wr + 1))
                    t = jnp.where(col_ok, scs[s][rows, lanes] + bias_s[s, rr[p][wr]], NEG_INF)
                    st[rows, lanes] = t
                    m = t if m is None else jnp.maximum(m, t)
                m = jnp.max(m, axis=0, keepdims=True)
                l = None
                for wr in range(NA_WIN_ROWS):
                    rows = slice(GRID_W * wr, GRID_W * (wr + 1))
                    e = jnp.exp2(st[rows, lanes] - m)
                    pt_s[par][s][rows, lanes] = e.astype(BF16)
                    l = e if l is None else l + e
                slab_sums.append(jnp.sum(l, axis=0, keepdims=True))
            sums.append(jnp.concatenate(slab_sums, axis=1))
        return tuple(sums)

    def o_stage_pv(pi, par, sums):
        _, win = window(pi)
        blk0 = lax.shift_right_logical(win, 1)
        ots = []
        for s in range(4):
            ch = slice(128 * s, 128 * (s + 1))
            vw = jnp.concatenate([vt_ref[blk0 + i, ch, :] for i in range(NA_WIN_BLKS)], axis=1)
            ot = jnp.dot(vw, pt_s[par][s][...], preferred_element_type=F32)
            ots.append(jnp.where(same_head2, (ot / sums[s]).astype(BF16), jnp.zeros((), BF16)))
        return ots

    def o_stage_store(pi, ots):
        for s in range(4):
            ch = slice(128 * s, 128 * (s + 1))
            for p in range(2):
                nat = lax.dot_general(esum, ots[s][:, 128 * p:128 * (p + 1)], NT_DIMS,
                                      preferred_element_type=F32)
                qoff = pl.multiple_of((2 * pi + p) * GRID_W, GRID_W)
                o_ref[0, pl.ds(qoff, GRID_W), ch] = nat.astype(BF16)

    def step(j, prev_sums, has_prev=True):
        a, b = 2 * j, 2 * j + 1
        scs_a = q_stage(a)
        if has_prev:
            ots_a = o_stage_pv(a - 2, 0, prev_sums[0])
            ots_b = o_stage_pv(b - 2, 1, prev_sums[1])
        scs_b = q_stage(b)
        if has_prev:
            o_stage_store(a - 2, ots_a)
            o_stage_store(b - 2, ots_b)
        return s_stage(a, 0, scs_a), s_stage(b, 1, scs_b)

    sums = step(0, None, has_prev=False)
    sums = lax.fori_loop(1, NA_PAIRS_PER_STEP // 2, step, sums)
    for u in range(2):
        pi = NA_PAIRS_PER_STEP - 2 + u
        o_stage_store(pi, o_stage_pv(pi, u, sums[u]))


def _na(rpb, qw, k, vt):
    b, s, _ = k.shape
    tq = NA_ROWS_PER_STEP * GRID_W
    blks = s // TOK_BLK
    oob = jnp.full((GRID_W, 128), -jnp.inf, F32)
    return pl.pallas_call(
        _na_kernel,
        grid=(b, s // tq),
        in_specs=[pl.BlockSpec(memory_space=pltpu.SMEM),
                  pl.BlockSpec((NA_PAIRS_PER_STEP, 512, 2 * TOK_BLK),
                               lambda bi, ri: (bi * (blks // NA_PAIRS_PER_STEP) + ri, 0, 0)),
                  pl.BlockSpec((1, s, 512), lambda bi, ri: (bi, 0, 0)),
                  pl.BlockSpec((blks, 512, TOK_BLK), lambda bi, ri: (bi, 0, 0)),
                  _const_spec(oob.shape)],
        out_specs=pl.BlockSpec((1, tq, 512), lambda bi, ri: (bi, ri, 0)),
        out_shape=jax.ShapeDtypeStruct((b, s, 512), BF16),
        scratch_shapes=([pltpu.VMEM((NA_WIN_KEYS, 256), F32)] * 8
                        + [pltpu.VMEM((NA_WIN_KEYS, 256), BF16)] * 8
                        + [pltpu.VMEM((NA_HEADS // 2, 2 * NA_KH, GRID_W, 128), F32)]),
        compiler_params=_cparams(2),
        name="na",
    )(rpb, qw, k, vt, oob)


def _memkv_kernel(mem_ref, g_ref, w_ref, k_ref, v_ref):
    mn = _rms(mem_ref[0], g_ref[...]).astype(BF16)
    kv = jnp.dot(mn, w_ref[...], preferred_element_type=F32)
    k_ref[0] = kv[:, 0:512].astype(BF16)
    v_ref[0] = kv[:, 512:1024].astype(BF16)


def _memkv(mem, g, w):
    b, m, _ = mem.shape
    out = jax.ShapeDtypeStruct((b, m, 512), BF16)
    ospec = pl.BlockSpec((1, m, 512), lambda i: (i, 0, 0))
    return pl.pallas_call(
        _memkv_kernel,
        grid=(b,),
        in_specs=[pl.BlockSpec((1, m, D), lambda i: (i, 0, 0)),
                  _const_spec((1, D)),
                  _const_spec((D, 1024))],
        out_specs=[ospec, ospec],
        out_shape=[out, out],
        compiler_params=_cparams(1),
        name="memkv",
    )(mem, g, w)


def _mem_kernel(q_ref, k_ref, v_ref, o_ref):
    scale = MEM_HEAD_DIM ** -0.5
    for h in range(MEM_HEADS):
        cols = slice(MEM_HEAD_DIM * h, MEM_HEAD_DIM * (h + 1))
        sc = lax.dot_general(q_ref[0, :, cols], k_ref[0, :, cols], NT_DIMS,
                             preferred_element_type=F32) * scale
        m = jnp.max(sc, axis=-1, keepdims=True)
        p = jnp.exp(sc - m)
        l = jnp.sum(p, axis=-1, keepdims=True)
        o = jnp.dot(p.astype(BF16), v_ref[0, :, cols], preferred_element_type=F32)
        o_ref[0, :, cols] = (o / l).astype(BF16)


def _mem_attn(q, k, v):
    b, s, _ = q.shape
    m = k.shape[1]
    tq = 1024
    return pl.pallas_call(
        _mem_kernel,
        grid=(b, s // tq),
        in_specs=[pl.BlockSpec((1, tq, 512), lambda bi, i: (bi, i, 0)),
                  pl.BlockSpec((1, m, 512), lambda bi, i: (bi, 0, 0)),
                  pl.BlockSpec((1, m, 512), lambda bi, i: (bi, 0, 0))],
        out_specs=pl.BlockSpec((1, tq, 512), lambda bi, i: (bi, i, 0)),
        out_shape=jax.ShapeDtypeStruct((b, s, 512), BF16),
        compiler_params=_cparams(2),
        name="mem_attn",
    )(q, k, v)


def _merge_kernel(x_ref, g_ref, yna_ref, ys5_ref, ymem_ref, wg_ref, bg_ref, wb_ref, wo_ref, o_ref):
    x = x_ref[...]
    h = _rms(x, g_ref[...]).astype(BF16)
    merged = None
    for b, y_ref in enumerate((yna_ref, ys5_ref, ymem_ref)):
        cols = slice(D * b, D * (b + 1))
        gate = jax.nn.sigmoid(jnp.dot(h, wg_ref[:, cols], preferred_element_type=F32) + bg_ref[:, cols])
        up = jnp.dot(y_ref[...].astype(BF16), wb_ref[b], preferred_element_type=F32)
        merged = gate * up if merged is None else merged + gate * up
    o_ref[...] = x + jnp.dot(merged.astype(BF16), wo_ref[...], preferred_element_type=F32)


def _merge(x2d, g, yna, ys5, ymem, wg, bg, wb, wo):
    n = x2d.shape[0]
    tm = 1024
    yspec = pl.BlockSpec((tm, 512), lambda i: (i, 0))
    return pl.pallas_call(
        _merge_kernel,
        grid=(n // tm,),
        in_specs=[pl.BlockSpec((tm, D), lambda i: (i, 0)),
                  _const_spec((1, D)),
                  yspec, yspec, yspec,
                  _const_spec((D, 3 * D)),
                  _const_spec((1, 3 * D)),
                  _const_spec((3, 512, D)),
                  _const_spec((D, D))],
        out_specs=pl.BlockSpec((tm, D), lambda i: (i, 0)),
        out_shape=jax.ShapeDtypeStruct((n, D), F32),
        compiler_params=_cparams(1),
        name="merge",
    )(x2d, g, yna, ys5, ymem, wg, bg, wb, wo)


def _ffn_kernel(x_ref, g_ref, gf_ref, w1_ref, w3_ref, w2_ref, o_ref):
    x = x_ref[...]
    h = _rms(x, g_ref[...]).astype(BF16)
    a = jnp.dot(h, w1_ref[...], preferred_element_type=F32)
    c = jnp.dot(h, w3_ref[...], preferred_element_type=F32)
    mid = (a * jax.nn.sigmoid(a) * c).astype(BF16)
    x2 = x + jnp.dot(mid, w2_ref[...], preferred_element_type=F32)
    o_ref[...] = _rms(x2, gf_ref[...])


def _ffn(x2d, g, gf, w1, w3, w2):
    n = x2d.shape[0]
    tm = 512
    return pl.pallas_call(
        _ffn_kernel,
        grid=(n // tm,),
        in_specs=[pl.BlockSpec((tm, D), lambda i: (i, 0)),
                  _const_spec((1, D)),
                  _const_spec((1, D)),
                  _const_spec((D, D_FF)),
                  _const_spec((D, D_FF)),
                  _const_spec((D_FF, D))],
        out_specs=pl.BlockSpec((tm, D), lambda i: (i, 0)),
        out_shape=jax.ShapeDtypeStruct((n, D), F32),
        compiler_params=_cparams(1),
        name="ffn",
    )(x2d, g, gf, w1, w3, w2)


def _s5_tables(a_re, a_im, log_dt, b_re, b_im, c_re, c_im, s5_d):
    t = CHUNK
    ar, ai = a_re.astype(F32), a_im.astype(F32)
    dt = jnp.exp(log_dt.astype(F32))[..., None]
    lr, li = ar * dt, ai * dt
    mag = jnp.exp(lr)
    lbr, lbi = mag * jnp.cos(li), mag * jnp.sin(li)
    den = ar * ar + ai * ai
    rr = ((lbr - 1.0) * ar + lbi * ai) / den
    ri = (lbi * ar - (lbr - 1.0) * ai) / den
    br, bi = b_re.astype(F32), b_im.astype(F32)
    bbr = rr[..., None] * br - ri[..., None] * bi
    bbi = rr[..., None] * bi + ri[..., None] * br
    cmr, cmi = c_re.astype(F32), c_im.astype(F32)
    tau = jnp.arange(t + 1, dtype=F32)[:, None, None, None]
    pmag = jnp.exp(tau * lr[None])
    pwr, pwi = pmag * jnp.cos(tau * li[None]), pmag * jnp.sin(tau * li[None])

    dd = jnp.pad(jnp.eye(S5_GROUP, dtype=F32)[None] * s5_d.astype(F32).reshape(S5_GROUPS, S5_GROUP, 1),
                 ((0, 0), (0, 0), (128 - S5_GROUP, 0)))

    def lanes4(f0, f1, b0, b1):
        return jnp.concatenate([f0, f1, b0, b1], axis=-1)

    cm = jnp.stack([lanes4(cmr[0], -cmi[0], cmr[1], -cmi[1]),
                    lanes4(-cmi[0], -cmr[0], -cmi[1], -cmr[1])], axis=1)
    fr, fi = jnp.moveaxis(pwr[1:t + 1, 0], 0, 1), jnp.moveaxis(pwi[1:t + 1, 0], 0, 1)
    gr, gi = jnp.moveaxis(pwr[1:t + 1][::-1, 1], 0, 1), jnp.moveaxis(pwi[1:t + 1][::-1, 1], 0, 1)
    bt = lambda z: jnp.swapaxes(z, -1, -2)
    bb = jnp.stack([lanes4(bt(bbr[0]), bt(bbi[0]), bt(bbr[1]), bt(bbi[1])),
                    lanes4(-bt(bbi[0]), bt(bbr[0]), -bt(bbi[1]), bt(bbr[1]))], axis=1)
    wr, wi = jnp.moveaxis(pwr[:t][::-1, 0], 0, 1), jnp.moveaxis(pwi[:t][::-1, 0], 0, 1)
    vr, vi = jnp.moveaxis(pwr[:t, 1], 0, 1), jnp.moveaxis(pwi[:t, 1], 0, 1)
    pw = jnp.stack([lanes4(fr, fr, gr, gr), lanes4(fi, fi, gi, gi),
                    lanes4(wr, wr, vr, vr), lanes4(wi, wi, vi, vi)], axis=1)

    dr, di = pwr[t], pwi[t]
    a_rows, b_rows = [], []
    for _ in range(N_SCAN_LEVELS):
        a_rows.append(lanes4(dr[0], dr[0], dr[1], dr[1]))
        b_rows.append(lanes4(-di[0], di[0], -di[1], di[1]))
        dr, di = dr * dr - di * di, 2.0 * dr * di
    pad = [jnp.zeros_like(a_rows[0])] * (8 - N_SCAN_LEVELS)
    dtab = jnp.stack(a_rows + pad + b_rows + pad, axis=1)
    return cm, bb, pw, dtab, dd


def kernel(x, mem, g_mix, g_mem, g_ffn, g_final, w_in, w_gate, b_gate, rpb, w_mem_kv,
           a_re, a_im, log_dt, b_re, b_im, c_re, c_im, s5_d, w_glu, w_branch, w_o,
           w_ffn1, w_ffn3, w_ffn2):
    bsz, s, d = x.shape
    n = bsz * s
    x2d = x.reshape(n, d)
    gm = g_mix[0].reshape(1, d).astype(F32)

    w_in_b = w_in[0].astype(BF16)
    w_in_t = w_in[0].T.astype(BF16)
    cm_tab, bb_tab, pw_tab, dtab, dd_tab = _s5_tables(a_re[0], a_im[0], log_dt[0], b_re[0], b_im[0],
                                                      c_re[0], c_im[0], s5_d[0])

    k, qm, qw, vt = _proj(x2d, gm, w_in_b, w_in_t)

    ut = _proj_s5t(x.reshape(n // CHUNK, CHUNK, d), gm, w_in_t)
    zt = _s5_core(ut, cm_tab, bb_tab, pw_tab, dtab, dd_tab)
    y_s5 = _s5_glu(zt, w_glu[0].T.astype(BF16)).reshape(n, 512)

    y_na = _na(rpb[0].astype(F32), qw, k.reshape(bsz, s, 512), vt).reshape(n, 512)

    k_mem, v_mem = _memkv(mem, g_mem[0].reshape(1, d).astype(F32), w_mem_kv[0].astype(BF16))
    y_mem = _mem_attn(qm.reshape(bsz, s, 512), k_mem, v_mem).reshape(n, 512)

    x1 = _merge(x2d, gm, y_na, y_s5, y_mem, w_gate[0].astype(BF16),
                b_gate[0].reshape(1, 3 * d).astype(F32), w_branch[0].astype(BF16), w_o[0].astype(BF16))
    out = _ffn(x1, g_ffn[0].reshape(1, d).astype(F32), g_final.reshape(1, d).astype(F32),
               w_ffn1[0].astype(BF16), w_ffn3[0].astype(BF16), w_ffn2[0].astype(BF16))
    return out.reshape(bsz, s, d)
```

```python
import functools

import jax
import jax.numpy as jnp
from jax import lax
from jax.experimental import pallas as pl
from jax.experimental.pallas import tpu as pltpu

F32 = jnp.float32
BF16 = jnp.bfloat16

D = 1024
GRID_W = 64
NA_HEADS = 8
NA_KH = 8
NA_KW = 16
S5_GROUPS = 32
S5_GROUP = 16
S5_STATE = 64
CHUNK = 64
MEM_HEADS = 4
MEM_HEAD_DIM = 128
D_FF = 2816
EPS = 1e-6
NEG_INF = -1e30

VMEM_LIMIT = 56 * 1024 * 1024

NT_DIMS = (((1,), (1,)), ((), ()))


def _cparams(n_axes):
    return pltpu.CompilerParams(
        dimension_semantics=("arbitrary",) * n_axes,
        vmem_limit_bytes=VMEM_LIMIT)


def _rms(x, g):
    return x * lax.rsqrt(jnp.mean(x * x, axis=-1, keepdims=True) + EPS) * g


def _const_spec(shape):
    nd = len(shape)
    return pl.BlockSpec(shape, lambda *_: (0,) * nd, pipeline_mode=pl.Buffered(1))


TOK_BLK = 128


LOG2E = 1.4426950408889634
NA_Q_SCALE = 64 ** -0.5 * LOG2E


def _proj_kernel(x_ref, g_ref, wk_ref, wqm_ref, wqt_ref, wvt_ref, k_ref, qm_ref, qw_ref, vt_ref):
    h = _rms(x_ref[...], g_ref[...]).astype(BF16)
    k_ref[...] = jnp.dot(h, wk_ref[...], preferred_element_type=F32).astype(BF16)
    qm_ref[...] = jnp.dot(h, wqm_ref[...], preferred_element_type=F32).astype(BF16)
    qt = lax.dot_general(wqt_ref[...], h, NT_DIMS, preferred_element_type=F32) * NA_Q_SCALE
    vt = lax.dot_general(wvt_ref[...], h, NT_DIMS, preferred_element_type=F32)
    lo = lax.broadcasted_iota(jnp.int32, (128, 128), 1) < 64
    same_head = (lax.broadcasted_iota(jnp.int32, (128, 128), 0) < 64) == lo
    for i in range(x_ref.shape[0] // TOK_BLK):
        toks = slice(TOK_BLK * i, TOK_BLK * (i + 1))
        vt_ref[i] = vt[:, toks].astype(BF16)
        for s in range(4):
            a = qt[128 * s:128 * (s + 1), toks]
            ar = pltpu.roll(a, 64, axis=1)
            qw_ref[i, 128 * s:128 * (s + 1), :] = jnp.concatenate(
                [jnp.where(same_head, jnp.where(lo, a, ar), 0.0),
                 jnp.where(same_head, jnp.where(lo, ar, a), 0.0)], axis=1).astype(BF16)


W_IN_BLK = 512


def _w_in_spec(blk, transposed):
    if transposed:
        return pl.BlockSpec((W_IN_BLK, D), lambda *_: (blk, 0), pipeline_mode=pl.Buffered(1))
    return pl.BlockSpec((D, W_IN_BLK), lambda *_: (0, blk), pipeline_mode=pl.Buffered(1))


def _proj(x2d, g, w, wt):
    n = x2d.shape[0]
    tm = 1024
    nat = jax.ShapeDtypeStruct((n, 512), BF16)
    nat_spec = pl.BlockSpec((tm, 512), lambda i: (i, 0))
    chm = lambda width: (jax.ShapeDtypeStruct((n // TOK_BLK, 512, width), BF16),
                         pl.BlockSpec((tm // TOK_BLK, 512, width), lambda i: (i, 0, 0)))
    (qw_shape, qw_spec), (vt_shape, vt_spec) = chm(2 * TOK_BLK), chm(TOK_BLK)
    return pl.pallas_call(
        _proj_kernel,
        grid=(n // tm,),
        in_specs=[pl.BlockSpec((tm, D), lambda i: (i, 0)),
                  _const_spec((1, D)),
                  _w_in_spec(1, False),
                  _w_in_spec(4, False),
                  _w_in_spec(0, True),
                  _w_in_spec(2, True)],
        out_specs=[nat_spec, nat_spec, qw_spec, vt_spec],
        out_shape=[nat, nat, qw_shape, vt_shape],
        compiler_params=_cparams(1),
        name="proj",
    )(x2d, g, w, w, wt, wt)


S5T_DT = 8


def _proj_s5t_kernel(x_ref, g_ref, wt_ref, o_ref, h_s):
    wt = wt_ref[...]
    nrows = x_ref.shape[0]
    hn = _rms(x_ref[...], g_ref[...]).reshape(nrows * S5T_DT, D)
    for c in range(D // 128):
        h_s[c] = hn[:, 128 * c:128 * (c + 1)]
    for j in range(S5T_DT):
        h = jnp.concatenate([h_s[c, pl.ds(j, nrows, stride=S5T_DT), :] for c in range(D // 128)],
                            axis=1).astype(BF16)
        ut = lax.dot_general(wt, h, NT_DIMS, preferred_element_type=F32)
        o_ref[:, j * S5_GROUP:(j + 1) * S5_GROUP, :] = (
            ut.astype(BF16).reshape(S5_GROUPS, S5_GROUP, 256))


def _proj_s5t(xc, g, wt):
    nrows = xc.shape[0]
    return pl.pallas_call(
        _proj_s5t_kernel,
        grid=(CHUNK // S5T_DT,),
        in_specs=[pl.BlockSpec((nrows, S5T_DT, D), lambda i: (0, i, 0)),
                  _const_spec((1, D)),
                  _w_in_spec(3, True)],
        out_specs=pl.BlockSpec((S5_GROUPS, S5T_DT * S5_GROUP, nrows), lambda i: (0, i, 0)),
        out_shape=jax.ShapeDtypeStruct((S5_GROUPS, CHUNK * S5_GROUP, nrows), BF16),
        scratch_shapes=[pltpu.VMEM((D // 128, nrows * S5T_DT, 128), F32)],
        compiler_params=_cparams(1),
        name="proj_s5t",
    )(xc, g, wt)


N_SCAN_LEVELS = 6


def _gelu_tanh(x):
    c = 0.7978845608028654
    return 0.5 * x * (1.0 + jnp.tanh(c * (x + 0.044715 * (x * x * x))))


def _split_bf16(x):
    hi = x.astype(BF16)
    return hi, (x - hi.astype(F32)).astype(BF16)


def _nt_f32(a, b):
    ah, al = _split_bf16(a)
    bh, bl = _split_bf16(b)
    nt = lambda u, w: lax.dot_general(u, w, NT_DIMS, preferred_element_type=F32)
    return nt(ah, bh) + nt(ah, bl) + nt(al, bh)


S5_GROUPS_PER_STEP = 4
S5_SCRATCH_PER_GROUP = 5


def _s5_kernel(ut_ref, cm_ref, bb_ref, pw_ref, d_ref, dd_ref, z_ref, *scratch):
    groups = [scratch[S5_SCRATCH_PER_GROUP * gi:S5_SCRATCH_PER_GROUP * (gi + 1)]
              for gi in range(S5_GROUPS_PER_STEP)]
    for gi, (m_s, n_s, pt_s, ptf_s, rs_s) in enumerate(groups):
        _s5_operators(gi, cm_ref, bb_ref, pw_ref, dd_ref, m_s, n_s, pt_s, ptf_s, rs_s)
    yv = [_s5_chunk_matmuls(gi, ut_ref, m_s, pt_s) for gi, (m_s, _, pt_s, _, _) in enumerate(groups)]
    for gi, (_, n_s, _, _, _) in enumerate(groups):
        _s5_finish(gi, d_ref, z_ref, n_s, *yv[gi])


def _s5_operators(gi, cm_ref, bb_ref, pw_ref, dd_ref, m_s, n_s, pt_s, ptf_s, rs_s):
    tc = CHUNK * S5_GROUP
    cma, cmb = cm_ref[gi,0], cm_ref[gi,1]
    bba, bbb = bb_ref[gi,0], bb_ref[gi,1]
    for t in range(CHUNK):
        rows = slice(S5_GROUP * t, S5_GROUP * (t + 1))
        n_s[rows, :] = (cma * pw_ref[gi,0, t:t + 1, :] + cmb * pw_ref[gi,1, t:t + 1, :]).astype(BF16)
        ptf_s[rows, :] = bba * pw_ref[gi,2, t:t + 1, :] + bbb * pw_ref[gi,3, t:t + 1, :]
    ptf = ptf_s[...]
    pt_s[...] = ptf.astype(BF16)

    ra = _nt_f32(cma[:, 0:128], ptf[:, 0:128])
    rb = _nt_f32(cma[:, 128:256], ptf[:, 128:256])
    zeros = jnp.zeros((S5_GROUP, tc), F32)
    r = (jnp.concatenate([ra, zeros], axis=1)
         + pltpu.roll(jnp.concatenate([rb, zeros], axis=1), (CHUNK - 1) * S5_GROUP, axis=1)
         + jnp.concatenate([zeros[:, 0:tc - 128], dd_ref[gi], zeros], axis=1))
    for k in range(8):
        rk = r if k == 0 else pltpu.roll(r, 2 * tc - S5_GROUP * k, axis=1)
        rs_s[k] = rk[:, 0:RS_LANES].astype(BF16)
    for t in range(CHUNK):
        a, k = divmod(CHUNK - 1 - t, 8)
        m_s[S5_GROUP * t:S5_GROUP * (t + 1), :] = rs_s[k, :, 128 * a:128 * a + tc]


def _s5_chunk_matmuls(gi, ut_ref, m_s, pt_s):
    ut = ut_ref[gi]
    nrows = ut.shape[1]
    eye = jnp.where(lax.broadcasted_iota(jnp.int32, (nrows, nrows), 0)
                    == lax.broadcasted_iota(jnp.int32, (nrows, nrows), 1), 1.0, 0.0).astype(BF16)
    u_rows = lax.dot_general(eye, ut, NT_DIMS, preferred_element_type=F32).astype(BF16)
    y = jnp.dot(m_s[...], ut, preferred_element_type=F32)
    v = jnp.dot(u_rows, pt_s[...], preferred_element_type=F32)
    return y, v


def _s5_finish(gi, d_ref, z_ref, n_s, y, v):
    nrows = v.shape[0]
    pos = lax.broadcasted_iota(jnp.int32, (nrows, 128), 0) & (CHUNK - 1)

    def shift(x, s, up):
        if up:
            return jnp.where(pos < CHUNK - s, pltpu.roll(x, nrows - s, axis=0), 0.0)
        return jnp.where(pos >= s, pltpu.roll(x, s, axis=0), 0.0)

    def scan(vh, lanes, up):
        x = shift(vh, 1, up)
        for lvl in range(N_SCAN_LEVELS):
            xs = shift(x, 1 << lvl, up)
            a = d_ref[gi,lvl:lvl + 1, lanes]
            b = d_ref[gi,8 + lvl:9 + lvl, lanes]
            x = x + a * xs + b * pltpu.roll(xs, 64, axis=1)
        return x

    xf = scan(v[:, 0:128], slice(0, 128), False)
    xb = scan(v[:, 128:256], slice(128, 256), True)
    xin = jnp.concatenate([xf, xb], axis=1).astype(BF16)
    y = y + lax.dot_general(n_s[...], xin, NT_DIMS, preferred_element_type=F32)
    z_ref[gi] =_gelu_tanh(y).astype(BF16)


RS_LANES = 128 * 7 + CHUNK * S5_GROUP


def _s5_core(ut, cm, bb, pw, dtab, dd):
    nrows = ut.shape[2]
    tc = CHUNK * S5_GROUP
    gs = S5_GROUPS_PER_STEP
    return pl.pallas_call(
        _s5_kernel,
        grid=(S5_GROUPS // gs,),
        in_specs=[pl.BlockSpec((gs, tc, nrows), lambda g: (g, 0, 0)),
                  pl.BlockSpec((gs, 2, S5_GROUP, 256), lambda g: (g, 0, 0, 0)),
                  pl.BlockSpec((gs, 2, S5_GROUP, 256), lambda g: (g, 0, 0, 0)),
                  pl.BlockSpec((gs, 4, CHUNK, 256), lambda g: (g, 0, 0, 0)),
                  pl.BlockSpec((gs, 16, 256), lambda g: (g, 0, 0)),
                  pl.BlockSpec((gs, S5_GROUP, 128), lambda g: (g, 0, 0))],
        out_specs=pl.BlockSpec((gs, tc, nrows), lambda g: (g, 0, 0)),
        out_shape=jax.ShapeDtypeStruct((S5_GROUPS, tc, nrows), BF16),
        scratch_shapes=[pltpu.VMEM((tc, tc), BF16),
                        pltpu.VMEM((tc, 256), BF16),
                        pltpu.VMEM((tc, 256), BF16),
                        pltpu.VMEM((tc, 256), F32),
                        pltpu.VMEM((8, S5_GROUP, RS_LANES), BF16),
                        ] * gs,
        compiler_params=_cparams(1),
        name="s5_core",
    )(ut, cm, bb, pw, dtab, dd)


def _s5_glu_kernel(z_ref, wt_ref, o_ref):
    wt = wt_ref[...]
    nrows = z_ref.shape[2]
    eye = jnp.where(lax.broadcasted_iota(jnp.int32, (nrows, nrows), 0)
                    == lax.broadcasted_iota(jnp.int32, (nrows, nrows), 1), 1.0, 0.0).astype(BF16)
    for j in range(S5T_DT):
        zt = z_ref[:, j * S5_GROUP:(j + 1) * S5_GROUP, :].reshape(512, nrows)
        gl = jnp.dot(wt, zt, preferred_element_type=F32)
        o = (zt.astype(F32) * jax.nn.sigmoid(gl)).astype(BF16)
        nat = lax.dot_general(eye, o, NT_DIMS, preferred_element_type=F32)
        o_ref[:, j, :] = nat


def _s5_glu(zt, wglu_t):
    nrows = zt.shape[2]
    return pl.pallas_call(
        _s5_glu_kernel,
        grid=(CHUNK // S5T_DT,),
        in_specs=[pl.BlockSpec((S5_GROUPS, S5T_DT * S5_GROUP, nrows), lambda i: (0, i, 0)),
                  _const_spec((512, 512))],
        out_specs=pl.BlockSpec((nrows, S5T_DT, 512), lambda i: (0, i, 0)),
        out_shape=jax.ShapeDtypeStruct((nrows, CHUNK, 512), F32),
        compiler_params=_cparams(1),
        name="s5_glu",
    )(zt, wglu_t)


NA_ROWS_PER_STEP = 32
NA_PAIRS_PER_STEP = NA_ROWS_PER_STEP // 2
NA_WIN_ROWS = 10
NA_WIN_KEYS = NA_WIN_ROWS * GRID_W
NA_WIN_BLKS = NA_WIN_KEYS // TOK_BLK
NA_RR_OUTSIDE = 2 * NA_KH - 1


def _na_kernel(rpb_ref, qw_ref, k_ref, vt_ref, oob_ref, o_ref, *scratch):
    st_s, pt_s, bias_s = (scratch[0:4], scratch[4:8]), (scratch[8:12], scratch[12:16]), scratch[16]
    rb = pl.program_id(1)
    lo_q = lax.broadcasted_iota(jnp.int32, (GRID_W, 128), 1) < 64
    same_head2 = ((lax.broadcasted_iota(jnp.int32, (128, 256), 0) < 64)
                  == ((lax.broadcasted_iota(jnp.int32, (128, 256), 1) & 64) == 0))
    esum = jnp.where(lax.broadcasted_iota(jnp.int32, (GRID_W, 128), 0)
                     == (lax.broadcasted_iota(jnp.int32, (GRID_W, 128), 1) & (GRID_W - 1)),
                     1.0, 0.0).astype(BF16)
    half = NA_KH // 2

    kcol = lax.broadcasted_iota(jnp.int32, (GRID_W, 128), 0)
    qcol = lax.broadcasted_iota(jnp.int32, (GRID_W, 128), 1) & (GRID_W - 1)
    qstart = jnp.clip(qcol - NA_KW // 2, 0, GRID_W - NA_KW)
    col_ok = (kcol >= qstart) & (kcol < qstart + NA_KW)

    @pl.when((pl.program_id(0) == 0) & (rb == 0))
    def _build_bias():
        rel = jnp.clip(kcol - qcol + (NA_KW - 1), 0, 2 * NA_KW - 2)

        def rr_body(rr, carry):
            for s in range(4):
                acc = jnp.zeros((GRID_W, 128), F32)
                for j in range(2 * NA_KW - 1):
                    acc = jnp.where(rel == j, jnp.where(lo_q, rpb_ref[2 * s, rr, j], rpb_ref[2 * s + 1, rr, j]), acc)
                bias_s[s, rr] = acc * LOG2E
            return carry

        lax.fori_loop(0, NA_RR_OUTSIDE, rr_body, 0)
        for s in range(4):
            bias_s[s, NA_RR_OUTSIDE] = oob_ref[...]

    def window(pi):
        r0 = rb * NA_ROWS_PER_STEP + 2 * pi
        return r0, jnp.clip(r0 - half, 0, GRID_W - NA_WIN_ROWS)

    def q_stage(pi):
        _, win = window(pi)
        koff = pl.multiple_of(win * GRID_W, TOK_BLK)
        scs = []
        for s in range(4):
            ch = slice(128 * s, 128 * (s + 1))
            kw = k_ref[0, pl.ds(koff, NA_WIN_KEYS), ch]
            scs.append(jnp.dot(kw, qw_ref[pi, ch, :], preferred_element_type=F32))
        return scs

    def s_stage(pi, par, scs):
        r0, win = window(pi)
        rr = []
        for p in range(2):
            r = r0 + p
            first = jnp.clip(r - half, 0, GRID_W - NA_KH) - win
            rel = win - r + (NA_KH - 1)
            rr.append([jnp.where((wr >= first) & (wr < first + NA_KH), wr + rel, NA_RR_OUTSIDE)
                       for wr in range(NA_WIN_ROWS)])
        sums = []
        for s in range(4):
            st = st_s[par][s]
            slab_sums = []
            for p in range(2):
                lanes = slice(128 * p, 128 * (p + 1))
                m = None
                for wr in range(NA_WIN_ROWS):
                    rows = slice(GRID_W * wr, GRID_W * (wr + 1))
                    t = jnp.where(col_ok, scs[s][rows, lanes] + bias_s[s, rr[p][wr]], NEG_INF)
                    st[rows, lanes] = t
                    m = t if m is None else jnp.maximum(m, t)
                m = jnp.max(m, axis=0, keepdims=True)
                l = None
                for wr in range(NA_WIN_ROWS):
                    rows = slice(GRID_W * wr, GRID_W * (wr + 1))
                    e = jnp.exp2(st[rows, lanes] - m)
                    pt_s[par][s][rows, lanes] = e.astype(BF16)
                    l = e if l is None else l + e
                slab_sums.append(jnp.sum(l, axis=0, keepdims=True))
            sums.append(jnp.concatenate(slab_sums, axis=1))
        return tuple(sums)

    def o_stage_pv(pi, par, sums):
        _, win = window(pi)
        blk0 = lax.shift_right_logical(win, 1)
        ots = []
        for s in range(4):
            ch = slice(128 * s, 128 * (s + 1))
            vw = jnp.concatenate([vt_ref[blk0 + i, ch, :] for i in range(NA_WIN_BLKS)], axis=1)
            ot = jnp.dot(vw, pt_s[par][s][...], preferred_element_type=F32)
            ots.append(jnp.where(same_head2, (ot / sums[s]).astype(BF16), jnp.zeros((), BF16)))
        return ots

    def o_stage_store(pi, ots):
        for s in range(4):
            ch = slice(128 * s, 128 * (s + 1))
            for p in range(2):
                nat = lax.dot_general(esum, ots[s][:, 128 * p:128 * (p + 1)], NT_DIMS,
                                      preferred_element_type=F32)
                qoff = pl.multiple_of((2 * pi + p) * GRID_W, GRID_W)
                o_ref[0, pl.ds(qoff, GRID_W), ch] = nat.astype(BF16)

    def step(j, prev_sums, has_prev=True):
        a, b = 2 * j, 2 * j + 1
        scs_a = q_stage(a)
        if has_prev:
            ots_a = o_stage_pv(a - 2, 0, prev_sums[0])
            ots_b = o_stage_pv(b - 2, 1, prev_sums[1])
        scs_b = q_stage(b)
        if has_prev:
            o_stage_store(a - 2, ots_a)
            o_stage_store(b - 2, ots_b)
        return s_stage(a, 0, scs_a), s_stage(b, 1, scs_b)

    sums = step(0, None, has_prev=False)
    sums = lax.fori_loop(1, NA_PAIRS_PER_STEP // 2, step, sums)
    for u in range(2):
        pi = NA_PAIRS_PER_STEP - 2 + u
        o_stage_store(pi, o_stage_pv(pi, u, sums[u]))


def _na(rpb, qw, k, vt):
    b, s, _ = k.shape
    tq = NA_ROWS_PER_STEP * GRID_W
    blks = s // TOK_BLK
    oob = jnp.full((GRID_W, 128), -jnp.inf, F32)
    return pl.pallas_call(
        _na_kernel,
        grid=(b, s // tq),
        in_specs=[pl.BlockSpec(memory_space=pltpu.SMEM),
                  pl.BlockSpec((NA_PAIRS_PER_STEP, 512, 2 * TOK_BLK),
                               lambda bi, ri: (bi * (blks // NA_PAIRS_PER_STEP) + ri, 0, 0)),
                  pl.BlockSpec((1, s, 512), lambda bi, ri: (bi, 0, 0)),
                  pl.BlockSpec((blks, 512, TOK_BLK), lambda bi, ri: (bi, 0, 0)),
                  _const_spec(oob.shape)],
        out_specs=pl.BlockSpec((1, tq, 512), lambda bi, ri: (bi, ri, 0)),
        out_shape=jax.ShapeDtypeStruct((b, s, 512), BF16),
        scratch_shapes=([pltpu.VMEM((NA_WIN_KEYS, 256), F32)] * 8
                        + [pltpu.VMEM((NA_WIN_KEYS, 256), BF16)] * 8
                        + [pltpu.VMEM((NA_HEADS // 2, 2 * NA_KH, GRID_W, 128), F32)]),
        compiler_params=_cparams(2),
        name="na",
    )(rpb, qw, k, vt, oob)


def _memkv_kernel(mem_ref, g_ref, w_ref, k_ref, v_ref):
    mn = _rms(mem_ref[0], g_ref[...]).astype(BF16)
    kv = jnp.dot(mn, w_ref[...], preferred_element_type=F32)
    k_ref[0] = kv[:, 0:512].astype(BF16)
    v_ref[0] = kv[:, 512:1024].astype(BF16)


def _memkv(mem, g, w):
    b, m, _ = mem.shape
    out = jax.ShapeDtypeStruct((b, m, 512), BF16)
    ospec = pl.BlockSpec((1, m, 512), lambda i: (i, 0, 0))
    return pl.pallas_call(
        _memkv_kernel,
        grid=(b,),
        in_specs=[pl.BlockSpec((1, m, D), lambda i: (i, 0, 0)),
                  _const_spec((1, D)),
                  _const_spec((D, 1024))],
        out_specs=[ospec, ospec],
        out_shape=[out, out],
        compiler_params=_cparams(1),
        name="memkv",
    )(mem, g, w)


def _mem_kernel(q_ref, k_ref, v_ref, o_ref):
    scale = MEM_HEAD_DIM ** -0.5
    for h in range(MEM_HEADS):
        cols = slice(MEM_HEAD_DIM * h, MEM_HEAD_DIM * (h + 1))
        sc = lax.dot_general(q_ref[0, :, cols], k_ref[0, :, cols], NT_DIMS,
                             preferred_element_type=F32) * scale
        m = jnp.max(sc, axis=-1, keepdims=True)
        p = jnp.exp(sc - m)
        l = jnp.sum(p, axis=-1, keepdims=True)
        o = jnp.dot(p.astype(BF16), v_ref[0, :, cols], preferred_element_type=F32)
        o_ref[0, :, cols] = (o / l).astype(BF16)


def _mem_attn(q, k, v):
    b, s, _ = q.shape
    m = k.shape[1]
    tq = 1024
    return pl.pallas_call(
        _mem_kernel,
        grid=(b, s // tq),
        in_specs=[pl.BlockSpec((1, tq, 512), lambda bi, i: (bi, i, 0)),
                  pl.BlockSpec((1, m, 512), lambda bi, i: (bi, 0, 0)),
                  pl.BlockSpec((1, m, 512), lambda bi, i: (bi, 0, 0))],
        out_specs=pl.BlockSpec((1, tq, 512), lambda bi, i: (bi, i, 0)),
        out_shape=jax.ShapeDtypeStruct((b, s, 512), BF16),
        compiler_params=_cparams(2),
        name="mem_attn",
    )(q, k, v)


def _merge_kernel(x_ref, g_ref, yna_ref, ys5_ref, ymem_ref, wg_ref, bg_ref, wb_ref, wo_ref, o_ref):
    x = x_ref[...]
    h = _rms(x, g_ref[...]).astype(BF16)
    merged = None
    ys = (yna_ref[...], ys5_ref[...].astype(BF16), ymem_ref[...])
    for b, y in enumerate(ys):
        cols = slice(D * b, D * (b + 1))
        gate = jax.nn.sigmoid(jnp.dot(h, wg_ref[:, cols], preferred_element_type=F32) + bg_ref[:, cols])
        up = jnp.dot(y, wb_ref[b], preferred_element_type=F32)
        merged = gate * up if merged is None else merged + gate * up
    o_ref[...] = x + jnp.dot(merged.astype(BF16), wo_ref[...], preferred_element_type=F32)


def _merge(x2d, g, yna, ys5, ymem, wg, bg, wb, wo):
    n = x2d.shape[0]
    tm = 1024
    yspec = pl.BlockSpec((tm, 512), lambda i: (i, 0))
    return pl.pallas_call(
        _merge_kernel,
        grid=(n // tm,),
        in_specs=[pl.BlockSpec((tm, D), lambda i: (i, 0)),
                  _const_spec((1, D)),
                  yspec, yspec, yspec,
                  _const_spec((D, 3 * D)),
                  _const_spec((1, 3 * D)),
                  _const_spec((3, 512, D)),
                  _const_spec((D, D))],
        out_specs=pl.BlockSpec((tm, D), lambda i: (i, 0)),
        out_shape=jax.ShapeDtypeStruct((n, D), F32),
        compiler_params=_cparams(1),
        name="merge",
    )(x2d, g, yna, ys5, ymem, wg, bg, wb, wo)


def _ffn_kernel(x_ref, g_ref, gf_ref, w1_ref, w3_ref, w2_ref, o_ref):
    x = x_ref[...]
    h = _rms(x, g_ref[...]).astype(BF16)
    a = jnp.dot(h, w1_ref[...], preferred_element_type=F32)
    c = jnp.dot(h, w3_ref[...], preferred_element_type=F32)
    mid = (a * jax.nn.sigmoid(a) * c).astype(BF16)
    x2 = x + jnp.dot(mid, w2_ref[...], preferred_element_type=F32)
    o_ref[...] = _rms(x2, gf_ref[...])


def _ffn(x2d, g, gf, w1, w3, w2):
    n = x2d.shape[0]
    tm = 512
    return pl.pallas_call(
        _ffn_kernel,
        grid=(n // tm,),
        in_specs=[pl.BlockSpec((tm, D), lambda i: (i, 0)),
                  _const_spec((1, D)),
                  _const_spec((1, D)),
                  _const_spec((D, D_FF)),
                  _const_spec((D, D_FF)),
                  _const_spec((D_FF, D))],
        out_specs=pl.BlockSpec((tm, D), lambda i: (i, 0)),
        out_shape=jax.ShapeDtypeStruct((n, D), F32),
        compiler_params=_cparams(1),
        name="ffn",
    )(x2d, g, gf, w1, w3, w2)


def _s5_tables(a_re, a_im, log_dt, b_re, b_im, c_re, c_im, s5_d):
    t = CHUNK
    ar, ai = a_re.astype(F32), a_im.astype(F32)
    dt = jnp.exp(log_dt.astype(F32))[..., None]
    lr, li = ar * dt, ai * dt
    mag = jnp.exp(lr)
    lbr, lbi = mag * jnp.cos(li), mag * jnp.sin(li)
    den = ar * ar + ai * ai
    rr = ((lbr - 1.0) * ar + lbi * ai) / den
    ri = (lbi * ar - (lbr - 1.0) * ai) / den
    br, bi = b_re.astype(F32), b_im.astype(F32)
    bbr = rr[..., None] * br - ri[..., None] * bi
    bbi = rr[..., None] * bi + ri[..., None] * br
    cmr, cmi = c_re.astype(F32), c_im.astype(F32)
    tau = jnp.arange(t + 1, dtype=F32)[:, None, None, None]
    pmag = jnp.exp(tau * lr[None])
    pwr, pwi = pmag * jnp.cos(tau * li[None]), pmag * jnp.sin(tau * li[None])

    dd = jnp.pad(jnp.eye(S5_GROUP, dtype=F32)[None] * s5_d.astype(F32).reshape(S5_GROUPS, S5_GROUP, 1),
                 ((0, 0), (0, 0), (128 - S5_GROUP, 0)))

    def lanes4(f0, f1, b0, b1):
        return jnp.concatenate([f0, f1, b0, b1], axis=-1)

    cm = jnp.stack([lanes4(cmr[0], -cmi[0], cmr[1], -cmi[1]),
                    lanes4(-cmi[0], -cmr[0], -cmi[1], -cmr[1])], axis=1)
    fr, fi = jnp.moveaxis(pwr[1:t + 1, 0], 0, 1), jnp.moveaxis(pwi[1:t + 1, 0], 0, 1)
    gr, gi = jnp.moveaxis(pwr[1:t + 1][::-1, 1], 0, 1), jnp.moveaxis(pwi[1:t + 1][::-1, 1], 0, 1)
    bt = lambda z: jnp.swapaxes(z, -1, -2)
    bb = jnp.stack([lanes4(bt(bbr[0]), bt(bbi[0]), bt(bbr[1]), bt(bbi[1])),
                    lanes4(-bt(bbi[0]), bt(bbr[0]), -bt(bbi[1]), bt(bbr[1]))], axis=1)
    wr, wi = jnp.moveaxis(pwr[:t][::-1, 0], 0, 1), jnp.moveaxis(pwi[:t][::-1, 0], 0, 1)
    vr, vi = jnp.moveaxis(pwr[:t, 1], 0, 1), jnp.moveaxis(pwi[:t, 1], 0, 1)
    pw = jnp.stack([lanes4(fr, fr, gr, gr), lanes4(fi, fi, gi, gi),
                    lanes4(wr, wr, vr, vr), lanes4(wi, wi, vi, vi)], axis=1)

    dr, di = pwr[t], pwi[t]
    a_rows, b_rows = [], []
    for _ in range(N_SCAN_LEVELS):
        a_rows.append(lanes4(dr[0], dr[0], dr[1], dr[1]))
        b_rows.append(lanes4(-di[0], di[0], -di[1], di[1]))
        dr, di = dr * dr - di * di, 2.0 * dr * di
    pad = [jnp.zeros_like(a_rows[0])] * (8 - N_SCAN_LEVELS)
    dtab = jnp.stack(a_rows + pad + b_rows + pad, axis=1)
    return cm, bb, pw, dtab, dd


def kernel(x, mem, g_mix, g_mem, g_ffn, g_final, w_in, w_gate, b_gate, rpb, w_mem_kv,
           a_re, a_im, log_dt, b_re, b_im, c_re, c_im, s5_d, w_glu, w_branch, w_o,
           w_ffn1, w_ffn3, w_ffn2):
    bsz, s, d = x.shape
    n = bsz * s
    x2d = x.reshape(n, d)
    gm = g_mix[0].reshape(1, d).astype(F32)

    w_in_b = w_in[0].astype(BF16)
    w_in_t = w_in[0].T.astype(BF16)
    cm_tab, bb_tab, pw_tab, dtab, dd_tab = _s5_tables(a_re[0], a_im[0], log_dt[0], b_re[0], b_im[0],
                                                      c_re[0], c_im[0], s5_d[0])

    k, qm, qw, vt = _proj(x2d, gm, w_in_b, w_in_t)

    ut = _proj_s5t(x.reshape(n // CHUNK, CHUNK, d), gm, w_in_t)
    zt = _s5_core(ut, cm_tab, bb_tab, pw_tab, dtab, dd_tab)
    y_s5 = _s5_glu(zt, w_glu[0].T.astype(BF16)).reshape(n, 512)

    y_na = _na(rpb[0].astype(F32), qw, k.reshape(bsz, s, 512), vt).reshape(n, 512)

    k_mem, v_mem = _memkv(mem, g_mem[0].reshape(1, d).astype(F32), w_mem_kv[0].astype(BF16))
    y_mem = _mem_attn(qm.reshape(bsz, s, 512), k_mem, v_mem).reshape(n, 512)

    x1 = _merge(x2d, gm, y_na, y_s5, y_mem, w_gate[0].astype(BF16),
                b_gate[0].reshape(1, 3 * d).astype(F32), w_branch[0].astype(BF16), w_o[0].astype(BF16))
    out = _ffn(x1, g_ffn[0].reshape(1, d).astype(F32), g_final.reshape(1, d).astype(F32),
               w_ffn1[0].astype(BF16), w_ffn3[0].astype(BF16), w_ffn2[0].astype(BF16))
    return out.reshape(bsz, s, d)
```

```python
import functools

import jax
import jax.numpy as jnp
from jax import lax
from jax.experimental import pallas as pl
from jax.experimental.pallas import tpu as pltpu

F32 = jnp.float32
BF16 = jnp.bfloat16

D = 1024
GRID_W = 64
NA_HEADS = 8
NA_KH = 8
NA_KW = 16
S5_GROUPS = 32
S5_GROUP = 16
S5_STATE = 64
CHUNK = 64
MEM_HEADS = 4
MEM_HEAD_DIM = 128
D_FF = 2816
EPS = 1e-6
NEG_INF = -1e30

VMEM_LIMIT = 56 * 1024 * 1024

NT_DIMS = (((1,), (1,)), ((), ()))


def _cparams(n_axes):
    return pltpu.CompilerParams(
        dimension_semantics=("arbitrary",) * n_axes,
        vmem_limit_bytes=VMEM_LIMIT)


def _rms(x, g):
    return x * lax.rsqrt(jnp.mean(x * x, axis=-1, keepdims=True) + EPS) * g


def _const_spec(shape):
    nd = len(shape)
    return pl.BlockSpec(shape, lambda *_: (0,) * nd, pipeline_mode=pl.Buffered(1))


TOK_BLK = 128


LOG2E = 1.4426950408889634
NA_Q_SCALE = 64 ** -0.5 * LOG2E


def _proj_kernel(x_ref, g_ref, wk_ref, wqm_ref, wqt_ref, wvt_ref, k_ref, qm_ref, qw_ref, vt_ref):
    h = _rms(x_ref[...], g_ref[...]).astype(BF16)
    k_ref[...] = jnp.dot(h, wk_ref[...].astype(BF16), preferred_element_type=F32).astype(BF16)
    qm_ref[...] = jnp.dot(h, wqm_ref[...].astype(BF16), preferred_element_type=F32).astype(BF16)
    qt = lax.dot_general(wqt_ref[...], h, NT_DIMS, preferred_element_type=F32) * NA_Q_SCALE
    vt = lax.dot_general(wvt_ref[...], h, NT_DIMS, preferred_element_type=F32)
    lo = lax.broadcasted_iota(jnp.int32, (128, 128), 1) < 64
    same_head = (lax.broadcasted_iota(jnp.int32, (128, 128), 0) < 64) == lo
    for i in range(x_ref.shape[0] // TOK_BLK):
        toks = slice(TOK_BLK * i, TOK_BLK * (i + 1))
        vt_ref[i] = vt[:, toks].astype(BF16)
        for s in range(4):
            a = qt[128 * s:128 * (s + 1), toks]
            ar = pltpu.roll(a, 64, axis=1)
            qw_ref[i, 128 * s:128 * (s + 1), :] = jnp.concatenate(
                [jnp.where(same_head, jnp.where(lo, a, ar), 0.0),
                 jnp.where(same_head, jnp.where(lo, ar, a), 0.0)], axis=1).astype(BF16)


W_IN_BLK = 512


def _w_in_spec(blk, transposed):
    if transposed:
        return pl.BlockSpec((W_IN_BLK, D), lambda *_: (blk, 0), pipeline_mode=pl.Buffered(1))
    return pl.BlockSpec((D, W_IN_BLK), lambda *_: (0, blk), pipeline_mode=pl.Buffered(1))


def _proj(x2d, g, w, wt):
    n = x2d.shape[0]
    tm = 1024
    nat = jax.ShapeDtypeStruct((n, 512), BF16)
    nat_spec = pl.BlockSpec((tm, 512), lambda i: (i, 0))
    chm = lambda width: (jax.ShapeDtypeStruct((n // TOK_BLK, 512, width), BF16),
                         pl.BlockSpec((tm // TOK_BLK, 512, width), lambda i: (i, 0, 0)))
    (qw_shape, qw_spec), (vt_shape, vt_spec) = chm(2 * TOK_BLK), chm(TOK_BLK)
    return pl.pallas_call(
        _proj_kernel,
        grid=(n // tm,),
        in_specs=[pl.BlockSpec((tm, D), lambda i: (i, 0)),
                  _const_spec((1, D)),
                  _w_in_spec(1, False),
                  _w_in_spec(4, False),
                  _w_in_spec(0, True),
                  _w_in_spec(2, True)],
        out_specs=[nat_spec, nat_spec, qw_spec, vt_spec],
        out_shape=[nat, nat, qw_shape, vt_shape],
        compiler_params=_cparams(1),
        name="proj",
    )(x2d, g, w, w, wt, wt)


S5T_DT = 8


def _proj_s5t_kernel(x_ref, g_ref, wt_ref, o_ref, h_s):
    wt = wt_ref[...]
    nrows = x_ref.shape[0]
    hn = _rms(x_ref[...], g_ref[...]).reshape(nrows * S5T_DT, D)
    for c in range(D // 128):
        h_s[c] = hn[:, 128 * c:128 * (c + 1)]
    for j in range(S5T_DT):
        h = jnp.concatenate([h_s[c, pl.ds(j, nrows, stride=S5T_DT), :] for c in range(D // 128)],
                            axis=1).astype(BF16)
        ut = lax.dot_general(wt, h, NT_DIMS, preferred_element_type=F32)
        o_ref[:, j * S5_GROUP:(j + 1) * S5_GROUP, :] = (
            ut.astype(BF16).reshape(S5_GROUPS, S5_GROUP, 256))


def _proj_s5t(xc, g, wt):
    nrows = xc.shape[0]
    return pl.pallas_call(
        _proj_s5t_kernel,
        grid=(CHUNK // S5T_DT,),
        in_specs=[pl.BlockSpec((nrows, S5T_DT, D), lambda i: (0, i, 0)),
                  _const_spec((1, D)),
                  _w_in_spec(3, True)],
        out_specs=pl.BlockSpec((S5_GROUPS, S5T_DT * S5_GROUP, nrows), lambda i: (0, i, 0)),
        out_shape=jax.ShapeDtypeStruct((S5_GROUPS, CHUNK * S5_GROUP, nrows), BF16),
        scratch_shapes=[pltpu.VMEM((D // 128, nrows * S5T_DT, 128), F32)],
        compiler_params=_cparams(1),
        name="proj_s5t",
    )(xc, g, wt)


N_SCAN_LEVELS = 6


def _gelu_tanh(x):
    c = 0.7978845608028654
    return 0.5 * x * (1.0 + jnp.tanh(c * (x + 0.044715 * (x * x * x))))


def _split_bf16(x):
    hi = x.astype(BF16)
    return hi, (x - hi.astype(F32)).astype(BF16)


def _nt_f32(a, b):
    ah, al = _split_bf16(a)
    bh, bl = _split_bf16(b)
    nt = lambda u, w: lax.dot_general(u, w, NT_DIMS, preferred_element_type=F32)
    return nt(ah, bh) + nt(ah, bl) + nt(al, bh)


S5_GROUPS_PER_STEP = 4
S5_SCRATCH_PER_GROUP = 5


def _s5_kernel(ut_ref, cm_ref, bb_ref, pw_ref, d_ref, dd_ref, z_ref, *scratch):
    groups = [scratch[S5_SCRATCH_PER_GROUP * gi:S5_SCRATCH_PER_GROUP * (gi + 1)]
              for gi in range(S5_GROUPS_PER_STEP)]
    for gi, (m_s, n_s, pt_s, ptf_s, rs_s) in enumerate(groups):
        _s5_operators(gi, cm_ref, bb_ref, pw_ref, dd_ref, m_s, n_s, pt_s, ptf_s, rs_s)
    yv = [_s5_chunk_matmuls(gi, ut_ref, m_s, pt_s) for gi, (m_s, _, pt_s, _, _) in enumerate(groups)]
    for gi, (_, n_s, _, _, _) in enumerate(groups):
        _s5_finish(gi, d_ref, z_ref, n_s, *yv[gi])


def _s5_operators(gi, cm_ref, bb_ref, pw_ref, dd_ref, m_s, n_s, pt_s, ptf_s, rs_s):
    tc = CHUNK * S5_GROUP
    cma, cmb = cm_ref[gi,0], cm_ref[gi,1]
    bba, bbb = bb_ref[gi,0], bb_ref[gi,1]
    for t in range(CHUNK):
        rows = slice(S5_GROUP * t, S5_GROUP * (t + 1))
        n_s[rows, :] = (cma * pw_ref[gi,0, t:t + 1, :] + cmb * pw_ref[gi,1, t:t + 1, :]).astype(BF16)
        ptf_s[rows, :] = bba * pw_ref[gi,2, t:t + 1, :] + bbb * pw_ref[gi,3, t:t + 1, :]
    ptf = ptf_s[...]
    pt_s[...] = ptf.astype(BF16)

    ra = _nt_f32(cma[:, 0:128], ptf[:, 0:128])
    rb = _nt_f32(cma[:, 128:256], ptf[:, 128:256])
    zeros = jnp.zeros((S5_GROUP, tc), F32)
    r = (jnp.concatenate([ra, zeros], axis=1)
         + pltpu.roll(jnp.concatenate([rb, zeros], axis=1), (CHUNK - 1) * S5_GROUP, axis=1)
         + jnp.concatenate([zeros[:, 0:tc - 128], dd_ref[gi], zeros], axis=1))
    for k in range(8):
        rk = r if k == 0 else pltpu.roll(r, 2 * tc - S5_GROUP * k, axis=1)
        rs_s[k] = rk[:, 0:RS_LANES].astype(BF16)
    for t in range(CHUNK):
        a, k = divmod(CHUNK - 1 - t, 8)
        m_s[S5_GROUP * t:S5_GROUP * (t + 1), :] = rs_s[k, :, 128 * a:128 * a + tc]


def _s5_chunk_matmuls(gi, ut_ref, m_s, pt_s):
    ut = ut_ref[gi]
    nrows = ut.shape[1]
    eye = jnp.where(lax.broadcasted_iota(jnp.int32, (nrows, nrows), 0)
                    == lax.broadcasted_iota(jnp.int32, (nrows, nrows), 1), 1.0, 0.0).astype(BF16)
    u_rows = lax.dot_general(eye, ut, NT_DIMS, preferred_element_type=F32).astype(BF16)
    y = jnp.dot(m_s[...], ut, preferred_element_type=F32)
    v = jnp.dot(u_rows, pt_s[...], preferred_element_type=F32)
    return y, v


def _s5_finish(gi, d_ref, z_ref, n_s, y, v):
    nrows = v.shape[0]
    pos = lax.broadcasted_iota(jnp.int32, (nrows, 128), 0) & (CHUNK - 1)

    def shift(x, s, up):
        if up:
            return jnp.where(pos < CHUNK - s, pltpu.roll(x, nrows - s, axis=0), 0.0)
        return jnp.where(pos >= s, pltpu.roll(x, s, axis=0), 0.0)

    def scan(vh, lanes, up):
        x = shift(vh, 1, up)
        for lvl in range(N_SCAN_LEVELS):
            xs = shift(x, 1 << lvl, up)
            a = d_ref[gi,lvl:lvl + 1, lanes]
            b = d_ref[gi,8 + lvl:9 + lvl, lanes]
            x = x + a * xs + b * pltpu.roll(xs, 64, axis=1)
        return x

    xf = scan(v[:, 0:128], slice(0, 128), False)
    xb = scan(v[:, 128:256], slice(128, 256), True)
    xin = jnp.concatenate([xf, xb], axis=1).astype(BF16)
    y = y + lax.dot_general(n_s[...], xin, NT_DIMS, preferred_element_type=F32)
    z_ref[gi] =_gelu_tanh(y).astype(BF16)


RS_LANES = 128 * 7 + CHUNK * S5_GROUP


def _s5_core(ut, cm, bb, pw, dtab, dd):
    nrows = ut.shape[2]
    tc = CHUNK * S5_GROUP
    gs = S5_GROUPS_PER_STEP
    return pl.pallas_call(
        _s5_kernel,
        grid=(S5_GROUPS // gs,),
        in_specs=[pl.BlockSpec((gs, tc, nrows), lambda g: (g, 0, 0)),
                  pl.BlockSpec((gs, 2, S5_GROUP, 256), lambda g: (g, 0, 0, 0)),
                  pl.BlockSpec((gs, 2, S5_GROUP, 256), lambda g: (g, 0, 0, 0)),
                  pl.BlockSpec((gs, 4, CHUNK, 256), lambda g: (g, 0, 0, 0)),
                  pl.BlockSpec((gs, 16, 256), lambda g: (g, 0, 0)),
                  pl.BlockSpec((gs, S5_GROUP, 128), lambda g: (g, 0, 0))],
        out_specs=pl.BlockSpec((gs, tc, nrows), lambda g: (g, 0, 0)),
        out_shape=jax.ShapeDtypeStruct((S5_GROUPS, tc, nrows), BF16),
        scratch_shapes=[pltpu.VMEM((tc, tc), BF16),
                        pltpu.VMEM((tc, 256), BF16),
                        pltpu.VMEM((tc, 256), BF16),
                        pltpu.VMEM((tc, 256), F32),
                        pltpu.VMEM((8, S5_GROUP, RS_LANES), BF16),
                        ] * gs,
        compiler_params=_cparams(1),
        name="s5_core",
    )(ut, cm, bb, pw, dtab, dd)


def _s5_glu_kernel(z_ref, wt_ref, o_ref):
    wt = wt_ref[...]
    nrows = z_ref.shape[2]
    eye = jnp.where(lax.broadcasted_iota(jnp.int32, (nrows, nrows), 0)
                    == lax.broadcasted_iota(jnp.int32, (nrows, nrows), 1), 1.0, 0.0).astype(BF16)
    for j in range(S5T_DT):
        zt = z_ref[:, j * S5_GROUP:(j + 1) * S5_GROUP, :].reshape(512, nrows)
        gl = jnp.dot(wt, zt, preferred_element_type=F32)
        o = (zt.astype(F32) * jax.nn.sigmoid(gl)).astype(BF16)
        nat = lax.dot_general(eye, o, NT_DIMS, preferred_element_type=F32)
        o_ref[:, j, :] = nat


def _s5_glu(zt, wglu_t):
    nrows = zt.shape[2]
    return pl.pallas_call(
        _s5_glu_kernel,
        grid=(CHUNK // S5T_DT,),
        in_specs=[pl.BlockSpec((S5_GROUPS, S5T_DT * S5_GROUP, nrows), lambda i: (0, i, 0)),
                  _const_spec((512, 512))],
        out_specs=pl.BlockSpec((nrows, S5T_DT, 512), lambda i: (0, i, 0)),
        out_shape=jax.ShapeDtypeStruct((nrows, CHUNK, 512), F32),
        compiler_params=_cparams(1),
        name="s5_glu",
    )(zt, wglu_t)


NA_ROWS_PER_STEP = 32
NA_PAIRS_PER_STEP = NA_ROWS_PER_STEP // 2
NA_WIN_ROWS = 10
NA_WIN_KEYS = NA_WIN_ROWS * GRID_W
NA_WIN_BLKS = NA_WIN_KEYS // TOK_BLK
NA_RR_OUTSIDE = 2 * NA_KH - 1


def _na_kernel(rpb_ref, qw_ref, k_ref, vt_ref, oob_ref, o_ref, *scratch):
    st_s, pt_s, bias_s = (scratch[0:4], scratch[4:8]), (scratch[8:12], scratch[12:16]), scratch[16]
    rb = pl.program_id(1)
    lo_q = lax.broadcasted_iota(jnp.int32, (GRID_W, 128), 1) < 64
    same_head2 = ((lax.broadcasted_iota(jnp.int32, (128, 256), 0) < 64)
                  == ((lax.broadcasted_iota(jnp.int32, (128, 256), 1) & 64) == 0))
    esum = jnp.where(lax.broadcasted_iota(jnp.int32, (GRID_W, 128), 0)
                     == (lax.broadcasted_iota(jnp.int32, (GRID_W, 128), 1) & (GRID_W - 1)),
                     1.0, 0.0).astype(BF16)
    half = NA_KH // 2

    kcol = lax.broadcasted_iota(jnp.int32, (GRID_W, 128), 0)
    qcol = lax.broadcasted_iota(jnp.int32, (GRID_W, 128), 1) & (GRID_W - 1)
    qstart = jnp.clip(qcol - NA_KW // 2, 0, GRID_W - NA_KW)
    col_ok = (kcol >= qstart) & (kcol < qstart + NA_KW)

    @pl.when((pl.program_id(0) == 0) & (rb == 0))
    def _build_bias():
        rel = jnp.clip(kcol - qcol + (NA_KW - 1), 0, 2 * NA_KW - 2)

        def rr_body(rr, carry):
            for s in range(4):
                acc = jnp.zeros((GRID_W, 128), F32)
                for j in range(2 * NA_KW - 1):
                    acc = jnp.where(rel == j, jnp.where(lo_q, rpb_ref[2 * s, rr, j], rpb_ref[2 * s + 1, rr, j]), acc)
                bias_s[s, rr] = acc * LOG2E
            return carry

        lax.fori_loop(0, NA_RR_OUTSIDE, rr_body, 0)
        for s in range(4):
            bias_s[s, NA_RR_OUTSIDE] = oob_ref[...]

    def window(pi):
        r0 = rb * NA_ROWS_PER_STEP + 2 * pi
        return r0, jnp.clip(r0 - half, 0, GRID_W - NA_WIN_ROWS)

    def q_stage(pi):
        _, win = window(pi)
        koff = pl.multiple_of(win * GRID_W, TOK_BLK)
        scs = []
        for s in range(4):
            ch = slice(128 * s, 128 * (s + 1))
            kw = k_ref[0, pl.ds(koff, NA_WIN_KEYS), ch]
            scs.append(jnp.dot(kw, qw_ref[pi, ch, :], preferred_element_type=F32))
        return scs

    def s_stage(pi, par, scs):
        r0, win = window(pi)
        rr = []
        for p in range(2):
            r = r0 + p
            first = jnp.clip(r - half, 0, GRID_W - NA_KH) - win
            rel = win - r + (NA_KH - 1)
            rr.append([jnp.where((wr >= first) & (wr < first + NA_KH), wr + rel, NA_RR_OUTSIDE)
                       for wr in range(NA_WIN_ROWS)])
        sums = []
        for s in range(4):
            st = st_s[par][s]
            slab_sums = []
            for p in range(2):
                lanes = slice(128 * p, 128 * (p + 1))
                m = None
                for wr in range(NA_WIN_ROWS):
                    rows = slice(GRID_W * wr, GRID_W * (wr + 1))
                    t = jnp.where(col_ok, scs[s][rows, lanes] + bias_s[s, rr[p][wr]], NEG_INF)
                    st[rows, lanes] = t
                    m = t if m is None else jnp.maximum(m, t)
                m = jnp.max(m, axis=0, keepdims=True)
                l = None
                for wr in range(NA_WIN_ROWS):
                    rows = slice(GRID_W * wr, GRID_W * (wr + 1))
                    e = jnp.exp2(st[rows, lanes] - m)
                    pt_s[par][s][rows, lanes] = e.astype(BF16)
                    l = e if l is None else l + e
                slab_sums.append(jnp.sum(l, axis=0, keepdims=True))
            sums.append(jnp.concatenate(slab_sums, axis=1))
        return tuple(sums)

    def o_stage_pv(pi, par, sums):
        _, win = window(pi)
        blk0 = lax.shift_right_logical(win, 1)
        ots = []
        for s in range(4):
            ch = slice(128 * s, 128 * (s + 1))
            vw = jnp.concatenate([vt_ref[blk0 + i, ch, :] for i in range(NA_WIN_BLKS)], axis=1)
            ot = jnp.dot(vw, pt_s[par][s][...], preferred_element_type=F32)
            ots.append(jnp.where(same_head2, (ot / sums[s]).astype(BF16), jnp.zeros((), BF16)))
        return ots

    def o_stage_store(pi, ots):
        for s in range(4):
            ch = slice(128 * s, 128 * (s + 1))
            for p in range(2):
                nat = lax.dot_general(esum, ots[s][:, 128 * p:128 * (p + 1)], NT_DIMS,
                                      preferred_element_type=F32)
                qoff = pl.multiple_of((2 * pi + p) * GRID_W, GRID_W)
                o_ref[0, pl.ds(qoff, GRID_W), ch] = nat.astype(BF16)

    def step(j, prev_sums, has_prev=True):
        a, b = 2 * j, 2 * j + 1
        scs_a = q_stage(a)
        if has_prev:
            ots_a = o_stage_pv(a - 2, 0, prev_sums[0])
            ots_b = o_stage_pv(b - 2, 1, prev_sums[1])
        scs_b = q_stage(b)
        if has_prev:
            o_stage_store(a - 2, ots_a)
            o_stage_store(b - 2, ots_b)
        return s_stage(a, 0, scs_a), s_stage(b, 1, scs_b)

    sums = step(0, None, has_prev=False)
    sums = lax.fori_loop(1, NA_PAIRS_PER_STEP // 2, step, sums)
    for u in range(2):
        pi = NA_PAIRS_PER_STEP - 2 + u
        o_stage_store(pi, o_stage_pv(pi, u, sums[u]))


def _na(rpb, qw, k, vt):
    b, s, _ = k.shape
    tq = NA_ROWS_PER_STEP * GRID_W
    blks = s // TOK_BLK
    oob = jnp.full((GRID_W, 128), -jnp.inf, F32)
    return pl.pallas_call(
        _na_kernel,
        grid=(b, s // tq),
        in_specs=[pl.BlockSpec(memory_space=pltpu.SMEM),
                  pl.BlockSpec((NA_PAIRS_PER_STEP, 512, 2 * TOK_BLK),
                               lambda bi, ri: (bi * (blks // NA_PAIRS_PER_STEP) + ri, 0, 0)),
                  pl.BlockSpec((1, s, 512), lambda bi, ri: (bi, 0, 0)),
                  pl.BlockSpec((blks, 512, TOK_BLK), lambda bi, ri: (bi, 0, 0)),
                  _const_spec(oob.shape)],
        out_specs=pl.BlockSpec((1, tq, 512), lambda bi, ri: (bi, ri, 0)),
        out_shape=jax.ShapeDtypeStruct((b, s, 512), BF16),
        scratch_shapes=([pltpu.VMEM((NA_WIN_KEYS, 256), F32)] * 8
                        + [pltpu.VMEM((NA_WIN_KEYS, 256), BF16)] * 8
                        + [pltpu.VMEM((NA_HEADS // 2, 2 * NA_KH, GRID_W, 128), F32)]),
        compiler_params=_cparams(2),
        name="na",
    )(rpb, qw, k, vt, oob)


def _memkv_kernel(mem_ref, g_ref, w_ref, k_ref, v_ref):
    mn = _rms(mem_ref[0], g_ref[...]).astype(BF16)
    kv = jnp.dot(mn, w_ref[...].astype(BF16), preferred_element_type=F32)
    k_ref[0] = kv[:, 0:512].astype(BF16)
    v_ref[0] = kv[:, 512:1024].astype(BF16)


def _memkv(mem, g, w):
    b, m, _ = mem.shape
    out = jax.ShapeDtypeStruct((b, m, 512), BF16)
    ospec = pl.BlockSpec((1, m, 512), lambda i: (i, 0, 0))
    return pl.pallas_call(
        _memkv_kernel,
        grid=(b,),
        in_specs=[pl.BlockSpec((1, m, D), lambda i: (i, 0, 0)),
                  _const_spec((1, D)),
                  _const_spec((D, 1024))],
        out_specs=[ospec, ospec],
        out_shape=[out, out],
        compiler_params=_cparams(1),
        name="memkv",
    )(mem, g, w)


def _mem_kernel(q_ref, k_ref, v_ref, o_ref):
    scale = MEM_HEAD_DIM ** -0.5
    for h in range(MEM_HEADS):
        cols = slice(MEM_HEAD_DIM * h, MEM_HEAD_DIM * (h + 1))
        sc = lax.dot_general(q_ref[0, :, cols], k_ref[0, :, cols], NT_DIMS,
                             preferred_element_type=F32) * scale
        m = jnp.max(sc, axis=-1, keepdims=True)
        p = jnp.exp(sc - m)
        l = jnp.sum(p, axis=-1, keepdims=True)
        o = jnp.dot(p.astype(BF16), v_ref[0, :, cols], preferred_element_type=F32)
        o_ref[0, :, cols] = (o / l).astype(BF16)


def _mem_attn(q, k, v):
    b, s, _ = q.shape
    m = k.shape[1]
    tq = 1024
    return pl.pallas_call(
        _mem_kernel,
        grid=(b, s // tq),
        in_specs=[pl.BlockSpec((1, tq, 512), lambda bi, i: (bi, i, 0)),
                  pl.BlockSpec((1, m, 512), lambda bi, i: (bi, 0, 0)),
                  pl.BlockSpec((1, m, 512), lambda bi, i: (bi, 0, 0))],
        out_specs=pl.BlockSpec((1, tq, 512), lambda bi, i: (bi, i, 0)),
        out_shape=jax.ShapeDtypeStruct((b, s, 512), BF16),
        compiler_params=_cparams(2),
        name="mem_attn",
    )(q, k, v)


def _merge_kernel(x_ref, g_ref, yna_ref, ys5_ref, ymem_ref, wg_ref, bg_ref, wb_ref, wo_ref, o_ref):
    x = x_ref[...]
    h = _rms(x, g_ref[...]).astype(BF16)
    merged = None
    ys = (yna_ref[...], ys5_ref[...].astype(BF16), ymem_ref[...])
    for b, y in enumerate(ys):
        cols = slice(D * b, D * (b + 1))
        gate = jax.nn.sigmoid(jnp.dot(h, wg_ref[:, cols].astype(BF16), preferred_element_type=F32)
                              + bg_ref[:, cols])
        up = jnp.dot(y, wb_ref[b], preferred_element_type=F32)
        merged = gate * up if merged is None else merged + gate * up
    o_ref[...] = x + jnp.dot(merged.astype(BF16), wo_ref[...], preferred_element_type=F32)


def _merge(x2d, g, yna, ys5, ymem, wg, bg, wb, wo):
    n = x2d.shape[0]
    tm = 1024
    yspec = pl.BlockSpec((tm, 512), lambda i: (i, 0))
    return pl.pallas_call(
        _merge_kernel,
        grid=(n // tm,),
        in_specs=[pl.BlockSpec((tm, D), lambda i: (i, 0)),
                  _const_spec((1, D)),
                  yspec, yspec, yspec,
                  _const_spec((D, 3 * D)),
                  _const_spec((1, 3 * D)),
                  _const_spec((3, 512, D)),
                  _const_spec((D, D))],
        out_specs=pl.BlockSpec((tm, D), lambda i: (i, 0)),
        out_shape=jax.ShapeDtypeStruct((n, D), F32),
        compiler_params=_cparams(1),
        name="merge",
    )(x2d, g, yna, ys5, ymem, wg, bg, wb, wo)


def _ffn_kernel(x_ref, g_ref, gf_ref, w1_ref, w3_ref, w2_ref, o_ref):
    x = x_ref[...]
    h = _rms(x, g_ref[...]).astype(BF16)
    a = jnp.dot(h, w1_ref[...], preferred_element_type=F32)
    c = jnp.dot(h, w3_ref[...], preferred_element_type=F32)
    mid = (a * jax.nn.sigmoid(a) * c).astype(BF16)
    x2 = x + jnp.dot(mid, w2_ref[...].astype(BF16), preferred_element_type=F32)
    o_ref[...] = _rms(x2, gf_ref[...])


def _ffn(x2d, g, gf, w1, w3, w2):
    n = x2d.shape[0]
    tm = 512
    return pl.pallas_call(
        _ffn_kernel,
        grid=(n // tm,),
        in_specs=[pl.BlockSpec((tm, D), lambda i: (i, 0)),
                  _const_spec((1, D)),
                  _const_spec((1, D)),
                  _const_spec((D, D_FF)),
                  _const_spec((D, D_FF)),
                  _const_spec((D_FF, D))],
        out_specs=pl.BlockSpec((tm, D), lambda i: (i, 0)),
        out_shape=jax.ShapeDtypeStruct((n, D), F32),
        compiler_params=_cparams(1),
        name="ffn",
    )(x2d, g, gf, w1, w3, w2)


def _s5_tables(a_re, a_im, log_dt, b_re, b_im, c_re, c_im, s5_d):
    t = CHUNK
    ar, ai = a_re.astype(F32), a_im.astype(F32)
    dt = jnp.exp(log_dt.astype(F32))[..., None]
    lr, li = ar * dt, ai * dt
    mag = jnp.exp(lr)
    lbr, lbi = mag * jnp.cos(li), mag * jnp.sin(li)
    den = ar * ar + ai * ai
    rr = ((lbr - 1.0) * ar + lbi * ai) / den
    ri = (lbi * ar - (lbr - 1.0) * ai) / den
    br, bi = b_re.astype(F32), b_im.astype(F32)
    bbr = rr[..., None] * br - ri[..., None] * bi
    bbi = rr[..., None] * bi + ri[..., None] * br
    cmr, cmi = c_re.astype(F32), c_im.astype(F32)
    tau = jnp.arange(t + 1, dtype=F32)[:, None, None, None]
    pmag = jnp.exp(tau * lr[None])
    pwr, pwi = pmag * jnp.cos(tau * li[None]), pmag * jnp.sin(tau * li[None])

    dd = jnp.pad(jnp.eye(S5_GROUP, dtype=F32)[None] * s5_d.astype(F32).reshape(S5_GROUPS, S5_GROUP, 1),
                 ((0, 0), (0, 0), (128 - S5_GROUP, 0)))

    def lanes4(f0, f1, b0, b1):
        return jnp.concatenate([f0, f1, b0, b1], axis=-1)

    cm = jnp.stack([lanes4(cmr[0], -cmi[0], cmr[1], -cmi[1]),
                    lanes4(-cmi[0], -cmr[0], -cmi[1], -cmr[1])], axis=1)
    fr, fi = jnp.moveaxis(pwr[1:t + 1, 0], 0, 1), jnp.moveaxis(pwi[1:t + 1, 0], 0, 1)
    gr, gi = jnp.moveaxis(pwr[1:t + 1][::-1, 1], 0, 1), jnp.moveaxis(pwi[1:t + 1][::-1, 1], 0, 1)
    bt = lambda z: jnp.swapaxes(z, -1, -2)
    bb = jnp.stack([lanes4(bt(bbr[0]), bt(bbi[0]), bt(bbr[1]), bt(bbi[1])),
                    lanes4(-bt(bbi[0]), bt(bbr[0]), -bt(bbi[1]), bt(bbr[1]))], axis=1)
    wr, wi = jnp.moveaxis(pwr[:t][::-1, 0], 0, 1), jnp.moveaxis(pwi[:t][::-1, 0], 0, 1)
    vr, vi = jnp.moveaxis(pwr[:t, 1], 0, 1), jnp.moveaxis(pwi[:t, 1], 0, 1)
    pw = jnp.stack([lanes4(fr, fr, gr, gr), lanes4(fi, fi, gi, gi),
                    lanes4(wr, wr, vr, vr), lanes4(wi, wi, vi, vi)], axis=1)

    dr, di = pwr[t], pwi[t]
    a_rows, b_rows = [], []
    for _ in range(N_SCAN_LEVELS):
        a_rows.append(lanes4(dr[0], dr[0], dr[1], dr[1]))
        b_rows.append(lanes4(-di[0], di[0], -di[1], di[1]))
        dr, di = dr * dr - di * di, 2.0 * dr * di
    pad = [jnp.zeros_like(a_rows[0])] * (8 - N_SCAN_LEVELS)
    dtab = jnp.stack(a_rows + pad + b_rows + pad, axis=1)
    return cm, bb, pw, dtab, dd


def kernel(x, mem, g_mix, g_mem, g_ffn, g_final, w_in, w_gate, b_gate, rpb, w_mem_kv,
           a_re, a_im, log_dt, b_re, b_im, c_re, c_im, s5_d, w_glu, w_branch, w_o,
           w_ffn1, w_ffn3, w_ffn2):
    bsz, s, d = x.shape
    n = bsz * s
    x2d = x.reshape(n, d)
    gm = g_mix[0].reshape(1, d).astype(F32)

    w_in_t = w_in[0].T.astype(BF16)
    cm_tab, bb_tab, pw_tab, dtab, dd_tab = _s5_tables(a_re[0], a_im[0], log_dt[0], b_re[0], b_im[0],
                                                      c_re[0], c_im[0], s5_d[0])

    k, qm, qw, vt = _proj(x2d, gm, w_in[0], w_in_t)

    ut = _proj_s5t(x.reshape(n // CHUNK, CHUNK, d), gm, w_in_t)
    zt = _s5_core(ut, cm_tab, bb_tab, pw_tab, dtab, dd_tab)
    y_s5 = _s5_glu(zt, w_glu[0].T.astype(BF16)).reshape(n, 512)

    y_na = _na(rpb[0].astype(F32), qw, k.reshape(bsz, s, 512), vt).reshape(n, 512)

    k_mem, v_mem = _memkv(mem, g_mem[0].reshape(1, d).astype(F32), w_mem_kv[0])
    y_mem = _mem_attn(qm.reshape(bsz, s, 512), k_mem, v_mem).reshape(n, 512)

    x1 = _merge(x2d, gm, y_na, y_s5, y_mem, w_gate[0],
                b_gate[0].reshape(1, 3 * d).astype(F32), w_branch[0].astype(BF16), w_o[0].astype(BF16))
    out = _ffn(x1, g_ffn[0].reshape(1, d).astype(F32), g_final.reshape(1, d).astype(F32),
               w_ffn1[0].astype(BF16), w_ffn3[0].astype(BF16), w_ffn2[0])
    return out.reshape(bsz, s, d)
```

```python
import functools

import jax
import jax.numpy as jnp
from jax import lax
from jax.experimental import pallas as pl
from jax.experimental.pallas import tpu as pltpu

F32 = jnp.float32
BF16 = jnp.bfloat16

D = 1024
GRID_W = 64
NA_HEADS = 8
NA_KH = 8
NA_KW = 16
S5_GROUPS = 32
S5_GROUP = 16
S5_STATE = 64
CHUNK = 64
MEM_HEADS = 4
MEM_HEAD_DIM = 128
D_FF = 2816
EPS = 1e-6
NEG_INF = -1e30

VMEM_LIMIT = 56 * 1024 * 1024

NT_DIMS = (((1,), (1,)), ((), ()))


def _cparams(n_axes):
    return pltpu.CompilerParams(
        dimension_semantics=("arbitrary",) * n_axes,
        vmem_limit_bytes=VMEM_LIMIT)


def _rms(x, g):
    return x * lax.rsqrt(jnp.mean(x * x, axis=-1, keepdims=True) + EPS) * g


def _const_spec(shape):
    nd = len(shape)
    return pl.BlockSpec(shape, lambda *_: (0,) * nd, pipeline_mode=pl.Buffered(1))


TOK_BLK = 128


LOG2E = 1.4426950408889634
NA_Q_SCALE = 64 ** -0.5 * LOG2E


def _proj_kernel(x_ref, g_ref, wk_ref, wqm_ref, wqt_ref, wvt_ref, k_ref, qm_ref, qw_ref, vt_ref):
    h = _rms(x_ref[...], g_ref[...]).astype(BF16)
    k_ref[...] = jnp.dot(h, wk_ref[...].astype(BF16), preferred_element_type=F32).astype(BF16)
    qm_ref[...] = jnp.dot(h, wqm_ref[...].astype(BF16), preferred_element_type=F32).astype(BF16)
    qt = lax.dot_general(wqt_ref[...], h, NT_DIMS, preferred_element_type=F32) * NA_Q_SCALE
    vt = lax.dot_general(wvt_ref[...], h, NT_DIMS, preferred_element_type=F32)
    lo = lax.broadcasted_iota(jnp.int32, (128, 128), 1) < 64
    same_head = (lax.broadcasted_iota(jnp.int32, (128, 128), 0) < 64) == lo
    for i in range(x_ref.shape[0] // TOK_BLK):
        toks = slice(TOK_BLK * i, TOK_BLK * (i + 1))
        vt_ref[i] = vt[:, toks].astype(BF16)
        for s in range(4):
            a = qt[128 * s:128 * (s + 1), toks]
            ar = pltpu.roll(a, 64, axis=1)
            qw_ref[i, 128 * s:128 * (s + 1), :] = jnp.concatenate(
                [jnp.where(same_head, jnp.where(lo, a, ar), 0.0),
                 jnp.where(same_head, jnp.where(lo, ar, a), 0.0)], axis=1).astype(BF16)


W_IN_BLK = 512


def _w_in_spec(blk, transposed):
    if transposed:
        return pl.BlockSpec((W_IN_BLK, D), lambda *_: (blk, 0), pipeline_mode=pl.Buffered(1))
    return pl.BlockSpec((D, W_IN_BLK), lambda *_: (0, blk), pipeline_mode=pl.Buffered(1))


def _proj(x2d, g, w, wt):
    n = x2d.shape[0]
    tm = 1024
    nat = jax.ShapeDtypeStruct((n, 512), BF16)
    nat_spec = pl.BlockSpec((tm, 512), lambda i: (i, 0))
    chm = lambda width: (jax.ShapeDtypeStruct((n // TOK_BLK, 512, width), BF16),
                         pl.BlockSpec((tm // TOK_BLK, 512, width), lambda i: (i, 0, 0)))
    (qw_shape, qw_spec), (vt_shape, vt_spec) = chm(2 * TOK_BLK), chm(TOK_BLK)
    return pl.pallas_call(
        _proj_kernel,
        grid=(n // tm,),
        in_specs=[pl.BlockSpec((tm, D), lambda i: (i, 0)),
                  _const_spec((1, D)),
                  _w_in_spec(1, False),
                  _w_in_spec(4, False),
                  _w_in_spec(0, True),
                  _w_in_spec(2, True)],
        out_specs=[nat_spec, nat_spec, qw_spec, vt_spec],
        out_shape=[nat, nat, qw_shape, vt_shape],
        compiler_params=_cparams(1),
        name="proj",
    )(x2d, g, w, w, wt, wt)


S5T_DT = 8


def _proj_s5t_kernel(x_ref, g_ref, wt_ref, o_ref, h_s):
    wt = wt_ref[...]
    nrows = x_ref.shape[0]
    hn = _rms(x_ref[...], g_ref[...]).reshape(nrows * S5T_DT, D)
    for c in range(D // 128):
        h_s[c] = hn[:, 128 * c:128 * (c + 1)]
    for j in range(S5T_DT):
        h = jnp.concatenate([h_s[c, pl.ds(j, nrows, stride=S5T_DT), :] for c in range(D // 128)],
                            axis=1).astype(BF16)
        ut = lax.dot_general(wt, h, NT_DIMS, preferred_element_type=F32)
        o_ref[:, j * S5_GROUP:(j + 1) * S5_GROUP, :] = (
            ut.astype(BF16).reshape(S5_GROUPS, S5_GROUP, 256))


def _proj_s5t(xc, g, wt):
    nrows = xc.shape[0]
    return pl.pallas_call(
        _proj_s5t_kernel,
        grid=(CHUNK // S5T_DT,),
        in_specs=[pl.BlockSpec((nrows, S5T_DT, D), lambda i: (0, i, 0)),
                  _const_spec((1, D)),
                  _w_in_spec(3, True)],
        out_specs=pl.BlockSpec((S5_GROUPS, S5T_DT * S5_GROUP, nrows), lambda i: (0, i, 0)),
        out_shape=jax.ShapeDtypeStruct((S5_GROUPS, CHUNK * S5_GROUP, nrows), BF16),
        scratch_shapes=[pltpu.VMEM((D // 128, nrows * S5T_DT, 128), F32)],
        compiler_params=_cparams(1),
        name="proj_s5t",
    )(xc, g, wt)


N_SCAN_LEVELS = 6


def _gelu_tanh(x):
    c = 0.7978845608028654
    return 0.5 * x * (1.0 + jnp.tanh(c * (x + 0.044715 * (x * x * x))))


def _split_bf16(x):
    hi = x.astype(BF16)
    return hi, (x - hi.astype(F32)).astype(BF16)


def _nt_f32(a, b):
    ah, al = _split_bf16(a)
    bh, bl = _split_bf16(b)
    nt = lambda u, w: lax.dot_general(u, w, NT_DIMS, preferred_element_type=F32)
    return nt(ah, bh) + nt(ah, bl) + nt(al, bh)


S5_GROUPS_PER_STEP = 4
S5_SCRATCH_PER_GROUP = 5


def _s5_kernel(ut_ref, cm_ref, bb_ref, pnr_ref, pni_ref, ppr_ref, ppi_ref, d_ref, dd_ref, z_ref, *scratch):
    pw_ref = (pnr_ref, pni_ref, ppr_ref, ppi_ref)
    groups = [scratch[S5_SCRATCH_PER_GROUP * gi:S5_SCRATCH_PER_GROUP * (gi + 1)]
              for gi in range(S5_GROUPS_PER_STEP)]
    for gi, (m_s, n_s, pt_s, ptf_s, rs_s) in enumerate(groups):
        _s5_operators(gi, cm_ref, bb_ref, pw_ref, dd_ref, m_s, n_s, pt_s, ptf_s, rs_s)
    yv = [_s5_chunk_matmuls(gi, ut_ref, m_s, pt_s) for gi, (m_s, _, pt_s, _, _) in enumerate(groups)]
    for gi, (_, n_s, _, _, _) in enumerate(groups):
        _s5_finish(gi, d_ref, z_ref, n_s, *yv[gi])


def _s5_operators(gi, cm_ref, bb_ref, pw_ref, dd_ref, m_s, n_s, pt_s, ptf_s, rs_s):
    tc = CHUNK * S5_GROUP
    cma, cmb = cm_ref[gi,0], cm_ref[gi,1]
    bba, bbb = bb_ref[gi,0], bb_ref[gi,1]
    for t in range(CHUNK):
        rows = slice(S5_GROUP * t, S5_GROUP * (t + 1))
        n_s[rows, :] = (cma * pw_ref[0][gi, t:t + 1, :] + cmb * pw_ref[1][gi, t:t + 1, :]).astype(BF16)
        ptf_s[rows, :] = bba * pw_ref[2][gi, t:t + 1, :] + bbb * pw_ref[3][gi, t:t + 1, :]
    ptf = ptf_s[...]
    pt_s[...] = ptf.astype(BF16)

    ra = _nt_f32(cma[:, 0:128], ptf[:, 0:128])
    rb = _nt_f32(cma[:, 128:256], ptf[:, 128:256])
    zeros = jnp.zeros((S5_GROUP, tc), F32)
    r = (jnp.concatenate([ra, zeros], axis=1)
         + pltpu.roll(jnp.concatenate([rb, zeros], axis=1), (CHUNK - 1) * S5_GROUP, axis=1)
         + jnp.concatenate([zeros[:, 0:tc - 128], dd_ref[gi], zeros], axis=1))
    for k in range(8):
        rk = r if k == 0 else pltpu.roll(r, 2 * tc - S5_GROUP * k, axis=1)
        rs_s[k] = rk[:, 0:RS_LANES].astype(BF16)
    for t in range(CHUNK):
        a, k = divmod(CHUNK - 1 - t, 8)
        m_s[S5_GROUP * t:S5_GROUP * (t + 1), :] = rs_s[k, :, 128 * a:128 * a + tc]


def _s5_chunk_matmuls(gi, ut_ref, m_s, pt_s):
    ut = ut_ref[gi]
    nrows = ut.shape[1]
    eye = jnp.where(lax.broadcasted_iota(jnp.int32, (nrows, nrows), 0)
                    == lax.broadcasted_iota(jnp.int32, (nrows, nrows), 1), 1.0, 0.0).astype(BF16)
    u_rows = lax.dot_general(eye, ut, NT_DIMS, preferred_element_type=F32).astype(BF16)
    y = jnp.dot(m_s[...], ut, preferred_element_type=F32)
    v = jnp.dot(u_rows, pt_s[...], preferred_element_type=F32)
    return y, v


def _s5_finish(gi, d_ref, z_ref, n_s, y, v):
    nrows = v.shape[0]
    pos = lax.broadcasted_iota(jnp.int32, (nrows, 128), 0) & (CHUNK - 1)

    def shift(x, s, up):
        if up:
            return jnp.where(pos < CHUNK - s, pltpu.roll(x, nrows - s, axis=0), 0.0)
        return jnp.where(pos >= s, pltpu.roll(x, s, axis=0), 0.0)

    def scan(vh, lanes, up):
        x = shift(vh, 1, up)
        for lvl in range(N_SCAN_LEVELS):
            xs = shift(x, 1 << lvl, up)
            a = d_ref[gi,lvl:lvl + 1, lanes]
            b = d_ref[gi,8 + lvl:9 + lvl, lanes]
            x = x + a * xs + b * pltpu.roll(xs, 64, axis=1)
        return x

    xf = scan(v[:, 0:128], slice(0, 128), False)
    xb = scan(v[:, 128:256], slice(128, 256), True)
    xin = jnp.concatenate([xf, xb], axis=1).astype(BF16)
    y = y + lax.dot_general(n_s[...], xin, NT_DIMS, preferred_element_type=F32)
    z_ref[gi] =_gelu_tanh(y).astype(BF16)


RS_LANES = 128 * 7 + CHUNK * S5_GROUP


def _s5_core(ut, cm, bb, pw, dtab, dd):
    nrows = ut.shape[2]
    tc = CHUNK * S5_GROUP
    gs = S5_GROUPS_PER_STEP
    return pl.pallas_call(
        _s5_kernel,
        grid=(S5_GROUPS // gs,),
        in_specs=[pl.BlockSpec((gs, tc, nrows), lambda g: (g, 0, 0)),
                  pl.BlockSpec((gs, 2, S5_GROUP, 256), lambda g: (g, 0, 0, 0)),
                  pl.BlockSpec((gs, 2, S5_GROUP, 256), lambda g: (g, 0, 0, 0)),
                  *[pl.BlockSpec((gs, CHUNK, 256), lambda g: (g, 0, 0))] * 4,
                  pl.BlockSpec((gs, 16, 256), lambda g: (g, 0, 0)),
                  pl.BlockSpec((gs, S5_GROUP, 128), lambda g: (g, 0, 0))],
        out_specs=pl.BlockSpec((gs, tc, nrows), lambda g: (g, 0, 0)),
        out_shape=jax.ShapeDtypeStruct((S5_GROUPS, tc, nrows), BF16),
        scratch_shapes=[pltpu.VMEM((tc, tc), BF16),
                        pltpu.VMEM((tc, 256), BF16),
                        pltpu.VMEM((tc, 256), BF16),
                        pltpu.VMEM((tc, 256), F32),
                        pltpu.VMEM((8, S5_GROUP, RS_LANES), BF16),
                        ] * gs,
        compiler_params=_cparams(1),
        name="s5_core",
    )(ut, cm, bb, *pw, dtab, dd)


def _s5_glu_kernel(z_ref, wt_ref, o_ref):
    wt = wt_ref[...]
    nrows = z_ref.shape[2]
    eye = jnp.where(lax.broadcasted_iota(jnp.int32, (nrows, nrows), 0)
                    == lax.broadcasted_iota(jnp.int32, (nrows, nrows), 1), 1.0, 0.0).astype(BF16)
    for j in range(S5T_DT):
        zt = z_ref[:, j * S5_GROUP:(j + 1) * S5_GROUP, :].reshape(512, nrows)
        gl = jnp.dot(wt, zt, preferred_element_type=F32)
        o = (zt.astype(F32) * jax.nn.sigmoid(gl)).astype(BF16)
        nat = lax.dot_general(eye, o, NT_DIMS, preferred_element_type=F32)
        o_ref[:, j, :] = nat


def _s5_glu(zt, wglu_t):
    nrows = zt.shape[2]
    return pl.pallas_call(
        _s5_glu_kernel,
        grid=(CHUNK // S5T_DT,),
        in_specs=[pl.BlockSpec((S5_GROUPS, S5T_DT * S5_GROUP, nrows), lambda i: (0, i, 0)),
                  _const_spec((512, 512))],
        out_specs=pl.BlockSpec((nrows, S5T_DT, 512), lambda i: (0, i, 0)),
        out_shape=jax.ShapeDtypeStruct((nrows, CHUNK, 512), F32),
        compiler_params=_cparams(1),
        name="s5_glu",
    )(zt, wglu_t)


NA_ROWS_PER_STEP = 32
NA_PAIRS_PER_STEP = NA_ROWS_PER_STEP // 2
NA_WIN_ROWS = 10
NA_WIN_KEYS = NA_WIN_ROWS * GRID_W
NA_WIN_BLKS = NA_WIN_KEYS // TOK_BLK
NA_RR_OUTSIDE = 2 * NA_KH - 1


def _na_kernel(rpb_ref, qw_ref, k_ref, vt_ref, oob_ref, o_ref, *scratch):
    st_s, pt_s, bias_s = (scratch[0:4], scratch[4:8]), (scratch[8:12], scratch[12:16]), scratch[16]
    rb = pl.program_id(1)
    lo_q = lax.broadcasted_iota(jnp.int32, (GRID_W, 128), 1) < 64
    same_head2 = ((lax.broadcasted_iota(jnp.int32, (128, 256), 0) < 64)
                  == ((lax.broadcasted_iota(jnp.int32, (128, 256), 1) & 64) == 0))
    esum = jnp.where(lax.broadcasted_iota(jnp.int32, (GRID_W, 128), 0)
                     == (lax.broadcasted_iota(jnp.int32, (GRID_W, 128), 1) & (GRID_W - 1)),
                     1.0, 0.0).astype(BF16)
    half = NA_KH // 2

    kcol = lax.broadcasted_iota(jnp.int32, (GRID_W, 128), 0)
    qcol = lax.broadcasted_iota(jnp.int32, (GRID_W, 128), 1) & (GRID_W - 1)
    qstart = jnp.clip(qcol - NA_KW // 2, 0, GRID_W - NA_KW)
    col_ok = (kcol >= qstart) & (kcol < qstart + NA_KW)

    @pl.when((pl.program_id(0) == 0) & (rb == 0))
    def _build_bias():
        rel = jnp.clip(kcol - qcol + (NA_KW - 1), 0, 2 * NA_KW - 2)

        def rr_body(rr, carry):
            for s in range(4):
                acc = jnp.zeros((GRID_W, 128), F32)
                for j in range(2 * NA_KW - 1):
                    acc = jnp.where(rel == j, jnp.where(lo_q, rpb_ref[2 * s, rr, j], rpb_ref[2 * s + 1, rr, j]), acc)
                bias_s[s, rr] = acc * LOG2E
            return carry

        lax.fori_loop(0, NA_RR_OUTSIDE, rr_body, 0)
        for s in range(4):
            bias_s[s, NA_RR_OUTSIDE] = oob_ref[...]

    def window(pi):
        r0 = rb * NA_ROWS_PER_STEP + 2 * pi
        return r0, jnp.clip(r0 - half, 0, GRID_W - NA_WIN_ROWS)

    def q_stage(pi):
        _, win = window(pi)
        koff = pl.multiple_of(win * GRID_W, TOK_BLK)
        scs = []
        for s in range(4):
            ch = slice(128 * s, 128 * (s + 1))
            kw = k_ref[0, pl.ds(koff, NA_WIN_KEYS), ch]
            scs.append(jnp.dot(kw, qw_ref[pi, ch, :], preferred_element_type=F32))
        return scs

    def s_stage(pi, par, scs):
        r0, win = window(pi)
        rr = []
        for p in range(2):
            r = r0 + p
            first = jnp.clip(r - half, 0, GRID_W - NA_KH) - win
            rel = win - r + (NA_KH - 1)
            rr.append([jnp.where((wr >= first) & (wr < first + NA_KH), wr + rel, NA_RR_OUTSIDE)
                       for wr in range(NA_WIN_ROWS)])
        sums = []
        for s in range(4):
            st = st_s[par][s]
            slab_sums = []
            for p in range(2):
                lanes = slice(128 * p, 128 * (p + 1))
                m = None
                for wr in range(NA_WIN_ROWS):
                    rows = slice(GRID_W * wr, GRID_W * (wr + 1))
                    t = jnp.where(col_ok, scs[s][rows, lanes] + bias_s[s, rr[p][wr]], NEG_INF)
                    st[rows, lanes] = t
                    m = t if m is None else jnp.maximum(m, t)
                m = jnp.max(m, axis=0, keepdims=True)
                l = None
                for wr in range(NA_WIN_ROWS):
                    rows = slice(GRID_W * wr, GRID_W * (wr + 1))
                    e = jnp.exp2(st[rows, lanes] - m)
                    pt_s[par][s][rows, lanes] = e.astype(BF16)
                    l = e if l is None else l + e
                slab_sums.append(jnp.sum(l, axis=0, keepdims=True))
            sums.append(jnp.concatenate(slab_sums, axis=1))
        return tuple(sums)

    def o_stage_pv(pi, par, sums):
        _, win = window(pi)
        blk0 = lax.shift_right_logical(win, 1)
        ots = []
        for s in range(4):
            ch = slice(128 * s, 128 * (s + 1))
            vw = jnp.concatenate([vt_ref[blk0 + i, ch, :] for i in range(NA_WIN_BLKS)], axis=1)
            ot = jnp.dot(vw, pt_s[par][s][...], preferred_element_type=F32)
            ots.append(jnp.where(same_head2, (ot / sums[s]).astype(BF16), jnp.zeros((), BF16)))
        return ots

    def o_stage_store(pi, ots):
        for s in range(4):
            ch = slice(128 * s, 128 * (s + 1))
            for p in range(2):
                nat = lax.dot_general(esum, ots[s][:, 128 * p:128 * (p + 1)], NT_DIMS,
                                      preferred_element_type=F32)
                qoff = pl.multiple_of((2 * pi + p) * GRID_W, GRID_W)
                o_ref[0, pl.ds(qoff, GRID_W), ch] = nat.astype(BF16)

    def step(j, prev_sums, has_prev=True):
        a, b = 2 * j, 2 * j + 1
        scs_a = q_stage(a)
        if has_prev:
            ots_a = o_stage_pv(a - 2, 0, prev_sums[0])
            ots_b = o_stage_pv(b - 2, 1, prev_sums[1])
        scs_b = q_stage(b)
        if has_prev:
            o_stage_store(a - 2, ots_a)
            o_stage_store(b - 2, ots_b)
        return s_stage(a, 0, scs_a), s_stage(b, 1, scs_b)

    sums = step(0, None, has_prev=False)
    sums = lax.fori_loop(1, NA_PAIRS_PER_STEP // 2, step, sums)
    for u in range(2):
        pi = NA_PAIRS_PER_STEP - 2 + u
        o_stage_store(pi, o_stage_pv(pi, u, sums[u]))


def _na(rpb, qw, k, vt):
    b, s, _ = k.shape
    tq = NA_ROWS_PER_STEP * GRID_W
    blks = s // TOK_BLK
    oob = jnp.full((GRID_W, 128), -jnp.inf, F32)
    return pl.pallas_call(
        _na_kernel,
        grid=(b, s // tq),
        in_specs=[pl.BlockSpec(memory_space=pltpu.SMEM),
                  pl.BlockSpec((NA_PAIRS_PER_STEP, 512, 2 * TOK_BLK),
                               lambda bi, ri: (bi * (blks // NA_PAIRS_PER_STEP) + ri, 0, 0)),
                  pl.BlockSpec((1, s, 512), lambda bi, ri: (bi, 0, 0)),
                  pl.BlockSpec((blks, 512, TOK_BLK), lambda bi, ri: (bi, 0, 0)),
                  _const_spec(oob.shape)],
        out_specs=pl.BlockSpec((1, tq, 512), lambda bi, ri: (bi, ri, 0)),
        out_shape=jax.ShapeDtypeStruct((b, s, 512), BF16),
        scratch_shapes=([pltpu.VMEM((NA_WIN_KEYS, 256), F32)] * 8
                        + [pltpu.VMEM((NA_WIN_KEYS, 256), BF16)] * 8
                        + [pltpu.VMEM((NA_HEADS // 2, 2 * NA_KH, GRID_W, 128), F32)]),
        compiler_params=_cparams(2),
        name="na",
    )(rpb, qw, k, vt, oob)


def _memkv_kernel(mem_ref, g_ref, w_ref, k_ref, v_ref):
    mn = _rms(mem_ref[0], g_ref[...]).astype(BF16)
    kv = jnp.dot(mn, w_ref[...].astype(BF16), preferred_element_type=F32)
    k_ref[0] = kv[:, 0:512].astype(BF16)
    v_ref[0] = kv[:, 512:1024].astype(BF16)


def _memkv(mem, g, w):
    b, m, _ = mem.shape
    out = jax.ShapeDtypeStruct((b, m, 512), BF16)
    ospec = pl.BlockSpec((1, m, 512), lambda i: (i, 0, 0))
    return pl.pallas_call(
        _memkv_kernel,
        grid=(b,),
        in_specs=[pl.BlockSpec((1, m, D), lambda i: (i, 0, 0)),
                  _const_spec((1, D)),
                  _const_spec((D, 1024))],
        out_specs=[ospec, ospec],
        out_shape=[out, out],
        compiler_params=_cparams(1),
        name="memkv",
    )(mem, g, w)


def _mem_kernel(q_ref, k_ref, v_ref, o_ref):
    scale = MEM_HEAD_DIM ** -0.5
    for h in range(MEM_HEADS):
        cols = slice(MEM_HEAD_DIM * h, MEM_HEAD_DIM * (h + 1))
        sc = lax.dot_general(q_ref[0, :, cols], k_ref[0, :, cols], NT_DIMS,
                             preferred_element_type=F32) * scale
        m = jnp.max(sc, axis=-1, keepdims=True)
        p = jnp.exp(sc - m)
        l = jnp.sum(p, axis=-1, keepdims=True)
        o = jnp.dot(p.astype(BF16), v_ref[0, :, cols], preferred_element_type=F32)
        o_ref[0, :, cols] = (o / l).astype(BF16)


def _mem_attn(q, k, v):
    b, s, _ = q.shape
    m = k.shape[1]
    tq = 1024
    return pl.pallas_call(
        _mem_kernel,
        grid=(b, s // tq),
        in_specs=[pl.BlockSpec((1, tq, 512), lambda bi, i: (bi, i, 0)),
                  pl.BlockSpec((1, m, 512), lambda bi, i: (bi, 0, 0)),
                  pl.BlockSpec((1, m, 512), lambda bi, i: (bi, 0, 0))],
        out_specs=pl.BlockSpec((1, tq, 512), lambda bi, i: (bi, i, 0)),
        out_shape=jax.ShapeDtypeStruct((b, s, 512), BF16),
        compiler_params=_cparams(2),
        name="mem_attn",
    )(q, k, v)


def _merge_kernel(x_ref, g_ref, yna_ref, ys5_ref, ymem_ref, wg_ref, bg_ref, wb_ref, wo_ref, o_ref):
    x = x_ref[...]
    h = _rms(x, g_ref[...]).astype(BF16)
    merged = None
    ys = (yna_ref[...], ys5_ref[...].astype(BF16), ymem_ref[...])
    for b, y in enumerate(ys):
        cols = slice(D * b, D * (b + 1))
        gate = jax.nn.sigmoid(jnp.dot(h, wg_ref[:, cols].astype(BF16), preferred_element_type=F32)
                              + bg_ref[:, cols])
        up = jnp.dot(y, wb_ref[b], preferred_element_type=F32)
        merged = gate * up if merged is None else merged + gate * up
    o_ref[...] = x + jnp.dot(merged.astype(BF16), wo_ref[...], preferred_element_type=F32)


def _merge(x2d, g, yna, ys5, ymem, wg, bg, wb, wo):
    n = x2d.shape[0]
    tm = 1024
    yspec = pl.BlockSpec((tm, 512), lambda i: (i, 0))
    return pl.pallas_call(
        _merge_kernel,
        grid=(n // tm,),
        in_specs=[pl.BlockSpec((tm, D), lambda i: (i, 0)),
                  _const_spec((1, D)),
                  yspec, yspec, yspec,
                  _const_spec((D, 3 * D)),
                  _const_spec((1, 3 * D)),
                  _const_spec((3, 512, D)),
                  _const_spec((D, D))],
        out_specs=pl.BlockSpec((tm, D), lambda i: (i, 0)),
        out_shape=jax.ShapeDtypeStruct((n, D), F32),
        compiler_params=_cparams(1),
        name="merge",
    )(x2d, g, yna, ys5, ymem, wg, bg, wb, wo)


def _ffn_kernel(x_ref, g_ref, gf_ref, w1_ref, w3_ref, w2_ref, o_ref):
    x = x_ref[...]
    h = _rms(x, g_ref[...]).astype(BF16)
    a = jnp.dot(h, w1_ref[...], preferred_element_type=F32)
    c = jnp.dot(h, w3_ref[...], preferred_element_type=F32)
    mid = (a * jax.nn.sigmoid(a) * c).astype(BF16)
    x2 = x + jnp.dot(mid, w2_ref[...].astype(BF16), preferred_element_type=F32)
    o_ref[...] = _rms(x2, gf_ref[...])


def _ffn(x2d, g, gf, w1, w3, w2):
    n = x2d.shape[0]
    tm = 512
    return pl.pallas_call(
        _ffn_kernel,
        grid=(n // tm,),
        in_specs=[pl.BlockSpec((tm, D), lambda i: (i, 0)),
                  _const_spec((1, D)),
                  _const_spec((1, D)),
                  _const_spec((D, D_FF)),
                  _const_spec((D, D_FF)),
                  _const_spec((D_FF, D))],
        out_specs=pl.BlockSpec((tm, D), lambda i: (i, 0)),
        out_shape=jax.ShapeDtypeStruct((n, D), F32),
        compiler_params=_cparams(1),
        name="ffn",
    )(x2d, g, gf, w1, w3, w2)


def _s5_tables(a_re, a_im, log_dt, b_re, b_im, c_re, c_im, s5_d):
    t = CHUNK
    ar, ai = a_re.astype(F32), a_im.astype(F32)
    dt = jnp.exp(log_dt.astype(F32))[..., None]
    lr, li = ar * dt, ai * dt
    mag = jnp.exp(lr)
    lbr, lbi = mag * jnp.cos(li), mag * jnp.sin(li)
    den = ar * ar + ai * ai
    rr = ((lbr - 1.0) * ar + lbi * ai) / den
    ri = (lbi * ar - (lbr - 1.0) * ai) / den
    br, bi = b_re.astype(F32), b_im.astype(F32)
    bbr = rr[..., None] * br - ri[..., None] * bi
    bbi = rr[..., None] * bi + ri[..., None] * br
    cmr, cmi = c_re.astype(F32), c_im.astype(F32)
    tau = jnp.arange(t + 1, dtype=F32)[:, None, None, None]
    pmag = jnp.exp(tau * lr[None])
    pwr, pwi = pmag * jnp.cos(tau * li[None]), pmag * jnp.sin(tau * li[None])

    dd = jnp.pad(jnp.eye(S5_GROUP, dtype=F32)[None] * s5_d.astype(F32).reshape(S5_GROUPS, S5_GROUP, 1),
                 ((0, 0), (0, 0), (128 - S5_GROUP, 0)))

    def lanes4(f0, f1, b0, b1):
        return jnp.concatenate([f0, f1, b0, b1], axis=-1)

    cm = jnp.stack([lanes4(cmr[0], -cmi[0], cmr[1], -cmi[1]),
                    lanes4(-cmi[0], -cmr[0], -cmi[1], -cmr[1])], axis=1)
    fr, fi = jnp.moveaxis(pwr[1:t + 1, 0], 0, 1), jnp.moveaxis(pwi[1:t + 1, 0], 0, 1)
    gr, gi = jnp.moveaxis(pwr[1:t + 1][::-1, 1], 0, 1), jnp.moveaxis(pwi[1:t + 1][::-1, 1], 0, 1)
    bt = lambda z: jnp.swapaxes(z, -1, -2)
    bb = jnp.stack([lanes4(bt(bbr[0]), bt(bbi[0]), bt(bbr[1]), bt(bbi[1])),
                    lanes4(-bt(bbi[0]), bt(bbr[0]), -bt(bbi[1]), bt(bbr[1]))], axis=1)
    wr, wi = jnp.moveaxis(pwr[:t][::-1, 0], 0, 1), jnp.moveaxis(pwi[:t][::-1, 0], 0, 1)
    vr, vi = jnp.moveaxis(pwr[:t, 1], 0, 1), jnp.moveaxis(pwi[:t, 1], 0, 1)
    pw = (lanes4(fr, fr, gr, gr), lanes4(fi, fi, gi, gi),
          lanes4(wr, wr, vr, vr), lanes4(wi, wi, vi, vi))

    dr, di = pwr[t], pwi[t]
    a_rows, b_rows = [], []
    for _ in range(N_SCAN_LEVELS):
        a_rows.append(lanes4(dr[0], dr[0], dr[1], dr[1]))
        b_rows.append(lanes4(-di[0], di[0], -di[1], di[1]))
        dr, di = dr * dr - di * di, 2.0 * dr * di
    pad = [jnp.zeros_like(a_rows[0])] * (8 - N_SCAN_LEVELS)
    dtab = jnp.stack(a_rows + pad + b_rows + pad, axis=1)
    return cm, bb, pw, dtab, dd


def kernel(x, mem, g_mix, g_mem, g_ffn, g_final, w_in, w_gate, b_gate, rpb, w_mem_kv,
           a_re, a_im, log_dt, b_re, b_im, c_re, c_im, s5_d, w_glu, w_branch, w_o,
           w_ffn1, w_ffn3, w_ffn2):
    bsz, s, d = x.shape
    n = bsz * s
    x2d = x.reshape(n, d)
    gm = g_mix[0].reshape(1, d).astype(F32)

    w_in_t = w_in[0].T.astype(BF16)
    cm_tab, bb_tab, pw_tab, dtab, dd_tab = _s5_tables(a_re[0], a_im[0], log_dt[0], b_re[0], b_im[0],
                                                      c_re[0], c_im[0], s5_d[0])

    k, qm, qw, vt = _proj(x2d, gm, w_in[0], w_in_t)

    ut = _proj_s5t(x.reshape(n // CHUNK, CHUNK, d), gm, w_in_t)
    zt = _s5_core(ut, cm_tab, bb_tab, pw_tab, dtab, dd_tab)
    y_s5 = _s5_glu(zt, w_glu[0].T.astype(BF16)).reshape(n, 512)

    y_na = _na(rpb[0].astype(F32), qw, k.reshape(bsz, s, 512), vt).reshape(n, 512)

    k_mem, v_mem = _memkv(mem, g_mem[0].reshape(1, d).astype(F32), w_mem_kv[0])
    y_mem = _mem_attn(qm.reshape(bsz, s, 512), k_mem, v_mem).reshape(n, 512)

    x1 = _merge(x2d, gm, y_na, y_s5, y_mem, w_gate[0],
                b_gate[0].reshape(1, 3 * d).astype(F32), w_branch[0].astype(BF16), w_o[0].astype(BF16))
    out = _ffn(x1, g_ffn[0].reshape(1, d).astype(F32), g_final.reshape(1, d).astype(F32),
               w_ffn1[0].astype(BF16), w_ffn3[0].astype(BF16), w_ffn2[0])
    return out.reshape(bsz, s, d)
```

```python
import functools

import jax
import jax.numpy as jnp
from jax import lax
from jax.experimental import pallas as pl
from jax.experimental.pallas import tpu as pltpu

F32 = jnp.float32
BF16 = jnp.bfloat16

D = 1024
GRID_W = 64
NA_HEADS = 8
NA_KH = 8
NA_KW = 16
S5_GROUPS = 32
S5_GROUP = 16
S5_STATE = 64
CHUNK = 64
MEM_HEADS = 4
MEM_HEAD_DIM = 128
D_FF = 2816
EPS = 1e-6
NEG_INF = -1e30

VMEM_LIMIT = 56 * 1024 * 1024

NT_DIMS = (((1,), (1,)), ((), ()))


def _cparams(n_axes):
    return pltpu.CompilerParams(
        dimension_semantics=("arbitrary",) * n_axes,
        vmem_limit_bytes=VMEM_LIMIT)


def _rms(x, g):
    return x * lax.rsqrt(jnp.mean(x * x, axis=-1, keepdims=True) + EPS) * g


def _const_spec(shape):
    nd = len(shape)
    return pl.BlockSpec(shape, lambda *_: (0,) * nd, pipeline_mode=pl.Buffered(1))


TOK_BLK = 128


LOG2E = 1.4426950408889634
NA_Q_SCALE = 64 ** -0.5 * LOG2E


def _proj_kernel(x_ref, g_ref, wk_ref, wqm_ref, wqt_ref, wvt_ref, k_ref, qm_ref, qw_ref, vt_ref):
    h = _rms(x_ref[...], g_ref[...]).astype(BF16)
    kk = jnp.dot(h, wk_ref[...].astype(BF16), preferred_element_type=F32).astype(BF16)
    for s in range(4):
        k_ref[s] = kk[:, 128 * s:128 * (s + 1)]
    qm_ref[...] = jnp.dot(h, wqm_ref[...].astype(BF16), preferred_element_type=F32).astype(BF16)
    qt = lax.dot_general(wqt_ref[...], h, NT_DIMS, preferred_element_type=F32) * NA_Q_SCALE
    vt = lax.dot_general(wvt_ref[...], h, NT_DIMS, preferred_element_type=F32)
    lo = lax.broadcasted_iota(jnp.int32, (128, 128), 1) < 64
    same_head = (lax.broadcasted_iota(jnp.int32, (128, 128), 0) < 64) == lo
    for i in range(x_ref.shape[0] // TOK_BLK):
        toks = slice(TOK_BLK * i, TOK_BLK * (i + 1))
        vt_ref[i] = vt[:, toks].astype(BF16)
        for s in range(4):
            a = qt[128 * s:128 * (s + 1), toks]
            ar = pltpu.roll(a, 64, axis=1)
            qw_ref[i, 128 * s:128 * (s + 1), :] = jnp.concatenate(
                [jnp.where(same_head, jnp.where(lo, a, ar), 0.0),
                 jnp.where(same_head, jnp.where(lo, ar, a), 0.0)], axis=1).astype(BF16)


W_IN_BLK = 512


def _w_in_spec(blk, transposed):
    if transposed:
        return pl.BlockSpec((W_IN_BLK, D), lambda *_: (blk, 0), pipeline_mode=pl.Buffered(1))
    return pl.BlockSpec((D, W_IN_BLK), lambda *_: (0, blk), pipeline_mode=pl.Buffered(1))


def _proj(x2d, g, w, wt):
    n = x2d.shape[0]
    tm = 1024
    nat = jax.ShapeDtypeStruct((n, 512), BF16)
    nat_spec = pl.BlockSpec((tm, 512), lambda i: (i, 0))
    chm = lambda width: (jax.ShapeDtypeStruct((n // TOK_BLK, 512, width), BF16),
                         pl.BlockSpec((tm // TOK_BLK, 512, width), lambda i: (i, 0, 0)))
    (qw_shape, qw_spec), (vt_shape, vt_spec) = chm(2 * TOK_BLK), chm(TOK_BLK)
    return pl.pallas_call(
        _proj_kernel,
        grid=(n // tm,),
        in_specs=[pl.BlockSpec((tm, D), lambda i: (i, 0)),
                  _const_spec((1, D)),
                  _w_in_spec(1, False),
                  _w_in_spec(4, False),
                  _w_in_spec(0, True),
                  _w_in_spec(2, True)],
        out_specs=[pl.BlockSpec((4, tm, 128), lambda i: (0, i, 0)), nat_spec, qw_spec, vt_spec],
        out_shape=[jax.ShapeDtypeStruct((4, n, 128), BF16), nat, qw_shape, vt_shape],
        compiler_params=_cparams(1),
        name="proj",
    )(x2d, g, w, w, wt, wt)


S5T_DT = 8


def _proj_s5t_kernel(x_ref, g_ref, wt_ref, o_ref, h_s):
    wt = wt_ref[...]
    nrows = x_ref.shape[0]
    hn = _rms(x_ref[...], g_ref[...]).reshape(nrows * S5T_DT, D)
    for c in range(D // 128):
        h_s[c] = hn[:, 128 * c:128 * (c + 1)]
    for j in range(S5T_DT):
        h = jnp.concatenate([h_s[c, pl.ds(j, nrows, stride=S5T_DT), :] for c in range(D // 128)],
                            axis=1).astype(BF16)
        ut = lax.dot_general(wt, h, NT_DIMS, preferred_element_type=F32)
        o_ref[:, j * S5_GROUP:(j + 1) * S5_GROUP, :] = (
            ut.astype(BF16).reshape(S5_GROUPS, S5_GROUP, 256))


def _proj_s5t(xc, g, wt):
    nrows = xc.shape[0]
    return pl.pallas_call(
        _proj_s5t_kernel,
        grid=(CHUNK // S5T_DT,),
        in_specs=[pl.BlockSpec((nrows, S5T_DT, D), lambda i: (0, i, 0)),
                  _const_spec((1, D)),
                  _w_in_spec(3, True)],
        out_specs=pl.BlockSpec((S5_GROUPS, S5T_DT * S5_GROUP, nrows), lambda i: (0, i, 0)),
        out_shape=jax.ShapeDtypeStruct((S5_GROUPS, CHUNK * S5_GROUP, nrows), BF16),
        scratch_shapes=[pltpu.VMEM((D // 128, nrows * S5T_DT, 128), F32)],
        compiler_params=_cparams(1),
        name="proj_s5t",
    )(xc, g, wt)


N_SCAN_LEVELS = 6


def _gelu_tanh(x):
    c = 0.7978845608028654
    return 0.5 * x * (1.0 + jnp.tanh(c * (x + 0.044715 * (x * x * x))))


def _split_bf16(x):
    hi = x.astype(BF16)
    return hi, (x - hi.astype(F32)).astype(BF16)


def _nt_f32(a, b):
    ah, al = _split_bf16(a)
    bh, bl = _split_bf16(b)
    nt = lambda u, w: lax.dot_general(u, w, NT_DIMS, preferred_element_type=F32)
    return nt(ah, bh) + nt(ah, bl) + nt(al, bh)


S5_GROUPS_PER_STEP = 4
S5_SCRATCH_PER_GROUP = 5


def _s5_kernel(ut_ref, cm_ref, bb_ref, pw_ref, d_ref, dd_ref, z_ref, *scratch):
    groups = [scratch[S5_SCRATCH_PER_GROUP * gi:S5_SCRATCH_PER_GROUP * (gi + 1)]
              for gi in range(S5_GROUPS_PER_STEP)]
    for gi, (m_s, n_s, pt_s, ptf_s, rs_s) in enumerate(groups):
        _s5_operators(gi, cm_ref, bb_ref, pw_ref, dd_ref, m_s, n_s, pt_s, ptf_s, rs_s)
    yv = [_s5_chunk_matmuls(gi, ut_ref, m_s, pt_s) for gi, (m_s, _, pt_s, _, _) in enumerate(groups)]
    for gi, (_, n_s, _, _, _) in enumerate(groups):
        _s5_finish(gi, d_ref, z_ref, n_s, *yv[gi])


def _s5_operators(gi, cm_ref, bb_ref, pw_ref, dd_ref, m_s, n_s, pt_s, ptf_s, rs_s):
    tc = CHUNK * S5_GROUP
    cma, cmb = cm_ref[gi,0], cm_ref[gi,1]
    bba, bbb = bb_ref[gi,0], bb_ref[gi,1]
    for t in range(CHUNK):
        rows = slice(S5_GROUP * t, S5_GROUP * (t + 1))
        n_s[rows, :] = (cma * pw_ref[gi,0, t:t + 1, :] + cmb * pw_ref[gi,1, t:t + 1, :]).astype(BF16)
        ptf_s[rows, :] = bba * pw_ref[gi,2, t:t + 1, :] + bbb * pw_ref[gi,3, t:t + 1, :]
    ptf = ptf_s[...]
    pt_s[...] = ptf.astype(BF16)

    ra = _nt_f32(cma[:, 0:128], ptf[:, 0:128])
    rb = _nt_f32(cma[:, 128:256], ptf[:, 128:256])
    zeros = jnp.zeros((S5_GROUP, tc), F32)
    r = (jnp.concatenate([ra, zeros], axis=1)
         + pltpu.roll(jnp.concatenate([rb, zeros], axis=1), (CHUNK - 1) * S5_GROUP, axis=1)
         + jnp.concatenate([zeros[:, 0:tc - 128], dd_ref[gi], zeros], axis=1))
    for k in range(8):
        rk = r if k == 0 else pltpu.roll(r, 2 * tc - S5_GROUP * k, axis=1)
        rs_s[k] = rk[:, 0:RS_LANES].astype(BF16)
    for t in range(CHUNK):
        a, k = divmod(CHUNK - 1 - t, 8)
        m_s[S5_GROUP * t:S5_GROUP * (t + 1), :] = rs_s[k, :, 128 * a:128 * a + tc]


def _s5_chunk_matmuls(gi, ut_ref, m_s, pt_s):
    ut = ut_ref[gi]
    nrows = ut.shape[1]
    eye = jnp.where(lax.broadcasted_iota(jnp.int32, (nrows, nrows), 0)
                    == lax.broadcasted_iota(jnp.int32, (nrows, nrows), 1), 1.0, 0.0).astype(BF16)
    u_rows = lax.dot_general(eye, ut, NT_DIMS, preferred_element_type=F32).astype(BF16)
    y = jnp.dot(m_s[...], ut, preferred_element_type=F32)
    v = jnp.dot(u_rows, pt_s[...], preferred_element_type=F32)
    return y, v


def _s5_finish(gi, d_ref, z_ref, n_s, y, v):
    nrows = v.shape[0]
    pos = lax.broadcasted_iota(jnp.int32, (nrows, 128), 0) & (CHUNK - 1)

    def shift(x, s, up):
        if up:
            return jnp.where(pos < CHUNK - s, pltpu.roll(x, nrows - s, axis=0), 0.0)
        return jnp.where(pos >= s, pltpu.roll(x, s, axis=0), 0.0)

    def scan(vh, lanes, up):
        x = shift(vh, 1, up)
        for lvl in range(N_SCAN_LEVELS):
            xs = shift(x, 1 << lvl, up)
            a = d_ref[gi,lvl:lvl + 1, lanes]
            b = d_ref[gi,8 + lvl:9 + lvl, lanes]
            x = x + a * xs + b * pltpu.roll(xs, 64, axis=1)
        return x

    xf = scan(v[:, 0:128], slice(0, 128), False)
    xb = scan(v[:, 128:256], slice(128, 256), True)
    xin = jnp.concatenate([xf, xb], axis=1).astype(BF16)
    y = y + lax.dot_general(n_s[...], xin, NT_DIMS, preferred_element_type=F32)
    z_ref[gi] =_gelu_tanh(y).astype(BF16)


RS_LANES = 128 * 7 + CHUNK * S5_GROUP


def _s5_core(ut, cm, bb, pw, dtab, dd):
    nrows = ut.shape[2]
    tc = CHUNK * S5_GROUP
    gs = S5_GROUPS_PER_STEP
    return pl.pallas_call(
        _s5_kernel,
        grid=(S5_GROUPS // gs,),
        in_specs=[pl.BlockSpec((gs, tc, nrows), lambda g: (g, 0, 0)),
                  pl.BlockSpec((gs, 2, S5_GROUP, 256), lambda g: (g, 0, 0, 0)),
                  pl.BlockSpec((gs, 2, S5_GROUP, 256), lambda g: (g, 0, 0, 0)),
                  pl.BlockSpec((gs, 4, CHUNK, 256), lambda g: (g, 0, 0, 0)),
                  pl.BlockSpec((gs, 16, 256), lambda g: (g, 0, 0)),
                  pl.BlockSpec((gs, S5_GROUP, 128), lambda g: (g, 0, 0))],
        out_specs=pl.BlockSpec((gs, tc, nrows), lambda g: (g, 0, 0)),
        out_shape=jax.ShapeDtypeStruct((S5_GROUPS, tc, nrows), BF16),
        scratch_shapes=[pltpu.VMEM((tc, tc), BF16),
                        pltpu.VMEM((tc, 256), BF16),
                        pltpu.VMEM((tc, 256), BF16),
                        pltpu.VMEM((tc, 256), F32),
                        pltpu.VMEM((8, S5_GROUP, RS_LANES), BF16),
                        ] * gs,
        compiler_params=_cparams(1),
        name="s5_core",
    )(ut, cm, bb, pw, dtab, dd)


def _s5_glu_kernel(z_ref, wt_ref, o_ref):
    wt = wt_ref[...]
    nrows = z_ref.shape[2]
    eye = jnp.where(lax.broadcasted_iota(jnp.int32, (nrows, nrows), 0)
                    == lax.broadcasted_iota(jnp.int32, (nrows, nrows), 1), 1.0, 0.0).astype(BF16)
    for j in range(S5T_DT):
        zt = z_ref[:, j * S5_GROUP:(j + 1) * S5_GROUP, :].reshape(512, nrows)
        gl = jnp.dot(wt, zt, preferred_element_type=F32)
        o = (zt.astype(F32) * jax.nn.sigmoid(gl)).astype(BF16)
        nat = lax.dot_general(eye, o, NT_DIMS, preferred_element_type=F32)
        o_ref[:, j, :] = nat


def _s5_glu(zt, wglu_t):
    nrows = zt.shape[2]
    return pl.pallas_call(
        _s5_glu_kernel,
        grid=(CHUNK // S5T_DT,),
        in_specs=[pl.BlockSpec((S5_GROUPS, S5T_DT * S5_GROUP, nrows), lambda i: (0, i, 0)),
                  _const_spec((512, 512))],
        out_specs=pl.BlockSpec((nrows, S5T_DT, 512), lambda i: (0, i, 0)),
        out_shape=jax.ShapeDtypeStruct((nrows, CHUNK, 512), F32),
        compiler_params=_cparams(1),
        name="s5_glu",
    )(zt, wglu_t)


NA_ROWS_PER_STEP = 32
NA_PAIRS_PER_STEP = NA_ROWS_PER_STEP // 2
NA_WIN_ROWS = 10
NA_WIN_KEYS = NA_WIN_ROWS * GRID_W
NA_WIN_BLKS = NA_WIN_KEYS // TOK_BLK
NA_RR_OUTSIDE = 2 * NA_KH - 1


def _na_kernel(rpb_ref, qw_ref, k_ref, vt_ref, oob_ref, o_ref, *scratch):
    st_s, pt_s, bias_s = (scratch[0:4], scratch[4:8]), (scratch[8:12], scratch[12:16]), scratch[16]
    rb = pl.program_id(1)
    lo_q = lax.broadcasted_iota(jnp.int32, (GRID_W, 128), 1) < 64
    same_head2 = ((lax.broadcasted_iota(jnp.int32, (128, 256), 0) < 64)
                  == ((lax.broadcasted_iota(jnp.int32, (128, 256), 1) & 64) == 0))
    esum = jnp.where(lax.broadcasted_iota(jnp.int32, (GRID_W, 128), 0)
                     == (lax.broadcasted_iota(jnp.int32, (GRID_W, 128), 1) & (GRID_W - 1)),
                     1.0, 0.0).astype(BF16)
    half = NA_KH // 2

    kcol = lax.broadcasted_iota(jnp.int32, (GRID_W, 128), 0)
    qcol = lax.broadcasted_iota(jnp.int32, (GRID_W, 128), 1) & (GRID_W - 1)
    qstart = jnp.clip(qcol - NA_KW // 2, 0, GRID_W - NA_KW)
    col_ok = (kcol >= qstart) & (kcol < qstart + NA_KW)

    @pl.when((pl.program_id(0) == 0) & (rb == 0))
    def _build_bias():
        rel = jnp.clip(kcol - qcol + (NA_KW - 1), 0, 2 * NA_KW - 2)

        def rr_body(rr, carry):
            for s in range(4):
                acc = jnp.zeros((GRID_W, 128), F32)
                for j in range(2 * NA_KW - 1):
                    acc = jnp.where(rel == j, jnp.where(lo_q, rpb_ref[2 * s, rr, j], rpb_ref[2 * s + 1, rr, j]), acc)
                bias_s[s, rr] = acc * LOG2E
            return carry

        lax.fori_loop(0, NA_RR_OUTSIDE, rr_body, 0)
        for s in range(4):
            bias_s[s, NA_RR_OUTSIDE] = oob_ref[...]

    def window(pi):
        r0 = rb * NA_ROWS_PER_STEP + 2 * pi
        return r0, jnp.clip(r0 - half, 0, GRID_W - NA_WIN_ROWS)

    def q_stage(pi):
        _, win = window(pi)
        koff = pl.multiple_of(win * GRID_W, TOK_BLK)
        scs = []
        for s in range(4):
            ch = slice(128 * s, 128 * (s + 1))
            kw = k_ref[s, pl.ds(koff, NA_WIN_KEYS), :]
            scs.append(jnp.dot(kw, qw_ref[pi, ch, :], preferred_element_type=F32))
        return scs

    def s_stage(pi, par, scs):
        r0, win = window(pi)
        rr = []
        for p in range(2):
            r = r0 + p
            first = jnp.clip(r - half, 0, GRID_W - NA_KH) - win
            rel = win - r + (NA_KH - 1)
            rr.append([jnp.where((wr >= first) & (wr < first + NA_KH), wr + rel, NA_RR_OUTSIDE)
                       for wr in range(NA_WIN_ROWS)])
        sums = []
        for s in range(4):
            st = st_s[par][s]
            slab_sums = []
            for p in range(2):
                lanes = slice(128 * p, 128 * (p + 1))
                m = None
                for wr in range(NA_WIN_ROWS):
                    rows = slice(GRID_W * wr, GRID_W * (wr + 1))
                    t = jnp.where(col_ok, scs[s][rows, lanes] + bias_s[s, rr[p][wr]], NEG_INF)
                    st[p, rows, :] = t
                    m = t if m is None else jnp.maximum(m, t)
                m = jnp.max(m, axis=0, keepdims=True)
                l = None
                for wr in range(NA_WIN_ROWS):
                    rows = slice(GRID_W * wr, GRID_W * (wr + 1))
                    e = jnp.exp2(st[p, rows, :] - m)
                    pt_s[par][s][p, rows, :] = e.astype(BF16)
                    l = e if l is None else l + e
                slab_sums.append(jnp.sum(l, axis=0, keepdims=True))
            sums.append(jnp.concatenate(slab_sums, axis=1))
        return tuple(sums)

    def o_stage_pv(pi, par, sums):
        _, win = window(pi)
        blk0 = lax.shift_right_logical(win, 1)
        ots = []
        for s in range(4):
            ch = slice(128 * s, 128 * (s + 1))
            vw = jnp.concatenate([vt_ref[blk0 + i, ch, :] for i in range(NA_WIN_BLKS)], axis=1)
            pt = jnp.concatenate([pt_s[par][s][0], pt_s[par][s][1]], axis=1)
            ot = jnp.dot(vw, pt, preferred_element_type=F32)
            ots.append(jnp.where(same_head2, (ot / sums[s]).astype(BF16), jnp.zeros((), BF16)))
        return ots

    def o_stage_store(pi, ots):
        for s in range(4):
            ch = slice(128 * s, 128 * (s + 1))
            for p in range(2):
                nat = lax.dot_general(esum, ots[s][:, 128 * p:128 * (p + 1)], NT_DIMS,
                                      preferred_element_type=F32)
                qoff = pl.multiple_of((2 * pi + p) * GRID_W, GRID_W)
                o_ref[s, pl.ds(qoff, GRID_W), :] = nat.astype(BF16)

    def step(j, prev_sums, has_prev=True):
        a, b = 2 * j, 2 * j + 1
        scs_a = q_stage(a)
        if has_prev:
            ots_a = o_stage_pv(a - 2, 0, prev_sums[0])
            ots_b = o_stage_pv(b - 2, 1, prev_sums[1])
        scs_b = q_stage(b)
        if has_prev:
            o_stage_store(a - 2, ots_a)
            o_stage_store(b - 2, ots_b)
        return s_stage(a, 0, scs_a), s_stage(b, 1, scs_b)

    sums = step(0, None, has_prev=False)
    sums = lax.fori_loop(1, NA_PAIRS_PER_STEP // 2, step, sums)
    for u in range(2):
        pi = NA_PAIRS_PER_STEP - 2 + u
        o_stage_store(pi, o_stage_pv(pi, u, sums[u]))


def _na(rpb, qw, k, vt, b, s):
    tq = NA_ROWS_PER_STEP * GRID_W
    blks = s // TOK_BLK
    steps = s // tq
    oob = jnp.full((GRID_W, 128), -jnp.inf, F32)
    return pl.pallas_call(
        _na_kernel,
        grid=(b, s // tq),
        in_specs=[pl.BlockSpec(memory_space=pltpu.SMEM),
                  pl.BlockSpec((NA_PAIRS_PER_STEP, 512, 2 * TOK_BLK),
                               lambda bi, ri: (bi * (blks // NA_PAIRS_PER_STEP) + ri, 0, 0)),
                  pl.BlockSpec((4, s, 128), lambda bi, ri: (0, bi, 0)),
                  pl.BlockSpec((blks, 512, TOK_BLK), lambda bi, ri: (bi, 0, 0)),
                  _const_spec(oob.shape)],
        out_specs=pl.BlockSpec((4, tq, 128), lambda bi, ri: (0, bi * steps + ri, 0)),
        out_shape=jax.ShapeDtypeStruct((4, b * s, 128), BF16),
        scratch_shapes=([pltpu.VMEM((2, NA_WIN_KEYS, 128), F32)] * 8
                        + [pltpu.VMEM((2, NA_WIN_KEYS, 128), BF16)] * 8
                        + [pltpu.VMEM((NA_HEADS // 2, 2 * NA_KH, GRID_W, 128), F32)]),
        compiler_params=_cparams(2),
        name="na",
    )(rpb, qw, k, vt, oob)


def _memkv_kernel(mem_ref, g_ref, w_ref, k_ref, v_ref):
    mn = _rms(mem_ref[0], g_ref[...]).astype(BF16)
    kv = jnp.dot(mn, w_ref[...].astype(BF16), preferred_element_type=F32)
    k_ref[0] = kv[:, 0:512].astype(BF16)
    v_ref[0] = kv[:, 512:1024].astype(BF16)


def _memkv(mem, g, w):
    b, m, _ = mem.shape
    out = jax.ShapeDtypeStruct((b, m, 512), BF16)
    ospec = pl.BlockSpec((1, m, 512), lambda i: (i, 0, 0))
    return pl.pallas_call(
        _memkv_kernel,
        grid=(b,),
        in_specs=[pl.BlockSpec((1, m, D), lambda i: (i, 0, 0)),
                  _const_spec((1, D)),
                  _const_spec((D, 1024))],
        out_specs=[ospec, ospec],
        out_shape=[out, out],
        compiler_params=_cparams(1),
        name="memkv",
    )(mem, g, w)


def _mem_kernel(q_ref, k_ref, v_ref, o_ref):
    scale = MEM_HEAD_DIM ** -0.5
    for h in range(MEM_HEADS):
        cols = slice(MEM_HEAD_DIM * h, MEM_HEAD_DIM * (h + 1))
        sc = lax.dot_general(q_ref[0, :, cols], k_ref[0, :, cols], NT_DIMS,
                             preferred_element_type=F32) * scale
        m = jnp.max(sc, axis=-1, keepdims=True)
        p = jnp.exp(sc - m)
        l = jnp.sum(p, axis=-1, keepdims=True)
        o = jnp.dot(p.astype(BF16), v_ref[0, :, cols], preferred_element_type=F32)
        o_ref[0, :, cols] = (o / l).astype(BF16)


def _mem_attn(q, k, v):
    b, s, _ = q.shape
    m = k.shape[1]
    tq = 1024
    return pl.pallas_call(
        _mem_kernel,
        grid=(b, s // tq),
        in_specs=[pl.BlockSpec((1, tq, 512), lambda bi, i: (bi, i, 0)),
                  pl.BlockSpec((1, m, 512), lambda bi, i: (bi, 0, 0)),
                  pl.BlockSpec((1, m, 512), lambda bi, i: (bi, 0, 0))],
        out_specs=pl.BlockSpec((1, tq, 512), lambda bi, i: (bi, i, 0)),
        out_shape=jax.ShapeDtypeStruct((b, s, 512), BF16),
        compiler_params=_cparams(2),
        name="mem_attn",
    )(q, k, v)


def _merge_kernel(x_ref, g_ref, yna_ref, ys5_ref, ymem_ref, wg_ref, bg_ref, wb_ref, wo_ref, o_ref):
    x = x_ref[...]
    h = _rms(x, g_ref[...]).astype(BF16)
    merged = None
    y_na = jnp.concatenate([yna_ref[s] for s in range(4)], axis=1)
    ys = (y_na, ys5_ref[...].astype(BF16), ymem_ref[...])
    for b, y in enumerate(ys):
        cols = slice(D * b, D * (b + 1))
        gate = jax.nn.sigmoid(jnp.dot(h, wg_ref[:, cols].astype(BF16), preferred_element_type=F32)
                              + bg_ref[:, cols])
        up = jnp.dot(y, wb_ref[b], preferred_element_type=F32)
        merged = gate * up if merged is None else merged + gate * up
    o_ref[...] = x + jnp.dot(merged.astype(BF16), wo_ref[...], preferred_element_type=F32)


def _merge(x2d, g, yna, ys5, ymem, wg, bg, wb, wo):
    n = x2d.shape[0]
    tm = 1024
    yspec = pl.BlockSpec((tm, 512), lambda i: (i, 0))
    return pl.pallas_call(
        _merge_kernel,
        grid=(n // tm,),
        in_specs=[pl.BlockSpec((tm, D), lambda i: (i, 0)),
                  _const_spec((1, D)),
                  pl.BlockSpec((4, tm, 128), lambda i: (0, i, 0)), yspec, yspec,
                  _const_spec((D, 3 * D)),
                  _const_spec((1, 3 * D)),
                  _const_spec((3, 512, D)),
                  _const_spec((D, D))],
        out_specs=pl.BlockSpec((tm, D), lambda i: (i, 0)),
        out_shape=jax.ShapeDtypeStruct((n, D), F32),
        compiler_params=_cparams(1),
        name="merge",
    )(x2d, g, yna, ys5, ymem, wg, bg, wb, wo)


def _ffn_kernel(x_ref, g_ref, gf_ref, w1_ref, w3_ref, w2_ref, o_ref):
    x = x_ref[...]
    h = _rms(x, g_ref[...]).astype(BF16)
    a = jnp.dot(h, w1_ref[...], preferred_element_type=F32)
    c = jnp.dot(h, w3_ref[...], preferred_element_type=F32)
    mid = (a * jax.nn.sigmoid(a) * c).astype(BF16)
    x2 = x + jnp.dot(mid, w2_ref[...].astype(BF16), preferred_element_type=F32)
    o_ref[...] = _rms(x2, gf_ref[...])


def _ffn(x2d, g, gf, w1, w3, w2):
    n = x2d.shape[0]
    tm = 512
    return pl.pallas_call(
        _ffn_kernel,
        grid=(n // tm,),
        in_specs=[pl.BlockSpec((tm, D), lambda i: (i, 0)),
                  _const_spec((1, D)),
                  _const_spec((1, D)),
                  _const_spec((D, D_FF)),
                  _const_spec((D, D_FF)),
                  _const_spec((D_FF, D))],
        out_specs=pl.BlockSpec((tm, D), lambda i: (i, 0)),
        out_shape=jax.ShapeDtypeStruct((n, D), F32),
        compiler_params=_cparams(1),
        name="ffn",
    )(x2d, g, gf, w1, w3, w2)


def _s5_tables(a_re, a_im, log_dt, b_re, b_im, c_re, c_im, s5_d):
    t = CHUNK
    ar, ai = a_re.astype(F32), a_im.astype(F32)
    dt = jnp.exp(log_dt.astype(F32))[..., None]
    lr, li = ar * dt, ai * dt
    mag = jnp.exp(lr)
    lbr, lbi = mag * jnp.cos(li), mag * jnp.sin(li)
    den = ar * ar + ai * ai
    rr = ((lbr - 1.0) * ar + lbi * ai) / den
    ri = (lbi * ar - (lbr - 1.0) * ai) / den
    br, bi = b_re.astype(F32), b_im.astype(F32)
    bbr = rr[..., None] * br - ri[..., None] * bi
    bbi = rr[..., None] * bi + ri[..., None] * br
    cmr, cmi = c_re.astype(F32), c_im.astype(F32)
    tau = jnp.arange(t + 1, dtype=F32)[:, None, None, None]
    pmag = jnp.exp(tau * lr[None])
    pwr, pwi = pmag * jnp.cos(tau * li[None]), pmag * jnp.sin(tau * li[None])

    dd = jnp.pad(jnp.eye(S5_GROUP, dtype=F32)[None] * s5_d.astype(F32).reshape(S5_GROUPS, S5_GROUP, 1),
                 ((0, 0), (0, 0), (128 - S5_GROUP, 0)))

    def lanes4(f0, f1, b0, b1):
        return jnp.concatenate([f0, f1, b0, b1], axis=-1)

    cm = jnp.stack([lanes4(cmr[0], -cmi[0], cmr[1], -cmi[1]),
                    lanes4(-cmi[0], -cmr[0], -cmi[1], -cmr[1])], axis=1)
    fr, fi = jnp.moveaxis(pwr[1:t + 1, 0], 0, 1), jnp.moveaxis(pwi[1:t + 1, 0], 0, 1)
    gr, gi = jnp.moveaxis(pwr[1:t + 1][::-1, 1], 0, 1), jnp.moveaxis(pwi[1:t + 1][::-1, 1], 0, 1)
    bt = lambda z: jnp.swapaxes(z, -1, -2)
    bb = jnp.stack([lanes4(bt(bbr[0]), bt(bbi[0]), bt(bbr[1]), bt(bbi[1])),
                    lanes4(-bt(bbi[0]), bt(bbr[0]), -bt(bbi[1]), bt(bbr[1]))], axis=1)
    wr, wi = jnp.moveaxis(pwr[:t][::-1, 0], 0, 1), jnp.moveaxis(pwi[:t][::-1, 0], 0, 1)
    vr, vi = jnp.moveaxis(pwr[:t, 1], 0, 1), jnp.moveaxis(pwi[:t, 1], 0, 1)
    pw = jnp.stack([lanes4(fr, fr, gr, gr), lanes4(fi, fi, gi, gi),
                    lanes4(wr, wr, vr, vr), lanes4(wi, wi, vi, vi)], axis=1)

    dr, di = pwr[t], pwi[t]
    a_rows, b_rows = [], []
    for _ in range(N_SCAN_LEVELS):
        a_rows.append(lanes4(dr[0], dr[0], dr[1], dr[1]))
        b_rows.append(lanes4(-di[0], di[0], -di[1], di[1]))
        dr, di = dr * dr - di * di, 2.0 * dr * di
    pad = [jnp.zeros_like(a_rows[0])] * (8 - N_SCAN_LEVELS)
    dtab = jnp.stack(a_rows + pad + b_rows + pad, axis=1)
    return cm, bb, pw, dtab, dd


def kernel(x, mem, g_mix, g_mem, g_ffn, g_final, w_in, w_gate, b_gate, rpb, w_mem_kv,
           a_re, a_im, log_dt, b_re, b_im, c_re, c_im, s5_d, w_glu, w_branch, w_o,
           w_ffn1, w_ffn3, w_ffn2):
    bsz, s, d = x.shape
    n = bsz * s
    x2d = x.reshape(n, d)
    gm = g_mix[0].reshape(1, d).astype(F32)

    w_in_t = w_in[0].T.astype(BF16)
    cm_tab, bb_tab, pw_tab, dtab, dd_tab = _s5_tables(a_re[0], a_im[0], log_dt[0], b_re[0], b_im[0],
                                                      c_re[0], c_im[0], s5_d[0])

    k, qm, qw, vt = _proj(x2d, gm, w_in[0], w_in_t)

    ut = _proj_s5t(x.reshape(n // CHUNK, CHUNK, d), gm, w_in_t)
    zt = _s5_core(ut, cm_tab, bb_tab, pw_tab, dtab, dd_tab)
    y_s5 = _s5_glu(zt, w_glu[0].T.astype(BF16)).reshape(n, 512)

    y_na = _na(rpb[0].astype(F32), qw, k, vt, bsz, s)

    k_mem, v_mem = _memkv(mem, g_mem[0].reshape(1, d).astype(F32), w_mem_kv[0])
    y_mem = _mem_attn(qm.reshape(bsz, s, 512), k_mem, v_mem).reshape(n, 512)

    x1 = _merge(x2d, gm, y_na, y_s5, y_mem, w_gate[0],
                b_gate[0].reshape(1, 3 * d).astype(F32), w_branch[0].astype(BF16), w_o[0].astype(BF16))
    out = _ffn(x1, g_ffn[0].reshape(1, d).astype(F32), g_final.reshape(1, d).astype(F32),
               w_ffn1[0].astype(BF16), w_ffn3[0].astype(BF16), w_ffn2[0])
    return out.reshape(bsz, s, d)
```

```python
import functools

import jax
import jax.numpy as jnp
from jax import lax
from jax.experimental import pallas as pl
from jax.experimental.pallas import tpu as pltpu

F32 = jnp.float32
BF16 = jnp.bfloat16

D = 1024
GRID_W = 64
NA_HEADS = 8
NA_KH = 8
NA_KW = 16
S5_GROUPS = 32
S5_GROUP = 16
S5_STATE = 64
CHUNK = 64
MEM_HEADS = 4
MEM_HEAD_DIM = 128
D_FF = 2816
EPS = 1e-6
NEG_INF = -1e30

VMEM_LIMIT = 56 * 1024 * 1024

NT_DIMS = (((1,), (1,)), ((), ()))


def _cparams(n_axes):
    return pltpu.CompilerParams(
        dimension_semantics=("arbitrary",) * n_axes,
        vmem_limit_bytes=VMEM_LIMIT)


def _rms(x, g):
    return x * lax.rsqrt(jnp.mean(x * x, axis=-1, keepdims=True) + EPS) * g


def _const_spec(shape):
    nd = len(shape)
    return pl.BlockSpec(shape, lambda *_: (0,) * nd, pipeline_mode=pl.Buffered(1))


TOK_BLK = 128


LOG2E = 1.4426950408889634
NA_Q_SCALE = 64 ** -0.5 * LOG2E


def _proj_kernel(x_ref, g_ref, wk_ref, wqm_ref, wqt_ref, wvt_ref, k_ref, qm_ref, qw_ref, vt_ref):
    h = _rms(x_ref[...], g_ref[...]).astype(BF16)
    kk = jnp.dot(h, wk_ref[...].astype(BF16), preferred_element_type=F32).astype(BF16)
    for s in range(4):
        k_ref[s] = kk[:, 128 * s:128 * (s + 1)]
    qm_ref[...] = jnp.dot(h, wqm_ref[...].astype(BF16), preferred_element_type=F32).astype(BF16)
    qt = lax.dot_general(wqt_ref[...], h, NT_DIMS, preferred_element_type=F32) * NA_Q_SCALE
    vt = lax.dot_general(wvt_ref[...], h, NT_DIMS, preferred_element_type=F32)
    lo = lax.broadcasted_iota(jnp.int32, (128, 128), 1) < 64
    same_head = (lax.broadcasted_iota(jnp.int32, (128, 128), 0) < 64) == lo
    for i in range(x_ref.shape[0] // TOK_BLK):
        toks = slice(TOK_BLK * i, TOK_BLK * (i + 1))
        vt_ref[i] = vt[:, toks].astype(BF16)
        for s in range(4):
            a = qt[128 * s:128 * (s + 1), toks]
            ar = pltpu.roll(a, 64, axis=1)
            qw_ref[i, 128 * s:128 * (s + 1), :] = jnp.concatenate(
                [jnp.where(same_head, jnp.where(lo, a, ar), 0.0),
                 jnp.where(same_head, jnp.where(lo, ar, a), 0.0)], axis=1).astype(BF16)


W_IN_BLK = 512


def _w_in_spec(blk, transposed):
    if transposed:
        return pl.BlockSpec((W_IN_BLK, D), lambda *_: (blk, 0), pipeline_mode=pl.Buffered(1))
    return pl.BlockSpec((D, W_IN_BLK), lambda *_: (0, blk), pipeline_mode=pl.Buffered(1))


def _proj(x2d, g, w, wt):
    n = x2d.shape[0]
    tm = 1024
    nat = jax.ShapeDtypeStruct((n, 512), BF16)
    nat_spec = pl.BlockSpec((tm, 512), lambda i: (i, 0))
    chm = lambda width: (jax.ShapeDtypeStruct((n // TOK_BLK, 512, width), BF16),
                         pl.BlockSpec((tm // TOK_BLK, 512, width), lambda i: (i, 0, 0)))
    (qw_shape, qw_spec), (vt_shape, vt_spec) = chm(2 * TOK_BLK), chm(TOK_BLK)
    return pl.pallas_call(
        _proj_kernel,
        grid=(n // tm,),
        in_specs=[pl.BlockSpec((tm, D), lambda i: (i, 0)),
                  _const_spec((1, D)),
                  _w_in_spec(1, False),
                  _w_in_spec(4, False),
                  _w_in_spec(0, True),
                  _w_in_spec(2, True)],
        out_specs=[pl.BlockSpec((4, tm, 128), lambda i: (0, i, 0)), nat_spec, qw_spec, vt_spec],
        out_shape=[jax.ShapeDtypeStruct((4, n, 128), BF16), nat, qw_shape, vt_shape],
        compiler_params=_cparams(1),
        name="proj",
    )(x2d, g, w, w, wt, wt)


S5T_DT = 8


def _proj_s5t_kernel(x_ref, g_ref, wt_ref, o_ref, h_s):
    wt = wt_ref[...]
    nrows = x_ref.shape[0]
    hn = _rms(x_ref[...], g_ref[...]).reshape(nrows * S5T_DT, D)
    for c in range(D // 128):
        h_s[c] = hn[:, 128 * c:128 * (c + 1)]
    for j in range(S5T_DT):
        h = jnp.concatenate([h_s[c, pl.ds(j, nrows, stride=S5T_DT), :] for c in range(D // 128)],
                            axis=1).astype(BF16)
        ut = lax.dot_general(wt, h, NT_DIMS, preferred_element_type=F32)
        o_ref[j] = ut.astype(BF16)


def _proj_s5t(xc, g, wt):
    nrows = xc.shape[0]
    return pl.pallas_call(
        _proj_s5t_kernel,
        grid=(CHUNK // S5T_DT,),
        in_specs=[pl.BlockSpec((nrows, S5T_DT, D), lambda i: (0, i, 0)),
                  _const_spec((1, D)),
                  _w_in_spec(3, True)],
        out_specs=pl.BlockSpec((S5T_DT, 512, nrows), lambda i: (i, 0, 0)),
        out_shape=jax.ShapeDtypeStruct((CHUNK, 512, nrows), BF16),
        scratch_shapes=[pltpu.VMEM((D // 128, nrows * S5T_DT, 128), F32)],
        compiler_params=_cparams(1),
        name="proj_s5t",
    )(xc, g, wt)


N_SCAN_LEVELS = 6


def _gelu_tanh(x):
    c = 0.7978845608028654
    return 0.5 * x * (1.0 + jnp.tanh(c * (x + 0.044715 * (x * x * x))))


def _split_bf16(x):
    hi = x.astype(BF16)
    return hi, (x - hi.astype(F32)).astype(BF16)


def _nt_f32(a, b):
    ah, al = _split_bf16(a)
    bh, bl = _split_bf16(b)
    nt = lambda u, w: lax.dot_general(u, w, NT_DIMS, preferred_element_type=F32)
    return nt(ah, bh) + nt(ah, bl) + nt(al, bh)


S5_GROUPS_PER_STEP = 4
S5_SCRATCH_PER_GROUP = 5


def _s5_kernel(*refs):
    gs = S5_GROUPS_PER_STEP
    ut_ref, (cm_ref, bb_ref, pw_ref, d_ref, dd_ref) = refs[:gs], refs[gs:gs + 5]
    z_ref, scratch = refs[gs + 5], refs[gs + 6:]
    groups = [scratch[S5_SCRATCH_PER_GROUP * gi:S5_SCRATCH_PER_GROUP * (gi + 1)]
              for gi in range(S5_GROUPS_PER_STEP)]
    for gi, (m_s, n_s, pt_s, ptf_s, rs_s) in enumerate(groups):
        _s5_operators(gi, cm_ref, bb_ref, pw_ref, dd_ref, m_s, n_s, pt_s, ptf_s, rs_s)
    yv = [_s5_chunk_matmuls(gi, ut_ref, m_s, pt_s) for gi, (m_s, _, pt_s, _, _) in enumerate(groups)]
    for gi, (_, n_s, _, _, _) in enumerate(groups):
        _s5_finish(gi, d_ref, z_ref, n_s, *yv[gi])


def _s5_operators(gi, cm_ref, bb_ref, pw_ref, dd_ref, m_s, n_s, pt_s, ptf_s, rs_s):
    tc = CHUNK * S5_GROUP
    cma, cmb = cm_ref[gi,0], cm_ref[gi,1]
    bba, bbb = bb_ref[gi,0], bb_ref[gi,1]
    for t in range(CHUNK):
        rows = slice(S5_GROUP * t, S5_GROUP * (t + 1))
        n_s[rows, :] = (cma * pw_ref[gi,0, t:t + 1, :] + cmb * pw_ref[gi,1, t:t + 1, :]).astype(BF16)
        ptf_s[rows, :] = bba * pw_ref[gi,2, t:t + 1, :] + bbb * pw_ref[gi,3, t:t + 1, :]
    ptf = ptf_s[...]
    pt_s[...] = ptf.astype(BF16)

    ra = _nt_f32(cma[:, 0:128], ptf[:, 0:128])
    rb = _nt_f32(cma[:, 128:256], ptf[:, 128:256])
    zeros = jnp.zeros((S5_GROUP, tc), F32)
    r = (jnp.concatenate([ra, zeros], axis=1)
         + pltpu.roll(jnp.concatenate([rb, zeros], axis=1), (CHUNK - 1) * S5_GROUP, axis=1)
         + jnp.concatenate([zeros[:, 0:tc - 128], dd_ref[gi], zeros], axis=1))
    for k in range(8):
        rk = r if k == 0 else pltpu.roll(r, 2 * tc - S5_GROUP * k, axis=1)
        rs_s[k] = rk[:, 0:RS_LANES].astype(BF16)
    for t in range(CHUNK):
        a, k = divmod(CHUNK - 1 - t, 8)
        m_s[S5_GROUP * t:S5_GROUP * (t + 1), :] = rs_s[k, :, 128 * a:128 * a + tc]


def _s5_chunk_matmuls(gi, ut_ref, m_s, pt_s):
    ut = ut_ref[gi][...].reshape(CHUNK * S5_GROUP, -1)
    nrows = ut.shape[1]
    eye = jnp.where(lax.broadcasted_iota(jnp.int32, (nrows, nrows), 0)
                    == lax.broadcasted_iota(jnp.int32, (nrows, nrows), 1), 1.0, 0.0).astype(BF16)
    u_rows = lax.dot_general(eye, ut, NT_DIMS, preferred_element_type=F32).astype(BF16)
    y = jnp.dot(m_s[...], ut, preferred_element_type=F32)
    v = jnp.dot(u_rows, pt_s[...], preferred_element_type=F32)
    return y, v


def _s5_finish(gi, d_ref, z_ref, n_s, y, v):
    nrows = v.shape[0]
    pos = lax.broadcasted_iota(jnp.int32, (nrows, 128), 0) & (CHUNK - 1)

    def shift(x, s, up):
        if up:
            return jnp.where(pos < CHUNK - s, pltpu.roll(x, nrows - s, axis=0), 0.0)
        return jnp.where(pos >= s, pltpu.roll(x, s, axis=0), 0.0)

    def scan(vh, lanes, up):
        x = shift(vh, 1, up)
        for lvl in range(N_SCAN_LEVELS):
            xs = shift(x, 1 << lvl, up)
            a = d_ref[gi,lvl:lvl + 1, lanes]
            b = d_ref[gi,8 + lvl:9 + lvl, lanes]
            x = x + a * xs + b * pltpu.roll(xs, 64, axis=1)
        return x

    xf = scan(v[:, 0:128], slice(0, 128), False)
    xb = scan(v[:, 128:256], slice(128, 256), True)
    xin = jnp.concatenate([xf, xb], axis=1).astype(BF16)
    y = y + lax.dot_general(n_s[...], xin, NT_DIMS, preferred_element_type=F32)
    z_ref[:, S5_GROUP * gi:S5_GROUP * (gi + 1), :] = _gelu_tanh(y).astype(BF16).reshape(CHUNK, S5_GROUP, -1)


RS_LANES = 128 * 7 + CHUNK * S5_GROUP


def _s5_core(ut, cm, bb, pw, dtab, dd):
    nrows = ut.shape[2]
    tc = CHUNK * S5_GROUP
    gs = S5_GROUPS_PER_STEP
    group_specs = [pl.BlockSpec((CHUNK, S5_GROUP, nrows), functools.partial(lambda i, g: (0, gs * g + i, 0), i))
                   for i in range(gs)]
    return pl.pallas_call(
        _s5_kernel,
        grid=(S5_GROUPS // gs,),
        in_specs=[*group_specs,
                  pl.BlockSpec((gs, 2, S5_GROUP, 256), lambda g: (g, 0, 0, 0)),
                  pl.BlockSpec((gs, 2, S5_GROUP, 256), lambda g: (g, 0, 0, 0)),
                  pl.BlockSpec((gs, 4, CHUNK, 256), lambda g: (g, 0, 0, 0)),
                  pl.BlockSpec((gs, 16, 256), lambda g: (g, 0, 0)),
                  pl.BlockSpec((gs, S5_GROUP, 128), lambda g: (g, 0, 0))],
        out_specs=pl.BlockSpec((CHUNK, gs * S5_GROUP, nrows), lambda g: (0, g, 0)),
        out_shape=jax.ShapeDtypeStruct(ut.shape, BF16),
        scratch_shapes=[pltpu.VMEM((tc, tc), BF16),
                        pltpu.VMEM((tc, 256), BF16),
                        pltpu.VMEM((tc, 256), BF16),
                        pltpu.VMEM((tc, 256), F32),
                        pltpu.VMEM((8, S5_GROUP, RS_LANES), BF16),
                        ] * gs,
        compiler_params=_cparams(1),
        name="s5_core",
    )(*([ut] * gs), cm, bb, pw, dtab, dd)


def _s5_glu_kernel(z_ref, wt_ref, o_ref):
    wt = wt_ref[...]
    nrows = z_ref.shape[2]
    eye = jnp.where(lax.broadcasted_iota(jnp.int32, (nrows, nrows), 0)
                    == lax.broadcasted_iota(jnp.int32, (nrows, nrows), 1), 1.0, 0.0).astype(BF16)
    for j in range(S5T_DT):
        zt = z_ref[j]
        gl = jnp.dot(wt, zt, preferred_element_type=F32)
        o = (zt.astype(F32) * jax.nn.sigmoid(gl)).astype(BF16)
        nat = lax.dot_general(eye, o, NT_DIMS, preferred_element_type=F32)
        o_ref[:, j, :] = nat


def _s5_glu(zt, wglu_t):
    nrows = zt.shape[2]
    return pl.pallas_call(
        _s5_glu_kernel,
        grid=(CHUNK // S5T_DT,),
        in_specs=[pl.BlockSpec((S5T_DT, 512, nrows), lambda i: (i, 0, 0)),
                  _const_spec((512, 512))],
        out_specs=pl.BlockSpec((nrows, S5T_DT, 512), lambda i: (0, i, 0)),
        out_shape=jax.ShapeDtypeStruct((nrows, CHUNK, 512), F32),
        compiler_params=_cparams(1),
        name="s5_glu",
    )(zt, wglu_t)


NA_ROWS_PER_STEP = 32
NA_PAIRS_PER_STEP = NA_ROWS_PER_STEP // 2
NA_WIN_ROWS = 10
NA_WIN_KEYS = NA_WIN_ROWS * GRID_W
NA_WIN_BLKS = NA_WIN_KEYS // TOK_BLK
NA_RR_OUTSIDE = 2 * NA_KH - 1


def _na_kernel(rpb_ref, qw_ref, k_ref, vt_ref, oob_ref, o_ref, *scratch):
    st_s, pt_s, bias_s = (scratch[0:4], scratch[4:8]), (scratch[8:12], scratch[12:16]), scratch[16]
    rb = pl.program_id(1)
    lo_q = lax.broadcasted_iota(jnp.int32, (GRID_W, 128), 1) < 64
    same_head2 = ((lax.broadcasted_iota(jnp.int32, (128, 256), 0) < 64)
                  == ((lax.broadcasted_iota(jnp.int32, (128, 256), 1) & 64) == 0))
    esum = jnp.where(lax.broadcasted_iota(jnp.int32, (GRID_W, 128), 0)
                     == (lax.broadcasted_iota(jnp.int32, (GRID_W, 128), 1) & (GRID_W - 1)),
                     1.0, 0.0).astype(BF16)
    half = NA_KH // 2

    kcol = lax.broadcasted_iota(jnp.int32, (GRID_W, 128), 0)
    qcol = lax.broadcasted_iota(jnp.int32, (GRID_W, 128), 1) & (GRID_W - 1)
    qstart = jnp.clip(qcol - NA_KW // 2, 0, GRID_W - NA_KW)
    col_ok = (kcol >= qstart) & (kcol < qstart + NA_KW)

    @pl.when((pl.program_id(0) == 0) & (rb == 0))
    def _build_bias():
        rel = jnp.clip(kcol - qcol + (NA_KW - 1), 0, 2 * NA_KW - 2)

        def rr_body(rr, carry):
            for s in range(4):
                acc = jnp.zeros((GRID_W, 128), F32)
                for j in range(2 * NA_KW - 1):
                    acc = jnp.where(rel == j, jnp.where(lo_q, rpb_ref[2 * s, rr, j], rpb_ref[2 * s + 1, rr, j]), acc)
                bias_s[s, rr] = acc * LOG2E
            return carry

        lax.fori_loop(0, NA_RR_OUTSIDE, rr_body, 0)
        for s in range(4):
            bias_s[s, NA_RR_OUTSIDE] = oob_ref[...]

    def window(pi):
        r0 = rb * NA_ROWS_PER_STEP + 2 * pi
        return r0, jnp.clip(r0 - half, 0, GRID_W - NA_WIN_ROWS)

    def q_stage(pi):
        _, win = window(pi)
        koff = pl.multiple_of(win * GRID_W, TOK_BLK)
        scs = []
        for s in range(4):
            ch = slice(128 * s, 128 * (s + 1))
            kw = k_ref[s, pl.ds(koff, NA_WIN_KEYS), :]
            scs.append(jnp.dot(kw, qw_ref[pi, ch, :], preferred_element_type=F32))
        return scs

    def s_stage(pi, par, scs):
        r0, win = window(pi)
        rr = []
        for p in range(2):
            r = r0 + p
            first = jnp.clip(r - half, 0, GRID_W - NA_KH) - win
            rel = win - r + (NA_KH - 1)
            rr.append([jnp.where((wr >= first) & (wr < first + NA_KH), wr + rel, NA_RR_OUTSIDE)
                       for wr in range(NA_WIN_ROWS)])
        sums = []
        for s in range(4):
            st = st_s[par][s]
            slab_sums = []
            for p in range(2):
                lanes = slice(128 * p, 128 * (p + 1))
                m = None
                for wr in range(NA_WIN_ROWS):
                    rows = slice(GRID_W * wr, GRID_W * (wr + 1))
                    t = jnp.where(col_ok, scs[s][rows, lanes] + bias_s[s, rr[p][wr]], NEG_INF)
                    st[p, rows, :] = t
                    m = t if m is None else jnp.maximum(m, t)
                m = jnp.max(m, axis=0, keepdims=True)
                l = None
                for wr in range(NA_WIN_ROWS):
                    rows = slice(GRID_W * wr, GRID_W * (wr + 1))
                    e = jnp.exp2(st[p, rows, :] - m)
                    pt_s[par][s][p, rows, :] = e.astype(BF16)
                    l = e if l is None else l + e
                slab_sums.append(jnp.sum(l, axis=0, keepdims=True))
            sums.append(jnp.concatenate(slab_sums, axis=1))
        return tuple(sums)

    def o_stage_pv(pi, par, sums):
        _, win = window(pi)
        blk0 = lax.shift_right_logical(win, 1)
        ots = []
        for s in range(4):
            ch = slice(128 * s, 128 * (s + 1))
            vw = jnp.concatenate([vt_ref[blk0 + i, ch, :] for i in range(NA_WIN_BLKS)], axis=1)
            pt = jnp.concatenate([pt_s[par][s][0], pt_s[par][s][1]], axis=1)
            ot = jnp.dot(vw, pt, preferred_element_type=F32)
            ots.append(jnp.where(same_head2, (ot / sums[s]).astype(BF16), jnp.zeros((), BF16)))
        return ots

    def o_stage_store(pi, ots):
        for s in range(4):
            ch = slice(128 * s, 128 * (s + 1))
            for p in range(2):
                nat = lax.dot_general(esum, ots[s][:, 128 * p:128 * (p + 1)], NT_DIMS,
                                      preferred_element_type=F32)
                qoff = pl.multiple_of((2 * pi + p) * GRID_W, GRID_W)
                o_ref[s, pl.ds(qoff, GRID_W), :] = nat.astype(BF16)

    def step(j, prev_sums, has_prev=True):
        a, b = 2 * j, 2 * j + 1
        scs_a = q_stage(a)
        if has_prev:
            ots_a = o_stage_pv(a - 2, 0, prev_sums[0])
            ots_b = o_stage_pv(b - 2, 1, prev_sums[1])
        scs_b = q_stage(b)
        if has_prev:
            o_stage_store(a - 2, ots_a)
            o_stage_store(b - 2, ots_b)
        return s_stage(a, 0, scs_a), s_stage(b, 1, scs_b)

    sums = step(0, None, has_prev=False)
    sums = lax.fori_loop(1, NA_PAIRS_PER_STEP // 2, step, sums)
    for u in range(2):
        pi = NA_PAIRS_PER_STEP - 2 + u
        o_stage_store(pi, o_stage_pv(pi, u, sums[u]))


def _na(rpb, qw, k, vt, b, s):
    tq = NA_ROWS_PER_STEP * GRID_W
    blks = s // TOK_BLK
    steps = s // tq
    oob = jnp.full((GRID_W, 128), -jnp.inf, F32)
    return pl.pallas_call(
        _na_kernel,
        grid=(b, s // tq),
        in_specs=[pl.BlockSpec(memory_space=pltpu.SMEM),
                  pl.BlockSpec((NA_PAIRS_PER_STEP, 512, 2 * TOK_BLK),
                               lambda bi, ri: (bi * (blks // NA_PAIRS_PER_STEP) + ri, 0, 0)),
                  pl.BlockSpec((4, s, 128), lambda bi, ri: (0, bi, 0)),
                  pl.BlockSpec((blks, 512, TOK_BLK), lambda bi, ri: (bi, 0, 0)),
                  _const_spec(oob.shape)],
        out_specs=pl.BlockSpec((4, tq, 128), lambda bi, ri: (0, bi * steps + ri, 0)),
        out_shape=jax.ShapeDtypeStruct((4, b * s, 128), BF16),
        scratch_shapes=([pltpu.VMEM((2, NA_WIN_KEYS, 128), F32)] * 8
                        + [pltpu.VMEM((2, NA_WIN_KEYS, 128), BF16)] * 8
                        + [pltpu.VMEM((NA_HEADS // 2, 2 * NA_KH, GRID_W, 128), F32)]),
        compiler_params=_cparams(2),
        name="na",
    )(rpb, qw, k, vt, oob)


def _memkv_kernel(mem_ref, g_ref, w_ref, k_ref, v_ref):
    mn = _rms(mem_ref[0], g_ref[...]).astype(BF16)
    kv = jnp.dot(mn, w_ref[...].astype(BF16), preferred_element_type=F32)
    k_ref[0] = kv[:, 0:512].astype(BF16)
    v_ref[0] = kv[:, 512:1024].astype(BF16)


def _memkv(mem, g, w):
    b, m, _ = mem.shape
    out = jax.ShapeDtypeStruct((b, m, 512), BF16)
    ospec = pl.BlockSpec((1, m, 512), lambda i: (i, 0, 0))
    return pl.pallas_call(
        _memkv_kernel,
        grid=(b,),
        in_specs=[pl.BlockSpec((1, m, D), lambda i: (i, 0, 0)),
                  _const_spec((1, D)),
                  _const_spec((D, 1024))],
        out_specs=[ospec, ospec],
        out_shape=[out, out],
        compiler_params=_cparams(1),
        name="memkv",
    )(mem, g, w)


def _mem_kernel(q_ref, k_ref, v_ref, o_ref):
    scale = MEM_HEAD_DIM ** -0.5
    for h in range(MEM_HEADS):
        cols = slice(MEM_HEAD_DIM * h, MEM_HEAD_DIM * (h + 1))
        sc = lax.dot_general(q_ref[0, :, cols], k_ref[0, :, cols], NT_DIMS,
                             preferred_element_type=F32) * scale
        m = jnp.max(sc, axis=-1, keepdims=True)
        p = jnp.exp(sc - m)
        l = jnp.sum(p, axis=-1, keepdims=True)
        o = jnp.dot(p.astype(BF16), v_ref[0, :, cols], preferred_element_type=F32)
        o_ref[0, :, cols] = (o / l).astype(BF16)


def _mem_attn(q, k, v):
    b, s, _ = q.shape
    m = k.shape[1]
    tq = 1024
    return pl.pallas_call(
        _mem_kernel,
        grid=(b, s // tq),
        in_specs=[pl.BlockSpec((1, tq, 512), lambda bi, i: (bi, i, 0)),
                  pl.BlockSpec((1, m, 512), lambda bi, i: (bi, 0, 0)),
                  pl.BlockSpec((1, m, 512), lambda bi, i: (bi, 0, 0))],
        out_specs=pl.BlockSpec((1, tq, 512), lambda bi, i: (bi, i, 0)),
        out_shape=jax.ShapeDtypeStruct((b, s, 512), BF16),
        compiler_params=_cparams(2),
        name="mem_attn",
    )(q, k, v)


def _merge_kernel(x_ref, g_ref, yna_ref, ys5_ref, ymem_ref, wg_ref, bg_ref, wb_ref, wo_ref, o_ref):
    x = x_ref[...]
    h = _rms(x, g_ref[...]).astype(BF16)
    merged = None
    y_na = jnp.concatenate([yna_ref[s] for s in range(4)], axis=1)
    ys = (y_na, ys5_ref[...].astype(BF16), ymem_ref[...])
    for b, y in enumerate(ys):
        cols = slice(D * b, D * (b + 1))
        gate = jax.nn.sigmoid(jnp.dot(h, wg_ref[:, cols].astype(BF16), preferred_element_type=F32)
                              + bg_ref[:, cols])
        up = jnp.dot(y, wb_ref[b], preferred_element_type=F32)
        merged = gate * up if merged is None else merged + gate * up
    o_ref[...] = x + jnp.dot(merged.astype(BF16), wo_ref[...], preferred_element_type=F32)


def _merge(x2d, g, yna, ys5, ymem, wg, bg, wb, wo):
    n = x2d.shape[0]
    tm = 1024
    yspec = pl.BlockSpec((tm, 512), lambda i: (i, 0))
    return pl.pallas_call(
        _merge_kernel,
        grid=(n // tm,),
        in_specs=[pl.BlockSpec((tm, D), lambda i: (i, 0)),
                  _const_spec((1, D)),
                  pl.BlockSpec((4, tm, 128), lambda i: (0, i, 0)), yspec, yspec,
                  _const_spec((D, 3 * D)),
                  _const_spec((1, 3 * D)),
                  _const_spec((3, 512, D)),
                  _const_spec((D, D))],
        out_specs=pl.BlockSpec((tm, D), lambda i: (i, 0)),
        out_shape=jax.ShapeDtypeStruct((n, D), F32),
        compiler_params=_cparams(1),
        name="merge",
    )(x2d, g, yna, ys5, ymem, wg, bg, wb, wo)


def _ffn_kernel(x_ref, g_ref, gf_ref, w1_ref, w3_ref, w2_ref, o_ref):
    x = x_ref[...]
    h = _rms(x, g_ref[...]).astype(BF16)
    a = jnp.dot(h, w1_ref[...], preferred_element_type=F32)
    c = jnp.dot(h, w3_ref[...], preferred_element_type=F32)
    mid = (a * jax.nn.sigmoid(a) * c).astype(BF16)
    x2 = x + jnp.dot(mid, w2_ref[...].astype(BF16), preferred_element_type=F32)
    o_ref[...] = _rms(x2, gf_ref[...])


def _ffn(x2d, g, gf, w1, w3, w2):
    n = x2d.shape[0]
    tm = 512
    return pl.pallas_call(
        _ffn_kernel,
        grid=(n // tm,),
        in_specs=[pl.BlockSpec((tm, D), lambda i: (i, 0)),
                  _const_spec((1, D)),
                  _const_spec((1, D)),
                  _const_spec((D, D_FF)),
                  _const_spec((D, D_FF)),
                  _const_spec((D_FF, D))],
        out_specs=pl.BlockSpec((tm, D), lambda i: (i, 0)),
        out_shape=jax.ShapeDtypeStruct((n, D), F32),
        compiler_params=_cparams(1),
        name="ffn",
    )(x2d, g, gf, w1, w3, w2)


def _s5_tables(a_re, a_im, log_dt, b_re, b_im, c_re, c_im, s5_d):
    t = CHUNK
    ar, ai = a_re.astype(F32), a_im.astype(F32)
    dt = jnp.exp(log_dt.astype(F32))[..., None]
    lr, li = ar * dt, ai * dt
    mag = jnp.exp(lr)
    lbr, lbi = mag * jnp.cos(li), mag * jnp.sin(li)
    den = ar * ar + ai * ai
    rr = ((lbr - 1.0) * ar + lbi * ai) / den
    ri = (lbi * ar - (lbr - 1.0) * ai) / den
    br, bi = b_re.astype(F32), b_im.astype(F32)
    bbr = rr[..., None] * br - ri[..., None] * bi
    bbi = rr[..., None] * bi + ri[..., None] * br
    cmr, cmi = c_re.astype(F32), c_im.astype(F32)
    tau = jnp.arange(t + 1, dtype=F32)[:, None, None, None]
    pmag = jnp.exp(tau * lr[None])
    pwr, pwi = pmag * jnp.cos(tau * li[None]), pmag * jnp.sin(tau * li[None])

    dd = jnp.pad(jnp.eye(S5_GROUP, dtype=F32)[None] * s5_d.astype(F32).reshape(S5_GROUPS, S5_GROUP, 1),
                 ((0, 0), (0, 0), (128 - S5_GROUP, 0)))

    def lanes4(f0, f1, b0, b1):
        return jnp.concatenate([f0, f1, b0, b1], axis=-1)

    cm = jnp.stack([lanes4(cmr[0], -cmi[0], cmr[1], -cmi[1]),
                    lanes4(-cmi[0], -cmr[0], -cmi[1], -cmr[1])], axis=1)
    fr, fi = jnp.moveaxis(pwr[1:t + 1, 0], 0, 1), jnp.moveaxis(pwi[1:t + 1, 0], 0, 1)
    gr, gi = jnp.moveaxis(pwr[1:t + 1][::-1, 1], 0, 1), jnp.moveaxis(pwi[1:t + 1][::-1, 1], 0, 1)
    bt = lambda z: jnp.swapaxes(z, -1, -2)
    bb = jnp.stack([lanes4(bt(bbr[0]), bt(bbi[0]), bt(bbr[1]), bt(bbi[1])),
                    lanes4(-bt(bbi[0]), bt(bbr[0]), -bt(bbi[1]), bt(bbr[1]))], axis=1)
    wr, wi = jnp.moveaxis(pwr[:t][::-1, 0], 0, 1), jnp.moveaxis(pwi[:t][::-1, 0], 0, 1)
    vr, vi = jnp.moveaxis(pwr[:t, 1], 0, 1), jnp.moveaxis(pwi[:t, 1], 0, 1)
    pw = jnp.stack([lanes4(fr, fr, gr, gr), lanes4(fi, fi, gi, gi),
                    lanes4(wr, wr, vr, vr), lanes4(wi, wi, vi, vi)], axis=1)

    dr, di = pwr[t], pwi[t]
    a_rows, b_rows = [], []
    for _ in range(N_SCAN_LEVELS):
        a_rows.append(lanes4(dr[0], dr[0], dr[1], dr[1]))
        b_rows.append(lanes4(-di[0], di[0], -di[1], di[1]))
        dr, di = dr * dr - di * di, 2.0 * dr * di
    pad = [jnp.zeros_like(a_rows[0])] * (8 - N_SCAN_LEVELS)
    dtab = jnp.stack(a_rows + pad + b_rows + pad, axis=1)
    return cm, bb, pw, dtab, dd


def kernel(x, mem, g_mix, g_mem, g_ffn, g_final, w_in, w_gate, b_gate, rpb, w_mem_kv,
           a_re, a_im, log_dt, b_re, b_im, c_re, c_im, s5_d, w_glu, w_branch, w_o,
           w_ffn1, w_ffn3, w_ffn2):
    bsz, s, d = x.shape
    n = bsz * s
    x2d = x.reshape(n, d)
    gm = g_mix[0].reshape(1, d).astype(F32)

    w_in_t = w_in[0].T.astype(BF16)
    cm_tab, bb_tab, pw_tab, dtab, dd_tab = _s5_tables(a_re[0], a_im[0], log_dt[0], b_re[0], b_im[0],
                                                      c_re[0], c_im[0], s5_d[0])

    k, qm, qw, vt = _proj(x2d, gm, w_in[0], w_in_t)

    ut = _proj_s5t(x.reshape(n // CHUNK, CHUNK, d), gm, w_in_t)
    zt = _s5_core(ut, cm_tab, bb_tab, pw_tab, dtab, dd_tab)
    y_s5 = _s5_glu(zt, w_glu[0].T.astype(BF16)).reshape(n, 512)

    y_na = _na(rpb[0].astype(F32), qw, k, vt, bsz, s)

    k_mem, v_mem = _memkv(mem, g_mem[0].reshape(1, d).astype(F32), w_mem_kv[0])
    y_mem = _mem_attn(qm.reshape(bsz, s, 512), k_mem, v_mem).reshape(n, 512)

    x1 = _merge(x2d, gm, y_na, y_s5, y_mem, w_gate[0],
                b_gate[0].reshape(1, 3 * d).astype(F32), w_branch[0].astype(BF16), w_o[0].astype(BF16))
    out = _ffn(x1, g_ffn[0].reshape(1, d).astype(F32), g_final.reshape(1, d).astype(F32),
               w_ffn1[0].astype(BF16), w_ffn3[0].astype(BF16), w_ffn2[0])
    return out.reshape(bsz, s, d)
```

```python
import functools

import jax
import jax.numpy as jnp
from jax import lax
from jax.experimental import pallas as pl
from jax.experimental.pallas import tpu as pltpu

F32 = jnp.float32
BF16 = jnp.bfloat16

D = 1024
GRID_W = 64
NA_HEADS = 8
NA_KH = 8
NA_KW = 16
S5_GROUPS = 32
S5_GROUP = 16
S5_STATE = 64
CHUNK = 64
MEM_HEADS = 4
MEM_HEAD_DIM = 128
D_FF = 2816
EPS = 1e-6
NEG_INF = -1e30

VMEM_LIMIT = 56 * 1024 * 1024

NT_DIMS = (((1,), (1,)), ((), ()))


def _cparams(n_axes):
    return pltpu.CompilerParams(
        dimension_semantics=("arbitrary",) * n_axes,
        vmem_limit_bytes=VMEM_LIMIT)


def _rms(x, g):
    return x * lax.rsqrt(jnp.mean(x * x, axis=-1, keepdims=True) + EPS) * g


def _const_spec(shape):
    nd = len(shape)
    return pl.BlockSpec(shape, lambda *_: (0,) * nd, pipeline_mode=pl.Buffered(1))


TOK_BLK = 128


LOG2E = 1.4426950408889634
NA_Q_SCALE = 64 ** -0.5 * LOG2E


def _proj_kernel(x_ref, g_ref, wk_ref, wqm_ref, wqt_ref, wvt_ref, k_ref, qm_ref, qw_ref, vt_ref):
    h = _rms(x_ref[...], g_ref[...]).astype(BF16)
    kk = jnp.dot(h, wk_ref[...].astype(BF16), preferred_element_type=F32).astype(BF16)
    for s in range(4):
        k_ref[s] = kk[:, 128 * s:128 * (s + 1)]
    qm_ref[...] = jnp.dot(h, wqm_ref[...].astype(BF16), preferred_element_type=F32).astype(BF16)
    qt = lax.dot_general(wqt_ref[...], h, NT_DIMS, preferred_element_type=F32) * NA_Q_SCALE
    vt = lax.dot_general(wvt_ref[...], h, NT_DIMS, preferred_element_type=F32)
    lo = lax.broadcasted_iota(jnp.int32, (128, 128), 1) < 64
    same_head = (lax.broadcasted_iota(jnp.int32, (128, 128), 0) < 64) == lo
    for i in range(x_ref.shape[0] // TOK_BLK):
        toks = slice(TOK_BLK * i, TOK_BLK * (i + 1))
        vt_ref[i] = vt[:, toks].astype(BF16)
        for s in range(4):
            a = qt[128 * s:128 * (s + 1), toks]
            ar = pltpu.roll(a, 64, axis=1)
            qw_ref[i, 128 * s:128 * (s + 1), :] = jnp.concatenate(
                [jnp.where(same_head, jnp.where(lo, a, ar), 0.0),
                 jnp.where(same_head, jnp.where(lo, ar, a), 0.0)], axis=1).astype(BF16)


W_IN_BLK = 512


def _w_in_spec(blk, transposed):
    if transposed:
        return pl.BlockSpec((W_IN_BLK, D), lambda *_: (blk, 0), pipeline_mode=pl.Buffered(1))
    return pl.BlockSpec((D, W_IN_BLK), lambda *_: (0, blk), pipeline_mode=pl.Buffered(1))


def _proj(x2d, g, w, wt):
    n = x2d.shape[0]
    tm = 1024
    nat = jax.ShapeDtypeStruct((n, 512), BF16)
    nat_spec = pl.BlockSpec((tm, 512), lambda i: (i, 0))
    chm = lambda width: (jax.ShapeDtypeStruct((n // TOK_BLK, 512, width), BF16),
                         pl.BlockSpec((tm // TOK_BLK, 512, width), lambda i: (i, 0, 0)))
    (qw_shape, qw_spec), (vt_shape, vt_spec) = chm(2 * TOK_BLK), chm(TOK_BLK)
    return pl.pallas_call(
        _proj_kernel,
        grid=(n // tm,),
        in_specs=[pl.BlockSpec((tm, D), lambda i: (i, 0)),
                  _const_spec((1, D)),
                  _w_in_spec(1, False),
                  _w_in_spec(4, False),
                  _w_in_spec(0, True),
                  _w_in_spec(2, True)],
        out_specs=[pl.BlockSpec((4, tm, 128), lambda i: (0, i, 0)), nat_spec, qw_spec, vt_spec],
        out_shape=[jax.ShapeDtypeStruct((4, n, 128), BF16), nat, qw_shape, vt_shape],
        compiler_params=_cparams(1),
        name="proj",
    )(x2d, g, w, w, wt, wt)


S5T_DT = 8


def _proj_s5t_kernel(x_ref, g_ref, wt_ref, o_ref, h_s):
    wt = wt_ref[...]
    nrows = x_ref.shape[0]
    hn = _rms(x_ref[...], g_ref[...]).reshape(nrows * S5T_DT, D)
    for c in range(D // 128):
        h_s[c] = hn[:, 128 * c:128 * (c + 1)]
    for j in range(S5T_DT):
        h = jnp.concatenate([h_s[c, pl.ds(j, nrows, stride=S5T_DT), :] for c in range(D // 128)],
                            axis=1).astype(BF16)
        ut = lax.dot_general(wt, h, NT_DIMS, preferred_element_type=F32)
        o_ref[j] = ut.astype(BF16)


def _proj_s5t(xc, g, wt):
    nrows = xc.shape[0]
    return pl.pallas_call(
        _proj_s5t_kernel,
        grid=(CHUNK // S5T_DT,),
        in_specs=[pl.BlockSpec((nrows, S5T_DT, D), lambda i: (0, i, 0)),
                  _const_spec((1, D)),
                  _w_in_spec(3, True)],
        out_specs=pl.BlockSpec((S5T_DT, 512, nrows), lambda i: (i, 0, 0)),
        out_shape=jax.ShapeDtypeStruct((CHUNK, 512, nrows), BF16),
        scratch_shapes=[pltpu.VMEM((D // 128, nrows * S5T_DT, 128), F32)],
        compiler_params=_cparams(1),
        name="proj_s5t",
    )(xc, g, wt)


N_SCAN_LEVELS = 6


def _gelu_tanh(x):
    c = 0.7978845608028654
    return 0.5 * x * (1.0 + jnp.tanh(c * (x + 0.044715 * (x * x * x))))


def _split_bf16(x):
    hi = x.astype(BF16)
    return hi, (x - hi.astype(F32)).astype(BF16)


def _nt_f32(a, b):
    ah, al = _split_bf16(a)
    bh, bl = _split_bf16(b)
    nt = lambda u, w: lax.dot_general(u, w, NT_DIMS, preferred_element_type=F32)
    return nt(ah, bh) + nt(ah, bl) + nt(al, bh)


S5_GROUPS_PER_STEP = 4
S5_SCRATCH_PER_GROUP = 5


def _s5_kernel(*refs):
    gs = S5_GROUPS_PER_STEP
    ut_ref, (cm_ref, bb_ref, pw_ref, d_ref, dd_ref) = refs[:gs], refs[gs:gs + 5]
    z_ref, scratch = refs[gs + 5], refs[gs + 6:]
    groups = [scratch[S5_SCRATCH_PER_GROUP * gi:S5_SCRATCH_PER_GROUP * (gi + 1)]
              for gi in range(S5_GROUPS_PER_STEP)]
    for gi, (m_s, n_s, pt_s, ptf_s, rs_s) in enumerate(groups):
        _s5_operators(gi, cm_ref, bb_ref, pw_ref, dd_ref, m_s, n_s, pt_s, ptf_s, rs_s)
    yv = [_s5_chunk_matmuls(gi, ut_ref, m_s, pt_s) for gi, (m_s, _, pt_s, _, _) in enumerate(groups)]
    for gi, (_, n_s, _, _, _) in enumerate(groups):
        _s5_finish(gi, d_ref, z_ref, n_s, *yv[gi])


def _s5_operators(gi, cm_ref, bb_ref, pw_ref, dd_ref, m_s, n_s, pt_s, ptf_s, rs_s):
    tc = CHUNK * S5_GROUP
    cma, cmb = cm_ref[gi,0], cm_ref[gi,1]
    bba, bbb = bb_ref[gi,0], bb_ref[gi,1]
    for t in range(CHUNK):
        rows = slice(S5_GROUP * t, S5_GROUP * (t + 1))
        n_s[rows, :] = (cma * pw_ref[gi,0, t:t + 1, :] + cmb * pw_ref[gi,1, t:t + 1, :]).astype(BF16)
        ptf_s[rows, :] = bba * pw_ref[gi,2, t:t + 1, :] + bbb * pw_ref[gi,3, t:t + 1, :]
    ptf = ptf_s[...]
    pt_s[...] = ptf.astype(BF16)

    ra = _nt_f32(cma[:, 0:128], ptf[:, 0:128])
    rb = _nt_f32(cma[:, 128:256], ptf[:, 128:256])
    zeros = jnp.zeros((S5_GROUP, tc), F32)
    r = (jnp.concatenate([ra, zeros], axis=1)
         + pltpu.roll(jnp.concatenate([rb, zeros], axis=1), (CHUNK - 1) * S5_GROUP, axis=1)
         + jnp.concatenate([zeros[:, 0:tc - 128], dd_ref[gi], zeros], axis=1))
    for k in range(8):
        rk = r if k == 0 else pltpu.roll(r, 2 * tc - S5_GROUP * k, axis=1)
        rs_s[k] = rk[:, 0:RS_LANES].astype(BF16)
    for t in range(CHUNK):
        a, k = divmod(CHUNK - 1 - t, 8)
        m_s[S5_GROUP * t:S5_GROUP * (t + 1), :] = rs_s[k, :, 128 * a:128 * a + tc]


def _s5_chunk_matmuls(gi, ut_ref, m_s, pt_s):
    ut = ut_ref[gi][...].reshape(CHUNK * S5_GROUP, -1)
    nrows = ut.shape[1]
    eye = jnp.where(lax.broadcasted_iota(jnp.int32, (nrows, nrows), 0)
                    == lax.broadcasted_iota(jnp.int32, (nrows, nrows), 1), 1.0, 0.0).astype(BF16)
    u_rows = lax.dot_general(eye, ut, NT_DIMS, preferred_element_type=F32).astype(BF16)
    y = jnp.dot(m_s[...], ut, preferred_element_type=F32)
    v = jnp.dot(u_rows, pt_s[...], preferred_element_type=F32)
    return y, v


def _s5_finish(gi, d_ref, z_ref, n_s, y, v):
    nrows = v.shape[0]
    pos = lax.broadcasted_iota(jnp.int32, (nrows, 128), 0) & (CHUNK - 1)

    def shift(x, s, up):
        if up:
            return jnp.where(pos < CHUNK - s, pltpu.roll(x, nrows - s, axis=0), 0.0)
        return jnp.where(pos >= s, pltpu.roll(x, s, axis=0), 0.0)

    def scan(vh, lanes, up):
        x = shift(vh, 1, up)
        for lvl in range(N_SCAN_LEVELS):
            xs = shift(x, 1 << lvl, up)
            a = d_ref[gi,lvl:lvl + 1, lanes]
            b = d_ref[gi,8 + lvl:9 + lvl, lanes]
            x = x + a * xs + b * pltpu.roll(xs, 64, axis=1)
        return x

    xf = scan(v[:, 0:128], slice(0, 128), False)
    xb = scan(v[:, 128:256], slice(128, 256), True)
    xin = jnp.concatenate([xf, xb], axis=1).astype(BF16)
    y = y + lax.dot_general(n_s[...], xin, NT_DIMS, preferred_element_type=F32)
    z_ref[:, S5_GROUP * gi:S5_GROUP * (gi + 1), :] = _gelu_tanh(y).astype(BF16).reshape(CHUNK, S5_GROUP, -1)


RS_LANES = 128 * 7 + CHUNK * S5_GROUP


def _s5_core(ut, cm, bb, pw, dtab, dd):
    nrows = ut.shape[2]
    tc = CHUNK * S5_GROUP
    gs = S5_GROUPS_PER_STEP
    group_specs = [pl.BlockSpec((CHUNK, S5_GROUP, nrows), functools.partial(lambda i, g: (0, gs * g + i, 0), i))
                   for i in range(gs)]
    return pl.pallas_call(
        _s5_kernel,
        grid=(S5_GROUPS // gs,),
        in_specs=[*group_specs,
                  pl.BlockSpec((gs, 2, S5_GROUP, 256), lambda g: (g, 0, 0, 0)),
                  pl.BlockSpec((gs, 2, S5_GROUP, 256), lambda g: (g, 0, 0, 0)),
                  pl.BlockSpec((gs, 4, CHUNK, 256), lambda g: (g, 0, 0, 0)),
                  pl.BlockSpec((gs, 16, 256), lambda g: (g, 0, 0)),
                  pl.BlockSpec((gs, S5_GROUP, 128), lambda g: (g, 0, 0))],
        out_specs=pl.BlockSpec((CHUNK, gs * S5_GROUP, nrows), lambda g: (0, g, 0)),
        out_shape=jax.ShapeDtypeStruct(ut.shape, BF16),
        scratch_shapes=[pltpu.VMEM((tc, tc), BF16),
                        pltpu.VMEM((tc, 256), BF16),
                        pltpu.VMEM((tc, 256), BF16),
                        pltpu.VMEM((tc, 256), F32),
                        pltpu.VMEM((8, S5_GROUP, RS_LANES), BF16),
                        ] * gs,
        compiler_params=_cparams(1),
        name="s5_core",
    )(*([ut] * gs), cm, bb, pw, dtab, dd)


def _s5_glu_kernel(z_ref, wt_ref, o_ref):
    wt = wt_ref[...]
    nrows = z_ref.shape[2]
    eye = jnp.where(lax.broadcasted_iota(jnp.int32, (nrows, nrows), 0)
                    == lax.broadcasted_iota(jnp.int32, (nrows, nrows), 1), 1.0, 0.0).astype(BF16)
    for j in range(S5T_DT):
        zt = z_ref[j]
        gl = jnp.dot(wt, zt, preferred_element_type=F32)
        o = (zt.astype(F32) * jax.nn.sigmoid(gl)).astype(BF16)
        nat = lax.dot_general(eye, o, NT_DIMS, preferred_element_type=F32)
        o_ref[:, j, :] = nat


def _s5_glu(zt, wglu_t):
    nrows = zt.shape[2]
    return pl.pallas_call(
        _s5_glu_kernel,
        grid=(CHUNK // S5T_DT,),
        in_specs=[pl.BlockSpec((S5T_DT, 512, nrows), lambda i: (i, 0, 0)),
                  _const_spec((512, 512))],
        out_specs=pl.BlockSpec((nrows, S5T_DT, 512), lambda i: (0, i, 0)),
        out_shape=jax.ShapeDtypeStruct((nrows, CHUNK, 512), F32),
        compiler_params=_cparams(1),
        name="s5_glu",
    )(zt, wglu_t)


NA_ROWS_PER_STEP = 32
NA_PAIRS_PER_STEP = NA_ROWS_PER_STEP // 2
NA_WIN_ROWS = NA_KH + 1
NA_WIN_KEYS = NA_WIN_ROWS * GRID_W
NA_WIN_BLKS = -(-NA_WIN_KEYS // TOK_BLK)
NA_PV_KEYS = NA_WIN_BLKS * TOK_BLK
NA_RR_OUTSIDE = 2 * NA_KH - 1


def _na_kernel(rpb_ref, qw_ref, k_ref, vt_ref, oob_ref, o_ref, *scratch):
    st_s, pt_s, bias_s = (scratch[0:4], scratch[4:8]), (scratch[8:12], scratch[12:16]), scratch[16]
    rb = pl.program_id(1)
    lo_q = lax.broadcasted_iota(jnp.int32, (GRID_W, 128), 1) < 64
    same_head2 = ((lax.broadcasted_iota(jnp.int32, (128, 256), 0) < 64)
                  == ((lax.broadcasted_iota(jnp.int32, (128, 256), 1) & 64) == 0))
    esum = jnp.where(lax.broadcasted_iota(jnp.int32, (GRID_W, 128), 0)
                     == (lax.broadcasted_iota(jnp.int32, (GRID_W, 128), 1) & (GRID_W - 1)),
                     1.0, 0.0).astype(BF16)
    half = NA_KH // 2

    kcol = lax.broadcasted_iota(jnp.int32, (GRID_W, 128), 0)
    qcol = lax.broadcasted_iota(jnp.int32, (GRID_W, 128), 1) & (GRID_W - 1)
    qstart = jnp.clip(qcol - NA_KW // 2, 0, GRID_W - NA_KW)
    col_ok = (kcol >= qstart) & (kcol < qstart + NA_KW)

    @pl.when((pl.program_id(0) == 0) & (rb == 0))
    def _build_bias():
        rel = jnp.clip(kcol - qcol + (NA_KW - 1), 0, 2 * NA_KW - 2)

        def rr_body(rr, carry):
            for s in range(4):
                acc = jnp.zeros((GRID_W, 128), F32)
                for j in range(2 * NA_KW - 1):
                    acc = jnp.where(rel == j, jnp.where(lo_q, rpb_ref[2 * s, rr, j], rpb_ref[2 * s + 1, rr, j]), acc)
                bias_s[s, rr] = acc * LOG2E
            return carry

        lax.fori_loop(0, NA_RR_OUTSIDE, rr_body, 0)
        for s in range(4):
            bias_s[s, NA_RR_OUTSIDE] = oob_ref[...]

    def window(pi):
        r0 = rb * NA_ROWS_PER_STEP + 2 * pi
        return r0, jnp.clip(r0 - half, 0, GRID_W - NA_WIN_ROWS)

    def q_stage(pi):
        _, win = window(pi)
        koff = pl.multiple_of(win * GRID_W, GRID_W)
        scs = []
        for s in range(4):
            ch = slice(128 * s, 128 * (s + 1))
            kw = k_ref[s, pl.ds(koff, NA_WIN_KEYS), :]
            scs.append(jnp.dot(kw, qw_ref[pi, ch, :], preferred_element_type=F32))
        return scs

    def s_stage(pi, par, scs):
        r0, win = window(pi)
        poff = pl.multiple_of((win & 1) * GRID_W, GRID_W)
        rr = []
        for p in range(2):
            r = r0 + p
            first = jnp.clip(r - half, 0, GRID_W - NA_KH) - win
            rel = win - r + (NA_KH - 1)
            rr.append([jnp.where((wr >= first) & (wr < first + NA_KH), wr + rel, NA_RR_OUTSIDE)
                       for wr in range(NA_WIN_ROWS)])
        sums = []
        for s in range(4):
            st = st_s[par][s]
            slab_sums = []
            for p in range(2):
                lanes = slice(128 * p, 128 * (p + 1))
                for edge in (0, NA_PV_KEYS - GRID_W):
                    pt_s[par][s][p, edge:edge + GRID_W, :] = jnp.zeros((GRID_W, 128), BF16)
                m = None
                for wr in range(NA_WIN_ROWS):
                    rows = slice(GRID_W * wr, GRID_W * (wr + 1))
                    t = jnp.where(col_ok, scs[s][rows, lanes] + bias_s[s, rr[p][wr]], NEG_INF)
                    st[p, rows, :] = t
                    m = t if m is None else jnp.maximum(m, t)
                m = jnp.max(m, axis=0, keepdims=True)
                l = None
                for wr in range(NA_WIN_ROWS):
                    rows = slice(GRID_W * wr, GRID_W * (wr + 1))
                    e = jnp.exp2(st[p, rows, :] - m)
                    pt_s[par][s][p, pl.ds(poff + GRID_W * wr, GRID_W), :] = e.astype(BF16)
                    l = e if l is None else l + e
                slab_sums.append(jnp.sum(l, axis=0, keepdims=True))
            sums.append(jnp.concatenate(slab_sums, axis=1))
        return tuple(sums)

    def o_stage_pv(pi, par, sums):
        _, win = window(pi)
        blk0 = lax.shift_right_logical(win, 1)
        ots = []
        for s in range(4):
            ch = slice(128 * s, 128 * (s + 1))
            vw = jnp.concatenate([vt_ref[blk0 + i, ch, :] for i in range(NA_WIN_BLKS)], axis=1)
            pt = jnp.concatenate([pt_s[par][s][0], pt_s[par][s][1]], axis=1)
            ot = jnp.dot(vw, pt, preferred_element_type=F32)
            ots.append(jnp.where(same_head2, (ot / sums[s]).astype(BF16), jnp.zeros((), BF16)))
        return ots

    def o_stage_store(pi, ots):
        for s in range(4):
            ch = slice(128 * s, 128 * (s + 1))
            for p in range(2):
                nat = lax.dot_general(esum, ots[s][:, 128 * p:128 * (p + 1)], NT_DIMS,
                                      preferred_element_type=F32)
                qoff = pl.multiple_of((2 * pi + p) * GRID_W, GRID_W)
                o_ref[s, pl.ds(qoff, GRID_W), :] = nat.astype(BF16)

    def step(j, prev_sums, has_prev=True):
        a, b = 2 * j, 2 * j + 1
        scs_a = q_stage(a)
        if has_prev:
            ots_a = o_stage_pv(a - 2, 0, prev_sums[0])
            ots_b = o_stage_pv(b - 2, 1, prev_sums[1])
        scs_b = q_stage(b)
        if has_prev:
            o_stage_store(a - 2, ots_a)
            o_stage_store(b - 2, ots_b)
        return s_stage(a, 0, scs_a), s_stage(b, 1, scs_b)

    sums = step(0, None, has_prev=False)
    sums = lax.fori_loop(1, NA_PAIRS_PER_STEP // 2, step, sums)
    for u in range(2):
        pi = NA_PAIRS_PER_STEP - 2 + u
        o_stage_store(pi, o_stage_pv(pi, u, sums[u]))


def _na(rpb, qw, k, vt, b, s):
    tq = NA_ROWS_PER_STEP * GRID_W
    blks = s // TOK_BLK
    steps = s // tq
    oob = jnp.full((GRID_W, 128), -jnp.inf, F32)
    return pl.pallas_call(
        _na_kernel,
        grid=(b, s // tq),
        in_specs=[pl.BlockSpec(memory_space=pltpu.SMEM),
                  pl.BlockSpec((NA_PAIRS_PER_STEP, 512, 2 * TOK_BLK),
                               lambda bi, ri: (bi * (blks // NA_PAIRS_PER_STEP) + ri, 0, 0)),
                  pl.BlockSpec((4, s, 128), lambda bi, ri: (0, bi, 0)),
                  pl.BlockSpec((blks, 512, TOK_BLK), lambda bi, ri: (bi, 0, 0)),
                  _const_spec(oob.shape)],
        out_specs=pl.BlockSpec((4, tq, 128), lambda bi, ri: (0, bi * steps + ri, 0)),
        out_shape=jax.ShapeDtypeStruct((4, b * s, 128), BF16),
        scratch_shapes=([pltpu.VMEM((2, NA_WIN_KEYS, 128), F32)] * 8
                        + [pltpu.VMEM((2, NA_PV_KEYS, 128), BF16)] * 8
                        + [pltpu.VMEM((NA_HEADS // 2, 2 * NA_KH, GRID_W, 128), F32)]),
        compiler_params=_cparams(2),
        name="na",
    )(rpb, qw, k, vt, oob)


def _memkv_kernel(mem_ref, g_ref, w_ref, k_ref, v_ref):
    mn = _rms(mem_ref[0], g_ref[...]).astype(BF16)
    kv = jnp.dot(mn, w_ref[...].astype(BF16), preferred_element_type=F32)
    k_ref[0] = kv[:, 0:512].astype(BF16)
    v_ref[0] = kv[:, 512:1024].astype(BF16)


def _memkv(mem, g, w):
    b, m, _ = mem.shape
    out = jax.ShapeDtypeStruct((b, m, 512), BF16)
    ospec = pl.BlockSpec((1, m, 512), lambda i: (i, 0, 0))
    return pl.pallas_call(
        _memkv_kernel,
        grid=(b,),
        in_specs=[pl.BlockSpec((1, m, D), lambda i: (i, 0, 0)),
                  _const_spec((1, D)),
                  _const_spec((D, 1024))],
        out_specs=[ospec, ospec],
        out_shape=[out, out],
        compiler_params=_cparams(1),
        name="memkv",
    )(mem, g, w)


def _mem_kernel(q_ref, k_ref, v_ref, o_ref):
    scale = MEM_HEAD_DIM ** -0.5
    for h in range(MEM_HEADS):
        cols = slice(MEM_HEAD_DIM * h, MEM_HEAD_DIM * (h + 1))
        sc = lax.dot_general(q_ref[0, :, cols], k_ref[0, :, cols], NT_DIMS,
                             preferred_element_type=F32) * scale
        m = jnp.max(sc, axis=-1, keepdims=True)
        p = jnp.exp(sc - m)
        l = jnp.sum(p, axis=-1, keepdims=True)
        o = jnp.dot(p.astype(BF16), v_ref[0, :, cols], preferred_element_type=F32)
        o_ref[0, :, cols] = (o / l).astype(BF16)


def _mem_attn(q, k, v):
    b, s, _ = q.shape
    m = k.shape[1]
    tq = 1024
    return pl.pallas_call(
        _mem_kernel,
        grid=(b, s // tq),
        in_specs=[pl.BlockSpec((1, tq, 512), lambda bi, i: (bi, i, 0)),
                  pl.BlockSpec((1, m, 512), lambda bi, i: (bi, 0, 0)),
                  pl.BlockSpec((1, m, 512), lambda bi, i: (bi, 0, 0))],
        out_specs=pl.BlockSpec((1, tq, 512), lambda bi, i: (bi, i, 0)),
        out_shape=jax.ShapeDtypeStruct((b, s, 512), BF16),
        compiler_params=_cparams(2),
        name="mem_attn",
    )(q, k, v)


def _merge_kernel(x_ref, g_ref, yna_ref, ys5_ref, ymem_ref, wg_ref, bg_ref, wb_ref, wo_ref, o_ref):
    x = x_ref[...]
    h = _rms(x, g_ref[...]).astype(BF16)
    merged = None
    y_na = jnp.concatenate([yna_ref[s] for s in range(4)], axis=1)
    ys = (y_na, ys5_ref[...].astype(BF16), ymem_ref[...])
    for b, y in enumerate(ys):
        cols = slice(D * b, D * (b + 1))
        gate = jax.nn.sigmoid(jnp.dot(h, wg_ref[:, cols].astype(BF16), preferred_element_type=F32)
                              + bg_ref[:, cols])
        up = jnp.dot(y, wb_ref[b], preferred_element_type=F32)
        merged = gate * up if merged is None else merged + gate * up
    o_ref[...] = x + jnp.dot(merged.astype(BF16), wo_ref[...], preferred_element_type=F32)


def _merge(x2d, g, yna, ys5, ymem, wg, bg, wb, wo):
    n = x2d.shape[0]
    tm = 1024
    yspec = pl.BlockSpec((tm, 512), lambda i: (i, 0))
    return pl.pallas_call(
        _merge_kernel,
        grid=(n // tm,),
        in_specs=[pl.BlockSpec((tm, D), lambda i: (i, 0)),
                  _const_spec((1, D)),
                  pl.BlockSpec((4, tm, 128), lambda i: (0, i, 0)), yspec, yspec,
                  _const_spec((D, 3 * D)),
                  _const_spec((1, 3 * D)),
                  _const_spec((3, 512, D)),
                  _const_spec((D, D))],
        out_specs=pl.BlockSpec((tm, D), lambda i: (i, 0)),
        out_shape=jax.ShapeDtypeStruct((n, D), F32),
        compiler_params=_cparams(1),
        name="merge",
    )(x2d, g, yna, ys5, ymem, wg, bg, wb, wo)


def _ffn_kernel(x_ref, g_ref, gf_ref, w1_ref, w3_ref, w2_ref, o_ref):
    x = x_ref[...]
    h = _rms(x, g_ref[...]).astype(BF16)
    a = jnp.dot(h, w1_ref[...], preferred_element_type=F32)
    c = jnp.dot(h, w3_ref[...], preferred_element_type=F32)
    mid = (a * jax.nn.sigmoid(a) * c).astype(BF16)
    x2 = x + jnp.dot(mid, w2_ref[...].astype(BF16), preferred_element_type=F32)
    o_ref[...] = _rms(x2, gf_ref[...])


def _ffn(x2d, g, gf, w1, w3, w2):
    n = x2d.shape[0]
    tm = 512
    return pl.pallas_call(
        _ffn_kernel,
        grid=(n // tm,),
        in_specs=[pl.BlockSpec((tm, D), lambda i: (i, 0)),
                  _const_spec((1, D)),
                  _const_spec((1, D)),
                  _const_spec((D, D_FF)),
                  _const_spec((D, D_FF)),
                  _const_spec((D_FF, D))],
        out_specs=pl.BlockSpec((tm, D), lambda i: (i, 0)),
        out_shape=jax.ShapeDtypeStruct((n, D), F32),
        compiler_params=_cparams(1),
        name="ffn",
    )(x2d, g, gf, w1, w3, w2)


def _s5_tables(a_re, a_im, log_dt, b_re, b_im, c_re, c_im, s5_d):
    t = CHUNK
    ar, ai = a_re.astype(F32), a_im.astype(F32)
    dt = jnp.exp(log_dt.astype(F32))[..., None]
    lr, li = ar * dt, ai * dt
    mag = jnp.exp(lr)
    lbr, lbi = mag * jnp.cos(li), mag * jnp.sin(li)
    den = ar * ar + ai * ai
    rr = ((lbr - 1.0) * ar + lbi * ai) / den
    ri = (lbi * ar - (lbr - 1.0) * ai) / den
    br, bi = b_re.astype(F32), b_im.astype(F32)
    bbr = rr[..., None] * br - ri[..., None] * bi
    bbi = rr[..., None] * bi + ri[..., None] * br
    cmr, cmi = c_re.astype(F32), c_im.astype(F32)
    tau = jnp.arange(t + 1, dtype=F32)[:, None, None, None]
    pmag = jnp.exp(tau * lr[None])
    pwr, pwi = pmag * jnp.cos(tau * li[None]), pmag * jnp.sin(tau * li[None])

    dd = jnp.pad(jnp.eye(S5_GROUP, dtype=F32)[None] * s5_d.astype(F32).reshape(S5_GROUPS, S5_GROUP, 1),
                 ((0, 0), (0, 0), (128 - S5_GROUP, 0)))

    def lanes4(f0, f1, b0, b1):
        return jnp.concatenate([f0, f1, b0, b1], axis=-1)

    cm = jnp.stack([lanes4(cmr[0], -cmi[0], cmr[1], -cmi[1]),
                    lanes4(-cmi[0], -cmr[0], -cmi[1], -cmr[1])], axis=1)
    fr, fi = jnp.moveaxis(pwr[1:t + 1, 0], 0, 1), jnp.moveaxis(pwi[1:t + 1, 0], 0, 1)
    gr, gi = jnp.moveaxis(pwr[1:t + 1][::-1, 1], 0, 1), jnp.moveaxis(pwi[1:t + 1][::-1, 1], 0, 1)
    bt = lambda z: jnp.swapaxes(z, -1, -2)
    bb = jnp.stack([lanes4(bt(bbr[0]), bt(bbi[0]), bt(bbr[1]), bt(bbi[1])),
                    lanes4(-bt(bbi[0]), bt(bbr[0]), -bt(bbi[1]), bt(bbr[1]))], axis=1)
    wr, wi = jnp.moveaxis(pwr[:t][::-1, 0], 0, 1), jnp.moveaxis(pwi[:t][::-1, 0], 0, 1)
    vr, vi = jnp.moveaxis(pwr[:t, 1], 0, 1), jnp.moveaxis(pwi[:t, 1], 0, 1)
    pw = jnp.stack([lanes4(fr, fr, gr, gr), lanes4(fi, fi, gi, gi),
                    lanes4(wr, wr, vr, vr), lanes4(wi, wi, vi, vi)], axis=1)

    dr, di = pwr[t], pwi[t]
    a_rows, b_rows = [], []
    for _ in range(N_SCAN_LEVELS):
        a_rows.append(lanes4(dr[0], dr[0], dr[1], dr[1]))
        b_rows.append(lanes4(-di[0], di[0], -di[1], di[1]))
        dr, di = dr * dr - di * di, 2.0 * dr * di
    pad = [jnp.zeros_like(a_rows[0])] * (8 - N_SCAN_LEVELS)
    dtab = jnp.stack(a_rows + pad + b_rows + pad, axis=1)
    return cm, bb, pw, dtab, dd


def kernel(x, mem, g_mix, g_mem, g_ffn, g_final, w_in, w_gate, b_gate, rpb, w_mem_kv,
           a_re, a_im, log_dt, b_re, b_im, c_re, c_im, s5_d, w_glu, w_branch, w_o,
           w_ffn1, w_ffn3, w_ffn2):
    bsz, s, d = x.shape
    n = bsz * s
    x2d = x.reshape(n, d)
    gm = g_mix[0].reshape(1, d).astype(F32)

    w_in_t = w_in[0].T.astype(BF16)
    cm_tab, bb_tab, pw_tab, dtab, dd_tab = _s5_tables(a_re[0], a_im[0], log_dt[0], b_re[0], b_im[0],
                                                      c_re[0], c_im[0], s5_d[0])

    k, qm, qw, vt = _proj(x2d, gm, w_in[0], w_in_t)

    ut = _proj_s5t(x.reshape(n // CHUNK, CHUNK, d), gm, w_in_t)
    zt = _s5_core(ut, cm_tab, bb_tab, pw_tab, dtab, dd_tab)
    y_s5 = _s5_glu(zt, w_glu[0].T.astype(BF16)).reshape(n, 512)

    y_na = _na(rpb[0].astype(F32), qw, k, vt, bsz, s)

    k_mem, v_mem = _memkv(mem, g_mem[0].reshape(1, d).astype(F32), w_mem_kv[0])
    y_mem = _mem_attn(qm.reshape(bsz, s, 512), k_mem, v_mem).reshape(n, 512)

    x1 = _merge(x2d, gm, y_na, y_s5, y_mem, w_gate[0],
                b_gate[0].reshape(1, 3 * d).astype(F32), w_branch[0].astype(BF16), w_o[0].astype(BF16))
    out = _ffn(x1, g_ffn[0].reshape(1, d).astype(F32), g_final.reshape(1, d).astype(F32),
               w_ffn1[0].astype(BF16), w_ffn3[0].astype(BF16), w_ffn2[0])
    return out.reshape(bsz, s, d)
```

```python
import functools

import jax
import jax.numpy as jnp
from jax import lax
from jax.experimental import pallas as pl
from jax.experimental.pallas import tpu as pltpu

F32 = jnp.float32
BF16 = jnp.bfloat16

D = 1024
GRID_W = 64
NA_HEADS = 8
NA_KH = 8
NA_KW = 16
S5_GROUPS = 32
S5_GROUP = 16
S5_STATE = 64
CHUNK = 64
MEM_HEADS = 4
MEM_HEAD_DIM = 128
D_FF = 2816
EPS = 1e-6
NEG_INF = -1e30

VMEM_LIMIT = 56 * 1024 * 1024

NT_DIMS = (((1,), (1,)), ((), ()))


def _cparams(n_axes):
    return pltpu.CompilerParams(
        dimension_semantics=("arbitrary",) * n_axes,
        vmem_limit_bytes=VMEM_LIMIT)


def _rms(x, g):
    return x * lax.rsqrt(jnp.mean(x * x, axis=-1, keepdims=True) + EPS) * g


def _const_spec(shape):
    nd = len(shape)
    return pl.BlockSpec(shape, lambda *_: (0,) * nd, pipeline_mode=pl.Buffered(1))


TOK_BLK = 128


LOG2E = 1.4426950408889634
NA_Q_SCALE = 64 ** -0.5 * LOG2E


def _proj_kernel(x_ref, g_ref, wk_ref, wqm_ref, wqt_ref, wvt_ref, k_ref, qm_ref, qw_ref, vt_ref):
    h = _rms(x_ref[...], g_ref[...]).astype(BF16)
    kk = jnp.dot(h, wk_ref[...].astype(BF16), preferred_element_type=F32).astype(BF16)
    for s in range(4):
        k_ref[s] = kk[:, 128 * s:128 * (s + 1)]
    qm_ref[...] = jnp.dot(h, wqm_ref[...].astype(BF16), preferred_element_type=F32).astype(BF16)
    qt = lax.dot_general(wqt_ref[...], h, NT_DIMS, preferred_element_type=F32) * NA_Q_SCALE
    vt = lax.dot_general(wvt_ref[...], h, NT_DIMS, preferred_element_type=F32)
    lo = lax.broadcasted_iota(jnp.int32, (128, 128), 1) < 64
    same_head = (lax.broadcasted_iota(jnp.int32, (128, 128), 0) < 64) == lo
    for i in range(x_ref.shape[0] // TOK_BLK):
        toks = slice(TOK_BLK * i, TOK_BLK * (i + 1))
        vt_ref[i] = vt[:, toks].astype(BF16)
        for s in range(4):
            a = qt[128 * s:128 * (s + 1), toks]
            ar = pltpu.roll(a, 64, axis=1)
            qw_ref[i, 128 * s:128 * (s + 1), :] = jnp.concatenate(
                [jnp.where(same_head, jnp.where(lo, a, ar), 0.0),
                 jnp.where(same_head, jnp.where(lo, ar, a), 0.0)], axis=1).astype(BF16)


W_IN_BLK = 512


def _w_in_spec(blk, transposed):
    if transposed:
        return pl.BlockSpec((W_IN_BLK, D), lambda *_: (blk, 0), pipeline_mode=pl.Buffered(1))
    return pl.BlockSpec((D, W_IN_BLK), lambda *_: (0, blk), pipeline_mode=pl.Buffered(1))


def _proj(x2d, g, w, wt):
    n = x2d.shape[0]
    tm = 1024
    nat = jax.ShapeDtypeStruct((n, 512), BF16)
    nat_spec = pl.BlockSpec((tm, 512), lambda i: (i, 0))
    chm = lambda width: (jax.ShapeDtypeStruct((n // TOK_BLK, 512, width), BF16),
                         pl.BlockSpec((tm // TOK_BLK, 512, width), lambda i: (i, 0, 0)))
    (qw_shape, qw_spec), (vt_shape, vt_spec) = chm(2 * TOK_BLK), chm(TOK_BLK)
    return pl.pallas_call(
        _proj_kernel,
        grid=(n // tm,),
        in_specs=[pl.BlockSpec((tm, D), lambda i: (i, 0)),
                  _const_spec((1, D)),
                  _w_in_spec(1, False),
                  _w_in_spec(4, False),
                  _w_in_spec(0, True),
                  _w_in_spec(2, True)],
        out_specs=[pl.BlockSpec((4, tm, 128), lambda i: (0, i, 0)), nat_spec, qw_spec, vt_spec],
        out_shape=[jax.ShapeDtypeStruct((4, n, 128), BF16), nat, qw_shape, vt_shape],
        compiler_params=_cparams(1),
        name="proj",
    )(x2d, g, w, w, wt, wt)


S5T_DT = 8


def _proj_s5t_kernel(x_ref, g_ref, wt_ref, o_ref, h_s):
    wt = wt_ref[...]
    nrows = x_ref.shape[0]
    hn = _rms(x_ref[...], g_ref[...]).reshape(nrows * S5T_DT, D)
    for c in range(D // 128):
        h_s[c] = hn[:, 128 * c:128 * (c + 1)]
    for j in range(S5T_DT):
        h = jnp.concatenate([h_s[c, pl.ds(j, nrows, stride=S5T_DT), :] for c in range(D // 128)],
                            axis=1).astype(BF16)
        ut = lax.dot_general(wt, h, NT_DIMS, preferred_element_type=F32)
        o_ref[j] = ut.astype(BF16)


def _proj_s5t(xc, g, wt):
    nrows = xc.shape[0]
    return pl.pallas_call(
        _proj_s5t_kernel,
        grid=(CHUNK // S5T_DT,),
        in_specs=[pl.BlockSpec((nrows, S5T_DT, D), lambda i: (0, i, 0)),
                  _const_spec((1, D)),
                  _w_in_spec(3, True)],
        out_specs=pl.BlockSpec((S5T_DT, 512, nrows), lambda i: (i, 0, 0)),
        out_shape=jax.ShapeDtypeStruct((CHUNK, 512, nrows), BF16),
        scratch_shapes=[pltpu.VMEM((D // 128, nrows * S5T_DT, 128), F32)],
        compiler_params=_cparams(1),
        name="proj_s5t",
    )(xc, g, wt)


N_SCAN_LEVELS = 6


def _gelu_tanh(x):
    c = 0.7978845608028654
    return 0.5 * x * (1.0 + jnp.tanh(c * (x + 0.044715 * (x * x * x))))


def _split_bf16(x):
    hi = x.astype(BF16)
    return hi, (x - hi.astype(F32)).astype(BF16)


def _nt_f32(a, b):
    ah, al = _split_bf16(a)
    bh, bl = _split_bf16(b)
    nt = lambda u, w: lax.dot_general(u, w, NT_DIMS, preferred_element_type=F32)
    return nt(ah, bh) + nt(ah, bl) + nt(al, bh)


S5_GROUPS_PER_STEP = 4
S5_SCRATCH_PER_GROUP = 5


def _s5_kernel(*refs):
    gs = S5_GROUPS_PER_STEP
    ut_ref, (cm_ref, bb_ref, pw_ref, d_ref, dd_ref) = refs[:gs], refs[gs:gs + 5]
    z_ref, scratch = refs[gs + 5], refs[gs + 6:]
    groups = [scratch[S5_SCRATCH_PER_GROUP * gi:S5_SCRATCH_PER_GROUP * (gi + 1)]
              for gi in range(S5_GROUPS_PER_STEP)]
    for gi, (m_s, n_s, pt_s, ptf_s, rs_s) in enumerate(groups):
        _s5_operators(gi, cm_ref, bb_ref, pw_ref, dd_ref, m_s, n_s, pt_s, ptf_s, rs_s)
    yv = [_s5_chunk_matmuls(gi, ut_ref, m_s, pt_s) for gi, (m_s, _, pt_s, _, _) in enumerate(groups)]
    for gi, (_, n_s, _, _, _) in enumerate(groups):
        _s5_finish(gi, d_ref, z_ref, n_s, *yv[gi])


def _s5_operators(gi, cm_ref, bb_ref, pw_ref, dd_ref, m_s, n_s, pt_s, ptf_s, rs_s):
    tc = CHUNK * S5_GROUP
    cma, cmb = cm_ref[gi,0], cm_ref[gi,1]
    bba, bbb = bb_ref[gi,0], bb_ref[gi,1]
    for t in range(CHUNK):
        rows = slice(S5_GROUP * t, S5_GROUP * (t + 1))
        n_s[rows, :] = (cma * pw_ref[gi,0, t:t + 1, :] + cmb * pw_ref[gi,1, t:t + 1, :]).astype(BF16)
        ptf_s[rows, :] = bba * pw_ref[gi,2, t:t + 1, :] + bbb * pw_ref[gi,3, t:t + 1, :]
    ptf = ptf_s[...]
    pt_s[...] = ptf.astype(BF16)

    ra = _nt_f32(cma[:, 0:128], ptf[:, 0:128])
    rb = _nt_f32(cma[:, 128:256], ptf[:, 128:256])
    zeros = jnp.zeros((S5_GROUP, tc), F32)
    r = (jnp.concatenate([ra, zeros], axis=1)
         + pltpu.roll(jnp.concatenate([rb, zeros], axis=1), (CHUNK - 1) * S5_GROUP, axis=1)
         + jnp.concatenate([zeros[:, 0:tc - 128], dd_ref[gi], zeros], axis=1))
    for k in range(8):
        rk = r if k == 0 else pltpu.roll(r, 2 * tc - S5_GROUP * k, axis=1)
        rs_s[k] = rk[:, 0:RS_LANES].astype(BF16)
    for t in range(CHUNK):
        a, k = divmod(CHUNK - 1 - t, 8)
        m_s[S5_GROUP * t:S5_GROUP * (t + 1), :] = rs_s[k, :, 128 * a:128 * a + tc]


def _s5_chunk_matmuls(gi, ut_ref, m_s, pt_s):
    ut = ut_ref[gi][...].reshape(CHUNK * S5_GROUP, -1)
    nrows = ut.shape[1]
    eye = jnp.where(lax.broadcasted_iota(jnp.int32, (nrows, nrows), 0)
                    == lax.broadcasted_iota(jnp.int32, (nrows, nrows), 1), 1.0, 0.0).astype(BF16)
    u_rows = lax.dot_general(eye, ut, NT_DIMS, preferred_element_type=F32).astype(BF16)
    y = jnp.dot(m_s[...], ut, preferred_element_type=F32)
    v = jnp.dot(u_rows, pt_s[...], preferred_element_type=F32)
    return y, v


def _s5_finish(gi, d_ref, z_ref, n_s, y, v):
    nrows = v.shape[0]
    pos = lax.broadcasted_iota(jnp.int32, (nrows, 128), 0) & (CHUNK - 1)

    def shift(x, s, up):
        if up:
            return jnp.where(pos < CHUNK - s, pltpu.roll(x, nrows - s, axis=0), 0.0)
        return jnp.where(pos >= s, pltpu.roll(x, s, axis=0), 0.0)

    def scan(vh, lanes, up):
        x = shift(vh, 1, up)
        for lvl in range(N_SCAN_LEVELS):
            xs = shift(x, 1 << lvl, up)
            a = d_ref[gi,lvl:lvl + 1, lanes]
            b = d_ref[gi,8 + lvl:9 + lvl, lanes]
            x = x + a * xs + b * pltpu.roll(xs, 64, axis=1)
        return x

    xf = scan(v[:, 0:128], slice(0, 128), False)
    xb = scan(v[:, 128:256], slice(128, 256), True)
    xin = jnp.concatenate([xf, xb], axis=1).astype(BF16)
    y = y + lax.dot_general(n_s[...], xin, NT_DIMS, preferred_element_type=F32)
    z_ref[:, S5_GROUP * gi:S5_GROUP * (gi + 1), :] = _gelu_tanh(y).astype(BF16).reshape(CHUNK, S5_GROUP, -1)


RS_LANES = 128 * 7 + CHUNK * S5_GROUP


def _s5_core(ut, cm, bb, pw, dtab, dd):
    nrows = ut.shape[2]
    tc = CHUNK * S5_GROUP
    gs = S5_GROUPS_PER_STEP
    group_specs = [pl.BlockSpec((CHUNK, S5_GROUP, nrows), functools.partial(lambda i, g: (0, gs * g + i, 0), i))
                   for i in range(gs)]
    return pl.pallas_call(
        _s5_kernel,
        grid=(S5_GROUPS // gs,),
        in_specs=[*group_specs,
                  pl.BlockSpec((gs, 2, S5_GROUP, 256), lambda g: (g, 0, 0, 0)),
                  pl.BlockSpec((gs, 2, S5_GROUP, 256), lambda g: (g, 0, 0, 0)),
                  pl.BlockSpec((gs, 4, CHUNK, 256), lambda g: (g, 0, 0, 0)),
                  pl.BlockSpec((gs, 16, 256), lambda g: (g, 0, 0)),
                  pl.BlockSpec((gs, S5_GROUP, 128), lambda g: (g, 0, 0))],
        out_specs=pl.BlockSpec((CHUNK, gs * S5_GROUP, nrows), lambda g: (0, g, 0)),
        out_shape=jax.ShapeDtypeStruct(ut.shape, BF16),
        scratch_shapes=[pltpu.VMEM((tc, tc), BF16),
                        pltpu.VMEM((tc, 256), BF16),
                        pltpu.VMEM((tc, 256), BF16),
                        pltpu.VMEM((tc, 256), F32),
                        pltpu.VMEM((8, S5_GROUP, RS_LANES), BF16),
                        ] * gs,
        compiler_params=_cparams(1),
        name="s5_core",
    )(*([ut] * gs), cm, bb, pw, dtab, dd)


def _s5_glu_kernel(z_ref, wt_ref, o_ref):
    wt = wt_ref[...]
    nrows = z_ref.shape[2]
    eye = jnp.where(lax.broadcasted_iota(jnp.int32, (nrows, nrows), 0)
                    == lax.broadcasted_iota(jnp.int32, (nrows, nrows), 1), 1.0, 0.0).astype(BF16)
    for j in range(S5T_DT):
        zt = z_ref[j]
        gl = jnp.dot(wt, zt, preferred_element_type=F32)
        o = (zt.astype(F32) * jax.nn.sigmoid(gl)).astype(BF16)
        nat = lax.dot_general(eye, o, NT_DIMS, preferred_element_type=F32)
        o_ref[:, j, :] = nat


def _s5_glu(zt, wglu_t):
    nrows = zt.shape[2]
    return pl.pallas_call(
        _s5_glu_kernel,
        grid=(CHUNK // S5T_DT,),
        in_specs=[pl.BlockSpec((S5T_DT, 512, nrows), lambda i: (i, 0, 0)),
                  _const_spec((512, 512))],
        out_specs=pl.BlockSpec((nrows, S5T_DT, 512), lambda i: (0, i, 0)),
        out_shape=jax.ShapeDtypeStruct((nrows, CHUNK, 512), F32),
        compiler_params=_cparams(1),
        name="s5_glu",
    )(zt, wglu_t)


NA_ROWS_PER_STEP = 32
NA_PAIRS_PER_STEP = NA_ROWS_PER_STEP // 2
NA_WIN_ROWS = NA_KH + 1
NA_WIN_KEYS = NA_WIN_ROWS * GRID_W
NA_WIN_BLKS = -(-NA_WIN_KEYS // TOK_BLK)
NA_PV_KEYS = NA_WIN_BLKS * TOK_BLK
NA_RR_OUTSIDE = 2 * NA_KH - 1


def _na_kernel(rpb_ref, qw_ref, k_ref, vt_ref, oob_ref, o_ref, *scratch):
    st_s, pt_s, bias_s = (scratch[0:4], scratch[4:8]), (scratch[8:12], scratch[12:16]), scratch[16]
    rb = pl.program_id(1)
    lo_q = lax.broadcasted_iota(jnp.int32, (GRID_W, 128), 1) < 64
    same_head2 = ((lax.broadcasted_iota(jnp.int32, (128, 256), 0) < 64)
                  == ((lax.broadcasted_iota(jnp.int32, (128, 256), 1) & 64) == 0))
    esum = jnp.where(lax.broadcasted_iota(jnp.int32, (GRID_W, 128), 0)
                     == (lax.broadcasted_iota(jnp.int32, (GRID_W, 128), 1) & (GRID_W - 1)),
                     1.0, 0.0).astype(BF16)
    half = NA_KH // 2

    kcol = lax.broadcasted_iota(jnp.int32, (GRID_W, 128), 0)
    qcol = lax.broadcasted_iota(jnp.int32, (GRID_W, 128), 1) & (GRID_W - 1)
    qstart = jnp.clip(qcol - NA_KW // 2, 0, GRID_W - NA_KW)
    col_ok = (kcol >= qstart) & (kcol < qstart + NA_KW)

    @pl.when((pl.program_id(0) == 0) & (rb == 0))
    def _build_bias():
        rel = jnp.clip(kcol - qcol + (NA_KW - 1), 0, 2 * NA_KW - 2)

        def rr_body(rr, carry):
            for s in range(4):
                acc = jnp.zeros((GRID_W, 128), F32)
                for j in range(2 * NA_KW - 1):
                    acc = jnp.where(rel == j, jnp.where(lo_q, rpb_ref[2 * s, rr, j], rpb_ref[2 * s + 1, rr, j]), acc)
                bias_s[s, rr] = acc * LOG2E
            return carry

        lax.fori_loop(0, NA_RR_OUTSIDE, rr_body, 0)
        for s in range(4):
            bias_s[s, NA_RR_OUTSIDE] = oob_ref[...]

    def window(pi):
        r0 = rb * NA_ROWS_PER_STEP + 2 * pi
        return r0, jnp.clip(r0 - half, 0, GRID_W - NA_WIN_ROWS)

    def q_stage(pi):
        _, win = window(pi)
        koff = pl.multiple_of(win * GRID_W, GRID_W)
        scs = []
        for s in range(4):
            ch = slice(128 * s, 128 * (s + 1))
            kw = k_ref[s, pl.ds(koff, NA_WIN_KEYS), :]
            scs.append(jnp.dot(kw, qw_ref[pi, ch, :], preferred_element_type=F32))
        return scs

    def s_stage(pi, par, scs):
        r0, win = window(pi)
        poff = pl.multiple_of((win & 1) * GRID_W, GRID_W)
        rr = []
        for p in range(2):
            r = r0 + p
            first = jnp.clip(r - half, 0, GRID_W - NA_KH) - win
            rel = win - r + (NA_KH - 1)
            rr.append([jnp.where((wr >= first) & (wr < first + NA_KH), wr + rel, NA_RR_OUTSIDE)
                       for wr in range(NA_WIN_ROWS)])
        sums = []
        for s in range(4):
            st = st_s[par][s]
            slab_sums = []
            for p in range(2):
                lanes = slice(128 * p, 128 * (p + 1))
                for edge in (0, NA_PV_KEYS - GRID_W):
                    pt_s[par][s][p, edge:edge + GRID_W, :] = jnp.zeros((GRID_W, 128), BF16)
                m = None
                for wr in range(NA_WIN_ROWS):
                    rows = slice(GRID_W * wr, GRID_W * (wr + 1))
                    t = jnp.where(col_ok, scs[s][rows, lanes] + bias_s[s, rr[p][wr]], NEG_INF)
                    st[p, rows, :] = t
                    m = t if m is None else jnp.maximum(m, t)
                m = jnp.max(m, axis=0, keepdims=True)
                l = None
                for wr in range(NA_WIN_ROWS):
                    rows = slice(GRID_W * wr, GRID_W * (wr + 1))
                    e = jnp.exp2(st[p, rows, :] - m)
                    pt_s[par][s][p, pl.ds(poff + GRID_W * wr, GRID_W), :] = e.astype(BF16)
                    l = e if l is None else l + e
                slab_sums.append(jnp.sum(l, axis=0, keepdims=True))
            sums.append(jnp.concatenate(slab_sums, axis=1))
        return tuple(sums)

    def o_stage_pv(pi, par, sums):
        _, win = window(pi)
        blk0 = lax.shift_right_logical(win, 1)
        ots = []
        for s in range(4):
            ch = slice(128 * s, 128 * (s + 1))
            vw = jnp.concatenate([vt_ref[blk0 + i, ch, :] for i in range(NA_WIN_BLKS)], axis=1)
            pt = jnp.concatenate([pt_s[par][s][0], pt_s[par][s][1]], axis=1)
            ot = jnp.dot(vw, pt, preferred_element_type=F32)
            ots.append(jnp.where(same_head2, (ot / sums[s]).astype(BF16), jnp.zeros((), BF16)))
        return ots

    def o_stage_store(pi, ots):
        for s in range(4):
            ch = slice(128 * s, 128 * (s + 1))
            for p in range(2):
                nat = lax.dot_general(esum, ots[s][:, 128 * p:128 * (p + 1)], NT_DIMS,
                                      preferred_element_type=F32)
                qoff = pl.multiple_of((2 * pi + p) * GRID_W, GRID_W)
                o_ref[s, pl.ds(qoff, GRID_W), :] = nat.astype(BF16)

    def step(j, prev_sums, has_prev=True):
        a, b = 2 * j, 2 * j + 1
        scs_a = q_stage(a)
        if has_prev:
            ots_a = o_stage_pv(a - 2, 0, prev_sums[0])
            ots_b = o_stage_pv(b - 2, 1, prev_sums[1])
        scs_b = q_stage(b)
        if has_prev:
            o_stage_store(a - 2, ots_a)
            o_stage_store(b - 2, ots_b)
        return s_stage(a, 0, scs_a), s_stage(b, 1, scs_b)

    sums = step(0, None, has_prev=False)
    sums = lax.fori_loop(1, NA_PAIRS_PER_STEP // 2, step, sums)
    for u in range(2):
        pi = NA_PAIRS_PER_STEP - 2 + u
        o_stage_store(pi, o_stage_pv(pi, u, sums[u]))


def _na(rpb, qw, k, vt, b, s):
    tq = NA_ROWS_PER_STEP * GRID_W
    blks = s // TOK_BLK
    steps = s // tq
    oob = jnp.full((GRID_W, 128), -jnp.inf, F32)
    return pl.pallas_call(
        _na_kernel,
        grid=(b, s // tq),
        in_specs=[pl.BlockSpec(memory_space=pltpu.SMEM),
                  pl.BlockSpec((NA_PAIRS_PER_STEP, 512, 2 * TOK_BLK),
                               lambda bi, ri: (bi * (blks // NA_PAIRS_PER_STEP) + ri, 0, 0)),
                  pl.BlockSpec((4, s, 128), lambda bi, ri: (0, bi, 0)),
                  pl.BlockSpec((blks, 512, TOK_BLK), lambda bi, ri: (bi, 0, 0)),
                  _const_spec(oob.shape)],
        out_specs=pl.BlockSpec((4, tq, 128), lambda bi, ri: (0, bi * steps + ri, 0)),
        out_shape=jax.ShapeDtypeStruct((4, b * s, 128), BF16),
        scratch_shapes=([pltpu.VMEM((2, NA_WIN_KEYS, 128), F32)] * 8
                        + [pltpu.VMEM((2, NA_PV_KEYS, 128), BF16)] * 8
                        + [pltpu.VMEM((NA_HEADS // 2, 2 * NA_KH, GRID_W, 128), F32)]),
        compiler_params=_cparams(2),
        name="na",
    )(rpb, qw, k, vt, oob)


def _memkv_kernel(mem_ref, g_ref, w_ref, k_ref, v_ref):
    b, m, _ = mem_ref.shape
    mn = _rms(mem_ref[...].reshape(b * m, D), g_ref[...]).astype(BF16)
    kv = jnp.dot(mn, w_ref[...].astype(BF16), preferred_element_type=F32)
    k_ref[...] = kv[:, 0:512].astype(BF16).reshape(b, m, 512)
    v_ref[...] = kv[:, 512:1024].astype(BF16).reshape(b, m, 512)


def _memkv(mem, g, w):
    b, m, _ = mem.shape
    out = jax.ShapeDtypeStruct((b, m, 512), BF16)
    ospec = pl.BlockSpec((b, m, 512), lambda i: (0, 0, 0))
    return pl.pallas_call(
        _memkv_kernel,
        grid=(1,),
        in_specs=[pl.BlockSpec((b, m, D), lambda i: (0, 0, 0)),
                  _const_spec((1, D)),
                  _const_spec((D, 1024))],
        out_specs=[ospec, ospec],
        out_shape=[out, out],
        compiler_params=_cparams(1),
        name="memkv",
    )(mem, g, w)


def _mem_kernel(q_ref, k_ref, v_ref, o_ref):
    scale = MEM_HEAD_DIM ** -0.5
    for h in range(MEM_HEADS):
        cols = slice(MEM_HEAD_DIM * h, MEM_HEAD_DIM * (h + 1))
        sc = lax.dot_general(q_ref[0, :, cols], k_ref[0, :, cols], NT_DIMS,
                             preferred_element_type=F32) * scale
        m = jnp.max(sc, axis=-1, keepdims=True)
        p = jnp.exp(sc - m)
        l = jnp.sum(p, axis=-1, keepdims=True)
        o = jnp.dot(p.astype(BF16), v_ref[0, :, cols], preferred_element_type=F32)
        o_ref[0, :, cols] = (o / l).astype(BF16)


def _mem_attn(q, k, v):
    b, s, _ = q.shape
    m = k.shape[1]
    tq = 4096
    return pl.pallas_call(
        _mem_kernel,
        grid=(b, s // tq),
        in_specs=[pl.BlockSpec((1, tq, 512), lambda bi, i: (bi, i, 0)),
                  pl.BlockSpec((1, m, 512), lambda bi, i: (bi, 0, 0)),
                  pl.BlockSpec((1, m, 512), lambda bi, i: (bi, 0, 0))],
        out_specs=pl.BlockSpec((1, tq, 512), lambda bi, i: (bi, i, 0)),
        out_shape=jax.ShapeDtypeStruct((b, s, 512), BF16),
        compiler_params=_cparams(2),
        name="mem_attn",
    )(q, k, v)


def _merge_kernel(x_ref, g_ref, yna_ref, ys5_ref, ymem_ref, wg_ref, bg_ref, wb_ref, wo_ref, o_ref):
    x = x_ref[...]
    h = _rms(x, g_ref[...]).astype(BF16)
    merged = None
    y_na = jnp.concatenate([yna_ref[s] for s in range(4)], axis=1)
    ys = (y_na, ys5_ref[...].astype(BF16), ymem_ref[...])
    for b, y in enumerate(ys):
        cols = slice(D * b, D * (b + 1))
        gate = jax.nn.sigmoid(jnp.dot(h, wg_ref[:, cols].astype(BF16), preferred_element_type=F32)
                              + bg_ref[:, cols])
        up = jnp.dot(y, wb_ref[b], preferred_element_type=F32)
        merged = gate * up if merged is None else merged + gate * up
    o_ref[...] = x + jnp.dot(merged.astype(BF16), wo_ref[...], preferred_element_type=F32)


def _merge(x2d, g, yna, ys5, ymem, wg, bg, wb, wo):
    n = x2d.shape[0]
    tm = 1024
    yspec = pl.BlockSpec((tm, 512), lambda i: (i, 0))
    return pl.pallas_call(
        _merge_kernel,
        grid=(n // tm,),
        in_specs=[pl.BlockSpec((tm, D), lambda i: (i, 0)),
                  _const_spec((1, D)),
                  pl.BlockSpec((4, tm, 128), lambda i: (0, i, 0)), yspec, yspec,
                  _const_spec((D, 3 * D)),
                  _const_spec((1, 3 * D)),
                  _const_spec((3, 512, D)),
                  _const_spec((D, D))],
        out_specs=pl.BlockSpec((tm, D), lambda i: (i, 0)),
        out_shape=jax.ShapeDtypeStruct((n, D), F32),
        compiler_params=_cparams(1),
        name="merge",
    )(x2d, g, yna, ys5, ymem, wg, bg, wb, wo)


FFN_SUB = 512


def _ffn_kernel(x_ref, g_ref, gf_ref, w1_ref, w3_ref, w2_ref, o_ref):
    w2 = w2_ref[...].astype(BF16)
    for i in range(x_ref.shape[0] // FFN_SUB):
        rows = slice(FFN_SUB * i, FFN_SUB * (i + 1))
        x = x_ref[rows, :]
        h = _rms(x, g_ref[...]).astype(BF16)
        a = jnp.dot(h, w1_ref[...], preferred_element_type=F32)
        c = jnp.dot(h, w3_ref[...], preferred_element_type=F32)
        mid = (a * jax.nn.sigmoid(a) * c).astype(BF16)
        x2 = x + jnp.dot(mid, w2, preferred_element_type=F32)
        o_ref[rows, :] = _rms(x2, gf_ref[...])


def _ffn(x2d, g, gf, w1, w3, w2):
    n = x2d.shape[0]
    tm = 2 * FFN_SUB
    return pl.pallas_call(
        _ffn_kernel,
        grid=(n // tm,),
        in_specs=[pl.BlockSpec((tm, D), lambda i: (i, 0)),
                  _const_spec((1, D)),
                  _const_spec((1, D)),
                  _const_spec((D, D_FF)),
                  _const_spec((D, D_FF)),
                  _const_spec((D_FF, D))],
        out_specs=pl.BlockSpec((tm, D), lambda i: (i, 0)),
        out_shape=jax.ShapeDtypeStruct((n, D), F32),
        compiler_params=_cparams(1),
        name="ffn",
    )(x2d, g, gf, w1, w3, w2)


def _s5_tables(a_re, a_im, log_dt, b_re, b_im, c_re, c_im, s5_d):
    t = CHUNK
    ar, ai = a_re.astype(F32), a_im.astype(F32)
    dt = jnp.exp(log_dt.astype(F32))[..., None]
    lr, li = ar * dt, ai * dt
    mag = jnp.exp(lr)
    lbr, lbi = mag * jnp.cos(li), mag * jnp.sin(li)
    den = ar * ar + ai * ai
    rr = ((lbr - 1.0) * ar + lbi * ai) / den
    ri = (lbi * ar - (lbr - 1.0) * ai) / den
    br, bi = b_re.astype(F32), b_im.astype(F32)
    bbr = rr[..., None] * br - ri[..., None] * bi
    bbi = rr[..., None] * bi + ri[..., None] * br
    cmr, cmi = c_re.astype(F32), c_im.astype(F32)
    tau = jnp.arange(t + 1, dtype=F32)[:, None, None, None]
    pmag = jnp.exp(tau * lr[None])
    pwr, pwi = pmag * jnp.cos(tau * li[None]), pmag * jnp.sin(tau * li[None])

    dd = jnp.pad(jnp.eye(S5_GROUP, dtype=F32)[None] * s5_d.astype(F32).reshape(S5_GROUPS, S5_GROUP, 1),
                 ((0, 0), (0, 0), (128 - S5_GROUP, 0)))

    def lanes4(f0, f1, b0, b1):
        return jnp.concatenate([f0, f1, b0, b1], axis=-1)

    cm = jnp.stack([lanes4(cmr[0], -cmi[0], cmr[1], -cmi[1]),
                    lanes4(-cmi[0], -cmr[0], -cmi[1], -cmr[1])], axis=1)
    fr, fi = jnp.moveaxis(pwr[1:t + 1, 0], 0, 1), jnp.moveaxis(pwi[1:t + 1, 0], 0, 1)
    gr, gi = jnp.moveaxis(pwr[1:t + 1][::-1, 1], 0, 1), jnp.moveaxis(pwi[1:t + 1][::-1, 1], 0, 1)
    bt = lambda z: jnp.swapaxes(z, -1, -2)
    bb = jnp.stack([lanes4(bt(bbr[0]), bt(bbi[0]), bt(bbr[1]), bt(bbi[1])),
                    lanes4(-bt(bbi[0]), bt(bbr[0]), -bt(bbi[1]), bt(bbr[1]))], axis=1)
    wr, wi = jnp.moveaxis(pwr[:t][::-1, 0], 0, 1), jnp.moveaxis(pwi[:t][::-1, 0], 0, 1)
    vr, vi = jnp.moveaxis(pwr[:t, 1], 0, 1), jnp.moveaxis(pwi[:t, 1], 0, 1)
    pw = jnp.stack([lanes4(fr, fr, gr, gr), lanes4(fi, fi, gi, gi),
                    lanes4(wr, wr, vr, vr), lanes4(wi, wi, vi, vi)], axis=1)

    dr, di = pwr[t], pwi[t]
    a_rows, b_rows = [], []
    for _ in range(N_SCAN_LEVELS):
        a_rows.append(lanes4(dr[0], dr[0], dr[1], dr[1]))
        b_rows.append(lanes4(-di[0], di[0], -di[1], di[1]))
        dr, di = dr * dr - di * di, 2.0 * dr * di
    pad = [jnp.zeros_like(a_rows[0])] * (8 - N_SCAN_LEVELS)
    dtab = jnp.stack(a_rows + pad + b_rows + pad, axis=1)
    return cm, bb, pw, dtab, dd


def kernel(x, mem, g_mix, g_mem, g_ffn, g_final, w_in, w_gate, b_gate, rpb, w_mem_kv,
           a_re, a_im, log_dt, b_re, b_im, c_re, c_im, s5_d, w_glu, w_branch, w_o,
           w_ffn1, w_ffn3, w_ffn2):
    bsz, s, d = x.shape
    n = bsz * s
    x2d = x.reshape(n, d)
    gm = g_mix[0].reshape(1, d).astype(F32)

    w_in_t = w_in[0].T.astype(BF16)
    cm_tab, bb_tab, pw_tab, dtab, dd_tab = _s5_tables(a_re[0], a_im[0], log_dt[0], b_re[0], b_im[0],
                                                      c_re[0], c_im[0], s5_d[0])

    k, qm, qw, vt = _proj(x2d, gm, w_in[0], w_in_t)

    ut = _proj_s5t(x.reshape(n // CHUNK, CHUNK, d), gm, w_in_t)
    zt = _s5_core(ut, cm_tab, bb_tab, pw_tab, dtab, dd_tab)
    y_s5 = _s5_glu(zt, w_glu[0].T.astype(BF16)).reshape(n, 512)

    y_na = _na(rpb[0].astype(F32), qw, k, vt, bsz, s)

    k_mem, v_mem = _memkv(mem, g_mem[0].reshape(1, d).astype(F32), w_mem_kv[0])
    y_mem = _mem_attn(qm.reshape(bsz, s, 512), k_mem, v_mem).reshape(n, 512)

    x1 = _merge(x2d, gm, y_na, y_s5, y_mem, w_gate[0],
                b_gate[0].reshape(1, 3 * d).astype(F32), w_branch[0].astype(BF16), w_o[0].astype(BF16))
    out = _ffn(x1, g_ffn[0].reshape(1, d).astype(F32), g_final.reshape(1, d).astype(F32),
               w_ffn1[0].astype(BF16), w_ffn3[0].astype(BF16), w_ffn2[0])
    return out.reshape(bsz, s, d)
```

```python
import functools

import jax
import jax.numpy as jnp
from jax import lax
from jax.experimental import pallas as pl
from jax.experimental.pallas import tpu as pltpu

F32 = jnp.float32
BF16 = jnp.bfloat16

D = 1024
GRID_W = 64
NA_HEADS = 8
NA_KH = 8
NA_KW = 16
S5_GROUPS = 32
S5_GROUP = 16
S5_STATE = 64
CHUNK = 64
MEM_HEADS = 4
MEM_HEAD_DIM = 128
D_FF = 2816
EPS = 1e-6
NEG_INF = -1e30

VMEM_LIMIT = 56 * 1024 * 1024

NT_DIMS = (((1,), (1,)), ((), ()))


def _cparams(n_axes):
    return pltpu.CompilerParams(
        dimension_semantics=("arbitrary",) * n_axes,
        vmem_limit_bytes=VMEM_LIMIT)


def _rms(x, g):
    return x * lax.rsqrt(jnp.mean(x * x, axis=-1, keepdims=True) + EPS) * g


def _const_spec(shape):
    nd = len(shape)
    return pl.BlockSpec(shape, lambda *_: (0,) * nd, pipeline_mode=pl.Buffered(1))


TOK_BLK = 128


LOG2E = 1.4426950408889634
NA_Q_SCALE = 64 ** -0.5 * LOG2E


def _proj_kernel(x_ref, g_ref, wk_ref, wqm_ref, wqt_ref, wvt_ref, k_ref, qm_ref, qw_ref, vt_ref):
    h = _rms(x_ref[...], g_ref[...]).astype(BF16)
    kk = jnp.dot(h, wk_ref[...].astype(BF16), preferred_element_type=F32).astype(BF16)
    for s in range(4):
        k_ref[s] = kk[:, 128 * s:128 * (s + 1)]
    qm_ref[...] = jnp.dot(h, wqm_ref[...].astype(BF16), preferred_element_type=F32).astype(BF16)
    qt = lax.dot_general(wqt_ref[...], h, NT_DIMS, preferred_element_type=F32) * NA_Q_SCALE
    vt = lax.dot_general(wvt_ref[...], h, NT_DIMS, preferred_element_type=F32)
    lo = lax.broadcasted_iota(jnp.int32, (128, 128), 1) < 64
    same_head = (lax.broadcasted_iota(jnp.int32, (128, 128), 0) < 64) == lo
    for i in range(x_ref.shape[0] // TOK_BLK):
        toks = slice(TOK_BLK * i, TOK_BLK * (i + 1))
        vt_ref[i] = vt[:, toks].astype(BF16)
        for s in range(4):
            a = qt[128 * s:128 * (s + 1), toks]
            ar = pltpu.roll(a, 64, axis=1)
            qw_ref[i, 128 * s:128 * (s + 1), :] = jnp.concatenate(
                [jnp.where(same_head, jnp.where(lo, a, ar), 0.0),
                 jnp.where(same_head, jnp.where(lo, ar, a), 0.0)], axis=1).astype(BF16)


W_IN_BLK = 512


def _w_in_spec(blk, transposed):
    if transposed:
        return pl.BlockSpec((W_IN_BLK, D), lambda *_: (blk, 0), pipeline_mode=pl.Buffered(1))
    return pl.BlockSpec((D, W_IN_BLK), lambda *_: (0, blk), pipeline_mode=pl.Buffered(1))


def _proj(x2d, g, w, wt):
    n = x2d.shape[0]
    tm = 1024
    nat = jax.ShapeDtypeStruct((n, 512), BF16)
    nat_spec = pl.BlockSpec((tm, 512), lambda i: (i, 0))
    chm = lambda width: (jax.ShapeDtypeStruct((n // TOK_BLK, 512, width), BF16),
                         pl.BlockSpec((tm // TOK_BLK, 512, width), lambda i: (i, 0, 0)))
    (qw_shape, qw_spec), (vt_shape, vt_spec) = chm(2 * TOK_BLK), chm(TOK_BLK)
    return pl.pallas_call(
        _proj_kernel,
        grid=(n // tm,),
        in_specs=[pl.BlockSpec((tm, D), lambda i: (i, 0)),
                  _const_spec((1, D)),
                  _w_in_spec(1, False),
                  _w_in_spec(4, False),
                  _w_in_spec(0, True),
                  _w_in_spec(2, True)],
        out_specs=[pl.BlockSpec((4, tm, 128), lambda i: (0, i, 0)), nat_spec, qw_spec, vt_spec],
        out_shape=[jax.ShapeDtypeStruct((4, n, 128), BF16), nat, qw_shape, vt_shape],
        compiler_params=_cparams(1),
        name="proj",
    )(x2d, g, w, w, wt, wt)


S5T_DT = 8


def _proj_s5t_kernel(x_ref, g_ref, wt_ref, o_ref, h_s):
    wt = wt_ref[...]
    nrows = x_ref.shape[0]
    hn = _rms(x_ref[...], g_ref[...]).reshape(nrows * S5T_DT, D)
    for c in range(D // 128):
        h_s[c] = hn[:, 128 * c:128 * (c + 1)]
    for j in range(S5T_DT):
        h = jnp.concatenate([h_s[c, pl.ds(j, nrows, stride=S5T_DT), :] for c in range(D // 128)],
                            axis=1).astype(BF16)
        ut = lax.dot_general(wt, h, NT_DIMS, preferred_element_type=F32)
        o_ref[j] = ut.astype(BF16)


def _proj_s5t(xc, g, wt):
    nrows = xc.shape[0]
    return pl.pallas_call(
        _proj_s5t_kernel,
        grid=(CHUNK // S5T_DT,),
        in_specs=[pl.BlockSpec((nrows, S5T_DT, D), lambda i: (0, i, 0)),
                  _const_spec((1, D)),
                  _w_in_spec(3, True)],
        out_specs=pl.BlockSpec((S5T_DT, 512, nrows), lambda i: (i, 0, 0)),
        out_shape=jax.ShapeDtypeStruct((CHUNK, 512, nrows), BF16),
        scratch_shapes=[pltpu.VMEM((D // 128, nrows * S5T_DT, 128), F32)],
        compiler_params=_cparams(1),
        name="proj_s5t",
    )(xc, g, wt)


N_SCAN_LEVELS = 6


def _gelu_tanh(x):
    c = 0.7978845608028654
    return 0.5 * x * (1.0 + jnp.tanh(c * (x + 0.044715 * (x * x * x))))


def _split_bf16(x):
    hi = x.astype(BF16)
    return hi, (x - hi.astype(F32)).astype(BF16)


def _nt_f32(a, b):
    ah, al = _split_bf16(a)
    bh, bl = _split_bf16(b)
    nt = lambda u, w: lax.dot_general(u, w, NT_DIMS, preferred_element_type=F32)
    return nt(ah, bh) + nt(ah, bl) + nt(al, bh)


S5_GROUPS_PER_STEP = 4
S5_SCRATCH_PER_GROUP = 5


def _s5_kernel(*refs):
    gs = S5_GROUPS_PER_STEP
    ut_ref, (cm_ref, bb_ref, pw_ref, d_ref, dd_ref) = refs[:gs], refs[gs:gs + 5]
    z_ref, scratch = refs[gs + 5], refs[gs + 6:]
    groups = [scratch[S5_SCRATCH_PER_GROUP * gi:S5_SCRATCH_PER_GROUP * (gi + 1)]
              for gi in range(S5_GROUPS_PER_STEP)]
    for gi, (m_s, n_s, pt_s, ptf_s, rs_s) in enumerate(groups):
        _s5_operators(gi, cm_ref, bb_ref, pw_ref, dd_ref, m_s, n_s, pt_s, ptf_s, rs_s)
    yv = [_s5_chunk_matmuls(gi, ut_ref, m_s, pt_s) for gi, (m_s, _, pt_s, _, _) in enumerate(groups)]
    for gi, (_, n_s, _, _, _) in enumerate(groups):
        _s5_finish(gi, d_ref, z_ref, n_s, *yv[gi])


def _s5_operators(gi, cm_ref, bb_ref, pw_ref, dd_ref, m_s, n_s, pt_s, ptf_s, rs_s):
    tc = CHUNK * S5_GROUP
    cma, cmb = cm_ref[gi,0], cm_ref[gi,1]
    bba, bbb = bb_ref[gi,0], bb_ref[gi,1]
    for t in range(CHUNK):
        rows = slice(S5_GROUP * t, S5_GROUP * (t + 1))
        n_s[rows, :] = (cma * pw_ref[gi,0, t:t + 1, :] + cmb * pw_ref[gi,1, t:t + 1, :]).astype(BF16)
        ptf_s[rows, :] = bba * pw_ref[gi,2, t:t + 1, :] + bbb * pw_ref[gi,3, t:t + 1, :]
    ptf = ptf_s[...]
    pt_s[...] = ptf.astype(BF16)

    ra = _nt_f32(cma[:, 0:128], ptf[:, 0:128])
    rb = _nt_f32(cma[:, 128:256], ptf[:, 128:256])
    zeros = jnp.zeros((S5_GROUP, tc), F32)
    r = (jnp.concatenate([ra, zeros], axis=1)
         + pltpu.roll(jnp.concatenate([rb, zeros], axis=1), (CHUNK - 1) * S5_GROUP, axis=1)
         + jnp.concatenate([zeros[:, 0:tc - 128], dd_ref[gi], zeros], axis=1))
    for k in range(8):
        rk = r if k == 0 else pltpu.roll(r, 2 * tc - S5_GROUP * k, axis=1)
        rs_s[k] = rk[:, 0:RS_LANES].astype(BF16)
    for t in range(CHUNK):
        a, k = divmod(CHUNK - 1 - t, 8)
        m_s[S5_GROUP * t:S5_GROUP * (t + 1), :] = rs_s[k, :, 128 * a:128 * a + tc]


def _s5_chunk_matmuls(gi, ut_ref, m_s, pt_s):
    ut = ut_ref[gi][...].reshape(CHUNK * S5_GROUP, -1)
    nrows = ut.shape[1]
    eye = jnp.where(lax.broadcasted_iota(jnp.int32, (nrows, nrows), 0)
                    == lax.broadcasted_iota(jnp.int32, (nrows, nrows), 1), 1.0, 0.0).astype(BF16)
    u_rows = lax.dot_general(eye, ut, NT_DIMS, preferred_element_type=F32).astype(BF16)
    y = jnp.dot(m_s[...], ut, preferred_element_type=F32)
    v = jnp.dot(u_rows, pt_s[...], preferred_element_type=F32)
    return y, v


def _s5_finish(gi, d_ref, z_ref, n_s, y, v):
    nrows = v.shape[0]
    pos = lax.broadcasted_iota(jnp.int32, (nrows, 128), 0) & (CHUNK - 1)

    def shift(x, s, up):
        if up:
            return jnp.where(pos < CHUNK - s, pltpu.roll(x, nrows - s, axis=0), 0.0)
        return jnp.where(pos >= s, pltpu.roll(x, s, axis=0), 0.0)

    def scan(vh, lanes, up):
        x = shift(vh, 1, up)
        for lvl in range(N_SCAN_LEVELS):
            xs = shift(x, 1 << lvl, up)
            a = d_ref[gi,lvl:lvl + 1, lanes]
            b = d_ref[gi,8 + lvl:9 + lvl, lanes]
            x = x + a * xs + b * pltpu.roll(xs, 64, axis=1)
        return x

    xf = scan(v[:, 0:128], slice(0, 128), False)
    xb = scan(v[:, 128:256], slice(128, 256), True)
    xin = jnp.concatenate([xf, xb], axis=1).astype(BF16)
    y = y + lax.dot_general(n_s[...], xin, NT_DIMS, preferred_element_type=F32)
    z_ref[:, S5_GROUP * gi:S5_GROUP * (gi + 1), :] = _gelu_tanh(y).astype(BF16).reshape(CHUNK, S5_GROUP, -1)


RS_LANES = 128 * 7 + CHUNK * S5_GROUP


def _s5_core(ut, cm, bb, pw, dtab, dd):
    nrows = ut.shape[2]
    tc = CHUNK * S5_GROUP
    gs = S5_GROUPS_PER_STEP
    group_specs = [pl.BlockSpec((CHUNK, S5_GROUP, nrows), functools.partial(lambda i, g: (0, gs * g + i, 0), i))
                   for i in range(gs)]
    return pl.pallas_call(
        _s5_kernel,
        grid=(S5_GROUPS // gs,),
        in_specs=[*group_specs,
                  pl.BlockSpec((gs, 2, S5_GROUP, 256), lambda g: (g, 0, 0, 0)),
                  pl.BlockSpec((gs, 2, S5_GROUP, 256), lambda g: (g, 0, 0, 0)),
                  pl.BlockSpec((gs, 4, CHUNK, 256), lambda g: (g, 0, 0, 0)),
                  pl.BlockSpec((gs, 16, 256), lambda g: (g, 0, 0)),
                  pl.BlockSpec((gs, S5_GROUP, 128), lambda g: (g, 0, 0))],
        out_specs=pl.BlockSpec((CHUNK, gs * S5_GROUP, nrows), lambda g: (0, g, 0)),
        out_shape=jax.ShapeDtypeStruct(ut.shape, BF16),
        scratch_shapes=[pltpu.VMEM((tc, tc), BF16),
                        pltpu.VMEM((tc, 256), BF16),
                        pltpu.VMEM((tc, 256), BF16),
                        pltpu.VMEM((tc, 256), F32),
                        pltpu.VMEM((8, S5_GROUP, RS_LANES), BF16),
                        ] * gs,
        compiler_params=_cparams(1),
        name="s5_core",
    )(*([ut] * gs), cm, bb, pw, dtab, dd)


S5G_DT = 16


def _s5_glu_kernel(z_ref, wt_ref, o_ref):
    wt = wt_ref[...]
    nrows = z_ref.shape[2]
    eye = jnp.where(lax.broadcasted_iota(jnp.int32, (nrows, nrows), 0)
                    == lax.broadcasted_iota(jnp.int32, (nrows, nrows), 1), 1.0, 0.0).astype(BF16)
    for j in range(S5G_DT):
        zt = z_ref[j]
        gl = jnp.dot(wt, zt, preferred_element_type=F32)
        o = (zt.astype(F32) * jax.nn.sigmoid(gl)).astype(BF16)
        nat = lax.dot_general(eye, o, NT_DIMS, preferred_element_type=F32)
        o_ref[:, j, :] = nat


def _s5_glu(zt, wglu_t):
    nrows = zt.shape[2]
    return pl.pallas_call(
        _s5_glu_kernel,
        grid=(CHUNK // S5G_DT,),
        in_specs=[pl.BlockSpec((S5G_DT, 512, nrows), lambda i: (i, 0, 0)),
                  _const_spec((512, 512))],
        out_specs=pl.BlockSpec((nrows, S5G_DT, 512), lambda i: (0, i, 0)),
        out_shape=jax.ShapeDtypeStruct((nrows, CHUNK, 512), F32),
        compiler_params=_cparams(1),
        name="s5_glu",
    )(zt, wglu_t)


NA_ROWS_PER_STEP = 32
NA_PAIRS_PER_STEP = NA_ROWS_PER_STEP // 2
NA_WIN_ROWS = NA_KH + 1
NA_WIN_KEYS = NA_WIN_ROWS * GRID_W
NA_WIN_BLKS = -(-NA_WIN_KEYS // TOK_BLK)
NA_PV_KEYS = NA_WIN_BLKS * TOK_BLK
NA_RR_OUTSIDE = 2 * NA_KH - 1


def _na_kernel(rpb_ref, qw_ref, k_ref, vt_ref, oob_ref, o_ref, *scratch):
    st_s, pt_s, bias_s = (scratch[0:4], scratch[4:8]), (scratch[8:12], scratch[12:16]), scratch[16]
    rb = pl.program_id(1)
    lo_q = lax.broadcasted_iota(jnp.int32, (GRID_W, 128), 1) < 64
    same_head2 = ((lax.broadcasted_iota(jnp.int32, (128, 256), 0) < 64)
                  == ((lax.broadcasted_iota(jnp.int32, (128, 256), 1) & 64) == 0))
    esum = jnp.where(lax.broadcasted_iota(jnp.int32, (GRID_W, 128), 0)
                     == (lax.broadcasted_iota(jnp.int32, (GRID_W, 128), 1) & (GRID_W - 1)),
                     1.0, 0.0).astype(BF16)
    half = NA_KH // 2

    kcol = lax.broadcasted_iota(jnp.int32, (GRID_W, 128), 0)
    qcol = lax.broadcasted_iota(jnp.int32, (GRID_W, 128), 1) & (GRID_W - 1)
    qstart = jnp.clip(qcol - NA_KW // 2, 0, GRID_W - NA_KW)
    col_ok = (kcol >= qstart) & (kcol < qstart + NA_KW)

    @pl.when((pl.program_id(0) == 0) & (rb == 0))
    def _build_bias():
        rel = jnp.clip(kcol - qcol + (NA_KW - 1), 0, 2 * NA_KW - 2)

        def rr_body(rr, carry):
            for s in range(4):
                acc = jnp.zeros((GRID_W, 128), F32)
                for j in range(2 * NA_KW - 1):
                    acc = jnp.where(rel == j, jnp.where(lo_q, rpb_ref[2 * s, rr, j], rpb_ref[2 * s + 1, rr, j]), acc)
                bias_s[s, rr] = acc * LOG2E
            return carry

        lax.fori_loop(0, NA_RR_OUTSIDE, rr_body, 0)
        for s in range(4):
            bias_s[s, NA_RR_OUTSIDE] = oob_ref[...]

    def window(pi):
        r0 = rb * NA_ROWS_PER_STEP + 2 * pi
        return r0, jnp.clip(r0 - half, 0, GRID_W - NA_WIN_ROWS)

    def q_stage(pi):
        _, win = window(pi)
        koff = pl.multiple_of(win * GRID_W, GRID_W)
        scs = []
        for s in range(4):
            ch = slice(128 * s, 128 * (s + 1))
            kw = k_ref[s, pl.ds(koff, NA_WIN_KEYS), :]
            scs.append(jnp.dot(kw, qw_ref[pi, ch, :], preferred_element_type=F32))
        return scs

    def s_stage(pi, par, scs):
        r0, win = window(pi)
        poff = pl.multiple_of((win & 1) * GRID_W, GRID_W)
        rr = []
        for p in range(2):
            r = r0 + p
            first = jnp.clip(r - half, 0, GRID_W - NA_KH) - win
            rel = win - r + (NA_KH - 1)
            rr.append([jnp.where((wr >= first) & (wr < first + NA_KH), wr + rel, NA_RR_OUTSIDE)
                       for wr in range(NA_WIN_ROWS)])
        sums = []
        for s in range(4):
            st = st_s[par][s]
            slab_sums = []
            for p in range(2):
                lanes = slice(128 * p, 128 * (p + 1))
                for edge in (0, NA_PV_KEYS - GRID_W):
                    pt_s[par][s][p, edge:edge + GRID_W, :] = jnp.zeros((GRID_W, 128), BF16)
                m = None
                for wr in range(NA_WIN_ROWS):
                    rows = slice(GRID_W * wr, GRID_W * (wr + 1))
                    t = jnp.where(col_ok, scs[s][rows, lanes] + bias_s[s, rr[p][wr]], NEG_INF)
                    st[p, rows, :] = t
                    m = t if m is None else jnp.maximum(m, t)
                m = jnp.max(m, axis=0, keepdims=True)
                l = None
                for wr in range(NA_WIN_ROWS):
                    rows = slice(GRID_W * wr, GRID_W * (wr + 1))
                    e = jnp.exp2(st[p, rows, :] - m)
                    pt_s[par][s][p, pl.ds(poff + GRID_W * wr, GRID_W), :] = e.astype(BF16)
                    l = e if l is None else l + e
                slab_sums.append(jnp.sum(l, axis=0, keepdims=True))
            sums.append(jnp.concatenate(slab_sums, axis=1))
        return tuple(sums)

    def o_stage_pv(pi, par, sums):
        _, win = window(pi)
        blk0 = lax.shift_right_logical(win, 1)
        ots = []
        for s in range(4):
            ch = slice(128 * s, 128 * (s + 1))
            vw = jnp.concatenate([vt_ref[blk0 + i, ch, :] for i in range(NA_WIN_BLKS)], axis=1)
            pt = jnp.concatenate([pt_s[par][s][0], pt_s[par][s][1]], axis=1)
            ot = jnp.dot(vw, pt, preferred_element_type=F32)
            ots.append(jnp.where(same_head2, (ot / sums[s]).astype(BF16), jnp.zeros((), BF16)))
        return ots

    def o_stage_store(pi, ots):
        for s in range(4):
            ch = slice(128 * s, 128 * (s + 1))
            for p in range(2):
                nat = lax.dot_general(esum, ots[s][:, 128 * p:128 * (p + 1)], NT_DIMS,
                                      preferred_element_type=F32)
                qoff = pl.multiple_of((2 * pi + p) * GRID_W, GRID_W)
                o_ref[s, pl.ds(qoff, GRID_W), :] = nat.astype(BF16)

    def step(j, prev_sums, has_prev=True):
        a, b = 2 * j, 2 * j + 1
        scs_a = q_stage(a)
        if has_prev:
            ots_a = o_stage_pv(a - 2, 0, prev_sums[0])
            ots_b = o_stage_pv(b - 2, 1, prev_sums[1])
        scs_b = q_stage(b)
        if has_prev:
            o_stage_store(a - 2, ots_a)
            o_stage_store(b - 2, ots_b)
        return s_stage(a, 0, scs_a), s_stage(b, 1, scs_b)

    sums = step(0, None, has_prev=False)
    sums = lax.fori_loop(1, NA_PAIRS_PER_STEP // 2, step, sums)
    for u in range(2):
        pi = NA_PAIRS_PER_STEP - 2 + u
        o_stage_store(pi, o_stage_pv(pi, u, sums[u]))


def _na(rpb, qw, k, vt, b, s):
    tq = NA_ROWS_PER_STEP * GRID_W
    blks = s // TOK_BLK
    steps = s // tq
    oob = jnp.full((GRID_W, 128), -jnp.inf, F32)
    return pl.pallas_call(
        _na_kernel,
        grid=(b, s // tq),
        in_specs=[pl.BlockSpec(memory_space=pltpu.SMEM),
                  pl.BlockSpec((NA_PAIRS_PER_STEP, 512, 2 * TOK_BLK),
                               lambda bi, ri: (bi * (blks // NA_PAIRS_PER_STEP) + ri, 0, 0)),
                  pl.BlockSpec((4, s, 128), lambda bi, ri: (0, bi, 0)),
                  pl.BlockSpec((blks, 512, TOK_BLK), lambda bi, ri: (bi, 0, 0)),
                  _const_spec(oob.shape)],
        out_specs=pl.BlockSpec((4, tq, 128), lambda bi, ri: (0, bi * steps + ri, 0)),
        out_shape=jax.ShapeDtypeStruct((4, b * s, 128), BF16),
        scratch_shapes=([pltpu.VMEM((2, NA_WIN_KEYS, 128), F32)] * 8
                        + [pltpu.VMEM((2, NA_PV_KEYS, 128), BF16)] * 8
                        + [pltpu.VMEM((NA_HEADS // 2, 2 * NA_KH, GRID_W, 128), F32)]),
        compiler_params=_cparams(2),
        name="na",
    )(rpb, qw, k, vt, oob)


def _memkv_kernel(mem_ref, g_ref, w_ref, k_ref, v_ref):
    b, m, _ = mem_ref.shape
    mn = _rms(mem_ref[...].reshape(b * m, D), g_ref[...]).astype(BF16)
    kv = jnp.dot(mn, w_ref[...].astype(BF16), preferred_element_type=F32)
    k_ref[...] = kv[:, 0:512].astype(BF16).reshape(b, m, 512)
    v_ref[...] = kv[:, 512:1024].astype(BF16).reshape(b, m, 512)


def _memkv(mem, g, w):
    b, m, _ = mem.shape
    out = jax.ShapeDtypeStruct((b, m, 512), BF16)
    ospec = pl.BlockSpec((b, m, 512), lambda i: (0, 0, 0))
    return pl.pallas_call(
        _memkv_kernel,
        grid=(1,),
        in_specs=[pl.BlockSpec((b, m, D), lambda i: (0, 0, 0)),
                  _const_spec((1, D)),
                  _const_spec((D, 1024))],
        out_specs=[ospec, ospec],
        out_shape=[out, out],
        compiler_params=_cparams(1),
        name="memkv",
    )(mem, g, w)


def _mem_kernel(q_ref, k_ref, v_ref, o_ref):
    scale = MEM_HEAD_DIM ** -0.5
    for h in range(MEM_HEADS):
        cols = slice(MEM_HEAD_DIM * h, MEM_HEAD_DIM * (h + 1))
        sc = lax.dot_general(q_ref[0, :, cols], k_ref[0, :, cols], NT_DIMS,
                             preferred_element_type=F32) * scale
        m = jnp.max(sc, axis=-1, keepdims=True)
        p = jnp.exp(sc - m)
        l = jnp.sum(p, axis=-1, keepdims=True)
        o = jnp.dot(p.astype(BF16), v_ref[0, :, cols], preferred_element_type=F32)
        o_ref[0, :, cols] = (o / l).astype(BF16)


def _mem_attn(q, k, v):
    b, s, _ = q.shape
    m = k.shape[1]
    tq = 4096
    return pl.pallas_call(
        _mem_kernel,
        grid=(b, s // tq),
        in_specs=[pl.BlockSpec((1, tq, 512), lambda bi, i: (bi, i, 0)),
                  pl.BlockSpec((1, m, 512), lambda bi, i: (bi, 0, 0)),
                  pl.BlockSpec((1, m, 512), lambda bi, i: (bi, 0, 0))],
        out_specs=pl.BlockSpec((1, tq, 512), lambda bi, i: (bi, i, 0)),
        out_shape=jax.ShapeDtypeStruct((b, s, 512), BF16),
        compiler_params=_cparams(2),
        name="mem_attn",
    )(q, k, v)


def _merge_kernel(x_ref, g_ref, yna_ref, ys5_ref, ymem_ref, wg_ref, bg_ref, wb_ref, wo_ref, o_ref):
    x = x_ref[...]
    h = _rms(x, g_ref[...]).astype(BF16)
    merged = None
    y_na = jnp.concatenate([yna_ref[s] for s in range(4)], axis=1)
    ys = (y_na, ys5_ref[...].astype(BF16), ymem_ref[...])
    for b, y in enumerate(ys):
        cols = slice(D * b, D * (b + 1))
        gate = jax.nn.sigmoid(jnp.dot(h, wg_ref[:, cols].astype(BF16), preferred_element_type=F32)
                              + bg_ref[:, cols])
        up = jnp.dot(y, wb_ref[b], preferred_element_type=F32)
        merged = gate * up if merged is None else merged + gate * up
    o_ref[...] = x + jnp.dot(merged.astype(BF16), wo_ref[...], preferred_element_type=F32)


def _merge(x2d, g, yna, ys5, ymem, wg, bg, wb, wo):
    n = x2d.shape[0]
    tm = 1024
    yspec = pl.BlockSpec((tm, 512), lambda i: (i, 0))
    return pl.pallas_call(
        _merge_kernel,
        grid=(n // tm,),
        in_specs=[pl.BlockSpec((tm, D), lambda i: (i, 0)),
                  _const_spec((1, D)),
                  pl.BlockSpec((4, tm, 128), lambda i: (0, i, 0)), yspec, yspec,
                  _const_spec((D, 3 * D)),
                  _const_spec((1, 3 * D)),
                  _const_spec((3, 512, D)),
                  _const_spec((D, D))],
        out_specs=pl.BlockSpec((tm, D), lambda i: (i, 0)),
        out_shape=jax.ShapeDtypeStruct((n, D), F32),
        compiler_params=_cparams(1),
        name="merge",
    )(x2d, g, yna, ys5, ymem, wg, bg, wb, wo)


def _ffn_kernel(x_ref, g_ref, gf_ref, w1_ref, w3_ref, w2_ref, o_ref):
    x = x_ref[...]
    h = _rms(x, g_ref[...]).astype(BF16)
    a = jnp.dot(h, w1_ref[...], preferred_element_type=F32)
    c = jnp.dot(h, w3_ref[...], preferred_element_type=F32)
    mid = (a * jax.nn.sigmoid(a) * c).astype(BF16)
    x2 = x + jnp.dot(mid, w2_ref[...].astype(BF16), preferred_element_type=F32)
    o_ref[...] = _rms(x2, gf_ref[...])


def _ffn(x2d, g, gf, w1, w3, w2):
    n = x2d.shape[0]
    tm = 512
    return pl.pallas_call(
        _ffn_kernel,
        grid=(n // tm,),
        in_specs=[pl.BlockSpec((tm, D), lambda i: (i, 0)),
                  _const_spec((1, D)),
                  _const_spec((1, D)),
                  _const_spec((D, D_FF)),
                  _const_spec((D, D_FF)),
                  _const_spec((D_FF, D))],
        out_specs=pl.BlockSpec((tm, D), lambda i: (i, 0)),
        out_shape=jax.ShapeDtypeStruct((n, D), F32),
        compiler_params=_cparams(1),
        name="ffn",
    )(x2d, g, gf, w1, w3, w2)


def _s5_tables(a_re, a_im, log_dt, b_re, b_im, c_re, c_im, s5_d):
    t = CHUNK
    ar, ai = a_re.astype(F32), a_im.astype(F32)
    dt = jnp.exp(log_dt.astype(F32))[..., None]
    lr, li = ar * dt, ai * dt
    mag = jnp.exp(lr)
    lbr, lbi = mag * jnp.cos(li), mag * jnp.sin(li)
    den = ar * ar + ai * ai
    rr = ((lbr - 1.0) * ar + lbi * ai) / den
    ri = (lbi * ar - (lbr - 1.0) * ai) / den
    br, bi = b_re.astype(F32), b_im.astype(F32)
    bbr = rr[..., None] * br - ri[..., None] * bi
    bbi = rr[..., None] * bi + ri[..., None] * br
    cmr, cmi = c_re.astype(F32), c_im.astype(F32)
    tau = jnp.arange(t + 1, dtype=F32)[:, None, None, None]
    pmag = jnp.exp(tau * lr[None])
    pwr, pwi = pmag * jnp.cos(tau * li[None]), pmag * jnp.sin(tau * li[None])

    dd = jnp.pad(jnp.eye(S5_GROUP, dtype=F32)[None] * s5_d.astype(F32).reshape(S5_GROUPS, S5_GROUP, 1),
                 ((0, 0), (0, 0), (128 - S5_GROUP, 0)))

    def lanes4(f0, f1, b0, b1):
        return jnp.concatenate([f0, f1, b0, b1], axis=-1)

    cm = jnp.stack([lanes4(cmr[0], -cmi[0], cmr[1], -cmi[1]),
                    lanes4(-cmi[0], -cmr[0], -cmi[1], -cmr[1])], axis=1)
    fr, fi = jnp.moveaxis(pwr[1:t + 1, 0], 0, 1), jnp.moveaxis(pwi[1:t + 1, 0], 0, 1)
    gr, gi = jnp.moveaxis(pwr[1:t + 1][::-1, 1], 0, 1), jnp.moveaxis(pwi[1:t + 1][::-1, 1], 0, 1)
    bt = lambda z: jnp.swapaxes(z, -1, -2)
    bb = jnp.stack([lanes4(bt(bbr[0]), bt(bbi[0]), bt(bbr[1]), bt(bbi[1])),
                    lanes4(-bt(bbi[0]), bt(bbr[0]), -bt(bbi[1]), bt(bbr[1]))], axis=1)
    wr, wi = jnp.moveaxis(pwr[:t][::-1, 0], 0, 1), jnp.moveaxis(pwi[:t][::-1, 0], 0, 1)
    vr, vi = jnp.moveaxis(pwr[:t, 1], 0, 1), jnp.moveaxis(pwi[:t, 1], 0, 1)
    pw = jnp.stack([lanes4(fr, fr, gr, gr), lanes4(fi, fi, gi, gi),
                    lanes4(wr, wr, vr, vr), lanes4(wi, wi, vi, vi)], axis=1)

    dr, di = pwr[t], pwi[t]
    a_rows, b_rows = [], []
    for _ in range(N_SCAN_LEVELS):
        a_rows.append(lanes4(dr[0], dr[0], dr[1], dr[1]))
        b_rows.append(lanes4(-di[0], di[0], -di[1], di[1]))
        dr, di = dr * dr - di * di, 2.0 * dr * di
    pad = [jnp.zeros_like(a_rows[0])] * (8 - N_SCAN_LEVELS)
    dtab = jnp.stack(a_rows + pad + b_rows + pad, axis=1)
    return cm, bb, pw, dtab, dd


def kernel(x, mem, g_mix, g_mem, g_ffn, g_final, w_in, w_gate, b_gate, rpb, w_mem_kv,
           a_re, a_im, log_dt, b_re, b_im, c_re, c_im, s5_d, w_glu, w_branch, w_o,
           w_ffn1, w_ffn3, w_ffn2):
    bsz, s, d = x.shape
    n = bsz * s
    x2d = x.reshape(n, d)
    gm = g_mix[0].reshape(1, d).astype(F32)

    w_in_t = w_in[0].T.astype(BF16)
    cm_tab, bb_tab, pw_tab, dtab, dd_tab = _s5_tables(a_re[0], a_im[0], log_dt[0], b_re[0], b_im[0],
                                                      c_re[0], c_im[0], s5_d[0])

    k, qm, qw, vt = _proj(x2d, gm, w_in[0], w_in_t)

    ut = _proj_s5t(x.reshape(n // CHUNK, CHUNK, d), gm, w_in_t)
    zt = _s5_core(ut, cm_tab, bb_tab, pw_tab, dtab, dd_tab)
    y_s5 = _s5_glu(zt, w_glu[0].T.astype(BF16)).reshape(n, 512)

    y_na = _na(rpb[0].astype(F32), qw, k, vt, bsz, s)

    k_mem, v_mem = _memkv(mem, g_mem[0].reshape(1, d).astype(F32), w_mem_kv[0])
    y_mem = _mem_attn(qm.reshape(bsz, s, 512), k_mem, v_mem).reshape(n, 512)

    x1 = _merge(x2d, gm, y_na, y_s5, y_mem, w_gate[0],
                b_gate[0].reshape(1, 3 * d).astype(F32), w_branch[0].astype(BF16), w_o[0].astype(BF16))
    out = _ffn(x1, g_ffn[0].reshape(1, d).astype(F32), g_final.reshape(1, d).astype(F32),
               w_ffn1[0].astype(BF16), w_ffn3[0].astype(BF16), w_ffn2[0])
    return out.reshape(bsz, s, d)
```

```python
import functools

import jax
import jax.numpy as jnp
from jax import lax
from jax.experimental import pallas as pl
from jax.experimental.pallas import tpu as pltpu

F32 = jnp.float32
BF16 = jnp.bfloat16

D = 1024
GRID_W = 64
NA_HEADS = 8
NA_KH = 8
NA_KW = 16
S5_GROUPS = 32
S5_GROUP = 16
S5_STATE = 64
CHUNK = 64
MEM_HEADS = 4
MEM_HEAD_DIM = 128
D_FF = 2816
EPS = 1e-6
NEG_INF = -1e30

VMEM_LIMIT = 56 * 1024 * 1024

NT_DIMS = (((1,), (1,)), ((), ()))


def _cparams(n_axes):
    return pltpu.CompilerParams(
        dimension_semantics=("arbitrary",) * n_axes,
        vmem_limit_bytes=VMEM_LIMIT)


def _rms(x, g):
    return x * lax.rsqrt(jnp.mean(x * x, axis=-1, keepdims=True) + EPS) * g


def _const_spec(shape):
    nd = len(shape)
    return pl.BlockSpec(shape, lambda *_: (0,) * nd, pipeline_mode=pl.Buffered(1))


TOK_BLK = 128


LOG2E = 1.4426950408889634
NA_Q_SCALE = 64 ** -0.5 * LOG2E


def _proj_kernel(x_ref, g_ref, wk_ref, wqm_ref, wqt_ref, wvt_ref, k_ref, qm_ref, qw_ref, vt_ref):
    h = _rms(x_ref[...], g_ref[...]).astype(BF16)
    kk = jnp.dot(h, wk_ref[...].astype(BF16), preferred_element_type=F32).astype(BF16)
    for s in range(4):
        k_ref[s] = kk[:, 128 * s:128 * (s + 1)]
    qm_ref[...] = jnp.dot(h, wqm_ref[...].astype(BF16), preferred_element_type=F32).astype(BF16)
    qt = lax.dot_general(wqt_ref[...], h, NT_DIMS, preferred_element_type=F32) * NA_Q_SCALE
    vt = lax.dot_general(wvt_ref[...], h, NT_DIMS, preferred_element_type=F32)
    lo = lax.broadcasted_iota(jnp.int32, (128, 128), 1) < 64
    same_head = (lax.broadcasted_iota(jnp.int32, (128, 128), 0) < 64) == lo
    for i in range(x_ref.shape[0] // TOK_BLK):
        toks = slice(TOK_BLK * i, TOK_BLK * (i + 1))
        vt_ref[i] = vt[:, toks].astype(BF16)
        for s in range(4):
            a = qt[128 * s:128 * (s + 1), toks]
            ar = pltpu.roll(a, 64, axis=1)
            qw_ref[i, 128 * s:128 * (s + 1), :] = jnp.concatenate(
                [jnp.where(same_head, jnp.where(lo, a, ar), 0.0),
                 jnp.where(same_head, jnp.where(lo, ar, a), 0.0)], axis=1).astype(BF16)


W_IN_BLK = 512


def _w_in_spec(blk, transposed):
    if transposed:
        return pl.BlockSpec((W_IN_BLK, D), lambda *_: (blk, 0), pipeline_mode=pl.Buffered(1))
    return pl.BlockSpec((D, W_IN_BLK), lambda *_: (0, blk), pipeline_mode=pl.Buffered(1))


def _proj(x2d, g, w, wt):
    n = x2d.shape[0]
    tm = 1024
    nat = jax.ShapeDtypeStruct((n, 512), BF16)
    nat_spec = pl.BlockSpec((tm, 512), lambda i: (i, 0))
    chm = lambda width: (jax.ShapeDtypeStruct((n // TOK_BLK, 512, width), BF16),
                         pl.BlockSpec((tm // TOK_BLK, 512, width), lambda i: (i, 0, 0)))
    (qw_shape, qw_spec), (vt_shape, vt_spec) = chm(2 * TOK_BLK), chm(TOK_BLK)
    return pl.pallas_call(
        _proj_kernel,
        grid=(n // tm,),
        in_specs=[pl.BlockSpec((tm, D), lambda i: (i, 0)),
                  _const_spec((1, D)),
                  _w_in_spec(1, False),
                  _w_in_spec(4, False),
                  _w_in_spec(0, True),
                  _w_in_spec(2, True)],
        out_specs=[pl.BlockSpec((4, tm, 128), lambda i: (0, i, 0)), nat_spec, qw_spec, vt_spec],
        out_shape=[jax.ShapeDtypeStruct((4, n, 128), BF16), nat, qw_shape, vt_shape],
        compiler_params=_cparams(1),
        name="proj",
    )(x2d, g, w, w, wt, wt)


S5T_DT = 8


def _proj_s5t_kernel(x_ref, g_ref, wt_ref, o_ref, h_s):
    wt = wt_ref[...]
    nrows = x_ref.shape[0]
    hn = _rms(x_ref[...], g_ref[...]).reshape(nrows * S5T_DT, D)
    for c in range(D // 128):
        h_s[c] = hn[:, 128 * c:128 * (c + 1)]
    for j in range(S5T_DT):
        h = jnp.concatenate([h_s[c, pl.ds(j, nrows, stride=S5T_DT), :] for c in range(D // 128)],
                            axis=1).astype(BF16)
        ut = lax.dot_general(wt, h, NT_DIMS, preferred_element_type=F32)
        o_ref[j] = ut.astype(BF16)


def _proj_s5t(xc, g, wt):
    nrows = xc.shape[0]
    return pl.pallas_call(
        _proj_s5t_kernel,
        grid=(CHUNK // S5T_DT,),
        in_specs=[pl.BlockSpec((nrows, S5T_DT, D), lambda i: (0, i, 0)),
                  _const_spec((1, D)),
                  _w_in_spec(3, True)],
        out_specs=pl.BlockSpec((S5T_DT, 512, nrows), lambda i: (i, 0, 0)),
        out_shape=jax.ShapeDtypeStruct((CHUNK, 512, nrows), BF16),
        scratch_shapes=[pltpu.VMEM((D // 128, nrows * S5T_DT, 128), F32)],
        compiler_params=_cparams(1),
        name="proj_s5t",
    )(xc, g, wt)


N_SCAN_LEVELS = 6


def _gelu_tanh(x):
    c = 0.7978845608028654
    return 0.5 * x * (1.0 + jnp.tanh(c * (x + 0.044715 * (x * x * x))))


def _split_bf16(x):
    hi = x.astype(BF16)
    return hi, (x - hi.astype(F32)).astype(BF16)


def _nt_f32(a, b):
    ah, al = _split_bf16(a)
    bh, bl = _split_bf16(b)
    nt = lambda u, w: lax.dot_general(u, w, NT_DIMS, preferred_element_type=F32)
    return nt(ah, bh) + nt(ah, bl) + nt(al, bh)


S5_GROUPS_PER_STEP = 4
S5_SCRATCH_PER_GROUP = 5


def _s5_kernel(*refs):
    gs = S5_GROUPS_PER_STEP
    ut_ref, (cm_ref, bb_ref, pw_ref, d_ref, dd_ref) = refs[:gs], refs[gs:gs + 5]
    z_ref, scratch = refs[gs + 5], refs[gs + 6:]
    groups = [scratch[S5_SCRATCH_PER_GROUP * gi:S5_SCRATCH_PER_GROUP * (gi + 1)]
              for gi in range(S5_GROUPS_PER_STEP)]
    for gi, (m_s, n_s, pt_s, ptf_s, rs_s) in enumerate(groups):
        _s5_operators(gi, cm_ref, bb_ref, pw_ref, dd_ref, m_s, n_s, pt_s, ptf_s, rs_s)
    yv = [_s5_chunk_matmuls(gi, ut_ref, m_s, pt_s) for gi, (m_s, _, pt_s, _, _) in enumerate(groups)]
    for gi, (_, n_s, _, _, _) in enumerate(groups):
        _s5_finish(gi, d_ref, z_ref, n_s, *yv[gi])


def _s5_operators(gi, cm_ref, bb_ref, pw_ref, dd_ref, m_s, n_s, pt_s, ptf_s, rs_s):
    tc = CHUNK * S5_GROUP
    cma, cmb = cm_ref[gi,0], cm_ref[gi,1]
    bba, bbb = bb_ref[gi,0], bb_ref[gi,1]
    for t in range(CHUNK):
        rows = slice(S5_GROUP * t, S5_GROUP * (t + 1))
        n_s[rows, :] = (cma * pw_ref[gi,0, t:t + 1, :] + cmb * pw_ref[gi,1, t:t + 1, :]).astype(BF16)
        ptf_s[rows, :] = bba * pw_ref[gi,2, t:t + 1, :] + bbb * pw_ref[gi,3, t:t + 1, :]
    ptf = ptf_s[...]
    pt_s[...] = ptf.astype(BF16)

    ra = _nt_f32(cma[:, 0:128], ptf[:, 0:128])
    rb = _nt_f32(cma[:, 128:256], ptf[:, 128:256])
    zeros = jnp.zeros((S5_GROUP, tc), F32)
    r = (jnp.concatenate([ra, zeros], axis=1)
         + pltpu.roll(jnp.concatenate([rb, zeros], axis=1), (CHUNK - 1) * S5_GROUP, axis=1)
         + jnp.concatenate([zeros[:, 0:tc - 128], dd_ref[gi], zeros], axis=1))
    for k in range(8):
        rk = r if k == 0 else pltpu.roll(r, 2 * tc - S5_GROUP * k, axis=1)
        rs_s[k] = rk[:, 0:RS_LANES].astype(BF16)
    for t in range(CHUNK):
        a, k = divmod(CHUNK - 1 - t, 8)
        m_s[S5_GROUP * t:S5_GROUP * (t + 1), :] = rs_s[k, :, 128 * a:128 * a + tc]


def _s5_chunk_matmuls(gi, ut_ref, m_s, pt_s):
    ut = ut_ref[gi][...].reshape(CHUNK * S5_GROUP, -1)
    nrows = ut.shape[1]
    eye = jnp.where(lax.broadcasted_iota(jnp.int32, (nrows, nrows), 0)
                    == lax.broadcasted_iota(jnp.int32, (nrows, nrows), 1), 1.0, 0.0).astype(BF16)
    u_rows = lax.dot_general(eye, ut, NT_DIMS, preferred_element_type=F32).astype(BF16)
    y = jnp.dot(m_s[...], ut, preferred_element_type=F32)
    v = jnp.dot(u_rows, pt_s[...], preferred_element_type=F32)
    return y, v


def _s5_finish(gi, d_ref, z_ref, n_s, y, v):
    nrows = v.shape[0]
    pos = lax.broadcasted_iota(jnp.int32, (nrows, 128), 0) & (CHUNK - 1)

    def shift(x, s, up):
        if up:
            return jnp.where(pos < CHUNK - s, pltpu.roll(x, nrows - s, axis=0), 0.0)
        return jnp.where(pos >= s, pltpu.roll(x, s, axis=0), 0.0)

    def scan(vh, lanes, up):
        x = shift(vh, 1, up)
        for lvl in range(N_SCAN_LEVELS):
            xs = shift(x, 1 << lvl, up)
            a = d_ref[gi,lvl:lvl + 1, lanes]
            b = d_ref[gi,8 + lvl:9 + lvl, lanes]
            x = x + a * xs + b * pltpu.roll(xs, 64, axis=1)
        return x

    xf = scan(v[:, 0:128], slice(0, 128), False)
    xb = scan(v[:, 128:256], slice(128, 256), True)
    xin = jnp.concatenate([xf, xb], axis=1).astype(BF16)
    y = y + lax.dot_general(n_s[...], xin, NT_DIMS, preferred_element_type=F32)
    z_ref[:, S5_GROUP * gi:S5_GROUP * (gi + 1), :] = _gelu_tanh(y).astype(BF16).reshape(CHUNK, S5_GROUP, -1)


RS_LANES = 128 * 7 + CHUNK * S5_GROUP


def _s5_core(ut, cm, bb, pw, dtab, dd):
    nrows = ut.shape[2]
    tc = CHUNK * S5_GROUP
    gs = S5_GROUPS_PER_STEP
    group_specs = [pl.BlockSpec((CHUNK, S5_GROUP, nrows), functools.partial(lambda i, g: (0, gs * g + i, 0), i))
                   for i in range(gs)]
    return pl.pallas_call(
        _s5_kernel,
        grid=(S5_GROUPS // gs,),
        in_specs=[*group_specs,
                  pl.BlockSpec((gs, 2, S5_GROUP, 256), lambda g: (g, 0, 0, 0)),
                  pl.BlockSpec((gs, 2, S5_GROUP, 256), lambda g: (g, 0, 0, 0)),
                  pl.BlockSpec((gs, 4, CHUNK, 256), lambda g: (g, 0, 0, 0)),
                  pl.BlockSpec((gs, 16, 256), lambda g: (g, 0, 0)),
                  pl.BlockSpec((gs, S5_GROUP, 128), lambda g: (g, 0, 0))],
        out_specs=pl.BlockSpec((CHUNK, gs * S5_GROUP, nrows), lambda g: (0, g, 0)),
        out_shape=jax.ShapeDtypeStruct(ut.shape, BF16),
        scratch_shapes=[pltpu.VMEM((tc, tc), BF16),
                        pltpu.VMEM((tc, 256), BF16),
                        pltpu.VMEM((tc, 256), BF16),
                        pltpu.VMEM((tc, 256), F32),
                        pltpu.VMEM((8, S5_GROUP, RS_LANES), BF16),
                        ] * gs,
        compiler_params=_cparams(1),
        name="s5_core",
    )(*([ut] * gs), cm, bb, pw, dtab, dd)


def _s5_glu_kernel(z_ref, wt_ref, o_ref):
    wt = wt_ref[...]
    nrows = z_ref.shape[2]
    eye = jnp.where(lax.broadcasted_iota(jnp.int32, (nrows, nrows), 0)
                    == lax.broadcasted_iota(jnp.int32, (nrows, nrows), 1), 1.0, 0.0).astype(BF16)
    for j in range(S5T_DT):
        zt = z_ref[j]
        gl = jnp.dot(wt, zt, preferred_element_type=F32)
        o = (zt.astype(F32) * jax.nn.sigmoid(gl)).astype(BF16)
        nat = lax.dot_general(eye, o, NT_DIMS, preferred_element_type=F32)
        o_ref[:, j, :] = nat


def _s5_glu(zt, wglu_t):
    nrows = zt.shape[2]
    return pl.pallas_call(
        _s5_glu_kernel,
        grid=(CHUNK // S5T_DT,),
        in_specs=[pl.BlockSpec((S5T_DT, 512, nrows), lambda i: (i, 0, 0)),
                  _const_spec((512, 512))],
        out_specs=pl.BlockSpec((nrows, S5T_DT, 512), lambda i: (0, i, 0)),
        out_shape=jax.ShapeDtypeStruct((nrows, CHUNK, 512), F32),
        compiler_params=_cparams(1),
        name="s5_glu",
    )(zt, wglu_t)


NA_ROWS_PER_STEP = 32
NA_PAIRS_PER_STEP = NA_ROWS_PER_STEP // 2
NA_WIN_ROWS = NA_KH + 1
NA_WIN_KEYS = NA_WIN_ROWS * GRID_W
NA_WIN_BLKS = -(-NA_WIN_KEYS // TOK_BLK)
NA_PV_KEYS = NA_WIN_BLKS * TOK_BLK
NA_RR_OUTSIDE = 2 * NA_KH - 1


def _na_kernel(rpb_ref, qw_ref, k_ref, vt_ref, oob_ref, o_ref, *scratch):
    st_s, pt_s, bias_s = (scratch[0:4], scratch[4:8]), (scratch[8:12], scratch[12:16]), scratch[16]
    rb = pl.program_id(1)
    lo_q = lax.broadcasted_iota(jnp.int32, (GRID_W, 128), 1) < 64
    same_head2 = ((lax.broadcasted_iota(jnp.int32, (128, 256), 0) < 64)
                  == ((lax.broadcasted_iota(jnp.int32, (128, 256), 1) & 64) == 0))
    esum = jnp.where(lax.broadcasted_iota(jnp.int32, (GRID_W, 128), 0)
                     == (lax.broadcasted_iota(jnp.int32, (GRID_W, 128), 1) & (GRID_W - 1)),
                     1.0, 0.0).astype(BF16)
    half = NA_KH // 2

    kcol = lax.broadcasted_iota(jnp.int32, (GRID_W, 128), 0)
    qcol = lax.broadcasted_iota(jnp.int32, (GRID_W, 128), 1) & (GRID_W - 1)
    qstart = jnp.clip(qcol - NA_KW // 2, 0, GRID_W - NA_KW)
    col_ok = (kcol >= qstart) & (kcol < qstart + NA_KW)

    @pl.when((pl.program_id(0) == 0) & (rb == 0))
    def _build_bias():
        rel = jnp.clip(kcol - qcol + (NA_KW - 1), 0, 2 * NA_KW - 2)

        def rr_body(rr, carry):
            for s in range(4):
                acc = jnp.zeros((GRID_W, 128), F32)
                for j in range(2 * NA_KW - 1):
                    acc = jnp.where(rel == j, jnp.where(lo_q, rpb_ref[2 * s, rr, j], rpb_ref[2 * s + 1, rr, j]), acc)
                bias_s[s, rr] = acc * LOG2E
            return carry

        lax.fori_loop(0, NA_RR_OUTSIDE, rr_body, 0)
        for s in range(4):
            bias_s[s, NA_RR_OUTSIDE] = oob_ref[...]

    def window(pi):
        r0 = rb * NA_ROWS_PER_STEP + 2 * pi
        return r0, jnp.clip(r0 - half, 0, GRID_W - NA_WIN_ROWS)

    def q_stage(pi):
        _, win = window(pi)
        koff = pl.multiple_of(win * GRID_W, GRID_W)
        scs = []
        for s in range(4):
            ch = slice(128 * s, 128 * (s + 1))
            kw = k_ref[s, pl.ds(koff, NA_WIN_KEYS), :]
            scs.append(jnp.dot(kw, qw_ref[pi, ch, :], preferred_element_type=F32))
        return scs

    def s_stage(pi, par, scs):
        r0, win = window(pi)
        poff = pl.multiple_of((win & 1) * GRID_W, GRID_W)
        rr = []
        for p in range(2):
            r = r0 + p
            first = jnp.clip(r - half, 0, GRID_W - NA_KH) - win
            rel = win - r + (NA_KH - 1)
            rr.append([jnp.where((wr >= first) & (wr < first + NA_KH), wr + rel, NA_RR_OUTSIDE)
                       for wr in range(NA_WIN_ROWS)])
        sums = []
        for s in range(4):
            st = st_s[par][s]
            slab_sums = []
            for p in range(2):
                lanes = slice(128 * p, 128 * (p + 1))
                for edge in (0, NA_PV_KEYS - GRID_W):
                    pt_s[par][s][p, edge:edge + GRID_W, :] = jnp.zeros((GRID_W, 128), BF16)
                m = None
                for wr in range(NA_WIN_ROWS):
                    rows = slice(GRID_W * wr, GRID_W * (wr + 1))
                    t = jnp.where(col_ok, scs[s][rows, lanes] + bias_s[s, rr[p][wr]], NEG_INF)
                    st[p, rows, :] = t
                    m = t if m is None else jnp.maximum(m, t)
                m = jnp.max(m, axis=0, keepdims=True)
                l = None
                for wr in range(NA_WIN_ROWS):
                    rows = slice(GRID_W * wr, GRID_W * (wr + 1))
                    e = jnp.exp2(st[p, rows, :] - m)
                    pt_s[par][s][p, pl.ds(poff + GRID_W * wr, GRID_W), :] = e.astype(BF16)
                    l = e if l is None else l + e
                slab_sums.append(jnp.sum(l, axis=0, keepdims=True))
            sums.append(jnp.concatenate(slab_sums, axis=1))
        return tuple(sums)

    def o_stage_pv(pi, par, sums):
        _, win = window(pi)
        blk0 = lax.shift_right_logical(win, 1)
        ots = []
        for s in range(4):
            ch = slice(128 * s, 128 * (s + 1))
            vw = jnp.concatenate([vt_ref[blk0 + i, ch, :] for i in range(NA_WIN_BLKS)], axis=1)
            pt = jnp.concatenate([pt_s[par][s][0], pt_s[par][s][1]], axis=1)
            ot = jnp.dot(vw, pt, preferred_element_type=F32)
            ots.append(jnp.where(same_head2, (ot / sums[s]).astype(BF16), jnp.zeros((), BF16)))
        return ots

    def o_stage_store(pi, ots):
        for s in range(4):
            ch = slice(128 * s, 128 * (s + 1))
            for p in range(2):
                nat = lax.dot_general(esum, ots[s][:, 128 * p:128 * (p + 1)], NT_DIMS,
                                      preferred_element_type=F32)
                qoff = pl.multiple_of((2 * pi + p) * GRID_W, GRID_W)
                o_ref[s, pl.ds(qoff, GRID_W), :] = nat.astype(BF16)

    def step(j, prev_sums, has_prev=True):
        a, b = 2 * j, 2 * j + 1
        scs_a = q_stage(a)
        if has_prev:
            ots_a = o_stage_pv(a - 2, 0, prev_sums[0])
            ots_b = o_stage_pv(b - 2, 1, prev_sums[1])
        scs_b = q_stage(b)
        if has_prev:
            o_stage_store(a - 2, ots_a)
            o_stage_store(b - 2, ots_b)
        return s_stage(a, 0, scs_a), s_stage(b, 1, scs_b)

    sums = step(0, None, has_prev=False)
    sums = lax.fori_loop(1, NA_PAIRS_PER_STEP // 2, step, sums)
    for u in range(2):
        pi = NA_PAIRS_PER_STEP - 2 + u
        o_stage_store(pi, o_stage_pv(pi, u, sums[u]))


def _na(rpb, qw, k, vt, b, s):
    tq = NA_ROWS_PER_STEP * GRID_W
    blks = s // TOK_BLK
    steps = s // tq
    oob = jnp.full((GRID_W, 128), -jnp.inf, F32)
    return pl.pallas_call(
        _na_kernel,
        grid=(b, s // tq),
        in_specs=[pl.BlockSpec(memory_space=pltpu.SMEM),
                  pl.BlockSpec((NA_PAIRS_PER_STEP, 512, 2 * TOK_BLK),
                               lambda bi, ri: (bi * (blks // NA_PAIRS_PER_STEP) + ri, 0, 0)),
                  pl.BlockSpec((4, s, 128), lambda bi, ri: (0, bi, 0)),
                  pl.BlockSpec((blks, 512, TOK_BLK), lambda bi, ri: (bi, 0, 0)),
                  _const_spec(oob.shape)],
        out_specs=pl.BlockSpec((4, tq, 128), lambda bi, ri: (0, bi * steps + ri, 0)),
        out_shape=jax.ShapeDtypeStruct((4, b * s, 128), BF16),
        scratch_shapes=([pltpu.VMEM((2, NA_WIN_KEYS, 128), F32)] * 8
                        + [pltpu.VMEM((2, NA_PV_KEYS, 128), BF16)] * 8
                        + [pltpu.VMEM((NA_HEADS // 2, 2 * NA_KH, GRID_W, 128), F32)]),
        compiler_params=_cparams(2),
        name="na",
    )(rpb, qw, k, vt, oob)


def _mem_kernel(q_ref, mem_ref, g_ref, w_ref, o_ref, kv_s):
    @pl.when(pl.program_id(1) == 0)
    def _():
        mn = _rms(mem_ref[0], g_ref[...]).astype(BF16)
        kv_s[...] = jnp.dot(mn, w_ref[...].astype(BF16), preferred_element_type=F32).astype(BF16)

    scale = MEM_HEAD_DIM ** -0.5
    for h in range(MEM_HEADS):
        cols = slice(MEM_HEAD_DIM * h, MEM_HEAD_DIM * (h + 1))
        vcols = slice(512 + MEM_HEAD_DIM * h, 512 + MEM_HEAD_DIM * (h + 1))
        sc = lax.dot_general(q_ref[0, :, cols], kv_s[:, cols], NT_DIMS,
                             preferred_element_type=F32) * scale
        m = jnp.max(sc, axis=-1, keepdims=True)
        p = jnp.exp(sc - m)
        l = jnp.sum(p, axis=-1, keepdims=True)
        o = jnp.dot(p.astype(BF16), kv_s[:, vcols], preferred_element_type=F32)
        o_ref[0, :, cols] = (o / l).astype(BF16)


def _mem_attn(q, mem, g, w):
    b, s, _ = q.shape
    m = mem.shape[1]
    tq = 4096
    return pl.pallas_call(
        _mem_kernel,
        grid=(b, s // tq),
        in_specs=[pl.BlockSpec((1, tq, 512), lambda bi, i: (bi, i, 0)),
                  pl.BlockSpec((1, m, D), lambda bi, i: (bi, 0, 0)),
                  _const_spec((1, D)),
                  _const_spec((D, 1024))],
        out_specs=pl.BlockSpec((1, tq, 512), lambda bi, i: (bi, i, 0)),
        out_shape=jax.ShapeDtypeStruct((b, s, 512), BF16),
        scratch_shapes=[pltpu.VMEM((m, 1024), BF16)],
        compiler_params=_cparams(2),
        name="mem_attn",
    )(q, mem, g, w)


def _merge_kernel(x_ref, g_ref, yna_ref, ys5_ref, ymem_ref, wg_ref, bg_ref, wb_ref, wo_ref, o_ref):
    x = x_ref[...]
    h = _rms(x, g_ref[...]).astype(BF16)
    merged = None
    y_na = jnp.concatenate([yna_ref[s] for s in range(4)], axis=1)
    ys = (y_na, ys5_ref[...].astype(BF16), ymem_ref[...])
    for b, y in enumerate(ys):
        cols = slice(D * b, D * (b + 1))
        gate = jax.nn.sigmoid(jnp.dot(h, wg_ref[:, cols].astype(BF16), preferred_element_type=F32)
                              + bg_ref[:, cols])
        up = jnp.dot(y, wb_ref[b], preferred_element_type=F32)
        merged = gate * up if merged is None else merged + gate * up
    o_ref[...] = x + jnp.dot(merged.astype(BF16), wo_ref[...], preferred_element_type=F32)


def _merge(x2d, g, yna, ys5, ymem, wg, bg, wb, wo):
    n = x2d.shape[0]
    tm = 1024
    yspec = pl.BlockSpec((tm, 512), lambda i: (i, 0))
    return pl.pallas_call(
        _merge_kernel,
        grid=(n // tm,),
        in_specs=[pl.BlockSpec((tm, D), lambda i: (i, 0)),
                  _const_spec((1, D)),
                  pl.BlockSpec((4, tm, 128), lambda i: (0, i, 0)), yspec, yspec,
                  _const_spec((D, 3 * D)),
                  _const_spec((1, 3 * D)),
                  _const_spec((3, 512, D)),
                  _const_spec((D, D))],
        out_specs=pl.BlockSpec((tm, D), lambda i: (i, 0)),
        out_shape=jax.ShapeDtypeStruct((n, D), F32),
        compiler_params=_cparams(1),
        name="merge",
    )(x2d, g, yna, ys5, ymem, wg, bg, wb, wo)


def _ffn_kernel(x_ref, g_ref, gf_ref, w1_ref, w3_ref, w2_ref, o_ref):
    x = x_ref[...]
    h = _rms(x, g_ref[...]).astype(BF16)
    a = jnp.dot(h, w1_ref[...], preferred_element_type=F32)
    c = jnp.dot(h, w3_ref[...], preferred_element_type=F32)
    mid = (a * jax.nn.sigmoid(a) * c).astype(BF16)
    x2 = x + jnp.dot(mid, w2_ref[...].astype(BF16), preferred_element_type=F32)
    o_ref[...] = _rms(x2, gf_ref[...])


def _ffn(x2d, g, gf, w1, w3, w2):
    n = x2d.shape[0]
    tm = 512
    return pl.pallas_call(
        _ffn_kernel,
        grid=(n // tm,),
        in_specs=[pl.BlockSpec((tm, D), lambda i: (i, 0)),
                  _const_spec((1, D)),
                  _const_spec((1, D)),
                  _const_spec((D, D_FF)),
                  _const_spec((D, D_FF)),
                  _const_spec((D_FF, D))],
        out_specs=pl.BlockSpec((tm, D), lambda i: (i, 0)),
        out_shape=jax.ShapeDtypeStruct((n, D), F32),
        compiler_params=_cparams(1),
        name="ffn",
    )(x2d, g, gf, w1, w3, w2)


def _s5_tables(a_re, a_im, log_dt, b_re, b_im, c_re, c_im, s5_d):
    t = CHUNK
    ar, ai = a_re.astype(F32), a_im.astype(F32)
    dt = jnp.exp(log_dt.astype(F32))[..., None]
    lr, li = ar * dt, ai * dt
    mag = jnp.exp(lr)
    lbr, lbi = mag * jnp.cos(li), mag * jnp.sin(li)
    den = ar * ar + ai * ai
    rr = ((lbr - 1.0) * ar + lbi * ai) / den
    ri = (lbi * ar - (lbr - 1.0) * ai) / den
    br, bi = b_re.astype(F32), b_im.astype(F32)
    bbr = rr[..., None] * br - ri[..., None] * bi
    bbi = rr[..., None] * bi + ri[..., None] * br
    cmr, cmi = c_re.astype(F32), c_im.astype(F32)
    tau = jnp.arange(t + 1, dtype=F32)[:, None, None, None]
    pmag = jnp.exp(tau * lr[None])
    pwr, pwi = pmag * jnp.cos(tau * li[None]), pmag * jnp.sin(tau * li[None])

    dd = jnp.pad(jnp.eye(S5_GROUP, dtype=F32)[None] * s5_d.astype(F32).reshape(S5_GROUPS, S5_GROUP, 1),
                 ((0, 0), (0, 0), (128 - S5_GROUP, 0)))

    def lanes4(f0, f1, b0, b1):
        return jnp.concatenate([f0, f1, b0, b1], axis=-1)

    cm = jnp.stack([lanes4(cmr[0], -cmi[0], cmr[1], -cmi[1]),
                    lanes4(-cmi[0], -cmr[0], -cmi[1], -cmr[1])], axis=1)
    fr, fi = jnp.moveaxis(pwr[1:t + 1, 0], 0, 1), jnp.moveaxis(pwi[1:t + 1, 0], 0, 1)
    gr, gi = jnp.moveaxis(pwr[1:t + 1][::-1, 1], 0, 1), jnp.moveaxis(pwi[1:t + 1][::-1, 1], 0, 1)
    bt = lambda z: jnp.swapaxes(z, -1, -2)
    bb = jnp.stack([lanes4(bt(bbr[0]), bt(bbi[0]), bt(bbr[1]), bt(bbi[1])),
                    lanes4(-bt(bbi[0]), bt(bbr[0]), -bt(bbi[1]), bt(bbr[1]))], axis=1)
    wr, wi = jnp.moveaxis(pwr[:t][::-1, 0], 0, 1), jnp.moveaxis(pwi[:t][::-1, 0], 0, 1)
    vr, vi = jnp.moveaxis(pwr[:t, 1], 0, 1), jnp.moveaxis(pwi[:t, 1], 0, 1)
    pw = jnp.stack([lanes4(fr, fr, gr, gr), lanes4(fi, fi, gi, gi),
                    lanes4(wr, wr, vr, vr), lanes4(wi, wi, vi, vi)], axis=1)

    dr, di = pwr[t], pwi[t]
    a_rows, b_rows = [], []
    for _ in range(N_SCAN_LEVELS):
        a_rows.append(lanes4(dr[0], dr[0], dr[1], dr[1]))
        b_rows.append(lanes4(-di[0], di[0], -di[1], di[1]))
        dr, di = dr * dr - di * di, 2.0 * dr * di
    pad = [jnp.zeros_like(a_rows[0])] * (8 - N_SCAN_LEVELS)
    dtab = jnp.stack(a_rows + pad + b_rows + pad, axis=1)
    return cm, bb, pw, dtab, dd


def kernel(x, mem, g_mix, g_mem, g_ffn, g_final, w_in, w_gate, b_gate, rpb, w_mem_kv,
           a_re, a_im, log_dt, b_re, b_im, c_re, c_im, s5_d, w_glu, w_branch, w_o,
           w_ffn1, w_ffn3, w_ffn2):
    bsz, s, d = x.shape
    n = bsz * s
    x2d = x.reshape(n, d)
    gm = g_mix[0].reshape(1, d).astype(F32)

    w_in_t = w_in[0].T.astype(BF16)
    cm_tab, bb_tab, pw_tab, dtab, dd_tab = _s5_tables(a_re[0], a_im[0], log_dt[0], b_re[0], b_im[0],
                                                      c_re[0], c_im[0], s5_d[0])

    k, qm, qw, vt = _proj(x2d, gm, w_in[0], w_in_t)

    ut = _proj_s5t(x.reshape(n // CHUNK, CHUNK, d), gm, w_in_t)
    zt = _s5_core(ut, cm_tab, bb_tab, pw_tab, dtab, dd_tab)
    y_s5 = _s5_glu(zt, w_glu[0].T.astype(BF16)).reshape(n, 512)

    y_na = _na(rpb[0].astype(F32), qw, k, vt, bsz, s)

    y_mem = _mem_attn(qm.reshape(bsz, s, 512), mem, g_mem[0].reshape(1, d).astype(F32),
                      w_mem_kv[0]).reshape(n, 512)

    x1 = _merge(x2d, gm, y_na, y_s5, y_mem, w_gate[0],
                b_gate[0].reshape(1, 3 * d).astype(F32), w_branch[0].astype(BF16), w_o[0].astype(BF16))
    out = _ffn(x1, g_ffn[0].reshape(1, d).astype(F32), g_final.reshape(1, d).astype(F32),
               w_ffn1[0].astype(BF16), w_ffn3[0].astype(BF16), w_ffn2[0])
    return out.reshape(bsz, s, d)
```

```python
import functools

import jax
import jax.numpy as jnp
from jax import lax
from jax.experimental import pallas as pl
from jax.experimental.pallas import tpu as pltpu

F32 = jnp.float32
BF16 = jnp.bfloat16

D = 1024
GRID_W = 64
NA_HEADS = 8
NA_KH = 8
NA_KW = 16
S5_GROUPS = 32
S5_GROUP = 16
S5_STATE = 64
CHUNK = 64
MEM_HEADS = 4
MEM_HEAD_DIM = 128
D_FF = 2816
EPS = 1e-6
NEG_INF = -1e30

VMEM_LIMIT = 56 * 1024 * 1024

NT_DIMS = (((1,), (1,)), ((), ()))


def _cparams(n_axes):
    return pltpu.CompilerParams(
        dimension_semantics=("arbitrary",) * n_axes,
        vmem_limit_bytes=VMEM_LIMIT)


def _rms(x, g):
    return x * lax.rsqrt(jnp.mean(x * x, axis=-1, keepdims=True) + EPS) * g


def _const_spec(shape):
    nd = len(shape)
    return pl.BlockSpec(shape, lambda *_: (0,) * nd, pipeline_mode=pl.Buffered(1))


TOK_BLK = 128


LOG2E = 1.4426950408889634
NA_Q_SCALE = 64 ** -0.5 * LOG2E


def _proj_kernel(x_ref, g_ref, wk_ref, wqm_ref, wqt_ref, wvt_ref, k_ref, qm_ref, qw_ref, vt_ref):
    h = _rms(x_ref[...], g_ref[...]).astype(BF16)
    kk = jnp.dot(h, wk_ref[...].astype(BF16), preferred_element_type=F32).astype(BF16)
    for s in range(4):
        k_ref[s] = kk[:, 128 * s:128 * (s + 1)]
    qm_ref[...] = jnp.dot(h, wqm_ref[...].astype(BF16), preferred_element_type=F32).astype(BF16)
    qt = lax.dot_general(wqt_ref[...], h, NT_DIMS, preferred_element_type=F32) * NA_Q_SCALE
    vt = lax.dot_general(wvt_ref[...], h, NT_DIMS, preferred_element_type=F32)
    lo = lax.broadcasted_iota(jnp.int32, (128, 128), 1) < 64
    same_head = (lax.broadcasted_iota(jnp.int32, (128, 128), 0) < 64) == lo
    for i in range(x_ref.shape[0] // TOK_BLK):
        toks = slice(TOK_BLK * i, TOK_BLK * (i + 1))
        vt_ref[i] = vt[:, toks].astype(BF16)
        for s in range(4):
            a = qt[128 * s:128 * (s + 1), toks]
            ar = pltpu.roll(a, 64, axis=1)
            qw_ref[i, 128 * s:128 * (s + 1), :] = jnp.concatenate(
                [jnp.where(same_head, jnp.where(lo, a, ar), 0.0),
                 jnp.where(same_head, jnp.where(lo, ar, a), 0.0)], axis=1).astype(BF16)


W_IN_BLK = 512


def _w_in_spec(blk, transposed):
    if transposed:
        return pl.BlockSpec((W_IN_BLK, D), lambda *_: (blk, 0), pipeline_mode=pl.Buffered(1))
    return pl.BlockSpec((D, W_IN_BLK), lambda *_: (0, blk), pipeline_mode=pl.Buffered(1))


def _proj(x2d, g, w, wt):
    n = x2d.shape[0]
    tm = 1024
    nat = jax.ShapeDtypeStruct((n, 512), BF16)
    nat_spec = pl.BlockSpec((tm, 512), lambda i: (i, 0))
    chm = lambda width: (jax.ShapeDtypeStruct((n // TOK_BLK, 512, width), BF16),
                         pl.BlockSpec((tm // TOK_BLK, 512, width), lambda i: (i, 0, 0)))
    (qw_shape, qw_spec), (vt_shape, vt_spec) = chm(2 * TOK_BLK), chm(TOK_BLK)
    return pl.pallas_call(
        _proj_kernel,
        grid=(n // tm,),
        in_specs=[pl.BlockSpec((tm, D), lambda i: (i, 0)),
                  _const_spec((1, D)),
                  _w_in_spec(1, False),
                  _w_in_spec(4, False),
                  _w_in_spec(0, True),
                  _w_in_spec(2, True)],
        out_specs=[pl.BlockSpec((4, tm, 128), lambda i: (0, i, 0)), nat_spec, qw_spec, vt_spec],
        out_shape=[jax.ShapeDtypeStruct((4, n, 128), BF16), nat, qw_shape, vt_shape],
        compiler_params=_cparams(1),
        name="proj",
    )(x2d, g, w, w, wt, wt)


S5T_DT = 8


def _proj_s5t_kernel(x_ref, g_ref, wt_ref, o_ref, h_s):
    wt = wt_ref[...]
    nrows = x_ref.shape[0]
    hn = _rms(x_ref[...], g_ref[...]).reshape(nrows * S5T_DT, D)
    for c in range(D // 128):
        h_s[c] = hn[:, 128 * c:128 * (c + 1)]
    for j in range(S5T_DT):
        h = jnp.concatenate([h_s[c, pl.ds(j, nrows, stride=S5T_DT), :] for c in range(D // 128)],
                            axis=1).astype(BF16)
        ut = lax.dot_general(wt, h, NT_DIMS, preferred_element_type=F32)
        o_ref[j] = ut.astype(BF16)


def _proj_s5t(xc, g, wt):
    nrows = xc.shape[0]
    return pl.pallas_call(
        _proj_s5t_kernel,
        grid=(CHUNK // S5T_DT,),
        in_specs=[pl.BlockSpec((nrows, S5T_DT, D), lambda i: (0, i, 0)),
                  _const_spec((1, D)),
                  _w_in_spec(3, True)],
        out_specs=pl.BlockSpec((S5T_DT, 512, nrows), lambda i: (i, 0, 0)),
        out_shape=jax.ShapeDtypeStruct((CHUNK, 512, nrows), BF16),
        scratch_shapes=[pltpu.VMEM((D // 128, nrows * S5T_DT, 128), F32)],
        compiler_params=_cparams(1),
        name="proj_s5t",
    )(xc, g, wt)


N_SCAN_LEVELS = 6


def _gelu_tanh(x):
    c = 0.7978845608028654
    return 0.5 * x * (1.0 + jnp.tanh(c * (x + 0.044715 * (x * x * x))))


def _split_bf16(x):
    hi = x.astype(BF16)
    return hi, (x - hi.astype(F32)).astype(BF16)


def _nt_f32(a, b):
    ah, al = _split_bf16(a)
    bh, bl = _split_bf16(b)
    nt = lambda u, w: lax.dot_general(u, w, NT_DIMS, preferred_element_type=F32)
    return nt(ah, bh) + nt(ah, bl) + nt(al, bh)


S5_GROUPS_PER_STEP = 4
S5_SCRATCH_PER_GROUP = 5


def _s5_kernel(*refs):
    gs = S5_GROUPS_PER_STEP
    ut_ref, (cm_ref, bb_ref, pw_ref, d_ref, dd_ref) = refs[:gs], refs[gs:gs + 5]
    z_ref, scratch = refs[gs + 5], refs[gs + 6:]
    groups = [scratch[S5_SCRATCH_PER_GROUP * gi:S5_SCRATCH_PER_GROUP * (gi + 1)]
              for gi in range(S5_GROUPS_PER_STEP)]
    for gi, (m_s, n_s, pt_s, ptf_s, rs_s) in enumerate(groups):
        _s5_operators(gi, cm_ref, bb_ref, pw_ref, dd_ref, m_s, n_s, pt_s, ptf_s, rs_s)
    yv = [_s5_chunk_matmuls(gi, ut_ref, m_s, pt_s) for gi, (m_s, _, pt_s, _, _) in enumerate(groups)]
    for gi, (_, n_s, _, _, _) in enumerate(groups):
        _s5_finish(gi, d_ref, z_ref, n_s, *yv[gi])


def _s5_operators(gi, cm_ref, bb_ref, pw_ref, dd_ref, m_s, n_s, pt_s, ptf_s, rs_s):
    tc = CHUNK * S5_GROUP
    cma, cmb = cm_ref[gi,0], cm_ref[gi,1]
    bba, bbb = bb_ref[gi,0], bb_ref[gi,1]
    for t in range(CHUNK):
        rows = slice(S5_GROUP * t, S5_GROUP * (t + 1))
        n_s[rows, :] = (cma * pw_ref[gi,0, t:t + 1, :] + cmb * pw_ref[gi,1, t:t + 1, :]).astype(BF16)
        ptf_s[rows, :] = bba * pw_ref[gi,2, t:t + 1, :] + bbb * pw_ref[gi,3, t:t + 1, :]
    ptf = ptf_s[...]
    pt_s[...] = ptf.astype(BF16)

    ra = _nt_f32(cma[:, 0:128], ptf[:, 0:128])
    rb = _nt_f32(cma[:, 128:256], ptf[:, 128:256])
    zeros = jnp.zeros((S5_GROUP, tc), F32)
    r = (jnp.concatenate([ra, zeros], axis=1)
         + pltpu.roll(jnp.concatenate([rb, zeros], axis=1), (CHUNK - 1) * S5_GROUP, axis=1)
         + jnp.concatenate([zeros[:, 0:tc - 128], dd_ref[gi], zeros], axis=1))
    for k in range(8):
        rk = r if k == 0 else pltpu.roll(r, 2 * tc - S5_GROUP * k, axis=1)
        rs_s[k] = rk[:, 0:RS_LANES].astype(BF16)
    for t in range(CHUNK):
        a, k = divmod(CHUNK - 1 - t, 8)
        m_s[S5_GROUP * t:S5_GROUP * (t + 1), :] = rs_s[k, :, 128 * a:128 * a + tc]


def _s5_chunk_matmuls(gi, ut_ref, m_s, pt_s):
    ut = ut_ref[gi][...].reshape(CHUNK * S5_GROUP, -1)
    nrows = ut.shape[1]
    eye = jnp.where(lax.broadcasted_iota(jnp.int32, (nrows, nrows), 0)
                    == lax.broadcasted_iota(jnp.int32, (nrows, nrows), 1), 1.0, 0.0).astype(BF16)
    u_rows = lax.dot_general(eye, ut, NT_DIMS, preferred_element_type=F32).astype(BF16)
    y = jnp.dot(m_s[...], ut, preferred_element_type=F32)
    v = jnp.dot(u_rows, pt_s[...], preferred_element_type=F32)
    return y, v


def _s5_finish(gi, d_ref, z_ref, n_s, y, v):
    nrows = v.shape[0]
    pos = lax.broadcasted_iota(jnp.int32, (nrows, 128), 0) & (CHUNK - 1)

    def shift(x, s, up):
        if up:
            return jnp.where(pos < CHUNK - s, pltpu.roll(x, nrows - s, axis=0), 0.0)
        return jnp.where(pos >= s, pltpu.roll(x, s, axis=0), 0.0)

    def scan(vh, lanes, up):
        x = shift(vh, 1, up)
        for lvl in range(N_SCAN_LEVELS):
            xs = shift(x, 1 << lvl, up)
            a = d_ref[gi,lvl:lvl + 1, lanes]
            b = d_ref[gi,8 + lvl:9 + lvl, lanes]
            x = x + a * xs + b * pltpu.roll(xs, 64, axis=1)
        return x

    xf = scan(v[:, 0:128], slice(0, 128), False)
    xb = scan(v[:, 128:256], slice(128, 256), True)
    xin = jnp.concatenate([xf, xb], axis=1).astype(BF16)
    y = y + lax.dot_general(n_s[...], xin, NT_DIMS, preferred_element_type=F32)
    z_ref[:, S5_GROUP * gi:S5_GROUP * (gi + 1), :] = _gelu_tanh(y).astype(BF16).reshape(CHUNK, S5_GROUP, -1)


RS_LANES = 128 * 7 + CHUNK * S5_GROUP


def _s5_core(ut, cm, bb, pw, dtab, dd):
    nrows = ut.shape[2]
    tc = CHUNK * S5_GROUP
    gs = S5_GROUPS_PER_STEP
    group_specs = [pl.BlockSpec((CHUNK, S5_GROUP, nrows), functools.partial(lambda i, g: (0, gs * g + i, 0), i))
                   for i in range(gs)]
    return pl.pallas_call(
        _s5_kernel,
        grid=(S5_GROUPS // gs,),
        in_specs=[*group_specs,
                  pl.BlockSpec((gs, 2, S5_GROUP, 256), lambda g: (g, 0, 0, 0)),
                  pl.BlockSpec((gs, 2, S5_GROUP, 256), lambda g: (g, 0, 0, 0)),
                  pl.BlockSpec((gs, 4, CHUNK, 256), lambda g: (g, 0, 0, 0)),
                  pl.BlockSpec((gs, 16, 256), lambda g: (g, 0, 0)),
                  pl.BlockSpec((gs, S5_GROUP, 128), lambda g: (g, 0, 0))],
        out_specs=pl.BlockSpec((CHUNK, gs * S5_GROUP, nrows), lambda g: (0, g, 0)),
        out_shape=jax.ShapeDtypeStruct(ut.shape, BF16),
        scratch_shapes=[pltpu.VMEM((tc, tc), BF16),
                        pltpu.VMEM((tc, 256), BF16),
                        pltpu.VMEM((tc, 256), BF16),
                        pltpu.VMEM((tc, 256), F32),
                        pltpu.VMEM((8, S5_GROUP, RS_LANES), BF16),
                        ] * gs,
        compiler_params=_cparams(1),
        name="s5_core",
    )(*([ut] * gs), cm, bb, pw, dtab, dd)


def _s5_glu_kernel(z_ref, wt_ref, o_ref):
    wt = wt_ref[...]
    nrows = z_ref.shape[2]
    eye = jnp.where(lax.broadcasted_iota(jnp.int32, (nrows, nrows), 0)
                    == lax.broadcasted_iota(jnp.int32, (nrows, nrows), 1), 1.0, 0.0).astype(BF16)
    for j in range(S5T_DT):
        zt = z_ref[j]
        gl = jnp.dot(wt, zt, preferred_element_type=F32)
        o = (zt.astype(F32) * jax.nn.sigmoid(gl)).astype(BF16)
        nat = lax.dot_general(eye, o, NT_DIMS, preferred_element_type=F32)
        o_ref[:, j, :] = nat


def _s5_glu(zt, wglu_t):
    nrows = zt.shape[2]
    return pl.pallas_call(
        _s5_glu_kernel,
        grid=(CHUNK // S5T_DT,),
        in_specs=[pl.BlockSpec((S5T_DT, 512, nrows), lambda i: (i, 0, 0)),
                  _const_spec((512, 512))],
        out_specs=pl.BlockSpec((nrows, S5T_DT, 512), lambda i: (0, i, 0)),
        out_shape=jax.ShapeDtypeStruct((nrows, CHUNK, 512), F32),
        compiler_params=_cparams(1),
        name="s5_glu",
    )(zt, wglu_t)


NA_ROWS_PER_STEP = 32
NA_PAIRS_PER_STEP = NA_ROWS_PER_STEP // 2
NA_WIN_ROWS = NA_KH + 1
NA_WIN_KEYS = NA_WIN_ROWS * GRID_W
NA_WIN_BLKS = -(-NA_WIN_KEYS // TOK_BLK)
NA_PV_KEYS = NA_WIN_BLKS * TOK_BLK
NA_RR_OUTSIDE = 2 * NA_KH - 1


def _na_kernel(rpb_ref, qw_ref, k_ref, vt_ref, oob_ref, o_ref, *scratch):
    st_s, pt_s, bias_s = (scratch[0:4], scratch[4:8]), (scratch[8:12], scratch[12:16]), scratch[16]
    rb = pl.program_id(1)
    lo_q = lax.broadcasted_iota(jnp.int32, (GRID_W, 128), 1) < 64
    same_head2 = ((lax.broadcasted_iota(jnp.int32, (128, 256), 0) < 64)
                  == ((lax.broadcasted_iota(jnp.int32, (128, 256), 1) & 64) == 0))
    esum = jnp.where(lax.broadcasted_iota(jnp.int32, (GRID_W, 128), 0)
                     == (lax.broadcasted_iota(jnp.int32, (GRID_W, 128), 1) & (GRID_W - 1)),
                     1.0, 0.0).astype(BF16)
    half = NA_KH // 2

    kcol = lax.broadcasted_iota(jnp.int32, (GRID_W, 128), 0)
    qcol = lax.broadcasted_iota(jnp.int32, (GRID_W, 128), 1) & (GRID_W - 1)
    qstart = jnp.clip(qcol - NA_KW // 2, 0, GRID_W - NA_KW)
    col_ok = (kcol >= qstart) & (kcol < qstart + NA_KW)

    @pl.when((pl.program_id(0) == 0) & (rb == 0))
    def _build_bias():
        rel = jnp.clip(kcol - qcol + (NA_KW - 1), 0, 2 * NA_KW - 2)

        def rr_body(rr, carry):
            for s in range(4):
                acc = jnp.zeros((GRID_W, 128), F32)
                for j in range(2 * NA_KW - 1):
                    acc = jnp.where(rel == j, jnp.where(lo_q, rpb_ref[2 * s, rr, j], rpb_ref[2 * s + 1, rr, j]), acc)
                bias_s[s, rr] = acc * LOG2E
            return carry

        lax.fori_loop(0, NA_RR_OUTSIDE, rr_body, 0)
        for s in range(4):
            bias_s[s, NA_RR_OUTSIDE] = oob_ref[...]

    def window(pi):
        r0 = rb * NA_ROWS_PER_STEP + 2 * pi
        return r0, jnp.clip(r0 - half, 0, GRID_W - NA_WIN_ROWS)

    def q_stage(pi):
        _, win = window(pi)
        koff = pl.multiple_of(win * GRID_W, GRID_W)
        scs = []
        for s in range(4):
            ch = slice(128 * s, 128 * (s + 1))
            kw = k_ref[s, pl.ds(koff, NA_WIN_KEYS), :]
            scs.append(jnp.dot(kw, qw_ref[pi, ch, :], preferred_element_type=F32))
        return scs

    def s_stage(pi, par, scs):
        r0, win = window(pi)
        poff = pl.multiple_of((win & 1) * GRID_W, GRID_W)
        rr = []
        for p in range(2):
            r = r0 + p
            first = jnp.clip(r - half, 0, GRID_W - NA_KH) - win
            rel = win - r + (NA_KH - 1)
            rr.append([jnp.where((wr >= first) & (wr < first + NA_KH), wr + rel, NA_RR_OUTSIDE)
                       for wr in range(NA_WIN_ROWS)])
        sums = []
        for s in range(4):
            st = st_s[par][s]
            slab_sums = []
            for p in range(2):
                lanes = slice(128 * p, 128 * (p + 1))
                for edge in (0, NA_PV_KEYS - GRID_W):
                    pt_s[par][s][p, edge:edge + GRID_W, :] = jnp.zeros((GRID_W, 128), BF16)
                m = None
                for wr in range(NA_WIN_ROWS):
                    rows = slice(GRID_W * wr, GRID_W * (wr + 1))
                    t = jnp.where(col_ok, scs[s][rows, lanes] + bias_s[s, rr[p][wr]], NEG_INF)
                    st[p, rows, :] = t
                    m = t if m is None else jnp.maximum(m, t)
                m = jnp.max(m, axis=0, keepdims=True)
                l = None
                for wr in range(NA_WIN_ROWS):
                    rows = slice(GRID_W * wr, GRID_W * (wr + 1))
                    e = jnp.exp2(st[p, rows, :] - m)
                    pt_s[par][s][p, pl.ds(poff + GRID_W * wr, GRID_W), :] = e.astype(BF16)
                    l = e if l is None else l + e
                slab_sums.append(jnp.sum(l, axis=0, keepdims=True))
            sums.append(jnp.concatenate(slab_sums, axis=1))
        return tuple(sums)

    def o_stage_pv(pi, par, sums):
        _, win = window(pi)
        blk0 = lax.shift_right_logical(win, 1)
        ots = []
        for s in range(4):
            ch = slice(128 * s, 128 * (s + 1))
            vw = jnp.concatenate([vt_ref[blk0 + i, ch, :] for i in range(NA_WIN_BLKS)], axis=1)
            pt = jnp.concatenate([pt_s[par][s][0], pt_s[par][s][1]], axis=1)
            ot = jnp.dot(vw, pt, preferred_element_type=F32)
            ots.append(jnp.where(same_head2, (ot / sums[s]).astype(BF16), jnp.zeros((), BF16)))
        return ots

    def o_stage_store(pi, ots):
        for s in range(4):
            ch = slice(128 * s, 128 * (s + 1))
            for p in range(2):
                nat = lax.dot_general(esum, ots[s][:, 128 * p:128 * (p + 1)], NT_DIMS,
                                      preferred_element_type=F32)
                qoff = pl.multiple_of((2 * pi + p) * GRID_W, GRID_W)
                o_ref[s, pl.ds(qoff, GRID_W), :] = nat.astype(BF16)

    def step(j, prev_sums, has_prev=True):
        a, b = 2 * j, 2 * j + 1
        scs_a = q_stage(a)
        if has_prev:
            ots_a = o_stage_pv(a - 2, 0, prev_sums[0])
            ots_b = o_stage_pv(b - 2, 1, prev_sums[1])
        scs_b = q_stage(b)
        if has_prev:
            o_stage_store(a - 2, ots_a)
            o_stage_store(b - 2, ots_b)
        return s_stage(a, 0, scs_a), s_stage(b, 1, scs_b)

    sums = step(0, None, has_prev=False)
    sums = lax.fori_loop(1, NA_PAIRS_PER_STEP // 2, step, sums)
    for u in range(2):
        pi = NA_PAIRS_PER_STEP - 2 + u
        o_stage_store(pi, o_stage_pv(pi, u, sums[u]))


def _na(rpb, qw, k, vt, b, s):
    tq = NA_ROWS_PER_STEP * GRID_W
    blks = s // TOK_BLK
    steps = s // tq
    oob = jnp.full((GRID_W, 128), -jnp.inf, F32)
    return pl.pallas_call(
        _na_kernel,
        grid=(b, s // tq),
        in_specs=[pl.BlockSpec(memory_space=pltpu.SMEM),
                  pl.BlockSpec((NA_PAIRS_PER_STEP, 512, 2 * TOK_BLK),
                               lambda bi, ri: (bi * (blks // NA_PAIRS_PER_STEP) + ri, 0, 0)),
                  pl.BlockSpec((4, s, 128), lambda bi, ri: (0, bi, 0)),
                  pl.BlockSpec((blks, 512, TOK_BLK), lambda bi, ri: (bi, 0, 0)),
                  _const_spec(oob.shape)],
        out_specs=pl.BlockSpec((4, tq, 128), lambda bi, ri: (0, bi * steps + ri, 0)),
        out_shape=jax.ShapeDtypeStruct((4, b * s, 128), BF16),
        scratch_shapes=([pltpu.VMEM((2, NA_WIN_KEYS, 128), F32)] * 8
                        + [pltpu.VMEM((2, NA_PV_KEYS, 128), BF16)] * 8
                        + [pltpu.VMEM((NA_HEADS // 2, 2 * NA_KH, GRID_W, 128), F32)]),
        compiler_params=_cparams(2),
        name="na",
    )(rpb, qw, k, vt, oob)


def _mem_kernel(q_ref, mem_ref, g_ref, w_ref, o_ref, kv_s):
    @pl.when(pl.program_id(1) == 0)
    def _():
        mn = _rms(mem_ref[0], g_ref[...]).astype(BF16)
        kv_s[...] = jnp.dot(mn, w_ref[...].astype(BF16), preferred_element_type=F32).astype(BF16)

    scale = MEM_HEAD_DIM ** -0.5
    for h in range(MEM_HEADS):
        cols = slice(MEM_HEAD_DIM * h, MEM_HEAD_DIM * (h + 1))
        vcols = slice(512 + MEM_HEAD_DIM * h, 512 + MEM_HEAD_DIM * (h + 1))
        sc = lax.dot_general(q_ref[0, :, cols], kv_s[:, cols], NT_DIMS,
                             preferred_element_type=F32) * scale
        m = jnp.max(sc, axis=-1, keepdims=True)
        p = jnp.exp(sc - m)
        l = jnp.sum(p, axis=-1, keepdims=True)
        o = jnp.dot(p.astype(BF16), kv_s[:, vcols], preferred_element_type=F32)
        o_ref[0, :, cols] = (o / l).astype(BF16)


def _mem_attn(q, mem, g, w):
    b, s, _ = q.shape
    m = mem.shape[1]
    tq = 4096
    return pl.pallas_call(
        _mem_kernel,
        grid=(b, s // tq),
        in_specs=[pl.BlockSpec((1, tq, 512), lambda bi, i: (bi, i, 0)),
                  pl.BlockSpec((1, m, D), lambda bi, i: (bi, 0, 0)),
                  _const_spec((1, D)),
                  _const_spec((D, 1024))],
        out_specs=pl.BlockSpec((1, tq, 512), lambda bi, i: (bi, i, 0)),
        out_shape=jax.ShapeDtypeStruct((b, s, 512), BF16),
        scratch_shapes=[pltpu.VMEM((m, 1024), BF16)],
        compiler_params=_cparams(2),
        name="mem_attn",
    )(q, mem, g, w)


def _merge_kernel(x_ref, g_ref, yna_ref, ys5_ref, qm_ref, mem_ref, gmem_ref, wkv_ref,
                  wg_ref, bg_ref, wb_ref, wo_ref, o_ref, kv_s, ymem_s, *, tiles_per_batch):
    @pl.when(pl.program_id(0) % tiles_per_batch == 0)
    def _():
        mn = _rms(mem_ref[0], gmem_ref[...]).astype(BF16)
        kv_s[...] = jnp.dot(mn, wkv_ref[...].astype(BF16), preferred_element_type=F32).astype(BF16)

    scale = MEM_HEAD_DIM ** -0.5
    for hd in range(MEM_HEADS):
        cols = slice(MEM_HEAD_DIM * hd, MEM_HEAD_DIM * (hd + 1))
        vcols = slice(512 + MEM_HEAD_DIM * hd, 512 + MEM_HEAD_DIM * (hd + 1))
        sc = lax.dot_general(qm_ref[:, cols], kv_s[:, cols], NT_DIMS,
                             preferred_element_type=F32) * scale
        mx = jnp.max(sc, axis=-1, keepdims=True)
        p = jnp.exp(sc - mx)
        l = jnp.sum(p, axis=-1, keepdims=True)
        o = jnp.dot(p.astype(BF16), kv_s[:, vcols], preferred_element_type=F32)
        ymem_s[:, cols] = (o / l).astype(BF16)

    x = x_ref[...]
    h = _rms(x, g_ref[...]).astype(BF16)
    merged = None
    y_na = jnp.concatenate([yna_ref[s] for s in range(4)], axis=1)
    ys = (y_na, ys5_ref[...].astype(BF16), ymem_s[...])
    for b, y in enumerate(ys):
        cols = slice(D * b, D * (b + 1))
        gate = jax.nn.sigmoid(jnp.dot(h, wg_ref[:, cols].astype(BF16), preferred_element_type=F32)
                              + bg_ref[:, cols])
        up = jnp.dot(y, wb_ref[b], preferred_element_type=F32)
        merged = gate * up if merged is None else merged + gate * up
    o_ref[...] = x + jnp.dot(merged.astype(BF16), wo_ref[...], preferred_element_type=F32)


def _merge(x2d, g, yna, ys5, qm, mem, gmem, wkv, wg, bg, wb, wo):
    n = x2d.shape[0]
    tm = 1024
    bsz, m, _ = mem.shape
    tpb = n // bsz // tm
    yspec = pl.BlockSpec((tm, 512), lambda i: (i, 0))
    return pl.pallas_call(
        functools.partial(_merge_kernel, tiles_per_batch=tpb),
        grid=(n // tm,),
        in_specs=[pl.BlockSpec((tm, D), lambda i: (i, 0)),
                  _const_spec((1, D)),
                  pl.BlockSpec((4, tm, 128), lambda i: (0, i, 0)), yspec, yspec,
                  pl.BlockSpec((1, m, D), lambda i: (i // tpb, 0, 0)),
                  _const_spec((1, D)),
                  _const_spec((D, 1024)),
                  _const_spec((D, 3 * D)),
                  _const_spec((1, 3 * D)),
                  _const_spec((3, 512, D)),
                  _const_spec((D, D))],
        out_specs=pl.BlockSpec((tm, D), lambda i: (i, 0)),
        out_shape=jax.ShapeDtypeStruct((n, D), F32),
        scratch_shapes=[pltpu.VMEM((m, 1024), BF16), pltpu.VMEM((tm, 512), BF16)],
        compiler_params=pltpu.CompilerParams(dimension_semantics=("arbitrary",),
                                             vmem_limit_bytes=60 * 1024 * 1024),
        name="merge",
    )(x2d, g, yna, ys5, qm, mem, gmem, wkv, wg, bg, wb, wo)


def _ffn_kernel(x_ref, g_ref, gf_ref, w1_ref, w3_ref, w2_ref, o_ref):
    x = x_ref[...]
    h = _rms(x, g_ref[...]).astype(BF16)
    a = jnp.dot(h, w1_ref[...], preferred_element_type=F32)
    c = jnp.dot(h, w3_ref[...], preferred_element_type=F32)
    mid = (a * jax.nn.sigmoid(a) * c).astype(BF16)
    x2 = x + jnp.dot(mid, w2_ref[...].astype(BF16), preferred_element_type=F32)
    o_ref[...] = _rms(x2, gf_ref[...])


def _ffn(x2d, g, gf, w1, w3, w2):
    n = x2d.shape[0]
    tm = 512
    return pl.pallas_call(
        _ffn_kernel,
        grid=(n // tm,),
        in_specs=[pl.BlockSpec((tm, D), lambda i: (i, 0)),
                  _const_spec((1, D)),
                  _const_spec((1, D)),
                  _const_spec((D, D_FF)),
                  _const_spec((D, D_FF)),
                  _const_spec((D_FF, D))],
        out_specs=pl.BlockSpec((tm, D), lambda i: (i, 0)),
        out_shape=jax.ShapeDtypeStruct((n, D), F32),
        compiler_params=_cparams(1),
        name="ffn",
    )(x2d, g, gf, w1, w3, w2)


def _s5_tables(a_re, a_im, log_dt, b_re, b_im, c_re, c_im, s5_d):
    t = CHUNK
    ar, ai = a_re.astype(F32), a_im.astype(F32)
    dt = jnp.exp(log_dt.astype(F32))[..., None]
    lr, li = ar * dt, ai * dt
    mag = jnp.exp(lr)
    lbr, lbi = mag * jnp.cos(li), mag * jnp.sin(li)
    den = ar * ar + ai * ai
    rr = ((lbr - 1.0) * ar + lbi * ai) / den
    ri = (lbi * ar - (lbr - 1.0) * ai) / den
    br, bi = b_re.astype(F32), b_im.astype(F32)
    bbr = rr[..., None] * br - ri[..., None] * bi
    bbi = rr[..., None] * bi + ri[..., None] * br
    cmr, cmi = c_re.astype(F32), c_im.astype(F32)
    tau = jnp.arange(t + 1, dtype=F32)[:, None, None, None]
    pmag = jnp.exp(tau * lr[None])
    pwr, pwi = pmag * jnp.cos(tau * li[None]), pmag * jnp.sin(tau * li[None])

    dd = jnp.pad(jnp.eye(S5_GROUP, dtype=F32)[None] * s5_d.astype(F32).reshape(S5_GROUPS, S5_GROUP, 1),
                 ((0, 0), (0, 0), (128 - S5_GROUP, 0)))

    def lanes4(f0, f1, b0, b1):
        return jnp.concatenate([f0, f1, b0, b1], axis=-1)

    cm = jnp.stack([lanes4(cmr[0], -cmi[0], cmr[1], -cmi[1]),
                    lanes4(-cmi[0], -cmr[0], -cmi[1], -cmr[1])], axis=1)
    fr, fi = jnp.moveaxis(pwr[1:t + 1, 0], 0, 1), jnp.moveaxis(pwi[1:t + 1, 0], 0, 1)
    gr, gi = jnp.moveaxis(pwr[1:t + 1][::-1, 1], 0, 1), jnp.moveaxis(pwi[1:t + 1][::-1, 1], 0, 1)
    bt = lambda z: jnp.swapaxes(z, -1, -2)
    bb = jnp.stack([lanes4(bt(bbr[0]), bt(bbi[0]), bt(bbr[1]), bt(bbi[1])),
                    lanes4(-bt(bbi[0]), bt(bbr[0]), -bt(bbi[1]), bt(bbr[1]))], axis=1)
    wr, wi = jnp.moveaxis(pwr[:t][::-1, 0], 0, 1), jnp.moveaxis(pwi[:t][::-1, 0], 0, 1)
    vr, vi = jnp.moveaxis(pwr[:t, 1], 0, 1), jnp.moveaxis(pwi[:t, 1], 0, 1)
    pw = jnp.stack([lanes4(fr, fr, gr, gr), lanes4(fi, fi, gi, gi),
                    lanes4(wr, wr, vr, vr), lanes4(wi, wi, vi, vi)], axis=1)

    dr, di = pwr[t], pwi[t]
    a_rows, b_rows = [], []
    for _ in range(N_SCAN_LEVELS):
        a_rows.append(lanes4(dr[0], dr[0], dr[1], dr[1]))
        b_rows.append(lanes4(-di[0], di[0], -di[1], di[1]))
        dr, di = dr * dr - di * di, 2.0 * dr * di
    pad = [jnp.zeros_like(a_rows[0])] * (8 - N_SCAN_LEVELS)
    dtab = jnp.stack(a_rows + pad + b_rows + pad, axis=1)
    return cm, bb, pw, dtab, dd


def kernel(x, mem, g_mix, g_mem, g_ffn, g_final, w_in, w_gate, b_gate, rpb, w_mem_kv,
           a_re, a_im, log_dt, b_re, b_im, c_re, c_im, s5_d, w_glu, w_branch, w_o,
           w_ffn1, w_ffn3, w_ffn2):
    bsz, s, d = x.shape
    n = bsz * s
    x2d = x.reshape(n, d)
    gm = g_mix[0].reshape(1, d).astype(F32)

    w_in_t = w_in[0].T.astype(BF16)
    cm_tab, bb_tab, pw_tab, dtab, dd_tab = _s5_tables(a_re[0], a_im[0], log_dt[0], b_re[0], b_im[0],
                                                      c_re[0], c_im[0], s5_d[0])

    k, qm, qw, vt = _proj(x2d, gm, w_in[0], w_in_t)

    ut = _proj_s5t(x.reshape(n // CHUNK, CHUNK, d), gm, w_in_t)
    zt = _s5_core(ut, cm_tab, bb_tab, pw_tab, dtab, dd_tab)
    y_s5 = _s5_glu(zt, w_glu[0].T.astype(BF16)).reshape(n, 512)

    y_na = _na(rpb[0].astype(F32), qw, k, vt, bsz, s)

    x1 = _merge(x2d, gm, y_na, y_s5, qm, mem, g_mem[0].reshape(1, d).astype(F32), w_mem_kv[0], w_gate[0],
                b_gate[0].reshape(1, 3 * d).astype(F32), w_branch[0].astype(BF16), w_o[0].astype(BF16))
    out = _ffn(x1, g_ffn[0].reshape(1, d).astype(F32), g_final.reshape(1, d).astype(F32),
               w_ffn1[0].astype(BF16), w_ffn3[0].astype(BF16), w_ffn2[0])
    return out.reshape(bsz, s, d)
```

```python
import functools

import jax
import jax.numpy as jnp
from jax import lax
from jax.experimental import pallas as pl
from jax.experimental.pallas import tpu as pltpu

F32 = jnp.float32
BF16 = jnp.bfloat16

D = 1024
GRID_W = 64
NA_HEADS = 8
NA_KH = 8
NA_KW = 16
S5_GROUPS = 32
S5_GROUP = 16
S5_STATE = 64
CHUNK = 64
MEM_HEADS = 4
MEM_HEAD_DIM = 128
D_FF = 2816
EPS = 1e-6
NEG_INF = -1e30

VMEM_LIMIT = 56 * 1024 * 1024

NT_DIMS = (((1,), (1,)), ((), ()))


def _cparams(n_axes):
    return pltpu.CompilerParams(
        dimension_semantics=("arbitrary",) * n_axes,
        vmem_limit_bytes=VMEM_LIMIT)


def _rms(x, g):
    return x * lax.rsqrt(jnp.mean(x * x, axis=-1, keepdims=True) + EPS) * g


def _const_spec(shape):
    nd = len(shape)
    return pl.BlockSpec(shape, lambda *_: (0,) * nd, pipeline_mode=pl.Buffered(1))


TOK_BLK = 128


LOG2E = 1.4426950408889634
NA_Q_SCALE = 64 ** -0.5 * LOG2E


def _proj_kernel(x_ref, g_ref, wk_ref, wqm_ref, wqt_ref, wvt_ref, k_ref, qm_ref, qw_ref, vt_ref):
    h = _rms(x_ref[...], g_ref[...]).astype(BF16)
    kk = jnp.dot(h, wk_ref[...].astype(BF16), preferred_element_type=F32).astype(BF16)
    for s in range(4):
        k_ref[s] = kk[:, 128 * s:128 * (s + 1)]
    qm_ref[...] = jnp.dot(h, wqm_ref[...].astype(BF16), preferred_element_type=F32).astype(BF16)
    qt = lax.dot_general(wqt_ref[...], h, NT_DIMS, preferred_element_type=F32) * NA_Q_SCALE
    vt = lax.dot_general(wvt_ref[...], h, NT_DIMS, preferred_element_type=F32)
    lo = lax.broadcasted_iota(jnp.int32, (128, 128), 1) < 64
    same_head = (lax.broadcasted_iota(jnp.int32, (128, 128), 0) < 64) == lo
    for i in range(x_ref.shape[0] // TOK_BLK):
        toks = slice(TOK_BLK * i, TOK_BLK * (i + 1))
        vt_ref[i] = vt[:, toks].astype(BF16)
        for s in range(4):
            a = qt[128 * s:128 * (s + 1), toks]
            ar = pltpu.roll(a, 64, axis=1)
            qw_ref[i, 128 * s:128 * (s + 1), :] = jnp.concatenate(
                [jnp.where(same_head, jnp.where(lo, a, ar), 0.0),
                 jnp.where(same_head, jnp.where(lo, ar, a), 0.0)], axis=1).astype(BF16)


W_IN_BLK = 512


def _w_in_spec(blk, transposed):
    if transposed:
        return pl.BlockSpec((W_IN_BLK, D), lambda *_: (blk, 0), pipeline_mode=pl.Buffered(1))
    return pl.BlockSpec((D, W_IN_BLK), lambda *_: (0, blk), pipeline_mode=pl.Buffered(1))


def _proj(x2d, g, w, wt):
    n = x2d.shape[0]
    tm = 1024
    nat = jax.ShapeDtypeStruct((n, 512), BF16)
    nat_spec = pl.BlockSpec((tm, 512), lambda i: (i, 0))
    chm = lambda width: (jax.ShapeDtypeStruct((n // TOK_BLK, 512, width), BF16),
                         pl.BlockSpec((tm // TOK_BLK, 512, width), lambda i: (i, 0, 0)))
    (qw_shape, qw_spec), (vt_shape, vt_spec) = chm(2 * TOK_BLK), chm(TOK_BLK)
    return pl.pallas_call(
        _proj_kernel,
        grid=(n // tm,),
        in_specs=[pl.BlockSpec((tm, D), lambda i: (i, 0)),
                  _const_spec((1, D)),
                  _w_in_spec(1, False),
                  _w_in_spec(4, False),
                  _w_in_spec(0, True),
                  _w_in_spec(2, True)],
        out_specs=[pl.BlockSpec((4, tm, 128), lambda i: (0, i, 0)), nat_spec, qw_spec, vt_spec],
        out_shape=[jax.ShapeDtypeStruct((4, n, 128), BF16), nat, qw_shape, vt_shape],
        compiler_params=_cparams(1),
        name="proj",
    )(x2d, g, w, w, wt, wt)


S5T_DT = 8


def _proj_s5t_kernel(x_ref, g_ref, wt_ref, o_ref, h_s):
    wt = wt_ref[...]
    nrows = x_ref.shape[0]
    hn = _rms(x_ref[...], g_ref[...]).reshape(nrows * S5T_DT, D)
    for c in range(D // 128):
        h_s[c] = hn[:, 128 * c:128 * (c + 1)]
    for j in range(S5T_DT):
        h = jnp.concatenate([h_s[c, pl.ds(j, nrows, stride=S5T_DT), :] for c in range(D // 128)],
                            axis=1).astype(BF16)
        ut = lax.dot_general(wt, h, NT_DIMS, preferred_element_type=F32)
        o_ref[j] = ut.astype(BF16)


def _proj_s5t(xc, g, wt):
    nrows = xc.shape[0]
    return pl.pallas_call(
        _proj_s5t_kernel,
        grid=(CHUNK // S5T_DT,),
        in_specs=[pl.BlockSpec((nrows, S5T_DT, D), lambda i: (0, i, 0)),
                  _const_spec((1, D)),
                  _w_in_spec(3, True)],
        out_specs=pl.BlockSpec((S5T_DT, 512, nrows), lambda i: (i, 0, 0)),
        out_shape=jax.ShapeDtypeStruct((CHUNK, 512, nrows), BF16),
        scratch_shapes=[pltpu.VMEM((D // 128, nrows * S5T_DT, 128), F32)],
        compiler_params=_cparams(1),
        name="proj_s5t",
    )(xc, g, wt)


N_SCAN_LEVELS = 6


def _gelu_tanh(x):
    c = 0.7978845608028654
    return 0.5 * x * (1.0 + jnp.tanh(c * (x + 0.044715 * (x * x * x))))


def _split_bf16(x):
    hi = x.astype(BF16)
    return hi, (x - hi.astype(F32)).astype(BF16)


def _nt_f32(a, b):
    ah, al = _split_bf16(a)
    bh, bl = _split_bf16(b)
    nt = lambda u, w: lax.dot_general(u, w, NT_DIMS, preferred_element_type=F32)
    return nt(ah, bh) + nt(ah, bl) + nt(al, bh)


S5_GROUPS_PER_STEP = 4
S5_SCRATCH_PER_GROUP = 5


def _s5_kernel(*refs):
    gs = S5_GROUPS_PER_STEP
    ut_ref, (cm_ref, bb_ref, pw_ref, d_ref, dd_ref) = refs[:gs], refs[gs:gs + 5]
    z_ref, scratch = refs[gs + 5], refs[gs + 6:]
    groups = [scratch[S5_SCRATCH_PER_GROUP * gi:S5_SCRATCH_PER_GROUP * (gi + 1)]
              for gi in range(S5_GROUPS_PER_STEP)]
    for gi, (m_s, n_s, pt_s, ptf_s, rs_s) in enumerate(groups):
        _s5_operators(gi, cm_ref, bb_ref, pw_ref, dd_ref, m_s, n_s, pt_s, ptf_s, rs_s)
    yv = [_s5_chunk_matmuls(gi, ut_ref, m_s, pt_s) for gi, (m_s, _, pt_s, _, _) in enumerate(groups)]
    for gi, (_, n_s, _, _, _) in enumerate(groups):
        _s5_finish(gi, d_ref, z_ref, n_s, *yv[gi])


def _s5_operators(gi, cm_ref, bb_ref, pw_ref, dd_ref, m_s, n_s, pt_s, ptf_s, rs_s):
    tc = CHUNK * S5_GROUP
    cma, cmb = cm_ref[gi,0], cm_ref[gi,1]
    bba, bbb = bb_ref[gi,0], bb_ref[gi,1]
    for t in range(CHUNK):
        rows = slice(S5_GROUP * t, S5_GROUP * (t + 1))
        n_s[rows, :] = (cma * pw_ref[gi,0, t:t + 1, :] + cmb * pw_ref[gi,1, t:t + 1, :]).astype(BF16)
        ptf_s[rows, :] = bba * pw_ref[gi,2, t:t + 1, :] + bbb * pw_ref[gi,3, t:t + 1, :]
    ptf = ptf_s[...]
    pt_s[...] = ptf.astype(BF16)

    ra = _nt_f32(cma[:, 0:128], ptf[:, 0:128])
    rb = _nt_f32(cma[:, 128:256], ptf[:, 128:256])
    zeros = jnp.zeros((S5_GROUP, tc), F32)
    r = (jnp.concatenate([ra, zeros], axis=1)
         + pltpu.roll(jnp.concatenate([rb, zeros], axis=1), (CHUNK - 1) * S5_GROUP, axis=1)
         + jnp.concatenate([zeros[:, 0:tc - 128], dd_ref[gi], zeros], axis=1))
    for k in range(8):
        rk = r if k == 0 else pltpu.roll(r, 2 * tc - S5_GROUP * k, axis=1)
        rs_s[k] = rk[:, 0:RS_LANES].astype(BF16)
    for t in range(CHUNK):
        a, k = divmod(CHUNK - 1 - t, 8)
        m_s[S5_GROUP * t:S5_GROUP * (t + 1), :] = rs_s[k, :, 128 * a:128 * a + tc]


def _s5_chunk_matmuls(gi, ut_ref, m_s, pt_s):
    ut = ut_ref[gi][...].reshape(CHUNK * S5_GROUP, -1)
    nrows = ut.shape[1]
    eye = jnp.where(lax.broadcasted_iota(jnp.int32, (nrows, nrows), 0)
                    == lax.broadcasted_iota(jnp.int32, (nrows, nrows), 1), 1.0, 0.0).astype(BF16)
    u_rows = lax.dot_general(eye, ut, NT_DIMS, preferred_element_type=F32).astype(BF16)
    y = jnp.dot(m_s[...], ut, preferred_element_type=F32)
    v = jnp.dot(u_rows, pt_s[...], preferred_element_type=F32)
    return y, v


def _s5_finish(gi, d_ref, z_ref, n_s, y, v):
    nrows = v.shape[0]
    pos = lax.broadcasted_iota(jnp.int32, (nrows, 128), 0) & (CHUNK - 1)

    def shift(x, s, up):
        if up:
            return jnp.where(pos < CHUNK - s, pltpu.roll(x, nrows - s, axis=0), 0.0)
        return jnp.where(pos >= s, pltpu.roll(x, s, axis=0), 0.0)

    def scan(vh, lanes, up):
        x = shift(vh, 1, up)
        for lvl in range(N_SCAN_LEVELS):
            xs = shift(x, 1 << lvl, up)
            a = d_ref[gi,lvl:lvl + 1, lanes]
            b = d_ref[gi,8 + lvl:9 + lvl, lanes]
            x = x + a * xs + b * pltpu.roll(xs, 64, axis=1)
        return x

    xf = scan(v[:, 0:128], slice(0, 128), False)
    xb = scan(v[:, 128:256], slice(128, 256), True)
    xin = jnp.concatenate([xf, xb], axis=1).astype(BF16)
    y = y + lax.dot_general(n_s[...], xin, NT_DIMS, preferred_element_type=F32)
    z_ref[:, S5_GROUP * gi:S5_GROUP * (gi + 1), :] = _gelu_tanh(y).astype(BF16).reshape(CHUNK, S5_GROUP, -1)


RS_LANES = 128 * 7 + CHUNK * S5_GROUP


def _s5_core(ut, cm, bb, pw, dtab, dd):
    nrows = ut.shape[2]
    tc = CHUNK * S5_GROUP
    gs = S5_GROUPS_PER_STEP
    group_specs = [pl.BlockSpec((CHUNK, S5_GROUP, nrows), functools.partial(lambda i, g: (0, gs * g + i, 0), i))
                   for i in range(gs)]
    return pl.pallas_call(
        _s5_kernel,
        grid=(S5_GROUPS // gs,),
        in_specs=[*group_specs,
                  pl.BlockSpec((gs, 2, S5_GROUP, 256), lambda g: (g, 0, 0, 0)),
                  pl.BlockSpec((gs, 2, S5_GROUP, 256), lambda g: (g, 0, 0, 0)),
                  pl.BlockSpec((gs, 4, CHUNK, 256), lambda g: (g, 0, 0, 0)),
                  pl.BlockSpec((gs, 16, 256), lambda g: (g, 0, 0)),
                  pl.BlockSpec((gs, S5_GROUP, 128), lambda g: (g, 0, 0))],
        out_specs=pl.BlockSpec((CHUNK, gs * S5_GROUP, nrows), lambda g: (0, g, 0)),
        out_shape=jax.ShapeDtypeStruct(ut.shape, BF16),
        scratch_shapes=[pltpu.VMEM((tc, tc), BF16),
                        pltpu.VMEM((tc, 256), BF16),
                        pltpu.VMEM((tc, 256), BF16),
                        pltpu.VMEM((tc, 256), F32),
                        pltpu.VMEM((8, S5_GROUP, RS_LANES), BF16),
                        ] * gs,
        compiler_params=_cparams(1),
        name="s5_core",
    )(*([ut] * gs), cm, bb, pw, dtab, dd)


def _s5_glu_kernel(z_ref, wt_ref, o_ref):
    wt = wt_ref[...]
    nrows = z_ref.shape[2]
    eye = jnp.where(lax.broadcasted_iota(jnp.int32, (nrows, nrows), 0)
                    == lax.broadcasted_iota(jnp.int32, (nrows, nrows), 1), 1.0, 0.0).astype(BF16)
    for j in range(S5T_DT):
        zt = z_ref[j]
        gl = jnp.dot(wt, zt, preferred_element_type=F32)
        o = (zt.astype(F32) * jax.nn.sigmoid(gl)).astype(BF16)
        nat = lax.dot_general(eye, o, NT_DIMS, preferred_element_type=F32)
        o_ref[:, j, :] = nat


def _s5_glu(zt, wglu_t):
    nrows = zt.shape[2]
    return pl.pallas_call(
        _s5_glu_kernel,
        grid=(CHUNK // S5T_DT,),
        in_specs=[pl.BlockSpec((S5T_DT, 512, nrows), lambda i: (i, 0, 0)),
                  _const_spec((512, 512))],
        out_specs=pl.BlockSpec((nrows, S5T_DT, 512), lambda i: (0, i, 0)),
        out_shape=jax.ShapeDtypeStruct((nrows, CHUNK, 512), F32),
        compiler_params=_cparams(1),
        name="s5_glu",
    )(zt, wglu_t)


NA_ROWS_PER_STEP = 32
NA_PAIRS_PER_STEP = NA_ROWS_PER_STEP // 2
NA_WIN_ROWS = NA_KH + 1
NA_WIN_KEYS = NA_WIN_ROWS * GRID_W
NA_WIN_BLKS = -(-NA_WIN_KEYS // TOK_BLK)
NA_PV_KEYS = NA_WIN_BLKS * TOK_BLK
NA_RR_OUTSIDE = 2 * NA_KH - 1


def _na_kernel(rpb_ref, qw_ref, k_ref, vt_ref, oob_ref, o_ref, *scratch):
    st_s, pt_s, bias_s = (scratch[0:4], scratch[4:8]), (scratch[8:12], scratch[12:16]), scratch[16]
    rb = pl.program_id(1)
    lo_q = lax.broadcasted_iota(jnp.int32, (GRID_W, 128), 1) < 64
    same_head2 = ((lax.broadcasted_iota(jnp.int32, (128, 256), 0) < 64)
                  == ((lax.broadcasted_iota(jnp.int32, (128, 256), 1) & 64) == 0))
    esum = jnp.where(lax.broadcasted_iota(jnp.int32, (GRID_W, 128), 0)
                     == (lax.broadcasted_iota(jnp.int32, (GRID_W, 128), 1) & (GRID_W - 1)),
                     1.0, 0.0).astype(BF16)
    half = NA_KH // 2

    kcol = lax.broadcasted_iota(jnp.int32, (GRID_W, 128), 0)
    qcol = lax.broadcasted_iota(jnp.int32, (GRID_W, 128), 1) & (GRID_W - 1)
    qstart = jnp.clip(qcol - NA_KW // 2, 0, GRID_W - NA_KW)
    col_ok = (kcol >= qstart) & (kcol < qstart + NA_KW)

    @pl.when((pl.program_id(0) == 0) & (rb == 0))
    def _build_bias():
        rel = jnp.clip(kcol - qcol + (NA_KW - 1), 0, 2 * NA_KW - 2)

        def rr_body(rr, carry):
            for s in range(4):
                acc = jnp.zeros((GRID_W, 128), F32)
                for j in range(2 * NA_KW - 1):
                    acc = jnp.where(rel == j, jnp.where(lo_q, rpb_ref[2 * s, rr, j], rpb_ref[2 * s + 1, rr, j]), acc)
                bias_s[s, rr] = acc * LOG2E
            return carry

        lax.fori_loop(0, NA_RR_OUTSIDE, rr_body, 0)
        for s in range(4):
            bias_s[s, NA_RR_OUTSIDE] = oob_ref[...]

    def window(pi):
        r0 = rb * NA_ROWS_PER_STEP + 2 * pi
        return r0, jnp.clip(r0 - half, 0, GRID_W - NA_WIN_ROWS)

    def q_stage(pi):
        _, win = window(pi)
        koff = pl.multiple_of(win * GRID_W, GRID_W)
        scs = []
        for s in range(4):
            ch = slice(128 * s, 128 * (s + 1))
            kw = k_ref[s, pl.ds(koff, NA_WIN_KEYS), :]
            scs.append(jnp.dot(kw, qw_ref[pi, ch, :], preferred_element_type=F32))
        return scs

    def s_stage(pi, par, scs):
        r0, win = window(pi)
        poff = pl.multiple_of((win & 1) * GRID_W, GRID_W)
        rr = []
        for p in range(2):
            r = r0 + p
            first = jnp.clip(r - half, 0, GRID_W - NA_KH) - win
            rel = win - r + (NA_KH - 1)
            rr.append([jnp.where((wr >= first) & (wr < first + NA_KH), wr + rel, NA_RR_OUTSIDE)
                       for wr in range(NA_WIN_ROWS)])
        sums = []
        for s in range(4):
            st = st_s[par][s]
            slab_sums = []
            for p in range(2):
                lanes = slice(128 * p, 128 * (p + 1))
                for edge in (0, NA_PV_KEYS - GRID_W):
                    pt_s[par][s][p, edge:edge + GRID_W, :] = jnp.zeros((GRID_W, 128), BF16)
                m = None
                for wr in range(NA_WIN_ROWS):
                    rows = slice(GRID_W * wr, GRID_W * (wr + 1))
                    t = jnp.where(col_ok, scs[s][rows, lanes] + bias_s[s, rr[p][wr]], NEG_INF)
                    st[p, rows, :] = t
                    m = t if m is None else jnp.maximum(m, t)
                m = jnp.max(m, axis=0, keepdims=True)
                l = None
                for wr in range(NA_WIN_ROWS):
                    rows = slice(GRID_W * wr, GRID_W * (wr + 1))
                    e = jnp.exp2(st[p, rows, :] - m)
                    pt_s[par][s][p, pl.ds(poff + GRID_W * wr, GRID_W), :] = e.astype(BF16)
                    l = e if l is None else l + e
                slab_sums.append(jnp.sum(l, axis=0, keepdims=True))
            sums.append(jnp.concatenate(slab_sums, axis=1))
        return tuple(sums)

    def o_stage_pv(pi, par, sums):
        _, win = window(pi)
        blk0 = lax.shift_right_logical(win, 1)
        ots = []
        for s in range(4):
            ch = slice(128 * s, 128 * (s + 1))
            vw = jnp.concatenate([vt_ref[blk0 + i, ch, :] for i in range(NA_WIN_BLKS)], axis=1)
            pt = jnp.concatenate([pt_s[par][s][0], pt_s[par][s][1]], axis=1)
            ot = jnp.dot(vw, pt, preferred_element_type=F32)
            ots.append(jnp.where(same_head2, (ot / sums[s]).astype(BF16), jnp.zeros((), BF16)))
        return ots

    def o_stage_store(pi, ots):
        for s in range(4):
            ch = slice(128 * s, 128 * (s + 1))
            for p in range(2):
                nat = lax.dot_general(esum, ots[s][:, 128 * p:128 * (p + 1)], NT_DIMS,
                                      preferred_element_type=F32)
                qoff = pl.multiple_of((2 * pi + p) * GRID_W, GRID_W)
                o_ref[s, pl.ds(qoff, GRID_W), :] = nat.astype(BF16)

    def step(j, prev_sums, has_prev=True):
        a, b = 2 * j, 2 * j + 1
        scs_a = q_stage(a)
        if has_prev:
            ots_a = o_stage_pv(a - 2, 0, prev_sums[0])
            ots_b = o_stage_pv(b - 2, 1, prev_sums[1])
        scs_b = q_stage(b)
        if has_prev:
            o_stage_store(a - 2, ots_a)
            o_stage_store(b - 2, ots_b)
        return s_stage(a, 0, scs_a), s_stage(b, 1, scs_b)

    sums = step(0, None, has_prev=False)
    sums = lax.fori_loop(1, NA_PAIRS_PER_STEP // 2, step, sums)
    for u in range(2):
        pi = NA_PAIRS_PER_STEP - 2 + u
        o_stage_store(pi, o_stage_pv(pi, u, sums[u]))


def _na(rpb, qw, k, vt, b, s):
    tq = NA_ROWS_PER_STEP * GRID_W
    blks = s // TOK_BLK
    steps = s // tq
    oob = jnp.full((GRID_W, 128), -jnp.inf, F32)
    return pl.pallas_call(
        _na_kernel,
        grid=(b, s // tq),
        in_specs=[pl.BlockSpec(memory_space=pltpu.SMEM),
                  pl.BlockSpec((NA_PAIRS_PER_STEP, 512, 2 * TOK_BLK),
                               lambda bi, ri: (bi * (blks // NA_PAIRS_PER_STEP) + ri, 0, 0)),
                  pl.BlockSpec((4, s, 128), lambda bi, ri: (0, bi, 0)),
                  pl.BlockSpec((blks, 512, TOK_BLK), lambda bi, ri: (bi, 0, 0)),
                  _const_spec(oob.shape)],
        out_specs=pl.BlockSpec((4, tq, 128), lambda bi, ri: (0, bi * steps + ri, 0)),
        out_shape=jax.ShapeDtypeStruct((4, b * s, 128), BF16),
        scratch_shapes=([pltpu.VMEM((2, NA_WIN_KEYS, 128), F32)] * 8
                        + [pltpu.VMEM((2, NA_PV_KEYS, 128), BF16)] * 8
                        + [pltpu.VMEM((NA_HEADS // 2, 2 * NA_KH, GRID_W, 128), F32)]),
        compiler_params=_cparams(2),
        name="na",
    )(rpb, qw, k, vt, oob)


def _merge_kernel(x_ref, g_ref, yna_ref, ys5_ref, qm_ref, mem_ref, gmem_ref, wkv_ref,
                  wg_ref, bg_ref, wb_ref, wo_ref, o_ref, kv_s, ymem_s, *, tiles_per_batch):
    @pl.when(pl.program_id(0) % tiles_per_batch == 0)
    def _():
        mn = _rms(mem_ref[0], gmem_ref[...]).astype(BF16)
        kv_s[...] = jnp.dot(mn, wkv_ref[...].astype(BF16), preferred_element_type=F32).astype(BF16)

    scale = MEM_HEAD_DIM ** -0.5
    for hd in range(MEM_HEADS):
        cols = slice(MEM_HEAD_DIM * hd, MEM_HEAD_DIM * (hd + 1))
        vcols = slice(512 + MEM_HEAD_DIM * hd, 512 + MEM_HEAD_DIM * (hd + 1))
        sc = lax.dot_general(qm_ref[:, cols], kv_s[:, cols], NT_DIMS,
                             preferred_element_type=F32) * scale
        mx = jnp.max(sc, axis=-1, keepdims=True)
        p = jnp.exp(sc - mx)
        l = jnp.sum(p, axis=-1, keepdims=True)
        o = jnp.dot(p.astype(BF16), kv_s[:, vcols], preferred_element_type=F32)
        ymem_s[:, cols] = (o / l).astype(BF16)

    x = x_ref[...]
    h = _rms(x, g_ref[...]).astype(BF16)
    merged = None
    y_na = jnp.concatenate([yna_ref[s] for s in range(4)], axis=1)
    ys = (y_na, ys5_ref[...].astype(BF16), ymem_s[...])
    for b, y in enumerate(ys):
        cols = slice(D * b, D * (b + 1))
        gate = jax.nn.sigmoid(jnp.dot(h, wg_ref[:, cols].astype(BF16), preferred_element_type=F32)
                              + bg_ref[:, cols])
        up = jnp.dot(y, wb_ref[b], preferred_element_type=F32)
        merged = gate * up if merged is None else merged + gate * up
    o_ref[...] = x + jnp.dot(merged.astype(BF16), wo_ref[...], preferred_element_type=F32)


def _merge(x2d, g, yna, ys5, qm, mem, gmem, wkv, wg, bg, wb, wo):
    n = x2d.shape[0]
    tm = 1024
    bsz, m, _ = mem.shape
    tpb = n // bsz // tm
    yspec = pl.BlockSpec((tm, 512), lambda i: (i, 0))
    return pl.pallas_call(
        functools.partial(_merge_kernel, tiles_per_batch=tpb),
        grid=(n // tm,),
        in_specs=[pl.BlockSpec((tm, D), lambda i: (i, 0)),
                  _const_spec((1, D)),
                  pl.BlockSpec((4, tm, 128), lambda i: (0, i, 0)), yspec, yspec,
                  pl.BlockSpec((1, m, D), lambda i: (i // tpb, 0, 0)),
                  _const_spec((1, D)),
                  _const_spec((D, 1024)),
                  _const_spec((D, 3 * D)),
                  _const_spec((1, 3 * D)),
                  _const_spec((3, 512, D)),
                  _const_spec((D, D))],
        out_specs=pl.BlockSpec((tm, D), lambda i: (i, 0)),
        out_shape=jax.ShapeDtypeStruct((n, D), F32),
        scratch_shapes=[pltpu.VMEM((m, 1024), BF16), pltpu.VMEM((tm, 512), BF16)],
        compiler_params=pltpu.CompilerParams(dimension_semantics=("arbitrary",),
                                             vmem_limit_bytes=60 * 1024 * 1024),
        name="merge",
    )(x2d, g, yna, ys5, qm, mem, gmem, wkv, wg, bg, wb, wo)


def _ffn_kernel(x_ref, g_ref, gf_ref, w1_ref, w3_ref, w2_ref, o_ref):
    x = x_ref[...]
    h = _rms(x, g_ref[...]).astype(BF16)
    a = jnp.dot(h, w1_ref[...], preferred_element_type=F32)
    c = jnp.dot(h, w3_ref[...], preferred_element_type=F32)
    mid = (a * jax.nn.sigmoid(a) * c).astype(BF16)
    x2 = x + jnp.dot(mid, w2_ref[...].astype(BF16), preferred_element_type=F32)
    o_ref[...] = _rms(x2, gf_ref[...])


def _ffn(x2d, g, gf, w1, w3, w2):
    n = x2d.shape[0]
    tm = 512
    return pl.pallas_call(
        _ffn_kernel,
        grid=(n // tm,),
        in_specs=[pl.BlockSpec((tm, D), lambda i: (i, 0)),
                  _const_spec((1, D)),
                  _const_spec((1, D)),
                  _const_spec((D, D_FF)),
                  _const_spec((D, D_FF)),
                  _const_spec((D_FF, D))],
        out_specs=pl.BlockSpec((tm, D), lambda i: (i, 0)),
        out_shape=jax.ShapeDtypeStruct((n, D), F32),
        compiler_params=_cparams(1),
        name="ffn",
    )(x2d, g, gf, w1, w3, w2)


def _s5_tables(a_re, a_im, log_dt, b_re, b_im, c_re, c_im, s5_d):
    t = CHUNK
    ar, ai = a_re.astype(F32), a_im.astype(F32)
    dt = jnp.exp(log_dt.astype(F32))[..., None]
    lr, li = ar * dt, ai * dt
    mag = jnp.exp(lr)
    lbr, lbi = mag * jnp.cos(li), mag * jnp.sin(li)
    den = ar * ar + ai * ai
    rr = ((lbr - 1.0) * ar + lbi * ai) / den
    ri = (lbi * ar - (lbr - 1.0) * ai) / den
    br, bi = b_re.astype(F32), b_im.astype(F32)
    bbr = rr[..., None] * br - ri[..., None] * bi
    bbi = rr[..., None] * bi + ri[..., None] * br
    cmr, cmi = c_re.astype(F32), c_im.astype(F32)
    tau = jnp.arange(t + 1, dtype=F32)[:, None, None, None]
    pmag = jnp.exp(tau * lr[None])
    pwr, pwi = pmag * jnp.cos(tau * li[None]), pmag * jnp.sin(tau * li[None])

    dd = jnp.pad(jnp.eye(S5_GROUP, dtype=F32)[None] * s5_d.astype(F32).reshape(S5_GROUPS, S5_GROUP, 1),
                 ((0, 0), (0, 0), (128 - S5_GROUP, 0)))

    def lanes4(f0, f1, b0, b1):
        return jnp.concatenate([f0, f1, b0, b1], axis=-1)

    cm = jnp.stack([lanes4(cmr[0], -cmi[0], cmr[1], -cmi[1]),
                    lanes4(-cmi[0], -cmr[0], -cmi[1], -cmr[1])], axis=1)
    fr, fi = jnp.moveaxis(pwr[1:t + 1, 0], 0, 1), jnp.moveaxis(pwi[1:t + 1, 0], 0, 1)
    gr, gi = jnp.moveaxis(pwr[1:t + 1][::-1, 1], 0, 1), jnp.moveaxis(pwi[1:t + 1][::-1, 1], 0, 1)
    bt = lambda z: jnp.swapaxes(z, -1, -2)
    bb = jnp.stack([lanes4(bt(bbr[0]), bt(bbi[0]), bt(bbr[1]), bt(bbi[1])),
                    lanes4(-bt(bbi[0]), bt(bbr[0]), -bt(bbi[1]), bt(bbr[1]))], axis=1)
    wr, wi = jnp.moveaxis(pwr[:t][::-1, 0], 0, 1), jnp.moveaxis(pwi[:t][::-1, 0], 0, 1)
    vr, vi = jnp.moveaxis(pwr[:t, 1], 0, 1), jnp.moveaxis(pwi[:t, 1], 0, 1)
    pw = jnp.stack([lanes4(fr, fr, gr, gr), lanes4(fi, fi, gi, gi),
                    lanes4(wr, wr, vr, vr), lanes4(wi, wi, vi, vi)], axis=1)

    dr, di = pwr[t], pwi[t]
    a_rows, b_rows = [], []
    for _ in range(N_SCAN_LEVELS):
        a_rows.append(lanes4(dr[0], dr[0], dr[1], dr[1]))
        b_rows.append(lanes4(-di[0], di[0], -di[1], di[1]))
        dr, di = dr * dr - di * di, 2.0 * dr * di
    pad = [jnp.zeros_like(a_rows[0])] * (8 - N_SCAN_LEVELS)
    dtab = jnp.stack(a_rows + pad + b_rows + pad, axis=1)
    return cm, bb, pw, dtab, dd


def kernel(x, mem, g_mix, g_mem, g_ffn, g_final, w_in, w_gate, b_gate, rpb, w_mem_kv,
           a_re, a_im, log_dt, b_re, b_im, c_re, c_im, s5_d, w_glu, w_branch, w_o,
           w_ffn1, w_ffn3, w_ffn2):
    bsz, s, d = x.shape
    n = bsz * s
    x2d = x.reshape(n, d)
    gm = g_mix[0].reshape(1, d).astype(F32)

    w_in_t = w_in[0].T.astype(BF16)
    cm_tab, bb_tab, pw_tab, dtab, dd_tab = _s5_tables(a_re[0], a_im[0], log_dt[0], b_re[0], b_im[0],
                                                      c_re[0], c_im[0], s5_d[0])

    k, qm, qw, vt = _proj(x2d, gm, w_in[0], w_in_t)

    ut = _proj_s5t(x.reshape(n // CHUNK, CHUNK, d), gm, w_in_t)
    zt = _s5_core(ut, cm_tab, bb_tab, pw_tab, dtab, dd_tab)
    y_s5 = _s5_glu(zt, w_glu[0].T.astype(BF16)).reshape(n, 512)

    y_na = _na(rpb[0].astype(F32), qw, k, vt, bsz, s)

    x1 = _merge(x2d, gm, y_na, y_s5, qm, mem, g_mem[0].reshape(1, d).astype(F32), w_mem_kv[0], w_gate[0],
                b_gate[0].reshape(1, 3 * d).astype(F32), w_branch[0].astype(BF16), w_o[0].astype(BF16))
    out = _ffn(x1, g_ffn[0].reshape(1, d).astype(F32), g_final.reshape(1, d).astype(F32),
               w_ffn1[0].astype(BF16), w_ffn3[0].astype(BF16), w_ffn2[0])
    return out.reshape(bsz, s, d)
```
